```python
import math
import jax, jax.numpy as jnp
from jax import lax
import numpy as np

D_MODEL = 2048
BATCH = 4
SEQ = 2048
DEPTH = 1
DEC_BATCH = 8
DEC_SEQ = 2048
PAST_LEN = 128

HEAD_DIM = 128
N_HEADS = 16
ATT_W = N_HEADS * HEAD_DIM
DILATED_PATTERNS = ((128, 1), (512, 4), (2048, 16))
ROPE_THETA = 500000.0
ROPE_DIM = HEAD_DIM // 4
HY_W = D_MODEL
HY_ORDER = 2
HY_EMB_DIM = 33
HY_BANDS = (HY_EMB_DIM - 1) // 2
HY_FFN = 64
HY_FAST_DECAY = 0.3
HY_SLOW_DECAY = 1.5
HY_DECAY_TARGET = 1e-2
HY_MIN_DECAY = math.log(HY_DECAY_TARGET) / HY_SLOW_DECAY
HY_MAX_DECAY = math.log(HY_DECAY_TARGET) / HY_FAST_DECAY
D_FF = 4 * D_MODEL
PLE_DIM = 256
N_BRANCH = 2
N_IN = 3 * ATT_W + 3 * HY_W + N_BRANCH * D_MODEL
EPS = 1e-6

kernel_name = 'dilated_attn_hyena_parallel_encoder'

F32 = jnp.float32


def rms_norm(x, g):
    xf = x.astype(F32)
    y = xf * lax.rsqrt(jnp.mean(xf * xf, axis=-1, keepdims=True) + EPS)
    return (y * g.astype(F32)).astype(x.dtype)


def partial_rope(x, seq_len):
    half = ROPE_DIM // 2
    inv_freq = ROPE_THETA ** (-jnp.arange(half, dtype=F32) / half)
    ang = jnp.arange(seq_len, dtype=F32)[:, None] * inv_freq[None, :]
    cos = jnp.cos(ang)[None, :, None, :]
    sin = jnp.sin(ang)[None, :, None, :]
    xf = x.astype(F32)
    x1 = xf[..., :half]
    x2 = xf[..., half:ROPE_DIM]
    return jnp.concatenate([x1 * cos - x2 * sin, x2 * cos + x1 * sin, xf[..., ROPE_DIM:]], axis=-1)


def dilated_band_attention(q, k, v, window, dilation):
    b, s, h, c = q.shape
    radius = window // (2 * dilation)
    blk = radius
    sub = s // dilation
    nb = -(-sub // blk)
    lp = nb * blk

    def split(t):
        return t.reshape(b, sub, dilation, h, c).transpose(0, 2, 1, 3, 4)

    qs = jnp.pad(split(q), ((0, 0), (0, 0), (0, lp - sub), (0, 0), (0, 0))).reshape(b, dilation, nb, blk, h, c)

    def windows(t):
        tp = jnp.pad(split(t), ((0, 0), (0, 0), (blk, lp - sub + blk), (0, 0), (0, 0)))
        tp = tp.reshape(b, dilation, nb + 2, blk, h, c)
        return jnp.concatenate([tp[:, :, :-2], tp[:, :, 1:-1], tp[:, :, 2:]], axis=3)

    kw = windows(k)
    vw = windows(v.astype(F32))
    scores = jnp.einsum('bdnqhc,bdnkhc->bdnhqk', qs, kw) * (c ** -0.5)
    qpos = jnp.arange(nb)[:, None, None] * blk + jnp.arange(blk)[None, :, None]
    kpos = jnp.arange(nb)[:, None, None] * blk - blk + jnp.arange(3 * blk)[None, None, :]
    valid = (jnp.abs(kpos - qpos) <= radius) & (kpos >= 0) & (kpos < sub)
    scores = jnp.where(valid[None, None, :, None], scores, -jnp.inf)
    lse = jax.nn.logsumexp(scores, axis=-1)
    probs = jnp.exp(scores - lse[..., None])
    out = jnp.einsum('bdnhqk,bdnkhc->bdnqhc', probs, vw)
    out = out.reshape(b, dilation, lp, h, c)[:, :, :sub].transpose(0, 2, 1, 3, 4).reshape(b, s, h, c)
    lse = lse.transpose(0, 1, 2, 4, 3).reshape(b, dilation, lp, h)[:, :, :sub]
    lse = lse.transpose(0, 2, 1, 3).reshape(b, s, h)
    return out, lse


def hyena_filters(seq_len, w1, b1, w2, b2, sin_freq, w3):
    t = jnp.linspace(0.0, 1.0, seq_len, dtype=F32)[:, None]
    wpos = 2.0 * math.pi * jnp.arange(seq_len, dtype=F32) / seq_len
    bands = jnp.linspace(1e-4, HY_BANDS - 1, HY_BANDS, dtype=F32)
    ang = wpos[:, None] * bands[None, :]
    z = jnp.concatenate([t, jnp.cos(ang), -jnp.sin(ang)], axis=-1)
    fr = sin_freq.astype(F32)
    hdn = jnp.sin(fr * (z @ w1.astype(F32) + b1.astype(F32)))
    hdn = jnp.sin(fr * (hdn @ w2.astype(F32) + b2.astype(F32)))
    hf = (hdn @ w3.astype(F32)).reshape(seq_len, HY_ORDER, 2, HY_W)
    deltas = jnp.abs(jnp.linspace(HY_MIN_DECAY, HY_MAX_DECAY, HY_W, dtype=F32))
    decay = jnp.exp(-t * deltas[None, :])
    hf = hf * decay[:, None, None, :]
    fwd = hf[:, :, 0]
    bwd = hf[:, :, 1]
    full = jnp.concatenate([fwd, jnp.zeros_like(fwd[:1]), bwd[1:][::-1]], axis=0)
    full = full / jnp.sum(jnp.abs(full), axis=0, keepdims=True)
    return jnp.fft.rfft(full, axis=0)


def long_conv(z, filt_f, bias):
    seq_len = z.shape[1]
    zf = jnp.fft.rfft(z, n=2 * seq_len, axis=1)
    y = jnp.fft.irfft(zf * filt_f[None], n=2 * seq_len, axis=1)[:, :seq_len]
    return y + bias.astype(F32) * z


def short_conv(u, w, b):
    up = jnp.pad(u, ((0, 0), (1, 1), (0, 0)))
    return up[:, :-2] * w[0] + up[:, 1:-1] * w[1] + up[:, 2:] * w[2] + b


def encoder_layer(x, p, norm_mix_g, w_in, gate_b, q_norm_g, k_norm_g, hy_conv_w, hy_conv_b,
                  hy_w1, hy_b1, hy_w2, hy_b2, hy_sin_freq, hy_w3, hy_bias,
                  w_att_out, w_hy_out, w_o, norm_mlp_g, w_up, w_down,
                  norm_ple_g, w_ple_gate, w_ple_proj):
    b, s, _ = x.shape
    h = rms_norm(x, norm_mix_g)
    proj = h @ w_in
    q, k, v, hy_in, gates = jnp.split(
        proj, [ATT_W, 2 * ATT_W, 3 * ATT_W, 3 * ATT_W + 3 * HY_W], axis=-1)

    q = partial_rope(rms_norm(q.reshape(b, s, N_HEADS, HEAD_DIM), q_norm_g), s)
    k = partial_rope(rms_norm(k.reshape(b, s, N_HEADS, HEAD_DIM), k_norm_g), s)
    v = v.reshape(b, s, N_HEADS, HEAD_DIM)
    res = [dilated_band_attention(q, k, v, wnd, dil) for (wnd, dil) in DILATED_PATTERNS]
    outs = jnp.stack([r[0] for r in res])
    lses = jnp.stack([r[1] for r in res])
    wts = jax.nn.softmax(lses, axis=0)
    attn = jnp.einsum('gbsh,gbshc->bshc', wts, outs).reshape(b, s, ATT_W).astype(x.dtype)

    u = short_conv(hy_in, hy_conv_w, hy_conv_b).astype(F32)
    hv, hx1, hx2 = jnp.split(u, 3, axis=-1)
    filt = hyena_filters(s, hy_w1, hy_b1, hy_w2, hy_b2, hy_sin_freq, hy_w3)
    z = hx1 * long_conv(hv, filt[:, 0], hy_bias[0])
    hy = (hx2 * long_conv(z, filt[:, 1], hy_bias[1])).astype(x.dtype)

    g = jax.nn.sigmoid((gates.reshape(b, s, N_BRANCH, D_MODEL) + gate_b).astype(F32)).astype(x.dtype)
    merged = g[:, :, 0] * (attn @ w_att_out) + g[:, :, 1] * (hy @ w_hy_out)
    x = x + merged @ w_o

    hm = rms_norm(x, norm_mlp_g)
    x = x + jnp.square(jax.nn.relu(hm @ w_up)) @ w_down

    gp = jax.nn.sigmoid((rms_norm(x, norm_ple_g) @ w_ple_gate).astype(F32)).astype(x.dtype)
    return x + gp * (p @ w_ple_proj)


def setup_inputs(seed: int = 0) -> dict:
    key = jax.random.key(seed)
    ks = jax.random.split(key, 32)

    def nrm(k, shape, scale):
        return jax.random.normal(k, shape, dtype=F32) * scale

    def gain(k, shape):
        return 1.0 + nrm(k, shape, 0.02)

    L = DEPTH
    return {
        'x_prompt': nrm(ks[0], (BATCH, SEQ, D_MODEL), 1.0),
        'x_sample': nrm(ks[1], (DEC_BATCH, DEC_SEQ, D_MODEL), 1.0),
        'p_prompt': nrm(ks[2], (DEPTH, BATCH, SEQ, PLE_DIM), 1.0),
        'p_sample': nrm(ks[3], (DEPTH, DEC_BATCH, DEC_SEQ, PLE_DIM), 1.0),
        'norm_mix_g': gain(ks[4], (L, D_MODEL)),
        'w_in': nrm(ks[5], (L, D_MODEL, N_IN), D_MODEL ** -0.5),
        'gate_b': nrm(ks[6], (L, N_BRANCH, D_MODEL), 0.02),
        'q_norm_g': gain(ks[7], (L, HEAD_DIM)),
        'k_norm_g': gain(ks[8], (L, HEAD_DIM)),
        'hy_conv_w': nrm(ks[9], (L, 3, 3 * HY_W), 3 ** -0.5),
        'hy_conv_b': nrm(ks[10], (L, 3 * HY_W), 0.02),
        'hy_w1': nrm(ks[11], (L, HY_EMB_DIM, HY_FFN), HY_EMB_DIM ** -0.5),
        'hy_b1': nrm(ks[12], (L, HY_FFN), 0.1),
        'hy_w2': nrm(ks[13], (L, HY_FFN, HY_FFN), HY_FFN ** -0.5),
        'hy_b2': nrm(ks[14], (L, HY_FFN), 0.1),
        'hy_sin_freq': gain(ks[15], (L, HY_FFN)),
        'hy_w3': nrm(ks[16], (L, HY_FFN, HY_ORDER * 2 * HY_W), HY_FFN ** -0.5),
        'hy_bias': nrm(ks[17], (L, HY_ORDER, HY_W), 1.0),
        'w_att_out': nrm(ks[18], (L, ATT_W, D_MODEL), ATT_W ** -0.5),
        'w_hy_out': nrm(ks[19], (L, HY_W, D_MODEL), HY_W ** -0.5),
        'w_o': nrm(ks[20], (L, D_MODEL, D_MODEL), D_MODEL ** -0.5),
        'norm_mlp_g': gain(ks[21], (L, D_MODEL)),
        'w_up': nrm(ks[22], (L, D_MODEL, D_FF), D_MODEL ** -0.5),
        'w_down': nrm(ks[23], (L, D_FF, D_MODEL), D_FF ** -0.5),
        'norm_ple_g': gain(ks[24], (L, D_MODEL)),
        'w_ple_gate': nrm(ks[25], (L, D_MODEL, D_MODEL), D_MODEL ** -0.5),
        'w_ple_proj': nrm(ks[26], (L, PLE_DIM, D_MODEL), PLE_DIM ** -0.5),
    }


def reference(x_prompt, x_sample, p_prompt, p_sample, norm_mix_g, w_in, gate_b, q_norm_g, k_norm_g,
              hy_conv_w, hy_conv_b, hy_w1, hy_b1, hy_w2, hy_b2, hy_sin_freq, hy_w3, hy_bias,
              w_att_out, w_hy_out, w_o, norm_mlp_g, w_up, w_down, norm_ple_g, w_ple_gate, w_ple_proj):
    y_prompt = x_prompt
    y_sample = x_sample
    for i in range(DEPTH):
        lw = (norm_mix_g[i], w_in[i], gate_b[i], q_norm_g[i], k_norm_g[i], hy_conv_w[i], hy_conv_b[i],
              hy_w1[i], hy_b1[i], hy_w2[i], hy_b2[i], hy_sin_freq[i], hy_w3[i], hy_bias[i],
              w_att_out[i], w_hy_out[i], w_o[i], norm_mlp_g[i], w_up[i], w_down[i],
              norm_ple_g[i], w_ple_gate[i], w_ple_proj[i])
        y_prompt = encoder_layer(y_prompt, p_prompt[i], *lw)
        y_sample = encoder_layer(y_sample, p_sample[i], *lw)
    return (y_prompt, y_sample)
```

```python
import functools
import math

import jax
import jax.numpy as jnp
from jax import lax
from jax.experimental import pallas as pl
from jax.experimental.pallas import tpu as pltpu

F32 = jnp.float32
BF16 = jnp.bfloat16
HIGHEST = lax.Precision.HIGHEST

EPS = 1e-6
HEAD_DIM = 128
ROPE_DIM = HEAD_DIM // 4
ROPE_THETA = 500000.0
DILATED_PATTERNS = ((128, 1), (512, 4), (2048, 16))
ATT_RADIUS = 64
ATT_QBLK = 128
ATT_KWIN = ATT_QBLK + 2 * ATT_RADIUS
HY_ORDER = 2
HY_EMB_DIM = 33
HY_BANDS = (HY_EMB_DIM - 1) // 2
HY_FAST_DECAY = 0.3
HY_SLOW_DECAY = 1.5
HY_DECAY_TARGET = 1e-2
HY_MIN_DECAY = math.log(HY_DECAY_TARGET) / HY_SLOW_DECAY
HY_MAX_DECAY = math.log(HY_DECAY_TARGET) / HY_FAST_DECAY
N_BRANCH = 2
LANES = 128
MASK_NEG = -1e30

VMEM_CAP_BYTES = 60 * 1024 * 1024


def _cparams(sem, est_bytes):
    limit = int(min(VMEM_CAP_BYTES, max(32 * 1024 * 1024, est_bytes)))
    return pltpu.CompilerParams(dimension_semantics=sem, vmem_limit_bytes=limit)


def _nbytes(shape, dtype):
    return math.prod(shape) * jnp.dtype(dtype).itemsize


def _rmsnorm_kernel(x_ref, g_ref, o_ref):
    x = x_ref[...]
    ms = jnp.mean(x * x, axis=-1, keepdims=True)
    o_ref[...] = (x * lax.rsqrt(ms + EPS) * g_ref[...]).astype(o_ref.dtype)


def _rmsnorm_cast(x2d, g, tm=512):
    t, d = x2d.shape
    tm = min(tm, t)
    return pl.pallas_call(
        _rmsnorm_kernel,
        grid=(t // tm,),
        in_specs=[pl.BlockSpec((tm, d), lambda i: (i, 0)),
                  pl.BlockSpec((1, d), lambda i: (0, 0))],
        out_specs=pl.BlockSpec((tm, d), lambda i: (i, 0)),
        out_shape=jax.ShapeDtypeStruct((t, d), BF16),
        compiler_params=_cparams(("parallel",), 4 * _nbytes((tm, d), F32)),
        name="rmsnorm_cast",
    )(x2d, g.reshape(1, d))


def _inproj_qk_kernel(h_ref, w_ref, g_ref, c_ref, s1_ref, s2_ref, o_ref, *, scale):
    y = jnp.dot(h_ref[...], w_ref[...], preferred_element_type=F32)
    gs = g_ref[...] * scale
    for hh in range(y.shape[1] // HEAD_DIM):
        sl = slice(hh * HEAD_DIM, (hh + 1) * HEAD_DIM)
        yh = y[:, sl]
        ms = jnp.mean(yh * yh, axis=-1, keepdims=True)
        t = yh * lax.rsqrt(ms + EPS) * gs
        out = (t * c_ref[...]
               + pltpu.roll(t, HEAD_DIM - ROPE_DIM // 2, 1) * s1_ref[...]
               + pltpu.roll(t, ROPE_DIM // 2, 1) * s2_ref[...])
        o_ref[:, sl] = out.astype(o_ref.dtype)


def _inproj_plain_kernel(h_ref, w_ref, o_ref):
    o_ref[...] = jnp.dot(h_ref[...], w_ref[...], preferred_element_type=F32).astype(o_ref.dtype)


def _inproj_conv_kernel(h_ref, w_ref, cw_ref, cb_ref, o_ref):
    y = jnp.dot(h_ref[...], w_ref[...], preferred_element_type=F32)
    s = y.shape[0]
    row = lax.broadcasted_iota(jnp.int32, y.shape, 0)
    prev = jnp.where(row == 0, 0.0, pltpu.roll(y, 1, 0))
    nxt = jnp.where(row == s - 1, 0.0, pltpu.roll(y, s - 1, 0))
    u = prev * cw_ref[0:1, :] + y * cw_ref[1:2, :] + nxt * cw_ref[2:3, :] + cb_ref[...]
    o_ref[...] = u.astype(o_ref.dtype)


def _inproj_gate_kernel(h_ref, w_ref, gb_ref, o_ref):
    y = jnp.dot(h_ref[...], w_ref[...], preferred_element_type=F32)
    o_ref[...] = jax.nn.sigmoid(y + gb_ref[...]).astype(o_ref.dtype)


def _inproj(h3, w, col0, width, body, extras, extra_specs, tn):
    b, s, d = h3.shape
    tn = min(tn, width)
    nj = width // tn
    jb0 = col0 // tn
    est = (2 * _nbytes((s, d), BF16) + 2 * _nbytes((d, tn), BF16) + 2 * _nbytes((s, tn), BF16)
           + 4 * _nbytes((s, tn), F32) + (8 << 20))
    return pl.pallas_call(
        body,
        grid=(b, nj),
        in_specs=[pl.BlockSpec((None, s, d), lambda i, j: (i, 0, 0)),
                  pl.BlockSpec((d, tn), lambda i, j: (0, jb0 + j))] + extra_specs,
        out_specs=pl.BlockSpec((None, s, tn), lambda i, j: (i, 0, j)),
        out_shape=jax.ShapeDtypeStruct((b, s, width), BF16),
        compiler_params=_cparams(("parallel", "arbitrary"), est),
        name="inproj_" + getattr(body, "func", body).__name__,
    )(h3, w, *extras)


def _rope_tables(s):
    half = ROPE_DIM // 2
    inv_freq = ROPE_THETA ** (-jnp.arange(half, dtype=F32) / half)
    ang = jnp.arange(s, dtype=F32)[:, None] * inv_freq[None, :]
    cos, sin = jnp.cos(ang), jnp.sin(ang)
    c = jnp.concatenate([cos, cos, jnp.ones((s, HEAD_DIM - ROPE_DIM), F32)], axis=1)
    s1 = jnp.concatenate([-sin, jnp.zeros((s, HEAD_DIM - half), F32)], axis=1)
    s2 = jnp.concatenate([jnp.zeros((s, half), F32), sin, jnp.zeros((s, HEAD_DIM - ROPE_DIM), F32)], axis=1)
    return c, s1, s2


def _band_bias(off, kw):
    col = lax.broadcasted_iota(jnp.int32, (ATT_QBLK, kw), 1)
    row = lax.broadcasted_iota(jnp.int32, (ATT_QBLK, kw), 0)
    return jnp.where(jnp.abs(col - row + off) <= ATT_RADIUS, 0.0, MASK_NEG).astype(F32)


def _band_blocks(seg_len):
    kw = min(ATT_KWIN, seg_len)
    for bi in range(seg_len // ATT_QBLK):
        q0 = bi * ATT_QBLK
        yield q0, min(max(q0 - ATT_RADIUS, 0), seg_len - kw), kw


def _band_attend(qb, kb, va, bias):
    s = lax.dot_general(qb, kb, (((1,), (1,)), ((), ())), preferred_element_type=F32) + bias
    m = jnp.max(s, axis=-1, keepdims=True)
    p = jnp.exp(s - m).astype(BF16)
    oa = jnp.dot(p, va, preferred_element_type=F32)
    return oa[:, :HEAD_DIM], oa[:, HEAD_DIM:], jnp.broadcast_to(m, (ATT_QBLK, HEAD_DIM))


def _softmax_merge(a, b):
    (o1, l1, m1), (o2, l2, m2) = a, b
    m = jnp.maximum(m1, m2)
    a1 = jnp.exp(m1 - m)
    a2 = jnp.exp(m2 - m)
    return a1 * o1 + a2 * o2, a1 * l1 + a2 * l2, m


def _attn_kernel(q_ref, k_ref, v_ref, o_ref,
                 qf0, kf0, vf0, qf1, kf1, vf1, qb1, kb1, qb2, kb2, va0, va1, va2,
                 ao, al, am, bo, bl, bm):
    s = q_ref.shape[0]
    c = HEAD_DIM
    n1 = s // 4
    n2 = s // 16
    ones = jnp.ones((s, c), BF16)
    biases = {}

    def bias(off, kw):
        if (off, kw) not in biases:
            biases[(off, kw)] = _band_bias(off, kw)
        return biases[(off, kw)]

    qf0[...] = q_ref[...].astype(F32)
    kf0[...] = k_ref[...].astype(F32)
    vf0[...] = v_ref[...].astype(F32)
    va0[:, :c] = v_ref[...]
    va0[:, c:] = ones
    for r in range(4):
        rows = slice(r * n1, (r + 1) * n1)
        for src, dstf, dstb in ((qf0, qf1, qb1), (kf0, kf1, kb1)):
            x = src[pl.ds(r, n1, stride=4), :]
            dstf[rows, :] = x
            dstb[rows, :] = x.astype(BF16)
        x = vf0[pl.ds(r, n1, stride=4), :]
        vf1[rows, :] = x
        va1[rows, :c] = x.astype(BF16)
    va1[:, c:] = ones
    for r4 in range(4):
        for rp in range(4):
            rows = slice((r4 * 4 + rp) * n2, (r4 * 4 + rp + 1) * n2)
            src_rows = pl.ds(r4 * n1 + rp, n2, stride=4)
            qb2[rows, :] = qf1[src_rows, :].astype(BF16)
            kb2[rows, :] = kf1[src_rows, :].astype(BF16)
            va2[rows, :c] = vf1[src_rows, :].astype(BF16)
    va2[:, c:] = ones

    for r4 in range(4):
        for rp in range(4):
            base = (r4 * 4 + rp) * n2
            for q0, w0, kw in _band_blocks(n2):
                o, l, m = _band_attend(qb2[base + q0:base + q0 + ATT_QBLK, :],
                                       kb2[base + w0:base + w0 + kw, :],
                                       va2[base + w0:base + w0 + kw, :], bias(w0 - q0, kw))
                dst = pl.ds(r4 * n1 + 4 * q0 + rp, ATT_QBLK, stride=4)
                ao[dst, :] = o
                al[dst, :] = l
                am[dst, :] = m
    for r4 in range(4):
        base = r4 * n1
        for q0, w0, kw in _band_blocks(n1):
            cur = _band_attend(qb1[base + q0:base + q0 + ATT_QBLK, :],
                               kb1[base + w0:base + w0 + kw, :],
                               va1[base + w0:base + w0 + kw, :], bias(w0 - q0, kw))
            rows = slice(base + q0, base + q0 + ATT_QBLK)
            o, l, m = _softmax_merge(cur, (ao[rows, :], al[rows, :], am[rows, :]))
            dst = pl.ds(4 * q0 + r4, ATT_QBLK, stride=4)
            bo[dst, :] = o
            bl[dst, :] = l
            bm[dst, :] = m
    for q0, w0, kw in _band_blocks(s):
        cur = _band_attend(q_ref[q0:q0 + ATT_QBLK, :], k_ref[w0:w0 + kw, :], va0[w0:w0 + kw, :],
                           bias(w0 - q0, kw))
        rows = slice(q0, q0 + ATT_QBLK)
        o, l, _ = _softmax_merge(cur, (bo[rows, :], bl[rows, :], bm[rows, :]))
        o_ref[rows, :] = (o / l).astype(o_ref.dtype)


def _dilated_attention(q, k, v):
    b, s, aw = q.shape
    assert tuple(w // (2 * d) for w, d in DILATED_PATTERNS) == (ATT_RADIUS,) * 3
    assert tuple(d for _, d in DILATED_PATTERNS) == (1, 4, 16)
    assert s % (16 * ATT_QBLK) == 0
    c = HEAD_DIM
    spec = pl.BlockSpec((None, s, c), lambda i, h: (i, 0, h))
    f32buf = pltpu.VMEM((s, c), F32)
    bf16buf = pltpu.VMEM((s, c), BF16)
    augbuf = pltpu.VMEM((s, 2 * c), BF16)
    scratch = [f32buf] * 6 + [bf16buf] * 4 + [augbuf] * 3 + [f32buf] * 6
    est = 12 * _nbytes((s, c), F32) + 10 * _nbytes((s, c), BF16) + 8 * _nbytes((s, c), BF16) + (16 << 20)
    return pl.pallas_call(
        _attn_kernel,
        grid=(b, aw // c),
        in_specs=[spec, spec, spec],
        out_specs=spec,
        out_shape=jax.ShapeDtypeStruct((b, s, aw), BF16),
        scratch_shapes=scratch,
        compiler_params=_cparams(("parallel", "arbitrary"), est),
        name="dilated_attention",
    )(q, k, v)


def _filter_kernel(z_ref, w1_ref, b1_ref, w2_ref, b2_ref, fr_ref, w3_ref, t_ref, dl_ref,
                   filt_ref, asum_ref, *, blocks_per_dir):
    fr = fr_ref[...]
    hdn = jnp.sin(fr * (jnp.dot(z_ref[...], w1_ref[...], precision=HIGHEST,
                                preferred_element_type=F32) + b1_ref[...]))
    hdn = jnp.sin(fr * (jnp.dot(hdn, w2_ref[...], precision=HIGHEST,
                                preferred_element_type=F32) + b2_ref[...]))
    hf = jnp.dot(hdn, w3_ref[...], precision=HIGHEST, preferred_element_type=F32)
    hf = hf * jnp.exp(-t_ref[...] * dl_ref[...])
    is_bwd = (pl.program_id(0) // blocks_per_dir) % 2 == 1
    row = lax.broadcasted_iota(jnp.int32, hf.shape, 0)
    hf = jnp.where(jnp.logical_and(row == 0, is_bwd), 0.0, hf)
    filt_ref[...] = hf.astype(filt_ref.dtype)
    asum_ref[...] = jnp.sum(jnp.abs(hf), axis=0, keepdims=True)


def _hyena_filter_taps(seq_len, w1, b1, w2, b2, sin_freq, w3, hy_w, tn=512):
    ffn = w1.shape[1]
    pad = LANES
    t = jnp.linspace(0.0, 1.0, seq_len, dtype=F32)[:, None]
    wpos = 2.0 * math.pi * jnp.arange(seq_len, dtype=F32) / seq_len
    bands = jnp.linspace(1e-4, HY_BANDS - 1, HY_BANDS, dtype=F32)
    ang = wpos[:, None] * bands[None, :]
    z = jnp.concatenate([t, jnp.cos(ang), -jnp.sin(ang)], axis=-1)
    z = jnp.pad(z, ((0, 0), (0, pad - z.shape[1])))
    w1p = jnp.pad(w1.astype(F32), ((0, pad - w1.shape[0]), (0, pad - ffn)))
    w2p = jnp.pad(w2.astype(F32), ((0, pad - ffn), (0, pad - ffn)))
    w3p = jnp.pad(w3.astype(F32), ((0, pad - ffn), (0, 0)))
    padv = lambda a: jnp.pad(a.astype(F32), (0, pad - ffn)).reshape(1, pad)
    deltas = jnp.abs(jnp.linspace(HY_MIN_DECAY, HY_MAX_DECAY, hy_w, dtype=F32)).reshape(1, hy_w)
    ncol = w3.shape[1]
    tn = min(tn, hy_w)
    bpd = hy_w // tn
    full = lambda shape: pl.BlockSpec(shape, lambda j: (0, 0))
    return pl.pallas_call(
        functools.partial(_filter_kernel, blocks_per_dir=bpd),
        grid=(ncol // tn,),
        in_specs=[full((seq_len, pad)), full((pad, pad)), full((1, pad)), full((pad, pad)), full((1, pad)),
                  full((1, pad)), pl.BlockSpec((pad, tn), lambda j: (0, j)), full((seq_len, 1)),
                  pl.BlockSpec((1, tn), lambda j: (0, j % bpd))],
        out_specs=[pl.BlockSpec((seq_len, tn), lambda j: (0, j)), pl.BlockSpec((1, tn), lambda j: (0, j))],
        out_shape=[jax.ShapeDtypeStruct((seq_len, ncol), BF16), jax.ShapeDtypeStruct((1, ncol), F32)],
        compiler_params=_cparams(("arbitrary",), 32 << 20),
        name="hyena_filter_taps",
    )(z, w1p, padv(b1), w2p, padv(b2), padv(sin_freq), w3p, t, deltas)


def _filter_combine_kernel(af_ref, ab_ref, nf_ref, nb_ref, o_ref):
    l2 = af_ref.shape[0]
    row = lax.broadcasted_iota(jnp.int32, af_ref.shape, 0)
    sgn = jnp.where(row <= l2 // 2, 1.0, -1.0)
    inv = 1.0 / (nf_ref[...] + nb_ref[...])
    o_ref[...] = (af_ref[...] + sgn * ab_ref[...]) * inv


def _filter_combine(a, asum, hy_w, tc=256):
    l2 = a.shape[0]
    tc = min(tc, hy_w)
    nb = hy_w // tc
    col = lambda d: (lambda o, c: (0, (2 * o + d) * nb + c))
    return pl.pallas_call(
        _filter_combine_kernel,
        grid=(HY_ORDER, nb),
        in_specs=[pl.BlockSpec((l2, tc), col(0)), pl.BlockSpec((l2, tc), col(1)),
                  pl.BlockSpec((1, tc), col(0)), pl.BlockSpec((1, tc), col(1))],
        out_specs=pl.BlockSpec((None, l2, tc), lambda o, c: (o, 0, c)),
        out_shape=jax.ShapeDtypeStruct((HY_ORDER, l2, hy_w), F32),
        compiler_params=_cparams(("parallel", "arbitrary"), 8 * _nbytes((l2, tc), F32)),
        name="hyena_filter_combine",
    )(a, a, asum, asum)


def _dft_matrices(s):
    n = 2 * s
    k = jnp.arange(s, dtype=jnp.int32)
    ang = ((k[:, None] * k[None, :]) % n).astype(F32) * (2.0 * math.pi / n)
    alt = jnp.where(k % 2 == 0, 1.0, -1.0).astype(F32)
    im = (-jnp.sin(ang)).at[0].set(alt)
    fwd = jnp.concatenate([jnp.cos(ang), im], axis=0)
    scale = jnp.full((n,), 2.0 / n, F32).at[0].set(1.0 / n).at[s].set(1.0 / n)
    inv = (fwd * scale[:, None]).T
    return fwd.astype(BF16), inv.astype(BF16)


def _dft_fwd_plain_kernel(z_ref, f_ref, o_ref):
    o_ref[...] = jnp.dot(f_ref[...], z_ref[...], preferred_element_type=F32)


def _dft_fwd_filter_kernel(z_ref, f_ref, h_ref, o_ref):
    zf = jnp.dot(f_ref[...], z_ref[...], preferred_element_type=F32)
    s = zf.shape[0] // 2
    zr, zi = zf[:s], zf[s:]
    hr, hi = h_ref[:s, :], h_ref[s:, :]
    first = lax.broadcasted_iota(jnp.int32, zr.shape, 0) == 0
    ii = hi * zi
    o_ref[:s, :] = (hr * zr - jnp.where(first, 0.0, ii)).astype(o_ref.dtype)
    o_ref[s:, :] = jnp.where(first, ii, hr * zi + hi * zr).astype(o_ref.dtype)


def _dft_forward(z, col0, width, fwd, h=None, order=0, tc=256):
    b, s, _ = z.shape
    tc = min(tc, width)
    nc = width // tc
    cb0 = col0 // tc
    z_spec = pl.BlockSpec((None, s, tc), lambda c, i: (i, 0, cb0 + c))
    f_spec = pl.BlockSpec((2 * s, s), lambda c, i: (0, 0), pipeline_mode=pl.Buffered(1))
    o_spec = pl.BlockSpec((None, 2 * s, tc), lambda c, i: (i, 0, c))
    est = (_nbytes((2 * s, s), BF16) + 2 * _nbytes((s, tc), BF16) + 4 * _nbytes((2 * s, tc), F32)
           + 4 * _nbytes((2 * s, tc), F32) + (8 << 20))
    if h is None:
        body, specs, args, odt = _dft_fwd_plain_kernel, [z_spec, f_spec], (z, fwd), F32
    else:
        h_spec = pl.BlockSpec((None, 2 * s, tc), lambda c, i: (order, 0, c))
        body, specs, args, odt = _dft_fwd_filter_kernel, [z_spec, f_spec, h_spec], (z, fwd, h), BF16
    return pl.pallas_call(
        body,
        grid=(nc, b),
        in_specs=specs,
        out_specs=o_spec,
        out_shape=jax.ShapeDtypeStruct((b, 2 * s, width), odt),
        compiler_params=_cparams(("parallel", "arbitrary"), est),
        name=body.__name__.strip("_"),
    )(*args)


def _dft_inv_kernel(y_ref, f_ref, z_ref, g_ref, b_ref, o_ref):
    conv = jnp.dot(f_ref[...], y_ref[...], preferred_element_type=F32)
    z = z_ref[...].astype(F32)
    o_ref[...] = (g_ref[...].astype(F32) * (conv + b_ref[...] * z)).astype(o_ref.dtype)


def _dft_inverse_gated(y, inv, z, z_col0, gate, gate_col0, bias_row, width, tc=256):
    b, s2, _ = y.shape
    s = s2 // 2
    tc = min(tc, width)
    nc = width // tc
    zb0, gb0 = z_col0 // tc, gate_col0 // tc
    est = (_nbytes((s, s2), BF16) + 2 * _nbytes((s2, tc), BF16) + 6 * _nbytes((s, tc), BF16)
           + 4 * _nbytes((s, tc), F32) + (8 << 20))
    return pl.pallas_call(
        _dft_inv_kernel,
        grid=(nc, b),
        in_specs=[pl.BlockSpec((None, s2, tc), lambda c, i: (i, 0, c)),
                  pl.BlockSpec((s, s2), lambda c, i: (0, 0), pipeline_mode=pl.Buffered(1)),
                  pl.BlockSpec((None, s, tc), lambda c, i: (i, 0, zb0 + c)),
                  pl.BlockSpec((None, s, tc), lambda c, i: (i, 0, gb0 + c)),
                  pl.BlockSpec((1, tc), lambda c, i: (0, c))],
        out_specs=pl.BlockSpec((None, s, tc), lambda c, i: (i, 0, c)),
        out_shape=jax.ShapeDtypeStruct((b, s, width), BF16),
        compiler_params=_cparams(("parallel", "arbitrary"), est),
        name="dft_inverse_gated",
    )(y, inv, z, gate, bias_row)


def _merge_kernel(a_ref, hy_ref, wa_ref, wh_ref, g0_ref, g1_ref, o_ref):
    ya = jnp.dot(a_ref[...], wa_ref[...], preferred_element_type=F32)
    yh = jnp.dot(hy_ref[...], wh_ref[...], preferred_element_type=F32)
    o_ref[...] = (g0_ref[...].astype(F32) * ya + g1_ref[...].astype(F32) * yh).astype(o_ref.dtype)


def _gated_merge(attn, hy, w_att, w_hy, gates, tm=1024, tn=512):
    t, k = attn.shape
    n = w_att.shape[1]
    tm, tn = min(tm, t), min(tn, n)
    g1b0 = n // tn
    est = 4 * _nbytes((tm, k), BF16) + 4 * _nbytes((k, tn), BF16) + 6 * _nbytes((tm, tn), F32) + (8 << 20)
    return pl.pallas_call(
        _merge_kernel,
        grid=(t // tm, n // tn),
        in_specs=[pl.BlockSpec((tm, k), lambda i, j: (i, 0)),
                  pl.BlockSpec((tm, k), lambda i, j: (i, 0)),
                  pl.BlockSpec((k, tn), lambda i, j: (0, j)),
                  pl.BlockSpec((k, tn), lambda i, j: (0, j)),
                  pl.BlockSpec((tm, tn), lambda i, j: (i, j)),
                  pl.BlockSpec((tm, tn), lambda i, j: (i, g1b0 + j))],
        out_specs=pl.BlockSpec((tm, tn), lambda i, j: (i, j)),
        out_shape=jax.ShapeDtypeStruct((t, n), BF16),
        compiler_params=_cparams(("parallel", "arbitrary"), est),
        name="gated_merge",
    )(attn, hy, w_att, w_hy, gates, gates)


def _mm_residual_kernel(a_ref, w_ref, r_ref, o_ref):
    o_ref[...] = r_ref[...] + jnp.dot(a_ref[...], w_ref[...], preferred_element_type=F32)


def _matmul_residual(a, w, res, tm=1024, tn=512):
    t, k = a.shape
    n = w.shape[1]
    tm, tn = min(tm, t), min(tn, n)
    est = 2 * _nbytes((tm, k), BF16) + 2 * _nbytes((k, tn), BF16) + 6 * _nbytes((tm, tn), F32) + (8 << 20)
    return pl.pallas_call(
        _mm_residual_kernel,
        grid=(t // tm, n // tn),
        in_specs=[pl.BlockSpec((tm, k), lambda i, j: (i, 0)),
                  pl.BlockSpec((k, tn), lambda i, j: (0, j)),
                  pl.BlockSpec((tm, tn), lambda i, j: (i, j))],
        out_specs=pl.BlockSpec((tm, tn), lambda i, j: (i, j)),
        out_shape=jax.ShapeDtypeStruct((t, n), F32),
        compiler_params=_cparams(("parallel", "arbitrary"), est),
        name="matmul_residual",
    )(a, w, res)


def _mlp_kernel(x_ref, g_ref, wu_ref, wd_ref, o_ref, hm_ref):
    @pl.when(pl.program_id(1) == 0)
    def _():
        x = x_ref[...]
        ms = jnp.mean(x * x, axis=-1, keepdims=True)
        hm_ref[...] = (x * lax.rsqrt(ms + EPS) * g_ref[...]).astype(hm_ref.dtype)
        o_ref[...] = x

    a = jnp.dot(hm_ref[...], wu_ref[...], preferred_element_type=F32)
    a = jnp.square(jnp.maximum(a, 0.0)).astype(BF16)
    o_ref[...] += jnp.dot(a, wd_ref[...], preferred_element_type=F32)


def _relu2_mlp(x, g, w_up, w_down, tm=512, tf=512):
    t, d = x.shape
    ff = w_up.shape[1]
    tm, tf = min(tm, t), min(tf, ff)
    est = (4 * _nbytes((tm, d), F32) + _nbytes((tm, d), BF16) + 4 * _nbytes((d, tf), BF16)
           + 3 * _nbytes((tm, tf), F32) + (8 << 20))
    return pl.pallas_call(
        _mlp_kernel,
        grid=(t // tm, ff // tf),
        in_specs=[pl.BlockSpec((tm, d), lambda i, j: (i, 0)),
                  pl.BlockSpec((1, d), lambda i, j: (0, 0)),
                  pl.BlockSpec((d, tf), lambda i, j: (0, j)),
                  pl.BlockSpec((tf, d), lambda i, j: (j, 0))],
        out_specs=pl.BlockSpec((tm, d), lambda i, j: (i, 0)),
        out_shape=jax.ShapeDtypeStruct((t, d), F32),
        scratch_shapes=[pltpu.VMEM((tm, d), BF16)],
        compiler_params=_cparams(("parallel", "arbitrary"), est),
        name="relu2_mlp",
    )(x, g.reshape(1, d), w_up, w_down)


def _ple_kernel(x_ref, xt_ref, g_ref, wg_ref, p_ref, wp_ref, o_ref, hn_ref):
    @pl.when(pl.program_id(1) == 0)
    def _():
        x = x_ref[...]
        ms = jnp.mean(x * x, axis=-1, keepdims=True)
        hn_ref[...] = (x * lax.rsqrt(ms + EPS) * g_ref[...]).astype(hn_ref.dtype)

    gate = jax.nn.sigmoid(jnp.dot(hn_ref[...], wg_ref[...], preferred_element_type=F32))
    proj = jnp.dot(p_ref[...].astype(BF16), wp_ref[...], preferred_element_type=F32)
    o_ref[...] = xt_ref[...] + gate * proj


def _gated_ple(x, row0, p, g, w_gate, w_proj, tm=512, tn=512):
    t, pd = p.shape
    d = x.shape[1]
    tm, tn = min(tm, t), min(tn, d)
    rb0 = row0 // tm
    est = (2 * _nbytes((tm, d), F32) + _nbytes((tm, d), BF16) + 2 * _nbytes((d, tn), BF16)
           + 8 * _nbytes((tm, tn), F32) + (8 << 20))
    return pl.pallas_call(
        _ple_kernel,
        grid=(t // tm, d // tn),
        in_specs=[pl.BlockSpec((tm, d), lambda i, j: (rb0 + i, 0)),
                  pl.BlockSpec((tm, tn), lambda i, j: (rb0 + i, j)),
                  pl.BlockSpec((1, d), lambda i, j: (0, 0)),
                  pl.BlockSpec((d, tn), lambda i, j: (0, j)),
                  pl.BlockSpec((tm, pd), lambda i, j: (i, 0)),
                  pl.BlockSpec((pd, tn), lambda i, j: (0, j))],
        out_specs=pl.BlockSpec((tm, tn), lambda i, j: (i, j)),
        out_shape=jax.ShapeDtypeStruct((t, d), F32),
        scratch_shapes=[pltpu.VMEM((tm, d), BF16)],
        compiler_params=_cparams(("parallel", "arbitrary"), est),
        name="gated_ple",
    )(x, x, g.reshape(1, d), w_gate, p, w_proj)


def _encoder_layer(x3, p_list, norm_mix_g, w_in, gate_b, q_norm_g, k_norm_g, hy_conv_w, hy_conv_b,
                   hy_w1, hy_b1, hy_w2, hy_b2, hy_sin_freq, hy_w3, hy_bias,
                   w_att_out, w_hy_out, w_o, norm_mlp_g, w_up, w_down,
                   norm_ple_g, w_ple_gate, w_ple_proj):
    b, s, d = x3.shape
    att_w = w_att_out.shape[0]
    hy_w = w_hy_out.shape[0]
    x2 = x3.reshape(b * s, d)
    bf = lambda a: a.astype(BF16)

    h3 = _rmsnorm_cast(x2, norm_mix_g).reshape(b, s, d)
    w_in_b = bf(w_in)
    rope_c, rope_s1, rope_s2 = _rope_tables(s)
    tab_spec = pl.BlockSpec((s, HEAD_DIM), lambda i, j: (0, 0))
    vec_spec = lambda tn, blk0=0: pl.BlockSpec((1, tn), lambda i, j: (0, blk0 + j))
    head_spec = pl.BlockSpec((1, HEAD_DIM), lambda i, j: (0, 0))
    tn = 512
    qk = lambda g, col0, scale: _inproj(
        h3, w_in_b, col0, att_w, functools.partial(_inproj_qk_kernel, scale=scale),
        (g.reshape(1, HEAD_DIM), rope_c, rope_s1, rope_s2), [head_spec, tab_spec, tab_spec, tab_spec], tn)
    q = qk(q_norm_g, 0, HEAD_DIM ** -0.5)
    k = qk(k_norm_g, att_w, 1.0)
    v = _inproj(h3, w_in_b, 2 * att_w, att_w, _inproj_plain_kernel, (), [], tn)
    tnc = min(tn, hy_w)
    u = _inproj(h3, w_in_b, 3 * att_w, 3 * hy_w, _inproj_conv_kernel,
                (hy_conv_w, hy_conv_b.reshape(1, 3 * hy_w)),
                [pl.BlockSpec((3, tnc), lambda i, j: (0, j)), vec_spec(tnc)], tnc)
    tng = min(tn, d)
    gates = _inproj(h3, w_in_b, 3 * att_w + 3 * hy_w, N_BRANCH * d, _inproj_gate_kernel,
                    (gate_b.reshape(1, N_BRANCH * d),), [vec_spec(tng)], tng)

    attn = _dilated_attention(q, k, v)

    fwd, inv = _dft_matrices(s)
    taps, tap_asum = _hyena_filter_taps(s, hy_w1, hy_b1, hy_w2, hy_b2, hy_sin_freq, hy_w3, hy_w)
    tap_spec = _dft_forward(taps[None], 0, taps.shape[1], fwd)[0]
    filt = _filter_combine(tap_spec, tap_asum, hy_w)
    bias = hy_bias.astype(F32)
    y1 = _dft_forward(u, 0, hy_w, fwd, filt, 0)
    z = _dft_inverse_gated(y1, inv, u, 0, u, hy_w, bias[0:1], hy_w)
    y2 = _dft_forward(z, 0, hy_w, fwd, filt, 1)
    hy = _dft_inverse_gated(y2, inv, z, 0, u, 2 * hy_w, bias[1:2], hy_w)

    merged = _gated_merge(attn.reshape(b * s, att_w), hy.reshape(b * s, hy_w),
                          bf(w_att_out), bf(w_hy_out), gates.reshape(b * s, N_BRANCH * d))
    x2 = _matmul_residual(merged, bf(w_o), x2)
    x2 = _relu2_mlp(x2, norm_mlp_g, bf(w_up), bf(w_down))

    outs, row0 = [], 0
    w_pg, w_pp = bf(w_ple_gate), bf(w_ple_proj)
    for p in p_list:
        pb = p.shape[0]
        p2 = p.reshape(pb * s, p.shape[-1])
        outs.append(_gated_ple(x2, row0, p2, norm_ple_g, w_pg, w_pp).reshape(pb, s, d))
        row0 += pb * s
    return outs


def kernel(x_prompt, x_sample, p_prompt, p_sample, norm_mix_g, w_in, gate_b, q_norm_g, k_norm_g, hy_conv_w, hy_conv_b, hy_w1, hy_b1, hy_w2, hy_b2, hy_sin_freq, hy_w3, hy_bias, w_att_out, w_hy_out, w_o, norm_mlp_g, w_up, w_down, norm_ple_g, w_ple_gate, w_ple_proj):
    assert x_prompt.shape[1:] == x_sample.shape[1:]
    y_prompt, y_sample = x_prompt, x_sample
    for i in range(w_in.shape[0]):
        x3 = jnp.concatenate([y_prompt, y_sample], axis=0)
        y_prompt, y_sample = _encoder_layer(
            x3, (p_prompt[i], p_sample[i]), norm_mix_g[i], w_in[i], gate_b[i], q_norm_g[i], k_norm_g[i],
            hy_conv_w[i], hy_conv_b[i], hy_w1[i], hy_b1[i], hy_w2[i], hy_b2[i], hy_sin_freq[i], hy_w3[i],
            hy_bias[i], w_att_out[i], w_hy_out[i], w_o[i], norm_mlp_g[i], w_up[i], w_down[i],
            norm_ple_g[i], w_ple_gate[i], w_ple_proj[i])
    return (y_prompt, y_sample)
```

```python
import functools
import math

import jax
import jax.numpy as jnp
from jax import lax
from jax.experimental import pallas as pl
from jax.experimental.pallas import tpu as pltpu

F32 = jnp.float32
BF16 = jnp.bfloat16
HIGHEST = lax.Precision.HIGHEST

EPS = 1e-6
HEAD_DIM = 128
ROPE_DIM = HEAD_DIM // 4
ROPE_THETA = 500000.0
DILATED_PATTERNS = ((128, 1), (512, 4), (2048, 16))
ATT_RADIUS = 64
ATT_QBLK = 128
ATT_KWIN = ATT_QBLK + 2 * ATT_RADIUS
HY_ORDER = 2
HY_EMB_DIM = 33
HY_BANDS = (HY_EMB_DIM - 1) // 2
HY_FAST_DECAY = 0.3
HY_SLOW_DECAY = 1.5
HY_DECAY_TARGET = 1e-2
HY_MIN_DECAY = math.log(HY_DECAY_TARGET) / HY_SLOW_DECAY
HY_MAX_DECAY = math.log(HY_DECAY_TARGET) / HY_FAST_DECAY
N_BRANCH = 2
LANES = 128
MASK_NEG = -1e30

VMEM_CAP_BYTES = 60 * 1024 * 1024


def _cparams(sem, est_bytes):
    limit = int(min(VMEM_CAP_BYTES, max(32 * 1024 * 1024, est_bytes)))
    return pltpu.CompilerParams(dimension_semantics=sem, vmem_limit_bytes=limit)


def _nbytes(shape, dtype):
    return math.prod(shape) * jnp.dtype(dtype).itemsize


def _rmsnorm_kernel(x_ref, g_ref, o_ref):
    x = x_ref[...]
    ms = jnp.mean(x * x, axis=-1, keepdims=True)
    o_ref[...] = (x * lax.rsqrt(ms + EPS) * g_ref[...]).astype(o_ref.dtype)


def _rmsnorm_cast(x2d, g, tm=512):
    t, d = x2d.shape
    tm = min(tm, t)
    return pl.pallas_call(
        _rmsnorm_kernel,
        grid=(t // tm,),
        in_specs=[pl.BlockSpec((tm, d), lambda i: (i, 0)),
                  pl.BlockSpec((1, d), lambda i: (0, 0))],
        out_specs=pl.BlockSpec((tm, d), lambda i: (i, 0)),
        out_shape=jax.ShapeDtypeStruct((t, d), BF16),
        compiler_params=_cparams(("parallel",), 4 * _nbytes((tm, d), F32)),
        name="rmsnorm_cast",
    )(x2d, g.reshape(1, d))


def _inproj_qk_kernel(h_ref, w_ref, g_ref, c_ref, s1_ref, s2_ref, o_ref, *, scale):
    y = jnp.dot(h_ref[...], w_ref[...], preferred_element_type=F32)
    gs = g_ref[...] * scale
    pw = 2 * HEAD_DIM
    same_head = (lax.broadcasted_iota(jnp.int32, (pw, pw), 0) // HEAD_DIM
                 == lax.broadcasted_iota(jnp.int32, (pw, pw), 1) // HEAD_DIM)
    ones_bd = jnp.where(same_head, 1.0, 0.0).astype(BF16)
    yy = (y * y).astype(BF16)
    for hh in range(y.shape[1] // HEAD_DIM):
        sl = slice(hh * HEAD_DIM, (hh + 1) * HEAD_DIM)
        yh = y[:, sl]
        if hh % 2 == 0:
            ss_pair = jnp.dot(yy[:, hh * HEAD_DIM:hh * HEAD_DIM + pw], ones_bd, preferred_element_type=F32)
        ms = ss_pair[:, (hh % 2) * HEAD_DIM:(hh % 2 + 1) * HEAD_DIM] * (1.0 / HEAD_DIM)
        t = yh * lax.rsqrt(ms + EPS) * gs
        out = (t * c_ref[...]
               + pltpu.roll(t, HEAD_DIM - ROPE_DIM // 2, 1) * s1_ref[...]
               + pltpu.roll(t, ROPE_DIM // 2, 1) * s2_ref[...])
        o_ref[:, sl] = out.astype(o_ref.dtype)


def _inproj_plain_kernel(h_ref, w_ref, o_ref):
    o_ref[...] = jnp.dot(h_ref[...], w_ref[...], preferred_element_type=F32).astype(o_ref.dtype)


def _inproj_conv_kernel(h_ref, w_ref, cw_ref, cb_ref, o_ref):
    y = jnp.dot(h_ref[...], w_ref[...], preferred_element_type=F32)
    s = y.shape[0]
    w0, w1, w2, cb = cw_ref[0:1, :], cw_ref[1:2, :], cw_ref[2:3, :], cb_ref[...]
    u = pltpu.roll(y, 1, 0) * w0 + y * w1 + pltpu.roll(y, s - 1, 0) * w2 + cb
    o_ref[...] = u.astype(o_ref.dtype)
    edge = 16
    row = lax.broadcasted_iota(jnp.int32, (edge, y.shape[1]), 0)
    first = y[0:1, :] * w1 + y[1:2, :] * w2 + cb
    last = y[s - 2:s - 1, :] * w0 + y[s - 1:s, :] * w1 + cb
    o_ref[0:edge, :] = jnp.where(row == 0, first, u[0:edge, :]).astype(o_ref.dtype)
    o_ref[s - edge:s, :] = jnp.where(row == edge - 1, last, u[s - edge:s, :]).astype(o_ref.dtype)


def _inproj_gate_kernel(h_ref, w_ref, gb_ref, o_ref):
    y = jnp.dot(h_ref[...], w_ref[...], preferred_element_type=F32)
    o_ref[...] = jax.nn.sigmoid(y + gb_ref[...]).astype(o_ref.dtype)


def _inproj(h3, w, col0, width, body, extras, extra_specs, tn):
    b, s, d = h3.shape
    tn = min(tn, width)
    nj = width // tn
    jb0 = col0 // tn
    est = (2 * _nbytes((s, d), BF16) + 2 * _nbytes((d, tn), BF16) + 2 * _nbytes((s, tn), BF16)
           + 4 * _nbytes((s, tn), F32) + (8 << 20))
    return pl.pallas_call(
        body,
        grid=(b, nj),
        in_specs=[pl.BlockSpec((None, s, d), lambda i, j: (i, 0, 0)),
                  pl.BlockSpec((d, tn), lambda i, j: (0, jb0 + j))] + extra_specs,
        out_specs=pl.BlockSpec((None, s, tn), lambda i, j: (i, 0, j)),
        out_shape=jax.ShapeDtypeStruct((b, s, width), BF16),
        compiler_params=_cparams(("parallel", "arbitrary"), est),
        name="inproj_" + getattr(body, "func", body).__name__,
    )(h3, w, *extras)


def _rope_tables(s):
    half = ROPE_DIM // 2
    inv_freq = ROPE_THETA ** (-jnp.arange(half, dtype=F32) / half)
    ang = jnp.arange(s, dtype=F32)[:, None] * inv_freq[None, :]
    cos, sin = jnp.cos(ang), jnp.sin(ang)
    c = jnp.concatenate([cos, cos, jnp.ones((s, HEAD_DIM - ROPE_DIM), F32)], axis=1)
    s1 = jnp.concatenate([-sin, jnp.zeros((s, HEAD_DIM - half), F32)], axis=1)
    s2 = jnp.concatenate([jnp.zeros((s, half), F32), sin, jnp.zeros((s, HEAD_DIM - ROPE_DIM), F32)], axis=1)
    return c, s1, s2


def _band_bias(off, kw):
    col = lax.broadcasted_iota(jnp.int32, (ATT_QBLK, kw), 1)
    row = lax.broadcasted_iota(jnp.int32, (ATT_QBLK, kw), 0)
    return jnp.where(jnp.abs(col - row + off) <= ATT_RADIUS, 0.0, MASK_NEG).astype(F32)


def _band_blocks(seg_len):
    kw = min(ATT_KWIN, seg_len)
    for bi in range(seg_len // ATT_QBLK):
        q0 = bi * ATT_QBLK
        yield q0, min(max(q0 - ATT_RADIUS, 0), seg_len - kw), kw


def _band_attend(qb, kb, va, bias):
    s = lax.dot_general(qb, kb, (((1,), (1,)), ((), ())), preferred_element_type=F32) + bias
    m = jnp.max(s, axis=-1, keepdims=True)
    p = jnp.exp(s - m).astype(BF16)
    oa = jnp.dot(p, va, preferred_element_type=F32)
    return oa[:, :HEAD_DIM], oa[:, HEAD_DIM:], jnp.broadcast_to(m, (ATT_QBLK, HEAD_DIM))


def _softmax_merge(a, b):
    (o1, l1, m1), (o2, l2, m2) = a, b
    m = jnp.maximum(m1, m2)
    a1 = jnp.exp(m1 - m)
    a2 = jnp.exp(m2 - m)
    return a1 * o1 + a2 * o2, a1 * l1 + a2 * l2, m


def _attn_kernel(q_ref, k_ref, v_ref, o_ref,
                 qf0, kf0, vf0, qf1, kf1, vf1, qb1, kb1, qb2, kb2, va0, va1, va2,
                 ao, al, am, bo, bl, bm):
    s = q_ref.shape[0]
    c = HEAD_DIM
    n1 = s // 4
    n2 = s // 16
    ones = jnp.ones((s, c), BF16)
    biases = {}

    def bias(off, kw):
        if (off, kw) not in biases:
            biases[(off, kw)] = _band_bias(off, kw)
        return biases[(off, kw)]

    qf0[...] = q_ref[...].astype(F32)
    kf0[...] = k_ref[...].astype(F32)
    vf0[...] = v_ref[...].astype(F32)
    va0[:, :c] = v_ref[...]
    va0[:, c:] = ones
    for r in range(4):
        rows = slice(r * n1, (r + 1) * n1)
        for src, dstf, dstb in ((qf0, qf1, qb1), (kf0, kf1, kb1)):
            x = src[pl.ds(r, n1, stride=4), :]
            dstf[rows, :] = x
            dstb[rows, :] = x.astype(BF16)
        x = vf0[pl.ds(r, n1, stride=4), :]
        vf1[rows, :] = x
        va1[rows, :c] = x.astype(BF16)
    va1[:, c:] = ones
    for r4 in range(4):
        for rp in range(4):
            rows = slice((r4 * 4 + rp) * n2, (r4 * 4 + rp + 1) * n2)
            src_rows = pl.ds(r4 * n1 + rp, n2, stride=4)
            qb2[rows, :] = qf1[src_rows, :].astype(BF16)
            kb2[rows, :] = kf1[src_rows, :].astype(BF16)
            va2[rows, :c] = vf1[src_rows, :].astype(BF16)
    va2[:, c:] = ones

    for r4 in range(4):
        for rp in range(4):
            base = (r4 * 4 + rp) * n2
            for q0, w0, kw in _band_blocks(n2):
                o, l, m = _band_attend(qb2[base + q0:base + q0 + ATT_QBLK, :],
                                       kb2[base + w0:base + w0 + kw, :],
                                       va2[base + w0:base + w0 + kw, :], bias(w0 - q0, kw))
                dst = pl.ds(r4 * n1 + 4 * q0 + rp, ATT_QBLK, stride=4)
                ao[dst, :] = o
                al[dst, :] = l
                am[dst, :] = m
    for r4 in range(4):
        base = r4 * n1
        for q0, w0, kw in _band_blocks(n1):
            cur = _band_attend(qb1[base + q0:base + q0 + ATT_QBLK, :],
                               kb1[base + w0:base + w0 + kw, :],
                               va1[base + w0:base + w0 + kw, :], bias(w0 - q0, kw))
            rows = slice(base + q0, base + q0 + ATT_QBLK)
            o, l, m = _softmax_merge(cur, (ao[rows, :], al[rows, :], am[rows, :]))
            dst = pl.ds(4 * q0 + r4, ATT_QBLK, stride=4)
            bo[dst, :] = o
            bl[dst, :] = l
            bm[dst, :] = m
    for q0, w0, kw in _band_blocks(s):
        cur = _band_attend(q_ref[q0:q0 + ATT_QBLK, :], k_ref[w0:w0 + kw, :], va0[w0:w0 + kw, :],
                           bias(w0 - q0, kw))
        rows = slice(q0, q0 + ATT_QBLK)
        o, l, _ = _softmax_merge(cur, (bo[rows, :], bl[rows, :], bm[rows, :]))
        o_ref[rows, :] = (o / l).astype(o_ref.dtype)


def _dilated_attention(q, k, v):
    b, s, aw = q.shape
    assert tuple(w // (2 * d) for w, d in DILATED_PATTERNS) == (ATT_RADIUS,) * 3
    assert tuple(d for _, d in DILATED_PATTERNS) == (1, 4, 16)
    assert s % (16 * ATT_QBLK) == 0
    c = HEAD_DIM
    spec = pl.BlockSpec((None, s, c), lambda i, h: (i, 0, h))
    f32buf = pltpu.VMEM((s, c), F32)
    bf16buf = pltpu.VMEM((s, c), BF16)
    augbuf = pltpu.VMEM((s, 2 * c), BF16)
    scratch = [f32buf] * 6 + [bf16buf] * 4 + [augbuf] * 3 + [f32buf] * 6
    est = 12 * _nbytes((s, c), F32) + 10 * _nbytes((s, c), BF16) + 8 * _nbytes((s, c), BF16) + (16 << 20)
    return pl.pallas_call(
        _attn_kernel,
        grid=(b, aw // c),
        in_specs=[spec, spec, spec],
        out_specs=spec,
        out_shape=jax.ShapeDtypeStruct((b, s, aw), BF16),
        scratch_shapes=scratch,
        compiler_params=_cparams(("parallel", "arbitrary"), est),
        name="dilated_attention",
    )(q, k, v)


def _filter_kernel(z_ref, w1_ref, b1_ref, w2_ref, b2_ref, fr_ref, w3_ref, t_ref, dl_ref,
                   filt_ref, asum_ref, hdn_ref, *, blocks_per_dir):
    @pl.when(pl.program_id(0) == 0)
    def _():
        fr = fr_ref[...]
        hdn = jnp.sin(fr * (jnp.dot(z_ref[...], w1_ref[...], precision=HIGHEST,
                                    preferred_element_type=F32) + b1_ref[...]))
        hdn_ref[...] = jnp.sin(fr * (jnp.dot(hdn, w2_ref[...], precision=HIGHEST,
                                             preferred_element_type=F32) + b2_ref[...]))

    hf = jnp.dot(hdn_ref[...], w3_ref[...], precision=HIGHEST, preferred_element_type=F32)
    hf = hf * jnp.exp(-t_ref[...] * dl_ref[...])
    is_bwd = (pl.program_id(0) // blocks_per_dir) % 2 == 1
    row = lax.broadcasted_iota(jnp.int32, hf.shape, 0)
    hf = jnp.where(jnp.logical_and(row == 0, is_bwd), 0.0, hf)
    filt_ref[...] = hf.astype(filt_ref.dtype)
    asum_ref[...] = jnp.sum(jnp.abs(hf), axis=0, keepdims=True)


def _hyena_filter_taps(seq_len, w1, b1, w2, b2, sin_freq, w3, hy_w, tn=512):
    ffn = w1.shape[1]
    pad = LANES
    t = jnp.linspace(0.0, 1.0, seq_len, dtype=F32)[:, None]
    wpos = 2.0 * math.pi * jnp.arange(seq_len, dtype=F32) / seq_len
    bands = jnp.linspace(1e-4, HY_BANDS - 1, HY_BANDS, dtype=F32)
    ang = wpos[:, None] * bands[None, :]
    z = jnp.concatenate([t, jnp.cos(ang), -jnp.sin(ang)], axis=-1)
    z = jnp.pad(z, ((0, 0), (0, pad - z.shape[1])))
    w1p = jnp.pad(w1.astype(F32), ((0, pad - w1.shape[0]), (0, pad - ffn)))
    w2p = jnp.pad(w2.astype(F32), ((0, pad - ffn), (0, pad - ffn)))
    w3p = jnp.pad(w3.astype(F32), ((0, pad - ffn), (0, 0)))
    padv = lambda a: jnp.pad(a.astype(F32), (0, pad - ffn)).reshape(1, pad)
    deltas = jnp.abs(jnp.linspace(HY_MIN_DECAY, HY_MAX_DECAY, hy_w, dtype=F32)).reshape(1, hy_w)
    ncol = w3.shape[1]
    tn = min(tn, hy_w)
    bpd = hy_w // tn
    full = lambda shape: pl.BlockSpec(shape, lambda j: (0, 0))
    return pl.pallas_call(
        functools.partial(_filter_kernel, blocks_per_dir=bpd),
        grid=(ncol // tn,),
        in_specs=[full((seq_len, pad)), full((pad, pad)), full((1, pad)), full((pad, pad)), full((1, pad)),
                  full((1, pad)), pl.BlockSpec((pad, tn), lambda j: (0, j)), full((seq_len, 1)),
                  pl.BlockSpec((1, tn), lambda j: (0, j % bpd))],
        out_specs=[pl.BlockSpec((seq_len, tn), lambda j: (0, j)), pl.BlockSpec((1, tn), lambda j: (0, j))],
        out_shape=[jax.ShapeDtypeStruct((seq_len, ncol), BF16), jax.ShapeDtypeStruct((1, ncol), F32)],
        scratch_shapes=[pltpu.VMEM((seq_len, pad), F32)],
        compiler_params=_cparams(("arbitrary",), 32 << 20),
        name="hyena_filter_taps",
    )(z, w1p, padv(b1), w2p, padv(b2), padv(sin_freq), w3p, t, deltas)


def _filter_combine_kernel(af_ref, ab_ref, nf_ref, nb_ref, o_ref):
    l2 = af_ref.shape[0]
    row = lax.broadcasted_iota(jnp.int32, af_ref.shape, 0)
    sgn = jnp.where(row <= l2 // 2, 1.0, -1.0)
    inv = 1.0 / (nf_ref[...] + nb_ref[...])
    o_ref[...] = (af_ref[...] + sgn * ab_ref[...]) * inv


def _filter_combine(a, asum, hy_w, tc=256):
    l2 = a.shape[0]
    tc = min(tc, hy_w)
    nb = hy_w // tc
    col = lambda d: (lambda o, c: (0, (2 * o + d) * nb + c))
    return pl.pallas_call(
        _filter_combine_kernel,
        grid=(HY_ORDER, nb),
        in_specs=[pl.BlockSpec((l2, tc), col(0)), pl.BlockSpec((l2, tc), col(1)),
                  pl.BlockSpec((1, tc), col(0)), pl.BlockSpec((1, tc), col(1))],
        out_specs=pl.BlockSpec((None, l2, tc), lambda o, c: (o, 0, c)),
        out_shape=jax.ShapeDtypeStruct((HY_ORDER, l2, hy_w), F32),
        compiler_params=_cparams(("parallel", "arbitrary"), 8 * _nbytes((l2, tc), F32)),
        name="hyena_filter_combine",
    )(a, a, asum, asum)


def _dft_matrices(s):
    n = 2 * s
    k = jnp.arange(s, dtype=jnp.int32)

    def tables(cols):
        ang = ((k[:, None] * cols[None, :]) % n).astype(F32) * (2.0 * math.pi / n)
        return jnp.cos(ang), jnp.sin(ang)

    cl, sl = tables(jnp.arange(LANES, dtype=jnp.int32))
    ch, sh = tables(jnp.arange(s // LANES, dtype=jnp.int32) * LANES)
    cosm = (ch[:, :, None] * cl[:, None, :] - sh[:, :, None] * sl[:, None, :]).reshape(s, s)
    sinm = (sh[:, :, None] * cl[:, None, :] + ch[:, :, None] * sl[:, None, :]).reshape(s, s)
    alt = jnp.where(k % 2 == 0, 1.0, -1.0).astype(F32)
    im = (-sinm).at[0].set(alt)
    fwd = jnp.concatenate([cosm, im], axis=0)
    scale = jnp.full((n,), 2.0 / n, F32).at[0].set(1.0 / n).at[s].set(1.0 / n)
    inv = (fwd * scale[:, None]).T
    return fwd.astype(BF16), inv.astype(BF16)


def _dft_fwd_plain_kernel(z_ref, f_ref, o_ref):
    o_ref[...] = jnp.dot(f_ref[...], z_ref[...], preferred_element_type=F32)


def _dft_fwd_filter_kernel(z_ref, f_ref, h_ref, o_ref):
    zf = jnp.dot(f_ref[...], z_ref[...], preferred_element_type=F32)
    s = zf.shape[0] // 2
    zr, zi = zf[:s], zf[s:]
    hr, hi = h_ref[:s, :], h_ref[s:, :]
    first = lax.broadcasted_iota(jnp.int32, zr.shape, 0) == 0
    ii = hi * zi
    o_ref[:s, :] = (hr * zr - jnp.where(first, 0.0, ii)).astype(o_ref.dtype)
    o_ref[s:, :] = jnp.where(first, ii, hr * zi + hi * zr).astype(o_ref.dtype)


def _dft_forward(z, col0, width, fwd, h=None, order=0, tc=256):
    b, s, _ = z.shape
    tc = min(tc, width)
    nc = width // tc
    cb0 = col0 // tc
    z_spec = pl.BlockSpec((None, s, tc), lambda c, i: (i, 0, cb0 + c))
    f_spec = pl.BlockSpec((2 * s, s), lambda c, i: (0, 0), pipeline_mode=pl.Buffered(1))
    o_spec = pl.BlockSpec((None, 2 * s, tc), lambda c, i: (i, 0, c))
    est = (_nbytes((2 * s, s), BF16) + 2 * _nbytes((s, tc), BF16) + 4 * _nbytes((2 * s, tc), F32)
           + 4 * _nbytes((2 * s, tc), F32) + (8 << 20))
    if h is None:
        body, specs, args, odt = _dft_fwd_plain_kernel, [z_spec, f_spec], (z, fwd), F32
    else:
        h_spec = pl.BlockSpec((None, 2 * s, tc), lambda c, i: (order, 0, c))
        body, specs, args, odt = _dft_fwd_filter_kernel, [z_spec, f_spec, h_spec], (z, fwd, h), BF16
    return pl.pallas_call(
        body,
        grid=(nc, b),
        in_specs=specs,
        out_specs=o_spec,
        out_shape=jax.ShapeDtypeStruct((b, 2 * s, width), odt),
        compiler_params=_cparams(("parallel", "arbitrary"), est),
        name=body.__name__.strip("_"),
    )(*args)


def _dft_inv_kernel(y_ref, f_ref, z_ref, g_ref, b_ref, o_ref):
    conv = jnp.dot(f_ref[...], y_ref[...], preferred_element_type=F32)
    z = z_ref[...].astype(F32)
    o_ref[...] = (g_ref[...].astype(F32) * (conv + b_ref[...] * z)).astype(o_ref.dtype)


def _dft_inverse_gated(y, inv, z, z_col0, gate, gate_col0, bias_row, width, tc=256):
    b, s2, _ = y.shape
    s = s2 // 2
    tc = min(tc, width)
    nc = width // tc
    zb0, gb0 = z_col0 // tc, gate_col0 // tc
    est = (_nbytes((s, s2), BF16) + 2 * _nbytes((s2, tc), BF16) + 6 * _nbytes((s, tc), BF16)
           + 4 * _nbytes((s, tc), F32) + (8 << 20))
    return pl.pallas_call(
        _dft_inv_kernel,
        grid=(nc, b),
        in_specs=[pl.BlockSpec((None, s2, tc), lambda c, i: (i, 0, c)),
                  pl.BlockSpec((s, s2), lambda c, i: (0, 0), pipeline_mode=pl.Buffered(1)),
                  pl.BlockSpec((None, s, tc), lambda c, i: (i, 0, zb0 + c)),
                  pl.BlockSpec((None, s, tc), lambda c, i: (i, 0, gb0 + c)),
                  pl.BlockSpec((1, tc), lambda c, i: (0, c))],
        out_specs=pl.BlockSpec((None, s, tc), lambda c, i: (i, 0, c)),
        out_shape=jax.ShapeDtypeStruct((b, s, width), BF16),
        compiler_params=_cparams(("parallel", "arbitrary"), est),
        name="dft_inverse_gated",
    )(y, inv, z, gate, bias_row)


def _merge_kernel(a_ref, hy_ref, wa_ref, wh_ref, g0_ref, g1_ref, o_ref):
    ya = jnp.dot(a_ref[...], wa_ref[...], preferred_element_type=F32)
    yh = jnp.dot(hy_ref[...], wh_ref[...], preferred_element_type=F32)
    o_ref[...] = (g0_ref[...].astype(F32) * ya + g1_ref[...].astype(F32) * yh).astype(o_ref.dtype)


def _gated_merge(attn, hy, w_att, w_hy, gates, tm=1024, tn=512):
    t, k = attn.shape
    n = w_att.shape[1]
    tm, tn = min(tm, t), min(tn, n)
    g1b0 = n // tn
    est = 4 * _nbytes((tm, k), BF16) + 4 * _nbytes((k, tn), BF16) + 6 * _nbytes((tm, tn), F32) + (8 << 20)
    return pl.pallas_call(
        _merge_kernel,
        grid=(t // tm, n // tn),
        in_specs=[pl.BlockSpec((tm, k), lambda i, j: (i, 0)),
                  pl.BlockSpec((tm, k), lambda i, j: (i, 0)),
                  pl.BlockSpec((k, tn), lambda i, j: (0, j)),
                  pl.BlockSpec((k, tn), lambda i, j: (0, j)),
                  pl.BlockSpec((tm, tn), lambda i, j: (i, j)),
                  pl.BlockSpec((tm, tn), lambda i, j: (i, g1b0 + j))],
        out_specs=pl.BlockSpec((tm, tn), lambda i, j: (i, j)),
        out_shape=jax.ShapeDtypeStruct((t, n), BF16),
        compiler_params=_cparams(("parallel", "arbitrary"), est),
        name="gated_merge",
    )(attn, hy, w_att, w_hy, gates, gates)


def _mm_residual_kernel(a_ref, w_ref, r_ref, o_ref):
    o_ref[...] = r_ref[...] + jnp.dot(a_ref[...], w_ref[...], preferred_element_type=F32)


def _matmul_residual(a, w, res, tm=1024, tn=1024):
    t, k = a.shape
    n = w.shape[1]
    tm, tn = min(tm, t), min(tn, n)
    est = 2 * _nbytes((tm, k), BF16) + 2 * _nbytes((k, tn), BF16) + 6 * _nbytes((tm, tn), F32) + (8 << 20)
    return pl.pallas_call(
        _mm_residual_kernel,
        grid=(t // tm, n // tn),
        in_specs=[pl.BlockSpec((tm, k), lambda i, j: (i, 0)),
                  pl.BlockSpec((k, tn), lambda i, j: (0, j)),
                  pl.BlockSpec((tm, tn), lambda i, j: (i, j))],
        out_specs=pl.BlockSpec((tm, tn), lambda i, j: (i, j)),
        out_shape=jax.ShapeDtypeStruct((t, n), F32),
        compiler_params=_cparams(("parallel", "arbitrary"), est),
        name="matmul_residual",
    )(a, w, res)


def _mlp_kernel(x_ref, g_ref, wu_ref, wd_ref, o_ref, hm_ref):
    @pl.when(pl.program_id(1) == 0)
    def _():
        x = x_ref[...]
        ms = jnp.mean(x * x, axis=-1, keepdims=True)
        hm_ref[...] = (x * lax.rsqrt(ms + EPS) * g_ref[...]).astype(hm_ref.dtype)
        o_ref[...] = x

    a = jnp.dot(hm_ref[...], wu_ref[...], preferred_element_type=F32)
    a = jnp.square(jnp.maximum(a, 0.0)).astype(BF16)
    o_ref[...] += jnp.dot(a, wd_ref[...], preferred_element_type=F32)


def _relu2_mlp(x, g, w_up, w_down, tm=1024, tf=512):
    t, d = x.shape
    ff = w_up.shape[1]
    tm, tf = min(tm, t), min(tf, ff)
    est = (4 * _nbytes((tm, d), F32) + _nbytes((tm, d), BF16) + 4 * _nbytes((d, tf), BF16)
           + 3 * _nbytes((tm, tf), F32) + (8 << 20))
    return pl.pallas_call(
        _mlp_kernel,
        grid=(t // tm, ff // tf),
        in_specs=[pl.BlockSpec((tm, d), lambda i, j: (i, 0)),
                  pl.BlockSpec((1, d), lambda i, j: (0, 0)),
                  pl.BlockSpec((d, tf), lambda i, j: (0, j)),
                  pl.BlockSpec((tf, d), lambda i, j: (j, 0))],
        out_specs=pl.BlockSpec((tm, d), lambda i, j: (i, 0)),
        out_shape=jax.ShapeDtypeStruct((t, d), F32),
        scratch_shapes=[pltpu.VMEM((tm, d), BF16)],
        compiler_params=_cparams(("parallel", "arbitrary"), est),
        name="relu2_mlp",
    )(x, g.reshape(1, d), w_up, w_down)


def _ple_kernel(x_ref, g_ref, wg_ref, p_ref, wp_ref, o_ref, hn_ref):
    j = pl.program_id(1)

    @pl.when(j == 0)
    def _():
        x = x_ref[...]
        ms = jnp.mean(x * x, axis=-1, keepdims=True)
        hn_ref[...] = (x * lax.rsqrt(ms + EPS) * g_ref[...]).astype(hn_ref.dtype)

    tn = o_ref.shape[1]
    gate = jax.nn.sigmoid(jnp.dot(hn_ref[...], wg_ref[...], preferred_element_type=F32))
    proj = jnp.dot(p_ref[...].astype(BF16), wp_ref[...], preferred_element_type=F32)
    o_ref[...] = x_ref[:, pl.ds(pl.multiple_of(j * tn, tn), tn)] + gate * proj


def _gated_ple(x, row0, p, g, w_gate, w_proj, tm=1024, tn=512):
    t, pd = p.shape
    d = x.shape[1]
    tm, tn = min(tm, t), min(tn, d)
    rb0 = row0 // tm
    est = (2 * _nbytes((tm, d), F32) + _nbytes((tm, d), BF16) + 2 * _nbytes((d, tn), BF16)
           + 8 * _nbytes((tm, tn), F32) + (8 << 20))
    return pl.pallas_call(
        _ple_kernel,
        grid=(t // tm, d // tn),
        in_specs=[pl.BlockSpec((tm, d), lambda i, j: (rb0 + i, 0)),
                  pl.BlockSpec((1, d), lambda i, j: (0, 0)),
                  pl.BlockSpec((d, tn), lambda i, j: (0, j)),
                  pl.BlockSpec((tm, pd), lambda i, j: (i, 0)),
                  pl.BlockSpec((pd, tn), lambda i, j: (0, j))],
        out_specs=pl.BlockSpec((tm, tn), lambda i, j: (i, j)),
        out_shape=jax.ShapeDtypeStruct((t, d), F32),
        scratch_shapes=[pltpu.VMEM((tm, d), BF16)],
        compiler_params=_cparams(("parallel", "arbitrary"), est),
        name="gated_ple",
    )(x, g.reshape(1, d), w_gate, p, w_proj)


def _encoder_layer(x3, p_list, norm_mix_g, w_in, gate_b, q_norm_g, k_norm_g, hy_conv_w, hy_conv_b,
                   hy_w1, hy_b1, hy_w2, hy_b2, hy_sin_freq, hy_w3, hy_bias,
                   w_att_out, w_hy_out, w_o, norm_mlp_g, w_up, w_down,
                   norm_ple_g, w_ple_gate, w_ple_proj):
    b, s, d = x3.shape
    att_w = w_att_out.shape[0]
    hy_w = w_hy_out.shape[0]
    x2 = x3.reshape(b * s, d)
    bf = lambda a: a.astype(BF16)

    h3 = _rmsnorm_cast(x2, norm_mix_g).reshape(b, s, d)
    w_in_b = bf(w_in)
    rope_c, rope_s1, rope_s2 = _rope_tables(s)
    tab_spec = pl.BlockSpec((s, HEAD_DIM), lambda i, j: (0, 0))
    vec_spec = lambda tn, blk0=0: pl.BlockSpec((1, tn), lambda i, j: (0, blk0 + j))
    head_spec = pl.BlockSpec((1, HEAD_DIM), lambda i, j: (0, 0))
    tn = 512
    qk = lambda g, col0, scale: _inproj(
        h3, w_in_b, col0, att_w, functools.partial(_inproj_qk_kernel, scale=scale),
        (g.reshape(1, HEAD_DIM), rope_c, rope_s1, rope_s2), [head_spec, tab_spec, tab_spec, tab_spec], tn)
    q = qk(q_norm_g, 0, HEAD_DIM ** -0.5)
    k = qk(k_norm_g, att_w, 1.0)
    v = _inproj(h3, w_in_b, 2 * att_w, att_w, _inproj_plain_kernel, (), [], tn)
    tnc = min(tn, hy_w)
    u = _inproj(h3, w_in_b, 3 * att_w, 3 * hy_w, _inproj_conv_kernel,
                (hy_conv_w, hy_conv_b.reshape(1, 3 * hy_w)),
                [pl.BlockSpec((3, tnc), lambda i, j: (0, j)), vec_spec(tnc)], tnc)
    tng = min(tn, d)
    gates = _inproj(h3, w_in_b, 3 * att_w + 3 * hy_w, N_BRANCH * d, _inproj_gate_kernel,
                    (gate_b.reshape(1, N_BRANCH * d),), [vec_spec(tng)], tng)

    attn = _dilated_attention(q, k, v)

    fwd, inv = _dft_matrices(s)
    taps, tap_asum = _hyena_filter_taps(s, hy_w1, hy_b1, hy_w2, hy_b2, hy_sin_freq, hy_w3, hy_w)
    tap_spec = _dft_forward(taps[None], 0, taps.shape[1], fwd)[0]
    filt = _filter_combine(tap_spec, tap_asum, hy_w)
    bias = hy_bias.astype(F32)
    y1 = _dft_forward(u, 0, hy_w, fwd, filt, 0)
    z = _dft_inverse_gated(y1, inv, u, 0, u, hy_w, bias[0:1], hy_w)
    y2 = _dft_forward(z, 0, hy_w, fwd, filt, 1)
    hy = _dft_inverse_gated(y2, inv, z, 0, u, 2 * hy_w, bias[1:2], hy_w)

    merged = _gated_merge(attn.reshape(b * s, att_w), hy.reshape(b * s, hy_w),
                          bf(w_att_out), bf(w_hy_out), gates.reshape(b * s, N_BRANCH * d))
    x2 = _matmul_residual(merged, bf(w_o), x2)
    x2 = _relu2_mlp(x2, norm_mlp_g, bf(w_up), bf(w_down))

    outs, row0 = [], 0
    w_pg, w_pp = bf(w_ple_gate), bf(w_ple_proj)
    for p in p_list:
        pb = p.shape[0]
        p2 = p.reshape(pb * s, p.shape[-1])
        outs.append(_gated_ple(x2, row0, p2, norm_ple_g, w_pg, w_pp).reshape(pb, s, d))
        row0 += pb * s
    return outs


def kernel(x_prompt, x_sample, p_prompt, p_sample, norm_mix_g, w_in, gate_b, q_norm_g, k_norm_g, hy_conv_w, hy_conv_b, hy_w1, hy_b1, hy_w2, hy_b2, hy_sin_freq, hy_w3, hy_bias, w_att_out, w_hy_out, w_o, norm_mlp_g, w_up, w_down, norm_ple_g, w_ple_gate, w_ple_proj):
    assert x_prompt.shape[1:] == x_sample.shape[1:]
    y_prompt, y_sample = x_prompt, x_sample
    for i in range(w_in.shape[0]):
        x3 = jnp.concatenate([y_prompt, y_sample], axis=0)
        y_prompt, y_sample = _encoder_layer(
            x3, (p_prompt[i], p_sample[i]), norm_mix_g[i], w_in[i], gate_b[i], q_norm_g[i], k_norm_g[i],
            hy_conv_w[i], hy_conv_b[i], hy_w1[i], hy_b1[i], hy_w2[i], hy_b2[i], hy_sin_freq[i], hy_w3[i],
            hy_bias[i], w_att_out[i], w_hy_out[i], w_o[i], norm_mlp_g[i], w_up[i], w_down[i],
            norm_ple_g[i], w_ple_gate[i], w_ple_proj[i])
    return (y_prompt, y_sample)
```

```python
import functools
import math

import jax
import jax.numpy as jnp
from jax import lax
from jax.experimental import pallas as pl
from jax.experimental.pallas import tpu as pltpu

F32 = jnp.float32
BF16 = jnp.bfloat16
HIGHEST = lax.Precision.HIGHEST

EPS = 1e-6
HEAD_DIM = 128
ROPE_DIM = HEAD_DIM // 4
ROPE_THETA = 500000.0
DILATED_PATTERNS = ((128, 1), (512, 4), (2048, 16))
ATT_RADIUS = 64
ATT_QBLK = 128
ATT_KWIN = ATT_QBLK + 2 * ATT_RADIUS
HY_ORDER = 2
HY_EMB_DIM = 33
HY_BANDS = (HY_EMB_DIM - 1) // 2
HY_FAST_DECAY = 0.3
HY_SLOW_DECAY = 1.5
HY_DECAY_TARGET = 1e-2
HY_MIN_DECAY = math.log(HY_DECAY_TARGET) / HY_SLOW_DECAY
HY_MAX_DECAY = math.log(HY_DECAY_TARGET) / HY_FAST_DECAY
N_BRANCH = 2
LANES = 128
MASK_NEG = -1e30

VMEM_CAP_BYTES = 60 * 1024 * 1024


def _cparams(sem, est_bytes):
    limit = int(min(VMEM_CAP_BYTES, max(32 * 1024 * 1024, est_bytes)))
    return pltpu.CompilerParams(dimension_semantics=sem, vmem_limit_bytes=limit)


def _nbytes(shape, dtype):
    return math.prod(shape) * jnp.dtype(dtype).itemsize


def _rmsnorm_kernel(x_ref, g_ref, o_ref):
    x = x_ref[...]
    ms = jnp.mean(x * x, axis=-1, keepdims=True)
    o_ref[...] = (x * lax.rsqrt(ms + EPS) * g_ref[...]).astype(o_ref.dtype)


def _rmsnorm_cast(x2d, g, tm=512):
    t, d = x2d.shape
    tm = min(tm, t)
    return pl.pallas_call(
        _rmsnorm_kernel,
        grid=(t // tm,),
        in_specs=[pl.BlockSpec((tm, d), lambda i: (i, 0)),
                  pl.BlockSpec((1, d), lambda i: (0, 0))],
        out_specs=pl.BlockSpec((tm, d), lambda i: (i, 0)),
        out_shape=jax.ShapeDtypeStruct((t, d), BF16),
        compiler_params=_cparams(("parallel",), 4 * _nbytes((tm, d), F32)),
        name="rmsnorm_cast",
    )(x2d, g.reshape(1, d))


def _inproj_qk_kernel(h_ref, w_ref, g_ref, c_ref, s1_ref, s2_ref, o_ref, *, scale):
    y = jnp.dot(h_ref[...], w_ref[...], preferred_element_type=F32)
    gs = g_ref[...] * scale
    pw = 2 * HEAD_DIM
    same_head = (lax.broadcasted_iota(jnp.int32, (pw, pw), 0) // HEAD_DIM
                 == lax.broadcasted_iota(jnp.int32, (pw, pw), 1) // HEAD_DIM)
    ones_bd = jnp.where(same_head, 1.0, 0.0).astype(BF16)
    yy = (y * y).astype(BF16)
    for hh in range(y.shape[1] // HEAD_DIM):
        sl = slice(hh * HEAD_DIM, (hh + 1) * HEAD_DIM)
        yh = y[:, sl]
        if hh % 2 == 0:
            ss_pair = jnp.dot(yy[:, hh * HEAD_DIM:hh * HEAD_DIM + pw], ones_bd, preferred_element_type=F32)
        ms = ss_pair[:, (hh % 2) * HEAD_DIM:(hh % 2 + 1) * HEAD_DIM] * (1.0 / HEAD_DIM)
        t = yh * lax.rsqrt(ms + EPS) * gs
        out = (t * c_ref[...]
               + pltpu.roll(t, HEAD_DIM - ROPE_DIM // 2, 1) * s1_ref[...]
               + pltpu.roll(t, ROPE_DIM // 2, 1) * s2_ref[...])
        o_ref[:, sl] = out.astype(o_ref.dtype)


def _inproj_plain_kernel(h_ref, w_ref, o_ref):
    o_ref[...] = jnp.dot(h_ref[...], w_ref[...], preferred_element_type=F32).astype(o_ref.dtype)


CONV_PAD_ROWS = 8


def _inproj_conv_kernel(h_ref, w_ref, cw_ref, cb_ref, o_ref, scr):
    y = jnp.dot(h_ref[...], w_ref[...], preferred_element_type=F32)
    s, tn = y.shape
    m = s // 2
    pad = CONV_PAD_ROWS
    zeros = jnp.zeros((pad, LANES), F32)
    for c in range(tn // LANES):
        scr[c, 0:pad, :] = zeros
        scr[c, pad + s:2 * pad + s, :] = zeros
        scr[c, pad:pad + s, :] = y[:, c * LANES:(c + 1) * LANES]
    for c in range(tn // LANES):
        cols = slice(c * LANES, (c + 1) * LANES)
        w0, w1, w2, cb = cw_ref[0:1, cols], cw_ref[1:2, cols], cw_ref[2:3, cols], cb_ref[:, cols]
        ye = scr[c, pl.ds(pad, m, stride=2), :]
        yo = scr[c, pl.ds(pad + 1, m, stride=2), :]
        yo_prev = scr[c, pl.ds(pad - 1, m, stride=2), :]
        ye_next = scr[c, pl.ds(pad + 2, m, stride=2), :]
        o_ref[0:m, cols] = (yo_prev * w0 + ye * w1 + yo * w2 + cb).astype(o_ref.dtype)
        o_ref[m:s, cols] = (ye * w0 + yo * w1 + ye_next * w2 + cb).astype(o_ref.dtype)


def _inproj_gate_kernel(h_ref, w_ref, gb_ref, o_ref):
    y = jnp.dot(h_ref[...], w_ref[...], preferred_element_type=F32)
    o_ref[...] = jax.nn.sigmoid(y + gb_ref[...]).astype(o_ref.dtype)


def _inproj(h3, w, col0, width, body, extras, extra_specs, tn, scratch=()):
    b, s, d = h3.shape
    tn = min(tn, width)
    nj = width // tn
    jb0 = col0 // tn
    est = (2 * _nbytes((s, d), BF16) + 2 * _nbytes((d, tn), BF16) + 2 * _nbytes((s, tn), BF16)
           + 4 * _nbytes((s, tn), F32) + (8 << 20))
    return pl.pallas_call(
        body,
        grid=(b, nj),
        in_specs=[pl.BlockSpec((None, s, d), lambda i, j: (i, 0, 0)),
                  pl.BlockSpec((d, tn), lambda i, j: (0, jb0 + j))] + extra_specs,
        out_specs=pl.BlockSpec((None, s, tn), lambda i, j: (i, 0, j)),
        out_shape=jax.ShapeDtypeStruct((b, s, width), BF16),
        scratch_shapes=list(scratch),
        compiler_params=_cparams(("parallel", "arbitrary"), est),
        name="inproj_" + getattr(body, "func", body).__name__,
    )(h3, w, *extras)


def _rope_tables(s):
    half = ROPE_DIM // 2
    inv_freq = ROPE_THETA ** (-jnp.arange(half, dtype=F32) / half)
    ang = jnp.arange(s, dtype=F32)[:, None] * inv_freq[None, :]
    cos, sin = jnp.cos(ang), jnp.sin(ang)
    c = jnp.concatenate([cos, cos, jnp.ones((s, HEAD_DIM - ROPE_DIM), F32)], axis=1)
    s1 = jnp.concatenate([-sin, jnp.zeros((s, HEAD_DIM - half), F32)], axis=1)
    s2 = jnp.concatenate([jnp.zeros((s, half), F32), sin, jnp.zeros((s, HEAD_DIM - ROPE_DIM), F32)], axis=1)
    return c, s1, s2


def _band_bias(off, kw):
    col = lax.broadcasted_iota(jnp.int32, (ATT_QBLK, kw), 1)
    row = lax.broadcasted_iota(jnp.int32, (ATT_QBLK, kw), 0)
    return jnp.where(jnp.abs(col - row + off) <= ATT_RADIUS, 0.0, MASK_NEG).astype(F32)


def _band_blocks(seg_len):
    kw = min(ATT_KWIN, seg_len)
    for bi in range(seg_len // ATT_QBLK):
        q0 = bi * ATT_QBLK
        yield q0, min(max(q0 - ATT_RADIUS, 0), seg_len - kw), kw


def _band_attend(qb, kb, va, bias):
    s = lax.dot_general(qb, kb, (((1,), (1,)), ((), ())), preferred_element_type=F32) + bias
    m = jnp.max(s, axis=-1, keepdims=True)
    p = jnp.exp(s - m).astype(BF16)
    oa = jnp.dot(p, va, preferred_element_type=F32)
    return oa[:, :HEAD_DIM], oa[:, HEAD_DIM:], jnp.broadcast_to(m, (ATT_QBLK, HEAD_DIM))


def _softmax_merge(a, b):
    (o1, l1, m1), (o2, l2, m2) = a, b
    m = jnp.maximum(m1, m2)
    a1 = jnp.exp(m1 - m)
    a2 = jnp.exp(m2 - m)
    return a1 * o1 + a2 * o2, a1 * l1 + a2 * l2, m


def _attn_kernel(q_ref, k_ref, v_ref, o_ref,
                 qf0, kf0, vf0, qf1, kf1, vf1, qb1, kb1, qb2, kb2, va0, va1, va2,
                 ao, al, am, bo, bl, bm):
    s = q_ref.shape[0]
    c = HEAD_DIM
    n1 = s // 4
    n2 = s // 16
    ones = jnp.ones((s, c), BF16)
    biases = {}

    def bias(off, kw):
        if (off, kw) not in biases:
            biases[(off, kw)] = _band_bias(off, kw)
        return biases[(off, kw)]

    qf0[...] = q_ref[...].astype(F32)
    kf0[...] = k_ref[...].astype(F32)
    vf0[...] = v_ref[...].astype(F32)
    va0[:, :c] = v_ref[...]
    va0[:, c:] = ones
    for r in range(4):
        rows = slice(r * n1, (r + 1) * n1)
        for src, dstf, dstb in ((qf0, qf1, qb1), (kf0, kf1, kb1)):
            x = src[pl.ds(r, n1, stride=4), :]
            dstf[rows, :] = x
            dstb[rows, :] = x.astype(BF16)
        x = vf0[pl.ds(r, n1, stride=4), :]
        vf1[rows, :] = x
        va1[rows, :c] = x.astype(BF16)
    va1[:, c:] = ones
    for r4 in range(4):
        for rp in range(4):
            rows = slice((r4 * 4 + rp) * n2, (r4 * 4 + rp + 1) * n2)
            src_rows = pl.ds(r4 * n1 + rp, n2, stride=4)
            qb2[rows, :] = qf1[src_rows, :].astype(BF16)
            kb2[rows, :] = kf1[src_rows, :].astype(BF16)
            va2[rows, :c] = vf1[src_rows, :].astype(BF16)
    va2[:, c:] = ones

    for r4 in range(4):
        for rp in range(4):
            base = (r4 * 4 + rp) * n2
            for q0, w0, kw in _band_blocks(n2):
                o, l, m = _band_attend(qb2[base + q0:base + q0 + ATT_QBLK, :],
                                       kb2[base + w0:base + w0 + kw, :],
                                       va2[base + w0:base + w0 + kw, :], bias(w0 - q0, kw))
                dst = pl.ds(r4 * n1 + 4 * q0 + rp, ATT_QBLK, stride=4)
                ao[dst, :] = o
                al[dst, :] = l
                am[dst, :] = m
    for r4 in range(4):
        base = r4 * n1
        for q0, w0, kw in _band_blocks(n1):
            cur = _band_attend(qb1[base + q0:base + q0 + ATT_QBLK, :],
                               kb1[base + w0:base + w0 + kw, :],
                               va1[base + w0:base + w0 + kw, :], bias(w0 - q0, kw))
            rows = slice(base + q0, base + q0 + ATT_QBLK)
            o, l, m = _softmax_merge(cur, (ao[rows, :], al[rows, :], am[rows, :]))
            dst = pl.ds(4 * q0 + r4, ATT_QBLK, stride=4)
            bo[dst, :] = o
            bl[dst, :] = l
            bm[dst, :] = m
    for q0, w0, kw in _band_blocks(s):
        cur = _band_attend(q_ref[q0:q0 + ATT_QBLK, :], k_ref[w0:w0 + kw, :], va0[w0:w0 + kw, :],
                           bias(w0 - q0, kw))
        rows = slice(q0, q0 + ATT_QBLK)
        o, l, _ = _softmax_merge(cur, (bo[rows, :], bl[rows, :], bm[rows, :]))
        o_ref[rows, :] = (o / l).astype(o_ref.dtype)


def _dilated_attention(q, k, v):
    b, s, aw = q.shape
    assert tuple(w // (2 * d) for w, d in DILATED_PATTERNS) == (ATT_RADIUS,) * 3
    assert tuple(d for _, d in DILATED_PATTERNS) == (1, 4, 16)
    assert s % (16 * ATT_QBLK) == 0
    c = HEAD_DIM
    spec = pl.BlockSpec((None, s, c), lambda i, h: (i, 0, h))
    f32buf = pltpu.VMEM((s, c), F32)
    bf16buf = pltpu.VMEM((s, c), BF16)
    augbuf = pltpu.VMEM((s, 2 * c), BF16)
    scratch = [f32buf] * 6 + [bf16buf] * 4 + [augbuf] * 3 + [f32buf] * 6
    est = 12 * _nbytes((s, c), F32) + 10 * _nbytes((s, c), BF16) + 8 * _nbytes((s, c), BF16) + (16 << 20)
    return pl.pallas_call(
        _attn_kernel,
        grid=(b, aw // c),
        in_specs=[spec, spec, spec],
        out_specs=spec,
        out_shape=jax.ShapeDtypeStruct((b, s, aw), BF16),
        scratch_shapes=scratch,
        compiler_params=_cparams(("parallel", "arbitrary"), est),
        name="dilated_attention",
    )(q, k, v)


def _filter_kernel(z_ref, w1_ref, b1_ref, w2_ref, b2_ref, fr_ref, w3_ref, t_ref, dl_ref,
                   filt_ref, asum_ref, hdn_ref, *, blocks_per_dir):
    @pl.when(pl.program_id(0) == 0)
    def _():
        fr = fr_ref[...]
        hdn = jnp.sin(fr * (jnp.dot(z_ref[...], w1_ref[...], precision=HIGHEST,
                                    preferred_element_type=F32) + b1_ref[...]))
        hdn_ref[...] = jnp.sin(fr * (jnp.dot(hdn, w2_ref[...], precision=HIGHEST,
                                             preferred_element_type=F32) + b2_ref[...]))

    hf = jnp.dot(hdn_ref[...], w3_ref[...], precision=HIGHEST, preferred_element_type=F32)
    hf = hf * jnp.exp(-t_ref[...] * dl_ref[...])
    is_bwd = (pl.program_id(0) // blocks_per_dir) % 2 == 1
    row = lax.broadcasted_iota(jnp.int32, hf.shape, 0)
    hf = jnp.where(jnp.logical_and(row == 0, is_bwd), 0.0, hf)
    filt_ref[...] = hf.astype(filt_ref.dtype)
    asum_ref[...] = jnp.sum(jnp.abs(hf), axis=0, keepdims=True)


def _even_odd_rows(a):
    return jnp.concatenate([a[0::2], a[1::2]], axis=0)


def _hyena_filter_taps(seq_len, w1, b1, w2, b2, sin_freq, w3, hy_w, tn=512):
    ffn = w1.shape[1]
    pad = LANES
    t = jnp.linspace(0.0, 1.0, seq_len, dtype=F32)[:, None]
    wpos = 2.0 * math.pi * jnp.arange(seq_len, dtype=F32) / seq_len
    bands = jnp.linspace(1e-4, HY_BANDS - 1, HY_BANDS, dtype=F32)
    ang = wpos[:, None] * bands[None, :]
    z = jnp.concatenate([t, jnp.cos(ang), -jnp.sin(ang)], axis=-1)
    z = _even_odd_rows(jnp.pad(z, ((0, 0), (0, pad - z.shape[1]))))
    t = _even_odd_rows(t)
    w1p = jnp.pad(w1.astype(F32), ((0, pad - w1.shape[0]), (0, pad - ffn)))
    w2p = jnp.pad(w2.astype(F32), ((0, pad - ffn), (0, pad - ffn)))
    w3p = jnp.pad(w3.astype(F32), ((0, pad - ffn), (0, 0)))
    padv = lambda a: jnp.pad(a.astype(F32), (0, pad - ffn)).reshape(1, pad)
    deltas = jnp.abs(jnp.linspace(HY_MIN_DECAY, HY_MAX_DECAY, hy_w, dtype=F32)).reshape(1, hy_w)
    ncol = w3.shape[1]
    tn = min(tn, hy_w)
    bpd = hy_w // tn
    full = lambda shape: pl.BlockSpec(shape, lambda j: (0, 0))
    return pl.pallas_call(
        functools.partial(_filter_kernel, blocks_per_dir=bpd),
        grid=(ncol // tn,),
        in_specs=[full((seq_len, pad)), full((pad, pad)), full((1, pad)), full((pad, pad)), full((1, pad)),
                  full((1, pad)), pl.BlockSpec((pad, tn), lambda j: (0, j)), full((seq_len, 1)),
                  pl.BlockSpec((1, tn), lambda j: (0, j % bpd))],
        out_specs=[pl.BlockSpec((seq_len, tn), lambda j: (0, j)), pl.BlockSpec((1, tn), lambda j: (0, j))],
        out_shape=[jax.ShapeDtypeStruct((seq_len, ncol), BF16), jax.ShapeDtypeStruct((1, ncol), F32)],
        scratch_shapes=[pltpu.VMEM((seq_len, pad), F32)],
        compiler_params=_cparams(("arbitrary",), 32 << 20),
        name="hyena_filter_taps",
    )(z, w1p, padv(b1), w2p, padv(b2), padv(sin_freq), w3p, t, deltas)


def _filter_combine_kernel(af_ref, ab_ref, nf_ref, nb_ref, o_ref):
    first = lax.broadcasted_iota(jnp.int32, af_ref.shape[1:], 0) == 0
    inv = 1.0 / (nf_ref[...] + nb_ref[...])
    o_ref[0] = (af_ref[0] + ab_ref[0]) * inv
    o_ref[1] = (af_ref[1] + jnp.where(first, ab_ref[1], -ab_ref[1])) * inv
    o_ref[2] = (af_ref[2] + ab_ref[2]) * inv
    o_ref[3] = (af_ref[3] - ab_ref[3]) * inv


def _filter_combine(a, asum, hy_w, tc=256):
    _, m, _ = a.shape
    tc = min(tc, hy_w)
    nb = hy_w // tc
    col3 = lambda d: (lambda o, c: (0, 0, (2 * o + d) * nb + c))
    col2 = lambda d: (lambda o, c: (0, (2 * o + d) * nb + c))
    return pl.pallas_call(
        _filter_combine_kernel,
        grid=(HY_ORDER, nb),
        in_specs=[pl.BlockSpec((4, m, tc), col3(0)), pl.BlockSpec((4, m, tc), col3(1)),
                  pl.BlockSpec((1, tc), col2(0)), pl.BlockSpec((1, tc), col2(1))],
        out_specs=pl.BlockSpec((None, 4, m, tc), lambda o, c: (o, 0, 0, c)),
        out_shape=jax.ShapeDtypeStruct((HY_ORDER, 4, m, hy_w), F32),
        compiler_params=_cparams(("parallel", "arbitrary"), 8 * _nbytes((4, m, tc), F32)),
        name="hyena_filter_combine",
    )(a, a, asum, asum)


def _dft_matrices(s, n_total, tc):
    n = 2 * s
    k = jnp.arange(s, dtype=jnp.int32)

    def tables(cols):
        ang = ((k[:, None] * cols[None, :]) % n).astype(F32) * (2.0 * math.pi / n)
        return jnp.cos(ang), jnp.sin(ang)

    cl, sl = tables(jnp.arange(LANES, dtype=jnp.int32))
    ch, sh = tables(jnp.arange(s // LANES, dtype=jnp.int32) * LANES)
    cosm = (ch[:, :, None] * cl[:, None, :] - sh[:, :, None] * sl[:, None, :]).reshape(s, s)
    sinm = (sh[:, :, None] * cl[:, None, :] + ch[:, :, None] * sl[:, None, :]).reshape(s, s)
    alt = jnp.where(k % 2 == 0, 1.0, -1.0).astype(F32)
    im = (-sinm).at[0].set(alt)
    fwd = jnp.concatenate([cosm, im], axis=0)
    scale = jnp.full((n,), 2.0 / n_total, F32).at[0].set(1.0 / n_total).at[s].set(1.0 / n_total)
    inv = (fwd * scale[:, None]).T
    theta = k.astype(F32) * (2.0 * math.pi / n_total)
    cw = jnp.broadcast_to(jnp.cos(theta)[:, None], (s, tc))
    sw = jnp.broadcast_to(jnp.sin(theta)[:, None], (s, tc))
    return fwd.astype(BF16), inv.astype(BF16), cw, sw


def _half_spectrum(z_ref, f_ref, cw, sw, first):
    m = z_ref.shape[0] // 2
    ef = jnp.dot(f_ref[...], z_ref[0:m, :], preferred_element_type=F32)
    of = jnp.dot(f_ref[...], z_ref[m:2 * m, :], preferred_element_type=F32)
    er, ei, orr, oi = ef[:m], ef[m:], of[:m], of[m:]
    tr = orr * cw + oi * sw
    ti = oi * cw - orr * sw
    return er + tr, jnp.where(first, ei, ei + ti), er - tr, jnp.where(first, -oi, ti - ei)


def _spectrum_product(h, z, first):
    hpre, hpim, hqre, hqim = h
    zpre, zpim, zqre, zqim = z
    a = hpim * zpim
    b = hqim * zqim
    ypre = hpre * zpre - jnp.where(first, 0.0, a)
    ypim = jnp.where(first, a - b, hpre * zpim + hpim * zpre)
    yqre = hqre * zqre - jnp.where(first, 0.0, b)
    yqim = jnp.where(first, hpim * zqim + hqim * zpim, hqre * zqim + hqim * zqre)
    return ypre, ypim, yqre, yqim


def _half_inverse(y, inv_ref, cw, sw, first):
    ypre, ypim, yqre, yqim = y
    gre = ypre + yqre
    gim = jnp.where(first, 2.0 * ypim, ypim - yqim)
    dre = ypre - yqre
    dim = ypim + yqim
    g2re = dre * cw - dim * sw
    g2im = jnp.where(first, -2.0 * yqim, dre * sw + dim * cw)
    dt = inv_ref.dtype
    ge = jnp.concatenate([gre.astype(dt), gim.astype(dt)], axis=0)
    go = jnp.concatenate([g2re.astype(dt), g2im.astype(dt)], axis=0)
    return (jnp.dot(inv_ref[...], ge, preferred_element_type=F32),
            jnp.dot(inv_ref[...], go, preferred_element_type=F32))


def _tap_spectrum_kernel(z_ref, f_ref, cw_ref, sw_ref, o_ref):
    first = lax.broadcasted_iota(jnp.int32, cw_ref.shape, 0) == 0
    planes = _half_spectrum(z_ref, f_ref, cw_ref[...], sw_ref[...], first)
    for i, plane in enumerate(planes):
        o_ref[i] = plane


def _tap_spectrum(taps, fwd, cw, sw):
    l, ncol = taps.shape
    m, tc = cw.shape
    const = lambda shape: pl.BlockSpec(shape, lambda c: (0,) * len(shape), pipeline_mode=pl.Buffered(1))
    est = (_nbytes((2 * m, m), BF16) + 2 * _nbytes((l, tc), BF16) + 2 * _nbytes((m, tc), F32)
           + 16 * _nbytes((m, tc), F32) + (8 << 20))
    return pl.pallas_call(
        _tap_spectrum_kernel,
        grid=(ncol // tc,),
        in_specs=[pl.BlockSpec((l, tc), lambda c: (0, c)), const((2 * m, m)), const((m, tc)), const((m, tc))],
        out_specs=pl.BlockSpec((4, m, tc), lambda c: (0, 0, c)),
        out_shape=jax.ShapeDtypeStruct((4, m, ncol), F32),
        compiler_params=_cparams(("parallel",), est),
        name="hyena_tap_spectrum",
    )(taps, fwd, cw, sw)


def _hyena_conv_kernel(z_ref, g_ref, h_ref, b_ref, f_ref, inv_ref, cw_ref, sw_ref, o_ref, *scratch, natural_out):
    s, tc = z_ref.shape
    m = s // 2
    cw, sw = cw_ref[...], sw_ref[...]
    first = lax.broadcasted_iota(jnp.int32, (m, tc), 0) == 0
    spec = _half_spectrum(z_ref, f_ref, cw, sw, first)
    prod = _spectrum_product((h_ref[0], h_ref[1], h_ref[2], h_ref[3]), spec, first)
    conv_e, conv_o = _half_inverse(prod, inv_ref, cw, sw, first)
    bias = b_ref[...]
    out_e = g_ref[0:m, :].astype(F32) * (conv_e + bias * z_ref[0:m, :].astype(F32))
    out_o = g_ref[m:s, :].astype(F32) * (conv_o + bias * z_ref[m:s, :].astype(F32))
    if natural_out:
        (nat,) = scratch
        for c in range(tc // LANES):
            cols = slice(c * LANES, (c + 1) * LANES)
            nat[c, pl.ds(0, m, stride=2), :] = out_e[:, cols]
            nat[c, pl.ds(1, m, stride=2), :] = out_o[:, cols]
        for c in range(tc // LANES):
            o_ref[:, c * LANES:(c + 1) * LANES] = nat[c].astype(o_ref.dtype)
    else:
        o_ref[0:m, :] = out_e.astype(o_ref.dtype)
        o_ref[m:s, :] = out_o.astype(o_ref.dtype)


def _hyena_conv(z, z_col0, gate, gate_col0, filt, order, bias_row, mats, width, natural_out):
    fwd, inv, cw, sw = mats
    b, s, _ = z.shape
    m, tc = cw.shape
    nc = width // tc
    zb0, gb0 = z_col0 // tc, gate_col0 // tc
    const = lambda shape: pl.BlockSpec(shape, lambda c, i: (0,) * len(shape), pipeline_mode=pl.Buffered(1))
    scratch = [pltpu.VMEM((tc // LANES, s, LANES), F32)] if natural_out else []
    est = (2 * _nbytes((2 * m, m), BF16) + 2 * _nbytes((m, tc), F32) + 2 * _nbytes((4, m, tc), F32)
           + 6 * _nbytes((s, tc), BF16) + 24 * _nbytes((m, tc), F32) + (8 << 20))
    return pl.pallas_call(
        functools.partial(_hyena_conv_kernel, natural_out=natural_out),
        grid=(nc, b),
        in_specs=[pl.BlockSpec((None, s, tc), lambda c, i: (i, 0, zb0 + c)),
                  pl.BlockSpec((None, s, tc), lambda c, i: (i, 0, gb0 + c)),
                  pl.BlockSpec((None, 4, m, tc), lambda c, i: (order, 0, 0, c)),
                  pl.BlockSpec((1, tc), lambda c, i: (0, c)),
                  const((2 * m, m)), const((m, 2 * m)), const((m, tc)), const((m, tc))],
        out_specs=pl.BlockSpec((None, s, tc), lambda c, i: (i, 0, c)),
        out_shape=jax.ShapeDtypeStruct((b, s, width), BF16),
        scratch_shapes=scratch,
        compiler_params=_cparams(("parallel", "arbitrary"), est),
        name="hyena_conv",
    )(z, gate, filt, bias_row, fwd, inv, cw, sw)


def _merge_kernel(a_ref, hy_ref, wa_ref, wh_ref, g0_ref, g1_ref, o_ref):
    ya = jnp.dot(a_ref[...], wa_ref[...], preferred_element_type=F32)
    yh = jnp.dot(hy_ref[...], wh_ref[...], preferred_element_type=F32)
    o_ref[...] = (g0_ref[...].astype(F32) * ya + g1_ref[...].astype(F32) * yh).astype(o_ref.dtype)


def _gated_merge(attn, hy, w_att, w_hy, gates, tm=1024, tn=512):
    t, k = attn.shape
    n = w_att.shape[1]
    tm, tn = min(tm, t), min(tn, n)
    g1b0 = n // tn
    est = 4 * _nbytes((tm, k), BF16) + 4 * _nbytes((k, tn), BF16) + 6 * _nbytes((tm, tn), F32) + (8 << 20)
    return pl.pallas_call(
        _merge_kernel,
        grid=(t // tm, n // tn),
        in_specs=[pl.BlockSpec((tm, k), lambda i, j: (i, 0)),
                  pl.BlockSpec((tm, k), lambda i, j: (i, 0)),
                  pl.BlockSpec((k, tn), lambda i, j: (0, j)),
                  pl.BlockSpec((k, tn), lambda i, j: (0, j)),
                  pl.BlockSpec((tm, tn), lambda i, j: (i, j)),
                  pl.BlockSpec((tm, tn), lambda i, j: (i, g1b0 + j))],
        out_specs=pl.BlockSpec((tm, tn), lambda i, j: (i, j)),
        out_shape=jax.ShapeDtypeStruct((t, n), BF16),
        compiler_params=_cparams(("parallel", "arbitrary"), est),
        name="gated_merge",
    )(attn, hy, w_att, w_hy, gates, gates)


def _mm_residual_kernel(a_ref, w_ref, r_ref, o_ref):
    o_ref[...] = r_ref[...] + jnp.dot(a_ref[...], w_ref[...], preferred_element_type=F32)


def _matmul_residual(a, w, res, tm=1024, tn=1024):
    t, k = a.shape
    n = w.shape[1]
    tm, tn = min(tm, t), min(tn, n)
    est = 2 * _nbytes((tm, k), BF16) + 2 * _nbytes((k, tn), BF16) + 6 * _nbytes((tm, tn), F32) + (8 << 20)
    return pl.pallas_call(
        _mm_residual_kernel,
        grid=(t // tm, n // tn),
        in_specs=[pl.BlockSpec((tm, k), lambda i, j: (i, 0)),
                  pl.BlockSpec((k, tn), lambda i, j: (0, j)),
                  pl.BlockSpec((tm, tn), lambda i, j: (i, j))],
        out_specs=pl.BlockSpec((tm, tn), lambda i, j: (i, j)),
        out_shape=jax.ShapeDtypeStruct((t, n), F32),
        compiler_params=_cparams(("parallel", "arbitrary"), est),
        name="matmul_residual",
    )(a, w, res)


def _mlp_kernel(x_ref, g_ref, wu_ref, wd_ref, o_ref, hm_ref):
    @pl.when(pl.program_id(1) == 0)
    def _():
        x = x_ref[...]
        ms = jnp.mean(x * x, axis=-1, keepdims=True)
        hm_ref[...] = (x * lax.rsqrt(ms + EPS) * g_ref[...]).astype(hm_ref.dtype)
        o_ref[...] = x

    a = jnp.dot(hm_ref[...], wu_ref[...], preferred_element_type=F32)
    a = jnp.square(jnp.maximum(a, 0.0)).astype(BF16)
    o_ref[...] += jnp.dot(a, wd_ref[...], preferred_element_type=F32)


def _relu2_mlp(x, g, w_up, w_down, tm=1024, tf=512):
    t, d = x.shape
    ff = w_up.shape[1]
    tm, tf = min(tm, t), min(tf, ff)
    est = (4 * _nbytes((tm, d), F32) + _nbytes((tm, d), BF16) + 4 * _nbytes((d, tf), BF16)
           + 3 * _nbytes((tm, tf), F32) + (8 << 20))
    return pl.pallas_call(
        _mlp_kernel,
        grid=(t // tm, ff // tf),
        in_specs=[pl.BlockSpec((tm, d), lambda i, j: (i, 0)),
                  pl.BlockSpec((1, d), lambda i, j: (0, 0)),
                  pl.BlockSpec((d, tf), lambda i, j: (0, j)),
                  pl.BlockSpec((tf, d), lambda i, j: (j, 0))],
        out_specs=pl.BlockSpec((tm, d), lambda i, j: (i, 0)),
        out_shape=jax.ShapeDtypeStruct((t, d), F32),
        scratch_shapes=[pltpu.VMEM((tm, d), BF16)],
        compiler_params=_cparams(("parallel", "arbitrary"), est),
        name="relu2_mlp",
    )(x, g.reshape(1, d), w_up, w_down)


def _ple_kernel(x_ref, g_ref, wg_ref, p_ref, wp_ref, o_ref, hn_ref):
    j = pl.program_id(1)

    @pl.when(j == 0)
    def _():
        x = x_ref[...]
        ms = jnp.mean(x * x, axis=-1, keepdims=True)
        hn_ref[...] = (x * lax.rsqrt(ms + EPS) * g_ref[...]).astype(hn_ref.dtype)

    tn = o_ref.shape[1]
    gate = jax.nn.sigmoid(jnp.dot(hn_ref[...], wg_ref[...], preferred_element_type=F32))
    proj = jnp.dot(p_ref[...].astype(BF16), wp_ref[...], preferred_element_type=F32)
    o_ref[...] = x_ref[:, pl.ds(pl.multiple_of(j * tn, tn), tn)] + gate * proj


def _gated_ple(x, row0, p, g, w_gate, w_proj, tm=1024, tn=512):
    t, pd = p.shape
    d = x.shape[1]
    tm, tn = min(tm, t), min(tn, d)
    rb0 = row0 // tm
    est = (2 * _nbytes((tm, d), F32) + _nbytes((tm, d), BF16) + 2 * _nbytes((d, tn), BF16)
           + 8 * _nbytes((tm, tn), F32) + (8 << 20))
    return pl.pallas_call(
        _ple_kernel,
        grid=(t // tm, d // tn),
        in_specs=[pl.BlockSpec((tm, d), lambda i, j: (rb0 + i, 0)),
                  pl.BlockSpec((1, d), lambda i, j: (0, 0)),
                  pl.BlockSpec((d, tn), lambda i, j: (0, j)),
                  pl.BlockSpec((tm, pd), lambda i, j: (i, 0)),
                  pl.BlockSpec((pd, tn), lambda i, j: (0, j))],
        out_specs=pl.BlockSpec((tm, tn), lambda i, j: (i, j)),
        out_shape=jax.ShapeDtypeStruct((t, d), F32),
        scratch_shapes=[pltpu.VMEM((tm, d), BF16)],
        compiler_params=_cparams(("parallel", "arbitrary"), est),
        name="gated_ple",
    )(x, g.reshape(1, d), w_gate, p, w_proj)


def _encoder_layer(x3, p_list, norm_mix_g, w_in, gate_b, q_norm_g, k_norm_g, hy_conv_w, hy_conv_b,
                   hy_w1, hy_b1, hy_w2, hy_b2, hy_sin_freq, hy_w3, hy_bias,
                   w_att_out, w_hy_out, w_o, norm_mlp_g, w_up, w_down,
                   norm_ple_g, w_ple_gate, w_ple_proj):
    b, s, d = x3.shape
    att_w = w_att_out.shape[0]
    hy_w = w_hy_out.shape[0]
    x2 = x3.reshape(b * s, d)
    bf = lambda a: a.astype(BF16)

    h3 = _rmsnorm_cast(x2, norm_mix_g).reshape(b, s, d)
    w_in_b = bf(w_in)
    rope_c, rope_s1, rope_s2 = _rope_tables(s)
    tab_spec = pl.BlockSpec((s, HEAD_DIM), lambda i, j: (0, 0))
    vec_spec = lambda tn, blk0=0: pl.BlockSpec((1, tn), lambda i, j: (0, blk0 + j))
    head_spec = pl.BlockSpec((1, HEAD_DIM), lambda i, j: (0, 0))
    tn = 512
    qk = lambda g, col0, scale: _inproj(
        h3, w_in_b, col0, att_w, functools.partial(_inproj_qk_kernel, scale=scale),
        (g.reshape(1, HEAD_DIM), rope_c, rope_s1, rope_s2), [head_spec, tab_spec, tab_spec, tab_spec], tn)
    q = qk(q_norm_g, 0, HEAD_DIM ** -0.5)
    k = qk(k_norm_g, att_w, 1.0)
    v = _inproj(h3, w_in_b, 2 * att_w, att_w, _inproj_plain_kernel, (), [], tn)
    tnc = min(tn, hy_w)
    u = _inproj(h3, w_in_b, 3 * att_w, 3 * hy_w, _inproj_conv_kernel,
                (hy_conv_w, hy_conv_b.reshape(1, 3 * hy_w)),
                [pl.BlockSpec((3, tnc), lambda i, j: (0, j)), vec_spec(tnc)], tnc,
                scratch=[pltpu.VMEM((tnc // LANES, s + 2 * CONV_PAD_ROWS, LANES), F32)])
    tng = min(tn, d)
    gates = _inproj(h3, w_in_b, 3 * att_w + 3 * hy_w, N_BRANCH * d, _inproj_gate_kernel,
                    (gate_b.reshape(1, N_BRANCH * d),), [vec_spec(tng)], tng)

    attn = _dilated_attention(q, k, v)

    mats = _dft_matrices(s // 2, 2 * s, min(256, hy_w))
    taps, tap_asum = _hyena_filter_taps(s, hy_w1, hy_b1, hy_w2, hy_b2, hy_sin_freq, hy_w3, hy_w)
    filt = _filter_combine(_tap_spectrum(taps, mats[0], mats[2], mats[3]), tap_asum, hy_w)
    bias = hy_bias.astype(F32)
    z = _hyena_conv(u, 0, u, hy_w, filt, 0, bias[0:1], mats, hy_w, natural_out=False)
    hy = _hyena_conv(z, 0, u, 2 * hy_w, filt, 1, bias[1:2], mats, hy_w, natural_out=True)

    merged = _gated_merge(attn.reshape(b * s, att_w), hy.reshape(b * s, hy_w),
                          bf(w_att_out), bf(w_hy_out), gates.reshape(b * s, N_BRANCH * d))
    x2 = _matmul_residual(merged, bf(w_o), x2)
    x2 = _relu2_mlp(x2, norm_mlp_g, bf(w_up), bf(w_down))

    outs, row0 = [], 0
    w_pg, w_pp = bf(w_ple_gate), bf(w_ple_proj)
    for p in p_list:
        pb = p.shape[0]
        p2 = p.reshape(pb * s, p.shape[-1])
        outs.append(_gated_ple(x2, row0, p2, norm_ple_g, w_pg, w_pp).reshape(pb, s, d))
        row0 += pb * s
    return outs


def kernel(x_prompt, x_sample, p_prompt, p_sample, norm_mix_g, w_in, gate_b, q_norm_g, k_norm_g, hy_conv_w, hy_conv_b, hy_w1, hy_b1, hy_w2, hy_b2, hy_sin_freq, hy_w3, hy_bias, w_att_out, w_hy_out, w_o, norm_mlp_g, w_up, w_down, norm_ple_g, w_ple_gate, w_ple_proj):
    assert x_prompt.shape[1:] == x_sample.shape[1:]
    y_prompt, y_sample = x_prompt, x_sample
    for i in range(w_in.shape[0]):
        x3 = jnp.concatenate([y_prompt, y_sample], axis=0)
        y_prompt, y_sample = _encoder_layer(
            x3, (p_prompt[i], p_sample[i]), norm_mix_g[i], w_in[i], gate_b[i], q_norm_g[i], k_norm_g[i],
            hy_conv_w[i], hy_conv_b[i], hy_w1[i], hy_b1[i], hy_w2[i], hy_b2[i], hy_sin_freq[i], hy_w3[i],
            hy_bias[i], w_att_out[i], w_hy_out[i], w_o[i], norm_mlp_g[i], w_up[i], w_down[i],
            norm_ple_g[i], w_ple_gate[i], w_ple_proj[i])
    return (y_prompt, y_sample)
```

```python
import functools
import math

import jax
import jax.numpy as jnp
from jax import lax
from jax.experimental import pallas as pl
from jax.experimental.pallas import tpu as pltpu

F32 = jnp.float32
BF16 = jnp.bfloat16
HIGHEST = lax.Precision.HIGHEST

EPS = 1e-6
HEAD_DIM = 128
ROPE_DIM = HEAD_DIM // 4
ROPE_THETA = 500000.0
DILATED_PATTERNS = ((128, 1), (512, 4), (2048, 16))
ATT_RADIUS = 64
ATT_QBLK = 128
ATT_KWIN = ATT_QBLK + 2 * ATT_RADIUS
ATT_GROUP = 8
HY_ORDER = 2
HY_EMB_DIM = 33
HY_BANDS = (HY_EMB_DIM - 1) // 2
HY_FAST_DECAY = 0.3
HY_SLOW_DECAY = 1.5
HY_DECAY_TARGET = 1e-2
HY_MIN_DECAY = math.log(HY_DECAY_TARGET) / HY_SLOW_DECAY
HY_MAX_DECAY = math.log(HY_DECAY_TARGET) / HY_FAST_DECAY
N_BRANCH = 2
LANES = 128
MASK_NEG = -1e30

VMEM_CAP_BYTES = 60 * 1024 * 1024


def _cparams(sem, est_bytes):
    limit = int(min(VMEM_CAP_BYTES, max(32 * 1024 * 1024, est_bytes)))
    return pltpu.CompilerParams(dimension_semantics=sem, vmem_limit_bytes=limit)


def _nbytes(shape, dtype):
    return math.prod(shape) * jnp.dtype(dtype).itemsize


def _two_group_specs(block, na, col=lambda *j: 0):
    return (pl.BlockSpec(block, lambda i, *j: (jnp.minimum(i, na - 1), jnp.where(i < na, col(*j), 0))),
            pl.BlockSpec(block, lambda i, *j: (jnp.maximum(i - na, 0), jnp.where(i >= na, col(*j), 0))))


def _pick_group(na, xa_ref, xb_ref, body):
    i = pl.program_id(0)
    pl.when(i < na)(lambda: body(xa_ref))
    pl.when(i >= na)(lambda: body(xb_ref))


def _rmsnorm_kernel(xa_ref, xb_ref, g_ref, o_ref, *, na):
    def body(x_ref):
        x = x_ref[...]
        ms = jnp.mean(x * x, axis=-1, keepdims=True)
        o_ref[...] = (x * lax.rsqrt(ms + EPS) * g_ref[...]).astype(o_ref.dtype)

    _pick_group(na, xa_ref, xb_ref, body)


def _rmsnorm_cast(xa, xb, g, tm=512):
    d = xa.shape[1]
    assert xa.shape[0] % tm == 0 and xb.shape[0] % tm == 0
    na, t = xa.shape[0] // tm, xa.shape[0] + xb.shape[0]
    return pl.pallas_call(
        functools.partial(_rmsnorm_kernel, na=na),
        grid=(t // tm,),
        in_specs=[*_two_group_specs((tm, d), na), pl.BlockSpec((1, d), lambda i: (0, 0))],
        out_specs=pl.BlockSpec((tm, d), lambda i: (i, 0)),
        out_shape=jax.ShapeDtypeStruct((t, d), BF16),
        compiler_params=_cparams(("parallel",), 6 * _nbytes((tm, d), F32)),
        name="rmsnorm_cast",
    )(xa, xb, g.reshape(1, d))


def _inproj_qk_kernel(h_ref, w_ref, g_ref, c_ref, s1_ref, s2_ref, o_ref, *, scale):
    y = jnp.dot(h_ref[...], w_ref[...], preferred_element_type=F32)
    gs = g_ref[...] * scale
    pw = 2 * HEAD_DIM
    same_head = (lax.broadcasted_iota(jnp.int32, (pw, pw), 0) // HEAD_DIM
                 == lax.broadcasted_iota(jnp.int32, (pw, pw), 1) // HEAD_DIM)
    ones_bd = jnp.where(same_head, 1.0, 0.0).astype(BF16)
    yy = (y * y).astype(BF16)
    for hh in range(y.shape[1] // HEAD_DIM):
        sl = slice(hh * HEAD_DIM, (hh + 1) * HEAD_DIM)
        yh = y[:, sl]
        if hh % 2 == 0:
            ss_pair = jnp.dot(yy[:, hh * HEAD_DIM:hh * HEAD_DIM + pw], ones_bd, preferred_element_type=F32)
        ms = ss_pair[:, (hh % 2) * HEAD_DIM:(hh % 2 + 1) * HEAD_DIM] * (1.0 / HEAD_DIM)
        t = yh * lax.rsqrt(ms + EPS) * gs
        out = (t * c_ref[...]
               + pltpu.roll(t, HEAD_DIM - ROPE_DIM // 2, 1) * s1_ref[...]
               + pltpu.roll(t, ROPE_DIM // 2, 1) * s2_ref[...])
        o_ref[:, sl] = out.astype(o_ref.dtype)


def _inproj_plain_kernel(h_ref, w_ref, o_ref):
    o_ref[...] = jnp.dot(h_ref[...], w_ref[...], preferred_element_type=F32).astype(o_ref.dtype)


CONV_PAD_ROWS = 8


def _inproj_conv_kernel(h_ref, w_ref, cw_ref, cb_ref, o_ref, scr):
    y = jnp.dot(h_ref[...], w_ref[...], preferred_element_type=F32)
    s, tn = y.shape
    m = s // 2
    pad = CONV_PAD_ROWS
    zeros = jnp.zeros((pad, LANES), F32)
    for c in range(tn // LANES):
        scr[c, 0:pad, :] = zeros
        scr[c, pad + s:2 * pad + s, :] = zeros
        scr[c, pad:pad + s, :] = y[:, c * LANES:(c + 1) * LANES]
    for c in range(tn // LANES):
        cols = slice(c * LANES, (c + 1) * LANES)
        w0, w1, w2, cb = cw_ref[0:1, cols], cw_ref[1:2, cols], cw_ref[2:3, cols], cb_ref[:, cols]
        ye = scr[c, pl.ds(pad, m, stride=2), :]
        yo = scr[c, pl.ds(pad + 1, m, stride=2), :]
        yo_prev = scr[c, pl.ds(pad - 1, m, stride=2), :]
        ye_next = scr[c, pl.ds(pad + 2, m, stride=2), :]
        o_ref[0:m, cols] = (yo_prev * w0 + ye * w1 + yo * w2 + cb).astype(o_ref.dtype)
        o_ref[m:s, cols] = (ye * w0 + yo * w1 + ye_next * w2 + cb).astype(o_ref.dtype)


def _inproj_gate_kernel(h_ref, w_ref, gb_ref, o_ref):
    y = jnp.dot(h_ref[...], w_ref[...], preferred_element_type=F32)
    o_ref[...] = jax.nn.sigmoid(y + gb_ref[...]).astype(o_ref.dtype)


def _inproj(h3, w, col0, width, body, extras, extra_specs, tn, scratch=()):
    b, s, d = h3.shape
    tn = min(tn, width)
    nj = width // tn
    jb0 = col0 // tn
    est = (2 * _nbytes((s, d), BF16) + 2 * _nbytes((d, tn), BF16) + 2 * _nbytes((s, tn), BF16)
           + 4 * _nbytes((s, tn), F32) + (8 << 20))
    return pl.pallas_call(
        body,
        grid=(b, nj),
        in_specs=[pl.BlockSpec((None, s, d), lambda i, j: (i, 0, 0)),
                  pl.BlockSpec((d, tn), lambda i, j: (0, jb0 + j))] + extra_specs,
        out_specs=pl.BlockSpec((None, s, tn), lambda i, j: (i, 0, j)),
        out_shape=jax.ShapeDtypeStruct((b, s, width), BF16),
        scratch_shapes=list(scratch),
        compiler_params=_cparams(("parallel", "arbitrary"), est),
        name="inproj_" + getattr(body, "func", body).__name__,
    )(h3, w, *extras)


def _rope_tables(s):
    half = ROPE_DIM // 2
    inv_freq = ROPE_THETA ** (-jnp.arange(half, dtype=F32) / half)
    ang = jnp.arange(s, dtype=F32)[:, None] * inv_freq[None, :]
    cos, sin = jnp.cos(ang), jnp.sin(ang)
    c = jnp.concatenate([cos, cos, jnp.ones((s, HEAD_DIM - ROPE_DIM), F32)], axis=1)
    s1 = jnp.concatenate([-sin, jnp.zeros((s, HEAD_DIM - half), F32)], axis=1)
    s2 = jnp.concatenate([jnp.zeros((s, half), F32), sin, jnp.zeros((s, HEAD_DIM - ROPE_DIM), F32)], axis=1)
    return c, s1, s2


def _band_bias(off, kw):
    col = lax.broadcasted_iota(jnp.int32, (ATT_QBLK, kw), 1)
    row = lax.broadcasted_iota(jnp.int32, (ATT_QBLK, kw), 0)
    return jnp.where(jnp.abs(col - row + off) <= ATT_RADIUS, 0.0, MASK_NEG).astype(F32)


def _band_blocks(seg_len):
    kw = min(ATT_KWIN, seg_len)
    for bi in range(seg_len // ATT_QBLK):
        q0 = bi * ATT_QBLK
        yield q0, min(max(q0 - ATT_RADIUS, 0), seg_len - kw), kw


def _band_attend(blocks):
    scores = [lax.dot_general(qb, kb, (((1,), (1,)), ((), ())), preferred_element_type=F32) + bias
              for qb, kb, _, bias in blocks]
    maxes = [jnp.max(s, axis=-1, keepdims=True) for s in scores]
    probs = [jnp.exp2(s - m).astype(BF16) for s, m in zip(scores, maxes)]
    outs = [jnp.dot(p, blk[2], preferred_element_type=F32) for p, blk in zip(probs, blocks)]
    return [(oa[:, :HEAD_DIM], oa[:, HEAD_DIM:], jnp.broadcast_to(m, (ATT_QBLK, HEAD_DIM)))
            for oa, m in zip(outs, maxes)]


def _softmax_merge(a, b):
    (o1, l1, m1), (o2, l2, m2) = a, b
    m = jnp.maximum(m1, m2)
    a1 = jnp.exp2(m1 - m)
    a2 = jnp.exp2(m2 - m)
    return a1 * o1 + a2 * o2, a1 * l1 + a2 * l2, m


def _attn_kernel(q_ref, k_ref, v_ref, o_ref,
                 qf0, kf0, vf0, qf1, kf1, vf1, qb1, kb1, qb2, kb2, va0, va1, va2,
                 ao, al, am, bo, bl, bm):
    s = q_ref.shape[0]
    c = HEAD_DIM
    n1 = s // 4
    n2 = s // 16
    ones = jnp.ones((s, c), BF16)
    biases = {}

    def bias(off, kw):
        if (off, kw) not in biases:
            biases[(off, kw)] = _band_bias(off, kw)
        return biases[(off, kw)]

    def grouped(items, compute, store, group=ATT_GROUP):
        for g0 in range(0, len(items), group):
            chunk = items[g0:g0 + group]
            for it, val in zip(chunk, compute(chunk)):
                store(*it, val)

    def each(fn):
        return lambda chunk: [fn(*it) for it in chunk]

    qf0[...] = q_ref[...].astype(F32)
    kf0[...] = k_ref[...].astype(F32)
    vf0[...] = v_ref[...].astype(F32)
    va0[:, :c] = v_ref[...]
    va0[:, c:] = ones
    va1[:, c:] = ones
    va2[:, c:] = ones

    def split4_store(r, src, dstf, dstb, x):
        rows = slice(r * n1, (r + 1) * n1)
        dstf[rows, :] = x
        dstb[rows, 0:c] = x.astype(BF16)

    grouped([(r, src, dstf, dstb) for r in range(4)
             for src, dstf, dstb in ((qf0, qf1, qb1), (kf0, kf1, kb1), (vf0, vf1, va1))],
            each(lambda r, src, dstf, dstb: src[pl.ds(r, n1, stride=4), :]), split4_store)

    def split16_store(seg, srcf, dstb, x):
        dstb[seg * n2:(seg + 1) * n2, 0:c] = x

    grouped([(seg, srcf, dstb) for seg in range(16) for srcf, dstb in ((qf1, qb2), (kf1, kb2), (vf1, va2))],
            each(lambda seg, srcf, dstb: srcf[pl.ds((seg // 4) * n1 + seg % 4, n2, stride=4), :].astype(BF16)),
            split16_store)

    def attend(qb, kb, va, chunk):
        return _band_attend([(qb[base + q0:base + q0 + ATT_QBLK, :], kb[base + w0:base + w0 + kw, :],
                              va[base + w0:base + w0 + kw, :], bias(w0 - q0, kw)) for base, q0, w0, kw in chunk])

    def merged(cur, prev_refs, chunk):
        prev = [tuple(ref[base + q0:base + q0 + ATT_QBLK, :] for ref in prev_refs) for base, q0, _, _ in chunk]
        return [_softmax_merge(a, b) for a, b in zip(cur, prev)]

    def store16(base, q0, w0, kw, olm):
        seg = base // n2
        dst = pl.ds((seg // 4) * n1 + 4 * q0 + seg % 4, ATT_QBLK, stride=4)
        for ref, val in zip((ao, al, am), olm):
            ref[dst, :] = val

    grouped([(seg * n2, *blk) for seg in range(16) for blk in _band_blocks(n2)],
            lambda chunk: attend(qb2, kb2, va2, chunk), store16)

    def store4(base, q0, w0, kw, olm):
        dst = pl.ds(4 * q0 + base // n1, ATT_QBLK, stride=4)
        for ref, val in zip((bo, bl, bm), olm):
            ref[dst, :] = val

    grouped([(r4 * n1, *blk) for r4 in range(4) for blk in _band_blocks(n1)],
            lambda chunk: merged(attend(qb1, kb1, va1, chunk), (ao, al, am), chunk), store4)

    def store1(base, q0, w0, kw, olm):
        o, l, _ = olm
        o_ref[q0:q0 + ATT_QBLK, :] = (o / l).astype(o_ref.dtype)

    grouped([(0, *blk) for blk in _band_blocks(s)],
            lambda chunk: merged(attend(q_ref, k_ref, va0, chunk), (bo, bl, bm), chunk), store1)


def _dilated_attention(q, k, v):
    b, s, aw = q.shape
    assert tuple(w // (2 * d) for w, d in DILATED_PATTERNS) == (ATT_RADIUS,) * 3
    assert tuple(d for _, d in DILATED_PATTERNS) == (1, 4, 16)
    assert s % (16 * ATT_QBLK) == 0
    c = HEAD_DIM
    spec = pl.BlockSpec((None, s, c), lambda i, h: (i, 0, h))
    f32buf = pltpu.VMEM((s, c), F32)
    bf16buf = pltpu.VMEM((s, c), BF16)
    augbuf = pltpu.VMEM((s, 2 * c), BF16)
    scratch = [f32buf] * 6 + [bf16buf] * 4 + [augbuf] * 3 + [f32buf] * 6
    est = 12 * _nbytes((s, c), F32) + 10 * _nbytes((s, c), BF16) + 8 * _nbytes((s, c), BF16) + (16 << 20)
    return pl.pallas_call(
        _attn_kernel,
        grid=(b, aw // c),
        in_specs=[spec, spec, spec],
        out_specs=spec,
        out_shape=jax.ShapeDtypeStruct((b, s, aw), BF16),
        scratch_shapes=scratch,
        compiler_params=_cparams(("parallel", "arbitrary"), est),
        name="dilated_attention",
    )(q, k, v)


def _filter_kernel(z_ref, w1_ref, b1_ref, w2_ref, b2_ref, fr_ref, w3_ref, t_ref, dl_ref,
                   filt_ref, asum_ref, hdn_ref, *, blocks_per_dir):
    @pl.when(pl.program_id(0) == 0)
    def _():
        fr = fr_ref[...]
        hdn = jnp.sin(fr * (jnp.dot(z_ref[...], w1_ref[...], precision=HIGHEST,
                                    preferred_element_type=F32) + b1_ref[...]))
        hdn_ref[...] = jnp.sin(fr * (jnp.dot(hdn, w2_ref[...], precision=HIGHEST,
                                             preferred_element_type=F32) + b2_ref[...]))

    hf = jnp.dot(hdn_ref[...], w3_ref[...], precision=HIGHEST, preferred_element_type=F32)
    hf = hf * jnp.exp(-t_ref[...] * dl_ref[...])
    is_bwd = (pl.program_id(0) // blocks_per_dir) % 2 == 1
    row = lax.broadcasted_iota(jnp.int32, hf.shape, 0)
    hf = jnp.where(jnp.logical_and(row == 0, is_bwd), 0.0, hf)
    filt_ref[...] = hf.astype(filt_ref.dtype)
    asum_ref[...] = jnp.sum(jnp.abs(hf), axis=0, keepdims=True)


def _even_odd_rows(a):
    return jnp.concatenate([a[0::2], a[1::2]], axis=0)


def _hyena_filter_taps(seq_len, w1, b1, w2, b2, sin_freq, w3, hy_w, tn=512):
    ffn = w1.shape[1]
    pad = LANES
    t = jnp.linspace(0.0, 1.0, seq_len, dtype=F32)[:, None]
    wpos = 2.0 * math.pi * jnp.arange(seq_len, dtype=F32) / seq_len
    bands = jnp.linspace(1e-4, HY_BANDS - 1, HY_BANDS, dtype=F32)
    ang = wpos[:, None] * bands[None, :]
    z = jnp.concatenate([t, jnp.cos(ang), -jnp.sin(ang)], axis=-1)
    z = _even_odd_rows(jnp.pad(z, ((0, 0), (0, pad - z.shape[1]))))
    t = _even_odd_rows(t)
    w1p = jnp.pad(w1.astype(F32), ((0, pad - w1.shape[0]), (0, pad - ffn)))
    w2p = jnp.pad(w2.astype(F32), ((0, pad - ffn), (0, pad - ffn)))
    w3p = jnp.pad(w3.astype(F32), ((0, pad - ffn), (0, 0)))
    padv = lambda a: jnp.pad(a.astype(F32), (0, pad - ffn)).reshape(1, pad)
    deltas = jnp.abs(jnp.linspace(HY_MIN_DECAY, HY_MAX_DECAY, hy_w, dtype=F32)).reshape(1, hy_w)
    ncol = w3.shape[1]
    tn = min(tn, hy_w)
    bpd = hy_w // tn
    full = lambda shape: pl.BlockSpec(shape, lambda j: (0, 0))
    return pl.pallas_call(
        functools.partial(_filter_kernel, blocks_per_dir=bpd),
        grid=(ncol // tn,),
        in_specs=[full((seq_len, pad)), full((pad, pad)), full((1, pad)), full((pad, pad)), full((1, pad)),
                  full((1, pad)), pl.BlockSpec((pad, tn), lambda j: (0, j)), full((seq_len, 1)),
                  pl.BlockSpec((1, tn), lambda j: (0, j % bpd))],
        out_specs=[pl.BlockSpec((seq_len, tn), lambda j: (0, j)), pl.BlockSpec((1, tn), lambda j: (0, j))],
        out_shape=[jax.ShapeDtypeStruct((seq_len, ncol), BF16), jax.ShapeDtypeStruct((1, ncol), F32)],
        scratch_shapes=[pltpu.VMEM((seq_len, pad), F32)],
        compiler_params=_cparams(("arbitrary",), 32 << 20),
        name="hyena_filter_taps",
    )(z, w1p, padv(b1), w2p, padv(b2), padv(sin_freq), w3p, t, deltas)


def _dft_matrices(s, n_total, tc):
    n = 2 * s
    k = jnp.arange(s, dtype=jnp.int32)

    def tables(cols):
        ang = ((k[:, None] * cols[None, :]) % n).astype(F32) * (2.0 * math.pi / n)
        return jnp.cos(ang), jnp.sin(ang)

    cl, sl = tables(jnp.arange(LANES, dtype=jnp.int32))
    ch, sh = tables(jnp.arange(s // LANES, dtype=jnp.int32) * LANES)
    cosm = (ch[:, :, None] * cl[:, None, :] - sh[:, :, None] * sl[:, None, :]).reshape(s, s)
    sinm = (sh[:, :, None] * cl[:, None, :] + ch[:, :, None] * sl[:, None, :]).reshape(s, s)
    alt = jnp.where(k % 2 == 0, 1.0, -1.0).astype(F32)
    im = (-sinm).at[0].set(alt)
    fwd = jnp.concatenate([cosm, im], axis=0)
    scale = jnp.full((n,), 2.0 / n_total, F32).at[0].set(1.0 / n_total).at[s].set(1.0 / n_total)
    inv = (fwd * scale[:, None]).T
    theta = k.astype(F32) * (2.0 * math.pi / n_total)
    cw = jnp.broadcast_to(jnp.cos(theta)[:, None], (s, tc))
    sw = jnp.broadcast_to(jnp.sin(theta)[:, None], (s, tc))
    return fwd.astype(BF16), inv.astype(BF16), cw, sw


def _half_spectrum(z_ref, f_ref, cw, sw, first):
    m = z_ref.shape[0] // 2
    ef = jnp.dot(f_ref[...], z_ref[0:m, :], preferred_element_type=F32)
    of = jnp.dot(f_ref[...], z_ref[m:2 * m, :], preferred_element_type=F32)
    er, ei, orr, oi = ef[:m], ef[m:], of[:m], of[m:]
    tr = orr * cw + oi * sw
    ti = oi * cw - orr * sw
    return er + tr, jnp.where(first, ei, ei + ti), er - tr, jnp.where(first, -oi, ti - ei)


def _spectrum_product(h, z, first):
    hpre, hpim, hqre, hqim = h
    zpre, zpim, zqre, zqim = z
    a = hpim * zpim
    b = hqim * zqim
    ypre = hpre * zpre - jnp.where(first, 0.0, a)
    ypim = jnp.where(first, a - b, hpre * zpim + hpim * zpre)
    yqre = hqre * zqre - jnp.where(first, 0.0, b)
    yqim = jnp.where(first, hpim * zqim + hqim * zpim, hqre * zqim + hqim * zqre)
    return ypre, ypim, yqre, yqim


def _half_inverse(y, inv_ref, cw, sw, first):
    ypre, ypim, yqre, yqim = y
    gre = ypre + yqre
    gim = jnp.where(first, 2.0 * ypim, ypim - yqim)
    dre = ypre - yqre
    dim = ypim + yqim
    g2re = dre * cw - dim * sw
    g2im = jnp.where(first, -2.0 * yqim, dre * sw + dim * cw)
    dt = inv_ref.dtype
    ge = jnp.concatenate([gre.astype(dt), gim.astype(dt)], axis=0)
    go = jnp.concatenate([g2re.astype(dt), g2im.astype(dt)], axis=0)
    return (jnp.dot(inv_ref[...], ge, preferred_element_type=F32),
            jnp.dot(inv_ref[...], go, preferred_element_type=F32))


def _filter_spectrum_kernel(tf_ref, tb_ref, nf_ref, nb_ref, f_ref, cw_ref, sw_ref, o_ref):
    cw, sw = cw_ref[...], sw_ref[...]
    first = lax.broadcasted_iota(jnp.int32, cw.shape, 0) == 0
    fwd = _half_spectrum(tf_ref, f_ref, cw, sw, first)
    bwd = _half_spectrum(tb_ref, f_ref, cw, sw, first)
    inv = 1.0 / (nf_ref[...] + nb_ref[...])
    o_ref[0] = (fwd[0] + bwd[0]) * inv
    o_ref[1] = (fwd[1] + jnp.where(first, bwd[1], -bwd[1])) * inv
    o_ref[2] = (fwd[2] + bwd[2]) * inv
    o_ref[3] = (fwd[3] - bwd[3]) * inv


def _filter_spectrum(taps, asum, fwd, cw, sw, hy_w):
    l = taps.shape[0]
    m, tc = cw.shape
    nb = hy_w // tc
    col = lambda d: (lambda o, c: (0, (2 * o + d) * nb + c))
    const = lambda shape: pl.BlockSpec(shape, lambda o, c: (0,) * len(shape), pipeline_mode=pl.Buffered(1))
    est = (_nbytes((2 * m, m), BF16) + 4 * _nbytes((l, tc), BF16) + 2 * _nbytes((m, tc), F32)
           + 32 * _nbytes((m, tc), F32) + (8 << 20))
    return pl.pallas_call(
        _filter_spectrum_kernel,
        grid=(HY_ORDER, nb),
        in_specs=[pl.BlockSpec((l, tc), col(0)), pl.BlockSpec((l, tc), col(1)),
                  pl.BlockSpec((1, tc), col(0)), pl.BlockSpec((1, tc), col(1)),
                  const((2 * m, m)), const((m, tc)), const((m, tc))],
        out_specs=pl.BlockSpec((None, 4, m, tc), lambda o, c: (o, 0, 0, c)),
        out_shape=jax.ShapeDtypeStruct((HY_ORDER, 4, m, hy_w), F32),
        compiler_params=_cparams(("parallel", "arbitrary"), est),
        name="hyena_filter_spectrum",
    )(taps, taps, asum, asum, fwd, cw, sw)


def _hyena_conv_kernel(z_ref, g_ref, h_ref, b_ref, f_ref, inv_ref, cw_ref, sw_ref, o_ref, *scratch, natural_out):
    s, tc = z_ref.shape
    m = s // 2
    cw, sw = cw_ref[...], sw_ref[...]
    first = lax.broadcasted_iota(jnp.int32, (m, tc), 0) == 0
    spec = _half_spectrum(z_ref, f_ref, cw, sw, first)
    prod = _spectrum_product((h_ref[0], h_ref[1], h_ref[2], h_ref[3]), spec, first)
    conv_e, conv_o = _half_inverse(prod, inv_ref, cw, sw, first)
    bias = b_ref[...]
    out_e = g_ref[0:m, :].astype(F32) * (conv_e + bias * z_ref[0:m, :].astype(F32))
    out_o = g_ref[m:s, :].astype(F32) * (conv_o + bias * z_ref[m:s, :].astype(F32))
    if natural_out:
        (nat,) = scratch
        for c in range(tc // LANES):
            cols = slice(c * LANES, (c + 1) * LANES)
            nat[c, pl.ds(0, m, stride=2), :] = out_e[:, cols]
            nat[c, pl.ds(1, m, stride=2), :] = out_o[:, cols]
        for c in range(tc // LANES):
            o_ref[:, c * LANES:(c + 1) * LANES] = nat[c].astype(o_ref.dtype)
    else:
        o_ref[0:m, :] = out_e.astype(o_ref.dtype)
        o_ref[m:s, :] = out_o.astype(o_ref.dtype)


def _hyena_conv(z, z_col0, gate, gate_col0, filt, order, bias_row, mats, width, natural_out):
    fwd, inv, cw, sw = mats
    b, s, _ = z.shape
    m, tc = cw.shape
    nc = width // tc
    zb0, gb0 = z_col0 // tc, gate_col0 // tc
    const = lambda shape: pl.BlockSpec(shape, lambda c, i: (0,) * len(shape), pipeline_mode=pl.Buffered(1))
    scratch = [pltpu.VMEM((tc // LANES, s, LANES), F32)] if natural_out else []
    est = (2 * _nbytes((2 * m, m), BF16) + 2 * _nbytes((m, tc), F32) + 2 * _nbytes((4, m, tc), F32)
           + 6 * _nbytes((s, tc), BF16) + 24 * _nbytes((m, tc), F32) + (8 << 20))
    return pl.pallas_call(
        functools.partial(_hyena_conv_kernel, natural_out=natural_out),
        grid=(nc, b),
        in_specs=[pl.BlockSpec((None, s, tc), lambda c, i: (i, 0, zb0 + c)),
                  pl.BlockSpec((None, s, tc), lambda c, i: (i, 0, gb0 + c)),
                  pl.BlockSpec((None, 4, m, tc), lambda c, i: (order, 0, 0, c)),
                  pl.BlockSpec((1, tc), lambda c, i: (0, c)),
                  const((2 * m, m)), const((m, 2 * m)), const((m, tc)), const((m, tc))],
        out_specs=pl.BlockSpec((None, s, tc), lambda c, i: (i, 0, c)),
        out_shape=jax.ShapeDtypeStruct((b, s, width), BF16),
        scratch_shapes=scratch,
        compiler_params=_cparams(("parallel", "arbitrary"), est),
        name="hyena_conv",
    )(z, gate, filt, bias_row, fwd, inv, cw, sw)


def _merge_kernel(a_ref, hy_ref, wa_ref, wh_ref, g0_ref, g1_ref, o_ref):
    ya = jnp.dot(a_ref[...], wa_ref[...], preferred_element_type=F32)
    yh = jnp.dot(hy_ref[...], wh_ref[...], preferred_element_type=F32)
    o_ref[...] = (g0_ref[...].astype(F32) * ya + g1_ref[...].astype(F32) * yh).astype(o_ref.dtype)


def _gated_merge(attn, hy, w_att, w_hy, gates, tm=1024, tn=512):
    t, k = attn.shape
    n = w_att.shape[1]
    tm, tn = min(tm, t), min(tn, n)
    g1b0 = n // tn
    est = 4 * _nbytes((tm, k), BF16) + 4 * _nbytes((k, tn), BF16) + 6 * _nbytes((tm, tn), F32) + (8 << 20)
    return pl.pallas_call(
        _merge_kernel,
        grid=(t // tm, n // tn),
        in_specs=[pl.BlockSpec((tm, k), lambda i, j: (i, 0)),
                  pl.BlockSpec((tm, k), lambda i, j: (i, 0)),
                  pl.BlockSpec((k, tn), lambda i, j: (0, j)),
                  pl.BlockSpec((k, tn), lambda i, j: (0, j)),
                  pl.BlockSpec((tm, tn), lambda i, j: (i, j)),
                  pl.BlockSpec((tm, tn), lambda i, j: (i, g1b0 + j))],
        out_specs=pl.BlockSpec((tm, tn), lambda i, j: (i, j)),
        out_shape=jax.ShapeDtypeStruct((t, n), BF16),
        compiler_params=_cparams(("parallel", "arbitrary"), est),
        name="gated_merge",
    )(attn, hy, w_att, w_hy, gates, gates)


def _mm_residual_kernel(a_ref, w_ref, ra_ref, rb_ref, o_ref, *, na):
    y = jnp.dot(a_ref[...], w_ref[...], preferred_element_type=F32)

    def body(r_ref):
        o_ref[...] = r_ref[...] + y

    _pick_group(na, ra_ref, rb_ref, body)


def _matmul_residual(a, w, res_a, res_b, tm=1024, tn=1024):
    t, k = a.shape
    n = w.shape[1]
    tm, tn = min(tm, res_a.shape[0]), min(tn, n)
    assert res_a.shape[0] % tm == 0 and res_b.shape[0] % tm == 0
    na = res_a.shape[0] // tm
    est = 2 * _nbytes((tm, k), BF16) + 2 * _nbytes((k, tn), BF16) + 8 * _nbytes((tm, tn), F32) + (8 << 20)
    return pl.pallas_call(
        functools.partial(_mm_residual_kernel, na=na),
        grid=(t // tm, n // tn),
        in_specs=[pl.BlockSpec((tm, k), lambda i, j: (i, 0)),
                  pl.BlockSpec((k, tn), lambda i, j: (0, j)),
                  *_two_group_specs((tm, tn), na, col=lambda j: j)],
        out_specs=pl.BlockSpec((tm, tn), lambda i, j: (i, j)),
        out_shape=jax.ShapeDtypeStruct((t, n), F32),
        compiler_params=_cparams(("parallel", "arbitrary"), est),
        name="matmul_residual",
    )(a, w, res_a, res_b)


def _mlp_kernel(x_ref, g_ref, wu_ref, wd_ref, o_ref, hm_ref):
    @pl.when(pl.program_id(1) == 0)
    def _():
        x = x_ref[...]
        ms = jnp.mean(x * x, axis=-1, keepdims=True)
        hm_ref[...] = (x * lax.rsqrt(ms + EPS) * g_ref[...]).astype(hm_ref.dtype)
        o_ref[...] = x

    a = jnp.dot(hm_ref[...], wu_ref[...], preferred_element_type=F32)
    a = jnp.square(jnp.maximum(a, 0.0)).astype(BF16)
    o_ref[...] += jnp.dot(a, wd_ref[...], preferred_element_type=F32)


def _relu2_mlp(x, g, w_up, w_down, tm=1024, tf=512):
    t, d = x.shape
    ff = w_up.shape[1]
    tm, tf = min(tm, t), min(tf, ff)
    est = (4 * _nbytes((tm, d), F32) + _nbytes((tm, d), BF16) + 4 * _nbytes((d, tf), BF16)
           + 3 * _nbytes((tm, tf), F32) + (8 << 20))
    return pl.pallas_call(
        _mlp_kernel,
        grid=(t // tm, ff // tf),
        in_specs=[pl.BlockSpec((tm, d), lambda i, j: (i, 0)),
                  pl.BlockSpec((1, d), lambda i, j: (0, 0)),
                  pl.BlockSpec((d, tf), lambda i, j: (0, j)),
                  pl.BlockSpec((tf, d), lambda i, j: (j, 0))],
        out_specs=pl.BlockSpec((tm, d), lambda i, j: (i, 0)),
        out_shape=jax.ShapeDtypeStruct((t, d), F32),
        scratch_shapes=[pltpu.VMEM((tm, d), BF16)],
        compiler_params=_cparams(("parallel", "arbitrary"), est),
        name="relu2_mlp",
    )(x, g.reshape(1, d), w_up, w_down)


def _ple_kernel(x_ref, g_ref, wg_ref, p_ref, wp_ref, o_ref, hn_ref):
    j = pl.program_id(1)

    @pl.when(j == 0)
    def _():
        x = x_ref[...]
        ms = jnp.mean(x * x, axis=-1, keepdims=True)
        hn_ref[...] = (x * lax.rsqrt(ms + EPS) * g_ref[...]).astype(hn_ref.dtype)

    tn = o_ref.shape[1]
    gate = jax.nn.sigmoid(jnp.dot(hn_ref[...], wg_ref[...], preferred_element_type=F32))
    proj = jnp.dot(p_ref[...].astype(BF16), wp_ref[...], preferred_element_type=F32)
    o_ref[...] = x_ref[:, pl.ds(pl.multiple_of(j * tn, tn), tn)] + gate * proj


def _gated_ple(x, row0, p, g, w_gate, w_proj, tm=1024, tn=512):
    t, pd = p.shape
    d = x.shape[1]
    tm, tn = min(tm, t), min(tn, d)
    rb0 = row0 // tm
    est = (2 * _nbytes((tm, d), F32) + _nbytes((tm, d), BF16) + 2 * _nbytes((d, tn), BF16)
           + 8 * _nbytes((tm, tn), F32) + (8 << 20))
    return pl.pallas_call(
        _ple_kernel,
        grid=(t // tm, d // tn),
        in_specs=[pl.BlockSpec((tm, d), lambda i, j: (rb0 + i, 0)),
                  pl.BlockSpec((1, d), lambda i, j: (0, 0)),
                  pl.BlockSpec((d, tn), lambda i, j: (0, j)),
                  pl.BlockSpec((tm, pd), lambda i, j: (i, 0)),
                  pl.BlockSpec((pd, tn), lambda i, j: (0, j))],
        out_specs=pl.BlockSpec((tm, tn), lambda i, j: (i, j)),
        out_shape=jax.ShapeDtypeStruct((t, d), F32),
        scratch_shapes=[pltpu.VMEM((tm, d), BF16)],
        compiler_params=_cparams(("parallel", "arbitrary"), est),
        name="gated_ple",
    )(x, g.reshape(1, d), w_gate, p, w_proj)


def _encoder_layer(x_list, p_list, norm_mix_g, w_in, gate_b, q_norm_g, k_norm_g, hy_conv_w, hy_conv_b,
                   hy_w1, hy_b1, hy_w2, hy_b2, hy_sin_freq, hy_w3, hy_bias,
                   w_att_out, w_hy_out, w_o, norm_mlp_g, w_up, w_down,
                   norm_ple_g, w_ple_gate, w_ple_proj):
    xa3, xb3 = x_list
    s, d = xa3.shape[1:]
    b = xa3.shape[0] + xb3.shape[0]
    att_w = w_att_out.shape[0]
    hy_w = w_hy_out.shape[0]
    xa, xb = xa3.reshape(-1, d), xb3.reshape(-1, d)
    bf = lambda a: a.astype(BF16)

    h3 = _rmsnorm_cast(xa, xb, norm_mix_g).reshape(b, s, d)
    w_in_b = bf(w_in)
    rope_c, rope_s1, rope_s2 = _rope_tables(s)
    tab_spec = pl.BlockSpec((s, HEAD_DIM), lambda i, j: (0, 0))
    vec_spec = lambda tn, blk0=0: pl.BlockSpec((1, tn), lambda i, j: (0, blk0 + j))
    head_spec = pl.BlockSpec((1, HEAD_DIM), lambda i, j: (0, 0))
    tn = 512
    qk = lambda g, col0, scale: _inproj(
        h3, w_in_b, col0, att_w, functools.partial(_inproj_qk_kernel, scale=scale),
        (g.reshape(1, HEAD_DIM), rope_c, rope_s1, rope_s2), [head_spec, tab_spec, tab_spec, tab_spec], tn)
    q = qk(q_norm_g, 0, HEAD_DIM ** -0.5 * math.log2(math.e))
    k = qk(k_norm_g, att_w, 1.0)
    v = _inproj(h3, w_in_b, 2 * att_w, att_w, _inproj_plain_kernel, (), [], tn)
    tnc = min(tn, hy_w)
    u = _inproj(h3, w_in_b, 3 * att_w, 3 * hy_w, _inproj_conv_kernel,
                (hy_conv_w, hy_conv_b.reshape(1, 3 * hy_w)),
                [pl.BlockSpec((3, tnc), lambda i, j: (0, j)), vec_spec(tnc)], tnc,
                scratch=[pltpu.VMEM((tnc // LANES, s + 2 * CONV_PAD_ROWS, LANES), F32)])
    tng = min(tn, d)
    gates = _inproj(h3, w_in_b, 3 * att_w + 3 * hy_w, N_BRANCH * d, _inproj_gate_kernel,
                    (gate_b.reshape(1, N_BRANCH * d),), [vec_spec(tng)], tng)

    attn = _dilated_attention(q, k, v)

    mats = _dft_matrices(s // 2, 2 * s, min(256, hy_w))
    taps, tap_asum = _hyena_filter_taps(s, hy_w1, hy_b1, hy_w2, hy_b2, hy_sin_freq, hy_w3, hy_w)
    filt = _filter_spectrum(taps, tap_asum, mats[0], mats[2], mats[3], hy_w)
    bias = hy_bias.astype(F32)
    z = _hyena_conv(u, 0, u, hy_w, filt, 0, bias[0:1], mats, hy_w, natural_out=False)
    hy = _hyena_conv(z, 0, u, 2 * hy_w, filt, 1, bias[1:2], mats, hy_w, natural_out=True)

    merged = _gated_merge(attn.reshape(b * s, att_w), hy.reshape(b * s, hy_w),
                          bf(w_att_out), bf(w_hy_out), gates.reshape(b * s, N_BRANCH * d))
    x2 = _matmul_residual(merged, bf(w_o), xa, xb)
    x2 = _relu2_mlp(x2, norm_mlp_g, bf(w_up), bf(w_down))

    outs, row0 = [], 0
    w_pg, w_pp = bf(w_ple_gate), bf(w_ple_proj)
    for p in p_list:
        pb = p.shape[0]
        p2 = p.reshape(pb * s, p.shape[-1])
        outs.append(_gated_ple(x2, row0, p2, norm_ple_g, w_pg, w_pp).reshape(pb, s, d))
        row0 += pb * s
    return outs


def kernel(x_prompt, x_sample, p_prompt, p_sample, norm_mix_g, w_in, gate_b, q_norm_g, k_norm_g, hy_conv_w, hy_conv_b, hy_w1, hy_b1, hy_w2, hy_b2, hy_sin_freq, hy_w3, hy_bias, w_att_out, w_hy_out, w_o, norm_mlp_g, w_up, w_down, norm_ple_g, w_ple_gate, w_ple_proj):
    assert x_prompt.shape[1:] == x_sample.shape[1:]
    y_prompt, y_sample = x_prompt, x_sample
    for i in range(w_in.shape[0]):
        y_prompt, y_sample = _encoder_layer(
            (y_prompt, y_sample), (p_prompt[i], p_sample[i]), norm_mix_g[i], w_in[i], gate_b[i], q_norm_g[i], k_norm_g[i],
            hy_conv_w[i], hy_conv_b[i], hy_w1[i], hy_b1[i], hy_w2[i], hy_b2[i], hy_sin_freq[i], hy_w3[i],
            hy_bias[i], w_att_out[i], w_hy_out[i], w_o[i], norm_mlp_g[i], w_up[i], w_down[i],
            norm_ple_g[i], w_ple_gate[i], w_ple_proj[i])
    return (y_prompt, y_sample)
```

```python
import functools
import math

import jax
import jax.numpy as jnp
from jax import lax
from jax.experimental import pallas as pl
from jax.experimental.pallas import tpu as pltpu

F32 = jnp.float32
BF16 = jnp.bfloat16
HIGHEST = lax.Precision.HIGHEST

EPS = 1e-6
HEAD_DIM = 128
ROPE_DIM = HEAD_DIM // 4
ROPE_THETA = 500000.0
DILATED_PATTERNS = ((128, 1), (512, 4), (2048, 16))
ATT_RADIUS = 64
ATT_QBLK = 128
ATT_KWIN = ATT_QBLK + 2 * ATT_RADIUS
ATT_GROUP = 8
HY_ORDER = 2
HY_EMB_DIM = 33
HY_BANDS = (HY_EMB_DIM - 1) // 2
HY_FAST_DECAY = 0.3
HY_SLOW_DECAY = 1.5
HY_DECAY_TARGET = 1e-2
HY_MIN_DECAY = math.log(HY_DECAY_TARGET) / HY_SLOW_DECAY
HY_MAX_DECAY = math.log(HY_DECAY_TARGET) / HY_FAST_DECAY
N_BRANCH = 2
LANES = 128
MASK_NEG = -1e30

VMEM_CAP_BYTES = 60 * 1024 * 1024


def _cparams(sem, est_bytes):
    limit = int(min(VMEM_CAP_BYTES, max(32 * 1024 * 1024, est_bytes)))
    return pltpu.CompilerParams(dimension_semantics=sem, vmem_limit_bytes=limit)


def _nbytes(shape, dtype):
    return math.prod(shape) * jnp.dtype(dtype).itemsize


def _two_group_specs(block, na, col=lambda *j: 0):
    return (pl.BlockSpec(block, lambda i, *j: (jnp.minimum(i, na - 1), jnp.where(i < na, col(*j), 0))),
            pl.BlockSpec(block, lambda i, *j: (jnp.maximum(i - na, 0), jnp.where(i >= na, col(*j), 0))))


def _pick_group(na, xa_ref, xb_ref, body):
    i = pl.program_id(0)
    pl.when(i < na)(lambda: body(xa_ref))
    pl.when(i >= na)(lambda: body(xb_ref))


def _rmsnorm_kernel(xa_ref, xb_ref, g_ref, o_ref, *, na):
    def body(x_ref):
        x = x_ref[...]
        ms = jnp.mean(x * x, axis=-1, keepdims=True)
        o_ref[...] = (x * lax.rsqrt(ms + EPS) * g_ref[...]).astype(o_ref.dtype)

    _pick_group(na, xa_ref, xb_ref, body)


def _rmsnorm_cast(xa, xb, g, tm=512):
    d = xa.shape[1]
    assert xa.shape[0] % tm == 0 and xb.shape[0] % tm == 0
    na, t = xa.shape[0] // tm, xa.shape[0] + xb.shape[0]
    return pl.pallas_call(
        functools.partial(_rmsnorm_kernel, na=na),
        grid=(t // tm,),
        in_specs=[*_two_group_specs((tm, d), na), pl.BlockSpec((1, d), lambda i: (0, 0))],
        out_specs=pl.BlockSpec((tm, d), lambda i: (i, 0)),
        out_shape=jax.ShapeDtypeStruct((t, d), BF16),
        compiler_params=_cparams(("parallel",), 6 * _nbytes((tm, d), F32)),
        name="rmsnorm_cast",
    )(xa, xb, g.reshape(1, d))


INPROJ_ROW_CHUNK = 256


def _row_chunks(h_ref, w_ref):
    for r0 in range(0, h_ref.shape[0], INPROJ_ROW_CHUNK):
        rows = slice(r0, r0 + INPROJ_ROW_CHUNK)
        yield rows, jnp.dot(h_ref[rows, :], w_ref[...], preferred_element_type=F32)


def _inproj_qk_kernel(h_ref, w_ref, g_ref, c_ref, s1_ref, s2_ref, o_ref, *, scale):
    gs = g_ref[...] * scale
    pw = 2 * HEAD_DIM
    same_head = (lax.broadcasted_iota(jnp.int32, (pw, pw), 0) // HEAD_DIM
                 == lax.broadcasted_iota(jnp.int32, (pw, pw), 1) // HEAD_DIM)
    ones_bd = jnp.where(same_head, 1.0, 0.0).astype(BF16)
    for rows, y in _row_chunks(h_ref, w_ref):
        yy = (y * y).astype(BF16)
        c, s1, s2 = c_ref[rows, :], s1_ref[rows, :], s2_ref[rows, :]
        for hh in range(y.shape[1] // HEAD_DIM):
            sl = slice(hh * HEAD_DIM, (hh + 1) * HEAD_DIM)
            if hh % 2 == 0:
                ss_pair = jnp.dot(yy[:, hh * HEAD_DIM:hh * HEAD_DIM + pw], ones_bd, preferred_element_type=F32)
            ms = ss_pair[:, (hh % 2) * HEAD_DIM:(hh % 2 + 1) * HEAD_DIM] * (1.0 / HEAD_DIM)
            t = y[:, sl] * lax.rsqrt(ms + EPS) * gs
            out = (t * c + pltpu.roll(t, HEAD_DIM - ROPE_DIM // 2, 1) * s1 + pltpu.roll(t, ROPE_DIM // 2, 1) * s2)
            o_ref[rows, sl] = out.astype(o_ref.dtype)


def _inproj_plain_kernel(h_ref, w_ref, o_ref):
    o_ref[...] = jnp.dot(h_ref[...], w_ref[...], preferred_element_type=F32).astype(o_ref.dtype)


CONV_PAD_ROWS = 8


def _inproj_conv_kernel(h_ref, w_ref, cw_ref, cb_ref, o_ref, scr):
    s, tn = h_ref.shape[0], w_ref.shape[1]
    m = s // 2
    half = INPROJ_ROW_CHUNK // 2
    pad = CONV_PAD_ROWS
    zeros = jnp.zeros((pad, LANES), F32)
    for c in range(tn // LANES):
        scr[c, 0:pad, :] = zeros
        scr[c, pad + s:2 * pad + s, :] = zeros

    def finish(r0):
        for c in range(tn // LANES):
            cols = slice(c * LANES, (c + 1) * LANES)
            w0, w1, w2, cb = cw_ref[0:1, cols], cw_ref[1:2, cols], cw_ref[2:3, cols], cb_ref[:, cols]
            ye = scr[c, pl.ds(pad + r0, half, stride=2), :]
            yo = scr[c, pl.ds(pad + r0 + 1, half, stride=2), :]
            yo_prev = scr[c, pl.ds(pad + r0 - 1, half, stride=2), :]
            ye_next = scr[c, pl.ds(pad + r0 + 2, half, stride=2), :]
            o_ref[r0 // 2:r0 // 2 + half, cols] = (yo_prev * w0 + ye * w1 + yo * w2 + cb).astype(o_ref.dtype)
            o_ref[m + r0 // 2:m + r0 // 2 + half, cols] = (ye * w0 + yo * w1 + ye_next * w2 + cb).astype(o_ref.dtype)

    for rows, y in _row_chunks(h_ref, w_ref):
        for c in range(tn // LANES):
            scr[c, pad + rows.start:pad + rows.stop, :] = y[:, c * LANES:(c + 1) * LANES]
        if rows.start > 0:
            finish(rows.start - INPROJ_ROW_CHUNK)
    finish(s - INPROJ_ROW_CHUNK)


def _inproj_gate_kernel(h_ref, w_ref, gb_ref, o_ref):
    for rows, y in _row_chunks(h_ref, w_ref):
        o_ref[rows, :] = jax.nn.sigmoid(y + gb_ref[...]).astype(o_ref.dtype)


def _inproj(h3, w, col0, width, body, extras, extra_specs, tn, scratch=()):
    b, s, d = h3.shape
    tn = min(tn, width)
    nj = width // tn
    jb0 = col0 // tn
    est = (2 * _nbytes((s, d), BF16) + 2 * _nbytes((d, tn), BF16) + 2 * _nbytes((s, tn), BF16)
           + 4 * _nbytes((s, tn), F32) + (8 << 20))
    return pl.pallas_call(
        body,
        grid=(b, nj),
        in_specs=[pl.BlockSpec((None, s, d), lambda i, j: (i, 0, 0)),
                  pl.BlockSpec((d, tn), lambda i, j: (0, jb0 + j))] + extra_specs,
        out_specs=pl.BlockSpec((None, s, tn), lambda i, j: (i, 0, j)),
        out_shape=jax.ShapeDtypeStruct((b, s, width), BF16),
        scratch_shapes=list(scratch),
        compiler_params=_cparams(("parallel", "arbitrary"), est),
        name="inproj_" + getattr(body, "func", body).__name__,
    )(h3, w, *extras)


def _rope_tables(s):
    half = ROPE_DIM // 2
    inv_freq = ROPE_THETA ** (-jnp.arange(half, dtype=F32) / half)
    ang = jnp.arange(s, dtype=F32)[:, None] * inv_freq[None, :]
    cos, sin = jnp.cos(ang), jnp.sin(ang)
    c = jnp.concatenate([cos, cos, jnp.ones((s, HEAD_DIM - ROPE_DIM), F32)], axis=1)
    s1 = jnp.concatenate([-sin, jnp.zeros((s, HEAD_DIM - half), F32)], axis=1)
    s2 = jnp.concatenate([jnp.zeros((s, half), F32), sin, jnp.zeros((s, HEAD_DIM - ROPE_DIM), F32)], axis=1)
    return c, s1, s2


def _band_bias(off, kw):
    col = lax.broadcasted_iota(jnp.int32, (ATT_QBLK, kw), 1)
    row = lax.broadcasted_iota(jnp.int32, (ATT_QBLK, kw), 0)
    return jnp.where(jnp.abs(col - row + off) <= ATT_RADIUS, 0.0, MASK_NEG).astype(F32)


def _band_blocks(seg_len):
    kw = min(ATT_KWIN, seg_len)
    for bi in range(seg_len // ATT_QBLK):
        q0 = bi * ATT_QBLK
        yield q0, min(max(q0 - ATT_RADIUS, 0), seg_len - kw), kw


def _band_attend(blocks):
    scores = [lax.dot_general(qb, kb, (((1,), (1,)), ((), ())), preferred_element_type=F32) + bias
              for qb, kb, _, bias in blocks]
    maxes = [jnp.max(s, axis=-1, keepdims=True) for s in scores]
    probs = [jnp.exp2(s - m).astype(BF16) for s, m in zip(scores, maxes)]
    outs = [jnp.dot(p, blk[2], preferred_element_type=F32) for p, blk in zip(probs, blocks)]
    return [(oa[:, :HEAD_DIM], oa[:, HEAD_DIM:], jnp.broadcast_to(m, (ATT_QBLK, HEAD_DIM)))
            for oa, m in zip(outs, maxes)]


def _softmax_merge(a, b):
    (o1, l1, m1), (o2, l2, m2) = a, b
    m = jnp.maximum(m1, m2)
    a1 = jnp.exp2(m1 - m)
    a2 = jnp.exp2(m2 - m)
    return a1 * o1 + a2 * o2, a1 * l1 + a2 * l2, m


def _attn_kernel(q_ref, k_ref, v_ref, o_ref,
                 qf0, kf0, vf0, qf1, kf1, vf1, qb1, kb1, qb2, kb2, va0, va1, va2,
                 ao, al, am, bo, bl, bm):
    s = q_ref.shape[0]
    c = HEAD_DIM
    n1 = s // 4
    n2 = s // 16
    ones = jnp.ones((s, c), BF16)
    biases = {}

    def bias(off, kw):
        if (off, kw) not in biases:
            biases[(off, kw)] = _band_bias(off, kw)
        return biases[(off, kw)]

    def grouped(items, compute, store, group=ATT_GROUP):
        for g0 in range(0, len(items), group):
            chunk = items[g0:g0 + group]
            for it, val in zip(chunk, compute(chunk)):
                store(*it, val)

    def each(fn):
        return lambda chunk: [fn(*it) for it in chunk]

    qf0[...] = q_ref[...].astype(F32)
    kf0[...] = k_ref[...].astype(F32)
    vf0[...] = v_ref[...].astype(F32)
    va0[:, :c] = v_ref[...]
    va0[:, c:] = ones
    va1[:, c:] = ones
    va2[:, c:] = ones

    def split4_store(r, src, dstf, dstb, x):
        rows = slice(r * n1, (r + 1) * n1)
        dstf[rows, :] = x
        dstb[rows, 0:c] = x.astype(BF16)

    grouped([(r, src, dstf, dstb) for r in range(4)
             for src, dstf, dstb in ((qf0, qf1, qb1), (kf0, kf1, kb1), (vf0, vf1, va1))],
            each(lambda r, src, dstf, dstb: src[pl.ds(r, n1, stride=4), :]), split4_store)

    def split16_store(seg, srcf, dstb, x):
        dstb[seg * n2:(seg + 1) * n2, 0:c] = x

    grouped([(seg, srcf, dstb) for seg in range(16) for srcf, dstb in ((qf1, qb2), (kf1, kb2), (vf1, va2))],
            each(lambda seg, srcf, dstb: srcf[pl.ds((seg // 4) * n1 + seg % 4, n2, stride=4), :].astype(BF16)),
            split16_store)

    def attend(qb, kb, va, chunk):
        return _band_attend([(qb[base + q0:base + q0 + ATT_QBLK, :], kb[base + w0:base + w0 + kw, :],
                              va[base + w0:base + w0 + kw, :], bias(w0 - q0, kw)) for base, q0, w0, kw in chunk])

    def merged(cur, prev_refs, chunk):
        prev = [tuple(ref[base + q0:base + q0 + ATT_QBLK, :] for ref in prev_refs) for base, q0, _, _ in chunk]
        return [_softmax_merge(a, b) for a, b in zip(cur, prev)]

    def store16(base, q0, w0, kw, olm):
        seg = base // n2
        dst = pl.ds((seg // 4) * n1 + 4 * q0 + seg % 4, ATT_QBLK, stride=4)
        for ref, val in zip((ao, al, am), olm):
            ref[dst, :] = val

    grouped([(seg * n2, *blk) for seg in range(16) for blk in _band_blocks(n2)],
            lambda chunk: attend(qb2, kb2, va2, chunk), store16)

    def store4(base, q0, w0, kw, olm):
        dst = pl.ds(4 * q0 + base // n1, ATT_QBLK, stride=4)
        for ref, val in zip((bo, bl, bm), olm):
            ref[dst, :] = val

    grouped([(r4 * n1, *blk) for r4 in range(4) for blk in _band_blocks(n1)],
            lambda chunk: merged(attend(qb1, kb1, va1, chunk), (ao, al, am), chunk), store4)

    def store1(base, q0, w0, kw, olm):
        o, l, _ = olm
        o_ref[q0:q0 + ATT_QBLK, :] = (o / l).astype(o_ref.dtype)

    grouped([(0, *blk) for blk in _band_blocks(s)],
            lambda chunk: merged(attend(q_ref, k_ref, va0, chunk), (bo, bl, bm), chunk), store1)


def _dilated_attention(q, k, v):
    b, s, aw = q.shape
    assert tuple(w // (2 * d) for w, d in DILATED_PATTERNS) == (ATT_RADIUS,) * 3
    assert tuple(d for _, d in DILATED_PATTERNS) == (1, 4, 16)
    assert s % (16 * ATT_QBLK) == 0
    c = HEAD_DIM
    spec = pl.BlockSpec((None, s, c), lambda i, h: (i, 0, h))
    f32buf = pltpu.VMEM((s, c), F32)
    bf16buf = pltpu.VMEM((s, c), BF16)
    augbuf = pltpu.VMEM((s, 2 * c), BF16)
    scratch = [f32buf] * 6 + [bf16buf] * 4 + [augbuf] * 3 + [f32buf] * 6
    est = 12 * _nbytes((s, c), F32) + 10 * _nbytes((s, c), BF16) + 8 * _nbytes((s, c), BF16) + (16 << 20)
    return pl.pallas_call(
        _attn_kernel,
        grid=(b, aw // c),
        in_specs=[spec, spec, spec],
        out_specs=spec,
        out_shape=jax.ShapeDtypeStruct((b, s, aw), BF16),
        scratch_shapes=scratch,
        compiler_params=_cparams(("parallel", "arbitrary"), est),
        name="dilated_attention",
    )(q, k, v)


def _filter_kernel(z_ref, w1_ref, b1_ref, w2_ref, b2_ref, fr_ref, w3_ref, t_ref, dl_ref,
                   filt_ref, asum_ref, hdn_ref, *, blocks_per_dir):
    @pl.when(pl.program_id(0) == 0)
    def _():
        fr = fr_ref[...]
        hdn = jnp.sin(fr * (jnp.dot(z_ref[...], w1_ref[...], precision=HIGHEST,
                                    preferred_element_type=F32) + b1_ref[...]))
        hdn_ref[...] = jnp.sin(fr * (jnp.dot(hdn, w2_ref[...], precision=HIGHEST,
                                             preferred_element_type=F32) + b2_ref[...]))

    hf = jnp.dot(hdn_ref[...], w3_ref[...], precision=HIGHEST, preferred_element_type=F32)
    hf = hf * jnp.exp(-t_ref[...] * dl_ref[...])
    is_bwd = (pl.program_id(0) // blocks_per_dir) % 2 == 1
    row = lax.broadcasted_iota(jnp.int32, hf.shape, 0)
    hf = jnp.where(jnp.logical_and(row == 0, is_bwd), 0.0, hf)
    filt_ref[...] = hf.astype(filt_ref.dtype)
    asum_ref[...] = jnp.sum(jnp.abs(hf), axis=0, keepdims=True)


def _even_odd_rows(a):
    return jnp.concatenate([a[0::2], a[1::2]], axis=0)


def _hyena_filter_taps(seq_len, w1, b1, w2, b2, sin_freq, w3, hy_w, tn=512):
    ffn = w1.shape[1]
    pad = LANES
    t = jnp.linspace(0.0, 1.0, seq_len, dtype=F32)[:, None]
    wpos = 2.0 * math.pi * jnp.arange(seq_len, dtype=F32) / seq_len
    bands = jnp.linspace(1e-4, HY_BANDS - 1, HY_BANDS, dtype=F32)
    ang = wpos[:, None] * bands[None, :]
    z = jnp.concatenate([t, jnp.cos(ang), -jnp.sin(ang)], axis=-1)
    z = _even_odd_rows(jnp.pad(z, ((0, 0), (0, pad - z.shape[1]))))
    t = _even_odd_rows(t)
    w1p = jnp.pad(w1.astype(F32), ((0, pad - w1.shape[0]), (0, pad - ffn)))
    w2p = jnp.pad(w2.astype(F32), ((0, pad - ffn), (0, pad - ffn)))
    w3p = jnp.pad(w3.astype(F32), ((0, pad - ffn), (0, 0)))
    padv = lambda a: jnp.pad(a.astype(F32), (0, pad - ffn)).reshape(1, pad)
    deltas = jnp.abs(jnp.linspace(HY_MIN_DECAY, HY_MAX_DECAY, hy_w, dtype=F32)).reshape(1, hy_w)
    ncol = w3.shape[1]
    tn = min(tn, hy_w)
    bpd = hy_w // tn
    full = lambda shape: pl.BlockSpec(shape, lambda j: (0, 0))
    return pl.pallas_call(
        functools.partial(_filter_kernel, blocks_per_dir=bpd),
        grid=(ncol // tn,),
        in_specs=[full((seq_len, pad)), full((pad, pad)), full((1, pad)), full((pad, pad)), full((1, pad)),
                  full((1, pad)), pl.BlockSpec((pad, tn), lambda j: (0, j)), full((seq_len, 1)),
                  pl.BlockSpec((1, tn), lambda j: (0, j % bpd))],
        out_specs=[pl.BlockSpec((seq_len, tn), lambda j: (0, j)), pl.BlockSpec((1, tn), lambda j: (0, j))],
        out_shape=[jax.ShapeDtypeStruct((seq_len, ncol), BF16), jax.ShapeDtypeStruct((1, ncol), F32)],
        scratch_shapes=[pltpu.VMEM((seq_len, pad), F32)],
        compiler_params=_cparams(("arbitrary",), 32 << 20),
        name="hyena_filter_taps",
    )(z, w1p, padv(b1), w2p, padv(b2), padv(sin_freq), w3p, t, deltas)


def _dft_matrices(s, n_total, tc):
    n = 2 * s
    k = jnp.arange(s, dtype=jnp.int32)

    def tables(cols):
        ang = ((k[:, None] * cols[None, :]) % n).astype(F32) * (2.0 * math.pi / n)
        return jnp.cos(ang), jnp.sin(ang)

    cl, sl = tables(jnp.arange(LANES, dtype=jnp.int32))
    ch, sh = tables(jnp.arange(s // LANES, dtype=jnp.int32) * LANES)
    cosm = (ch[:, :, None] * cl[:, None, :] - sh[:, :, None] * sl[:, None, :]).reshape(s, s)
    sinm = (sh[:, :, None] * cl[:, None, :] + ch[:, :, None] * sl[:, None, :]).reshape(s, s)
    alt = jnp.where(k % 2 == 0, 1.0, -1.0).astype(F32)
    im = (-sinm).at[0].set(alt)
    fwd = jnp.concatenate([cosm, im], axis=0)
    scale = jnp.full((n,), 2.0 / n_total, F32).at[0].set(1.0 / n_total).at[s].set(1.0 / n_total)
    inv = (fwd * scale[:, None]).T
    r = min(HY_FREQ_CHUNK, s)
    chunked = lambda a: a.reshape(2, s // r, r, -1).transpose(1, 0, 2, 3).reshape(n, -1)
    fwd = chunked(fwd)
    inv = chunked(inv.T).T
    theta = k.astype(F32) * (2.0 * math.pi / n_total)
    cw = jnp.broadcast_to(jnp.cos(theta)[:, None], (s, tc))
    sw = jnp.broadcast_to(jnp.sin(theta)[:, None], (s, tc))
    return fwd.astype(BF16), inv.astype(BF16), cw, sw


HY_FREQ_CHUNK = 256


def _freq_chunks(m):
    r = min(HY_FREQ_CHUNK, m)
    return [slice(r0, r0 + r) for r0 in range(0, m, r)]


def _first_row(shape):
    return lax.broadcasted_iota(jnp.int32, shape, 0) == 0


def _half_spectrum(z_ref, f_ref, cw_ref, sw_ref, rows):
    m = z_ref.shape[0] // 2
    r = rows.stop - rows.start
    frows = slice(2 * rows.start, 2 * rows.stop)
    ef = jnp.dot(f_ref[frows, :], z_ref[0:m, :], preferred_element_type=F32)
    of = jnp.dot(f_ref[frows, :], z_ref[m:2 * m, :], preferred_element_type=F32)
    er, ei, orr, oi = ef[:r], ef[r:], of[:r], of[r:]
    cw, sw = cw_ref[rows, :], sw_ref[rows, :]
    tr = orr * cw + oi * sw
    ti = oi * cw - orr * sw
    pim, qim = ei + ti, ti - ei
    if rows.start == 0:
        first = _first_row(pim.shape)
        pim, qim = jnp.where(first, ei, pim), jnp.where(first, -oi, qim)
    return er + tr, pim, er - tr, qim


def _spectrum_product(h, z, with_first):
    hpre, hpim, hqre, hqim = h
    zpre, zpim, zqre, zqim = z
    a = hpim * zpim
    b = hqim * zqim
    if not with_first:
        return hpre * zpre - a, hpre * zpim + hpim * zpre, hqre * zqre - b, hqre * zqim + hqim * zqre
    first = _first_row(a.shape)
    ypre = hpre * zpre - jnp.where(first, 0.0, a)
    ypim = jnp.where(first, a - b, hpre * zpim + hpim * zpre)
    yqre = hqre * zqre - jnp.where(first, 0.0, b)
    yqim = jnp.where(first, hpim * zqim + hqim * zpim, hqre * zqim + hqim * zqre)
    return ypre, ypim, yqre, yqim


def _inverse_butterfly(y, cw, sw, with_first, dtype):
    ypre, ypim, yqre, yqim = y
    gre = ypre + yqre
    gim = ypim - yqim
    dre = ypre - yqre
    dim = ypim + yqim
    g2re = dre * cw - dim * sw
    g2im = dre * sw + dim * cw
    if with_first:
        first = _first_row(gim.shape)
        gim, g2im = jnp.where(first, 2.0 * ypim, gim), jnp.where(first, -2.0 * yqim, g2im)
    return (jnp.concatenate([gre.astype(dtype), gim.astype(dtype)], axis=0),
            jnp.concatenate([g2re.astype(dtype), g2im.astype(dtype)], axis=0))


def _filter_spectrum_kernel(tf_ref, tb_ref, nf_ref, nb_ref, f_ref, cw_ref, sw_ref, o_ref):
    inv = 1.0 / (nf_ref[...] + nb_ref[...])
    for rows in _freq_chunks(cw_ref.shape[0]):
        fwd = _half_spectrum(tf_ref, f_ref, cw_ref, sw_ref, rows)
        bwd = _half_spectrum(tb_ref, f_ref, cw_ref, sw_ref, rows)
        bwd_pim = -bwd[1]
        if rows.start == 0:
            bwd_pim = jnp.where(_first_row(bwd_pim.shape), bwd[1], bwd_pim)
        o_ref[0, rows, :] = (fwd[0] + bwd[0]) * inv
        o_ref[1, rows, :] = (fwd[1] + bwd_pim) * inv
        o_ref[2, rows, :] = (fwd[2] + bwd[2]) * inv
        o_ref[3, rows, :] = (fwd[3] - bwd[3]) * inv


def _filter_spectrum(taps, asum, fwd, cw, sw, hy_w):
    l = taps.shape[0]
    m, tc = cw.shape
    nb = hy_w // tc
    col = lambda d: (lambda o, c: (0, (2 * o + d) * nb + c))
    const = lambda shape: pl.BlockSpec(shape, lambda o, c: (0,) * len(shape), pipeline_mode=pl.Buffered(1))
    est = (_nbytes((2 * m, m), BF16) + 4 * _nbytes((l, tc), BF16) + 2 * _nbytes((m, tc), F32)
           + 32 * _nbytes((m, tc), F32) + (8 << 20))
    return pl.pallas_call(
        _filter_spectrum_kernel,
        grid=(HY_ORDER, nb),
        in_specs=[pl.BlockSpec((l, tc), col(0)), pl.BlockSpec((l, tc), col(1)),
                  pl.BlockSpec((1, tc), col(0)), pl.BlockSpec((1, tc), col(1)),
                  const((2 * m, m)), const((m, tc)), const((m, tc))],
        out_specs=pl.BlockSpec((None, 4, m, tc), lambda o, c: (o, 0, 0, c)),
        out_shape=jax.ShapeDtypeStruct((HY_ORDER, 4, m, hy_w), F32),
        compiler_params=_cparams(("parallel", "arbitrary"), est),
        name="hyena_filter_spectrum",
    )(taps, taps, asum, asum, fwd, cw, sw)


def _hyena_conv_kernel(z_ref, g_ref, h_ref, b_ref, f_ref, inv_ref, cw_ref, sw_ref, o_ref, ge_ref, go_ref,
                       *scratch, natural_out):
    s, tc = z_ref.shape
    m = s // 2
    for rows in _freq_chunks(m):
        with_first = rows.start == 0
        spec = _half_spectrum(z_ref, f_ref, cw_ref, sw_ref, rows)
        prod = _spectrum_product(tuple(h_ref[i, rows, :] for i in range(4)), spec, with_first)
        ge, go = _inverse_butterfly(prod, cw_ref[rows, :], sw_ref[rows, :], with_first, ge_ref.dtype)
        ge_ref[2 * rows.start:2 * rows.stop, :] = ge
        go_ref[2 * rows.start:2 * rows.stop, :] = go
    bias = b_ref[...]
    for rows in _freq_chunks(m):
        odd = slice(m + rows.start, m + rows.stop)
        conv_e = jnp.dot(inv_ref[rows, :], ge_ref[...], preferred_element_type=F32)
        conv_o = jnp.dot(inv_ref[rows, :], go_ref[...], preferred_element_type=F32)
        out_e = g_ref[rows, :].astype(F32) * (conv_e + bias * z_ref[rows, :].astype(F32))
        out_o = g_ref[odd, :].astype(F32) * (conv_o + bias * z_ref[odd, :].astype(F32))
        if natural_out:
            (nat,) = scratch
            r = rows.stop - rows.start
            both = slice(2 * rows.start, 2 * rows.stop)
            for c in range(tc // LANES):
                cols = slice(c * LANES, (c + 1) * LANES)
                nat[c, pl.ds(2 * rows.start, r, stride=2), :] = out_e[:, cols]
                nat[c, pl.ds(2 * rows.start + 1, r, stride=2), :] = out_o[:, cols]
                o_ref[both, cols] = nat[c, both, :].astype(o_ref.dtype)
        else:
            o_ref[rows, :] = out_e.astype(o_ref.dtype)
            o_ref[odd, :] = out_o.astype(o_ref.dtype)


def _hyena_conv(z, z_col0, gate, gate_col0, filt, order, bias_row, mats, width, natural_out):
    fwd, inv, cw, sw = mats
    b, s, _ = z.shape
    m, tc = cw.shape
    nc = width // tc
    zb0, gb0 = z_col0 // tc, gate_col0 // tc
    const = lambda shape: pl.BlockSpec(shape, lambda c, i: (0,) * len(shape), pipeline_mode=pl.Buffered(1))
    scratch = [pltpu.VMEM((2 * m, tc), BF16)] * 2
    if natural_out:
        scratch.append(pltpu.VMEM((tc // LANES, s, LANES), F32))
    est = (2 * _nbytes((2 * m, m), BF16) + 2 * _nbytes((m, tc), F32) + 2 * _nbytes((4, m, tc), F32)
           + 6 * _nbytes((s, tc), BF16) + 24 * _nbytes((m, tc), F32) + (8 << 20))
    return pl.pallas_call(
        functools.partial(_hyena_conv_kernel, natural_out=natural_out),
        grid=(nc, b),
        in_specs=[pl.BlockSpec((None, s, tc), lambda c, i: (i, 0, zb0 + c)),
                  pl.BlockSpec((None, s, tc), lambda c, i: (i, 0, gb0 + c)),
                  pl.BlockSpec((None, 4, m, tc), lambda c, i: (order, 0, 0, c)),
                  pl.BlockSpec((1, tc), lambda c, i: (0, c)),
                  const((2 * m, m)), const((m, 2 * m)), const((m, tc)), const((m, tc))],
        out_specs=pl.BlockSpec((None, s, tc), lambda c, i: (i, 0, c)),
        out_shape=jax.ShapeDtypeStruct((b, s, width), BF16),
        scratch_shapes=scratch,
        compiler_params=_cparams(("parallel", "arbitrary"), est),
        name="hyena_conv",
    )(z, gate, filt, bias_row, fwd, inv, cw, sw)


def _chunk_rows(n_rows):
    return [slice(r0, r0 + INPROJ_ROW_CHUNK) for r0 in range(0, n_rows, INPROJ_ROW_CHUNK)]


def _merge_kernel(a_ref, hy_ref, wa_ref, wh_ref, g0_ref, g1_ref, o_ref):
    for rows in _chunk_rows(a_ref.shape[0]):
        ya = jnp.dot(a_ref[rows, :], wa_ref[...], preferred_element_type=F32)
        yh = jnp.dot(hy_ref[rows, :], wh_ref[...], preferred_element_type=F32)
        o_ref[rows, :] = (g0_ref[rows, :].astype(F32) * ya + g1_ref[rows, :].astype(F32) * yh).astype(o_ref.dtype)


def _gated_merge(attn, hy, w_att, w_hy, gates, tm=1024, tn=512):
    t, k = attn.shape
    n = w_att.shape[1]
    tm, tn = min(tm, t), min(tn, n)
    g1b0 = n // tn
    est = 4 * _nbytes((tm, k), BF16) + 4 * _nbytes((k, tn), BF16) + 6 * _nbytes((tm, tn), F32) + (8 << 20)
    return pl.pallas_call(
        _merge_kernel,
        grid=(t // tm, n // tn),
        in_specs=[pl.BlockSpec((tm, k), lambda i, j: (i, 0)),
                  pl.BlockSpec((tm, k), lambda i, j: (i, 0)),
                  pl.BlockSpec((k, tn), lambda i, j: (0, j)),
                  pl.BlockSpec((k, tn), lambda i, j: (0, j)),
                  pl.BlockSpec((tm, tn), lambda i, j: (i, j)),
                  pl.BlockSpec((tm, tn), lambda i, j: (i, g1b0 + j))],
        out_specs=pl.BlockSpec((tm, tn), lambda i, j: (i, j)),
        out_shape=jax.ShapeDtypeStruct((t, n), BF16),
        compiler_params=_cparams(("parallel", "arbitrary"), est),
        name="gated_merge",
    )(attn, hy, w_att, w_hy, gates, gates)


def _mm_residual_kernel(a_ref, w_ref, ra_ref, rb_ref, o_ref, *, na):
    def body(r_ref):
        for rows in _chunk_rows(a_ref.shape[0]):
            o_ref[rows, :] = r_ref[rows, :] + jnp.dot(a_ref[rows, :], w_ref[...], preferred_element_type=F32)

    _pick_group(na, ra_ref, rb_ref, body)


def _matmul_residual(a, w, res_a, res_b, tm=1024, tn=1024):
    t, k = a.shape
    n = w.shape[1]
    tm, tn = min(tm, res_a.shape[0]), min(tn, n)
    assert res_a.shape[0] % tm == 0 and res_b.shape[0] % tm == 0
    na = res_a.shape[0] // tm
    est = 2 * _nbytes((tm, k), BF16) + 2 * _nbytes((k, tn), BF16) + 8 * _nbytes((tm, tn), F32) + (8 << 20)
    return pl.pallas_call(
        functools.partial(_mm_residual_kernel, na=na),
        grid=(t // tm, n // tn),
        in_specs=[pl.BlockSpec((tm, k), lambda i, j: (i, 0)),
                  pl.BlockSpec((k, tn), lambda i, j: (0, j)),
                  *_two_group_specs((tm, tn), na, col=lambda j: j)],
        out_specs=pl.BlockSpec((tm, tn), lambda i, j: (i, j)),
        out_shape=jax.ShapeDtypeStruct((t, n), F32),
        compiler_params=_cparams(("parallel", "arbitrary"), est),
        name="matmul_residual",
    )(a, w, res_a, res_b)


def _mlp_kernel(x_ref, g_ref, wu_ref, wd_ref, o_ref, hm_ref):
    @pl.when(pl.program_id(1) == 0)
    def _():
        x = x_ref[...]
        ms = jnp.mean(x * x, axis=-1, keepdims=True)
        hm_ref[...] = (x * lax.rsqrt(ms + EPS) * g_ref[...]).astype(hm_ref.dtype)
        o_ref[...] = x

    for rows in _chunk_rows(hm_ref.shape[0]):
        a = jnp.dot(hm_ref[rows, :], wu_ref[...], preferred_element_type=F32)
        a = jnp.square(jnp.maximum(a, 0.0)).astype(BF16)
        o_ref[rows, :] += jnp.dot(a, wd_ref[...], preferred_element_type=F32)


def _relu2_mlp(x, g, w_up, w_down, tm=1024, tf=1024):
    t, d = x.shape
    ff = w_up.shape[1]
    tm, tf = min(tm, t), min(tf, ff)
    est = (4 * _nbytes((tm, d), F32) + _nbytes((tm, d), BF16) + 4 * _nbytes((d, tf), BF16)
           + 3 * _nbytes((tm, tf), F32) + (8 << 20))
    return pl.pallas_call(
        _mlp_kernel,
        grid=(t // tm, ff // tf),
        in_specs=[pl.BlockSpec((tm, d), lambda i, j: (i, 0), pipeline_mode=pl.Buffered(1)),
                  pl.BlockSpec((1, d), lambda i, j: (0, 0)),
                  pl.BlockSpec((d, tf), lambda i, j: (0, j)),
                  pl.BlockSpec((tf, d), lambda i, j: (j, 0))],
        out_specs=pl.BlockSpec((tm, d), lambda i, j: (i, 0)),
        out_shape=jax.ShapeDtypeStruct((t, d), F32),
        scratch_shapes=[pltpu.VMEM((tm, d), BF16)],
        compiler_params=_cparams(("parallel", "arbitrary"), est),
        name="relu2_mlp",
    )(x, g.reshape(1, d), w_up, w_down)


def _ple_kernel(x_ref, g_ref, wg_ref, p_ref, wp_ref, o_ref, hn_ref):
    j = pl.program_id(1)

    @pl.when(j == 0)
    def _():
        x = x_ref[...]
        ms = jnp.mean(x * x, axis=-1, keepdims=True)
        hn_ref[...] = (x * lax.rsqrt(ms + EPS) * g_ref[...]).astype(hn_ref.dtype)

    tn = o_ref.shape[1]
    cols = pl.ds(pl.multiple_of(j * tn, tn), tn)
    for rows in _chunk_rows(hn_ref.shape[0]):
        gate = jax.nn.sigmoid(jnp.dot(hn_ref[rows, :], wg_ref[...], preferred_element_type=F32))
        proj = jnp.dot(p_ref[rows, :].astype(BF16), wp_ref[...], preferred_element_type=F32)
        o_ref[rows, :] = x_ref[rows, cols] + gate * proj


def _gated_ple(x, row0, p, g, w_gate, w_proj, tm=1024, tn=1024):
    t, pd = p.shape
    d = x.shape[1]
    tm, tn = min(tm, t), min(tn, d)
    rb0 = row0 // tm
    est = (2 * _nbytes((tm, d), F32) + _nbytes((tm, d), BF16) + 2 * _nbytes((d, tn), BF16)
           + 8 * _nbytes((tm, tn), F32) + (8 << 20))
    return pl.pallas_call(
        _ple_kernel,
        grid=(t // tm, d // tn),
        in_specs=[pl.BlockSpec((tm, d), lambda i, j: (rb0 + i, 0)),
                  pl.BlockSpec((1, d), lambda i, j: (0, 0)),
                  pl.BlockSpec((d, tn), lambda i, j: (0, j)),
                  pl.BlockSpec((tm, pd), lambda i, j: (i, 0)),
                  pl.BlockSpec((pd, tn), lambda i, j: (0, j))],
        out_specs=pl.BlockSpec((tm, tn), lambda i, j: (i, j)),
        out_shape=jax.ShapeDtypeStruct((t, d), F32),
        scratch_shapes=[pltpu.VMEM((tm, d), BF16)],
        compiler_params=_cparams(("parallel", "arbitrary"), est),
        name="gated_ple",
    )(x, g.reshape(1, d), w_gate, p, w_proj)


def _encoder_layer(x_list, p_list, norm_mix_g, w_in, gate_b, q_norm_g, k_norm_g, hy_conv_w, hy_conv_b,
                   hy_w1, hy_b1, hy_w2, hy_b2, hy_sin_freq, hy_w3, hy_bias,
                   w_att_out, w_hy_out, w_o, norm_mlp_g, w_up, w_down,
                   norm_ple_g, w_ple_gate, w_ple_proj):
    xa3, xb3 = x_list
    s, d = xa3.shape[1:]
    b = xa3.shape[0] + xb3.shape[0]
    att_w = w_att_out.shape[0]
    hy_w = w_hy_out.shape[0]
    xa, xb = xa3.reshape(-1, d), xb3.reshape(-1, d)
    bf = lambda a: a.astype(BF16)

    h3 = _rmsnorm_cast(xa, xb, norm_mix_g).reshape(b, s, d)
    w_in_b = bf(w_in)
    rope_c, rope_s1, rope_s2 = _rope_tables(s)
    tab_spec = pl.BlockSpec((s, HEAD_DIM), lambda i, j: (0, 0))
    vec_spec = lambda tn, blk0=0: pl.BlockSpec((1, tn), lambda i, j: (0, blk0 + j))
    head_spec = pl.BlockSpec((1, HEAD_DIM), lambda i, j: (0, 0))
    tn = 1024
    qk = lambda g, col0, scale: _inproj(
        h3, w_in_b, col0, att_w, functools.partial(_inproj_qk_kernel, scale=scale),
        (g.reshape(1, HEAD_DIM), rope_c, rope_s1, rope_s2), [head_spec, tab_spec, tab_spec, tab_spec], tn)
    q = qk(q_norm_g, 0, HEAD_DIM ** -0.5 * math.log2(math.e))
    k = qk(k_norm_g, att_w, 1.0)
    v = _inproj(h3, w_in_b, 2 * att_w, att_w, _inproj_plain_kernel, (), [], tn)
    tnc = min(tn, hy_w)
    u = _inproj(h3, w_in_b, 3 * att_w, 3 * hy_w, _inproj_conv_kernel,
                (hy_conv_w, hy_conv_b.reshape(1, 3 * hy_w)),
                [pl.BlockSpec((3, tnc), lambda i, j: (0, j)), vec_spec(tnc)], tnc,
                scratch=[pltpu.VMEM((tnc // LANES, s + 2 * CONV_PAD_ROWS, LANES), F32)])
    tng = min(tn, d)
    gates = _inproj(h3, w_in_b, 3 * att_w + 3 * hy_w, N_BRANCH * d, _inproj_gate_kernel,
                    (gate_b.reshape(1, N_BRANCH * d),), [vec_spec(tng)], tng)

    attn = _dilated_attention(q, k, v)

    mats = _dft_matrices(s // 2, 2 * s, min(256, hy_w))
    taps, tap_asum = _hyena_filter_taps(s, hy_w1, hy_b1, hy_w2, hy_b2, hy_sin_freq, hy_w3, hy_w)
    filt = _filter_spectrum(taps, tap_asum, mats[0], mats[2], mats[3], hy_w)
    bias = hy_bias.astype(F32)
    z = _hyena_conv(u, 0, u, hy_w, filt, 0, bias[0:1], mats, hy_w, natural_out=False)
    hy = _hyena_conv(z, 0, u, 2 * hy_w, filt, 1, bias[1:2], mats, hy_w, natural_out=True)

    merged = _gated_merge(attn.reshape(b * s, att_w), hy.reshape(b * s, hy_w),
                          bf(w_att_out), bf(w_hy_out), gates.reshape(b * s, N_BRANCH * d))
    x2 = _matmul_residual(merged, bf(w_o), xa, xb)
    x2 = _relu2_mlp(x2, norm_mlp_g, bf(w_up), bf(w_down))

    outs, row0 = [], 0
    w_pg, w_pp = bf(w_ple_gate), bf(w_ple_proj)
    for p in p_list:
        pb = p.shape[0]
        p2 = p.reshape(pb * s, p.shape[-1])
        outs.append(_gated_ple(x2, row0, p2, norm_ple_g, w_pg, w_pp).reshape(pb, s, d))
        row0 += pb * s
    return outs


def kernel(x_prompt, x_sample, p_prompt, p_sample, norm_mix_g, w_in, gate_b, q_norm_g, k_norm_g, hy_conv_w, hy_conv_b, hy_w1, hy_b1, hy_w2, hy_b2, hy_sin_freq, hy_w3, hy_bias, w_att_out, w_hy_out, w_o, norm_mlp_g, w_up, w_down, norm_ple_g, w_ple_gate, w_ple_proj):
    assert x_prompt.shape[1:] == x_sample.shape[1:]
    y_prompt, y_sample = x_prompt, x_sample
    for i in range(w_in.shape[0]):
        y_prompt, y_sample = _encoder_layer(
            (y_prompt, y_sample), (p_prompt[i], p_sample[i]), norm_mix_g[i], w_in[i], gate_b[i], q_norm_g[i], k_norm_g[i],
            hy_conv_w[i], hy_conv_b[i], hy_w1[i], hy_b1[i], hy_w2[i], hy_b2[i], hy_sin_freq[i], hy_w3[i],
            hy_bias[i], w_att_out[i], w_hy_out[i], w_o[i], norm_mlp_g[i], w_up[i], w_down[i],
            norm_ple_g[i], w_ple_gate[i], w_ple_proj[i])
    return (y_prompt, y_sample)
```

```python
import functools
import math

import jax
import jax.numpy as jnp
from jax import lax
from jax.experimental import pallas as pl
from jax.experimental.pallas import tpu as pltpu

F32 = jnp.float32
BF16 = jnp.bfloat16
HIGHEST = lax.Precision.HIGHEST

EPS = 1e-6
HEAD_DIM = 128
ROPE_DIM = HEAD_DIM // 4
ROPE_THETA = 500000.0
DILATED_PATTERNS = ((128, 1), (512, 4), (2048, 16))
ATT_RADIUS = 64
ATT_QBLK = 128
ATT_KWIN = ATT_QBLK + 2 * ATT_RADIUS
ATT_GROUP = 8
HY_ORDER = 2
HY_EMB_DIM = 33
HY_BANDS = (HY_EMB_DIM - 1) // 2
HY_FAST_DECAY = 0.3
HY_SLOW_DECAY = 1.5
HY_DECAY_TARGET = 1e-2
HY_MIN_DECAY = math.log(HY_DECAY_TARGET) / HY_SLOW_DECAY
HY_MAX_DECAY = math.log(HY_DECAY_TARGET) / HY_FAST_DECAY
N_BRANCH = 2
LANES = 128
MASK_NEG = -1e30

VMEM_CAP_BYTES = 60 * 1024 * 1024


def _cparams(sem, est_bytes):
    limit = int(min(VMEM_CAP_BYTES, max(32 * 1024 * 1024, est_bytes)))
    return pltpu.CompilerParams(dimension_semantics=sem, vmem_limit_bytes=limit)


def _nbytes(shape, dtype):
    return math.prod(shape) * jnp.dtype(dtype).itemsize


def _two_group_specs(block, na, col=lambda *j: 0):
    return (pl.BlockSpec(block, lambda i, *j: (jnp.minimum(i, na - 1), jnp.where(i < na, col(*j), 0))),
            pl.BlockSpec(block, lambda i, *j: (jnp.maximum(i - na, 0), jnp.where(i >= na, col(*j), 0))))


def _pick_group(na, xa_ref, xb_ref, body):
    i = pl.program_id(0)
    pl.when(i < na)(lambda: body(xa_ref))
    pl.when(i >= na)(lambda: body(xb_ref))


def _rmsnorm_kernel(xa_ref, xb_ref, g_ref, o_ref, *, na):
    def body(x_ref):
        x = x_ref[...]
        ms = jnp.mean(x * x, axis=-1, keepdims=True)
        o_ref[...] = (x * lax.rsqrt(ms + EPS) * g_ref[...]).astype(o_ref.dtype)

    _pick_group(na, xa_ref, xb_ref, body)


def _rmsnorm_cast(xa, xb, g, tm=512):
    d = xa.shape[1]
    assert xa.shape[0] % tm == 0 and xb.shape[0] % tm == 0
    na, t = xa.shape[0] // tm, xa.shape[0] + xb.shape[0]
    return pl.pallas_call(
        functools.partial(_rmsnorm_kernel, na=na),
        grid=(t // tm,),
        in_specs=[*_two_group_specs((tm, d), na), pl.BlockSpec((1, d), lambda i: (0, 0))],
        out_specs=pl.BlockSpec((tm, d), lambda i: (i, 0)),
        out_shape=jax.ShapeDtypeStruct((t, d), BF16),
        compiler_params=_cparams(("parallel",), 6 * _nbytes((tm, d), F32)),
        name="rmsnorm_cast",
    )(xa, xb, g.reshape(1, d))


INPROJ_ROW_CHUNK = 256


def _row_chunks(h_ref, w_ref):
    for r0 in range(0, h_ref.shape[0], INPROJ_ROW_CHUNK):
        rows = slice(r0, r0 + INPROJ_ROW_CHUNK)
        yield rows, jnp.dot(h_ref[rows, :], w_ref[...], preferred_element_type=F32)


def _inproj_qk_kernel(h_ref, w_ref, g_ref, c_ref, s1_ref, s2_ref, o_ref, *, scale):
    gs = g_ref[...] * scale
    pw = 2 * HEAD_DIM
    same_head = (lax.broadcasted_iota(jnp.int32, (pw, pw), 0) // HEAD_DIM
                 == lax.broadcasted_iota(jnp.int32, (pw, pw), 1) // HEAD_DIM)
    ones_bd = jnp.where(same_head, 1.0, 0.0).astype(BF16)
    for rows, y in _row_chunks(h_ref, w_ref):
        yy = (y * y).astype(BF16)
        c, s1, s2 = c_ref[rows, :], s1_ref[rows, :], s2_ref[rows, :]
        for hh in range(y.shape[1] // HEAD_DIM):
            sl = slice(hh * HEAD_DIM, (hh + 1) * HEAD_DIM)
            if hh % 2 == 0:
                ss_pair = jnp.dot(yy[:, hh * HEAD_DIM:hh * HEAD_DIM + pw], ones_bd, preferred_element_type=F32)
            ms = ss_pair[:, (hh % 2) * HEAD_DIM:(hh % 2 + 1) * HEAD_DIM] * (1.0 / HEAD_DIM)
            t = y[:, sl] * lax.rsqrt(ms + EPS) * gs
            out = (t * c + pltpu.roll(t, HEAD_DIM - ROPE_DIM // 2, 1) * s1 + pltpu.roll(t, ROPE_DIM // 2, 1) * s2)
            o_ref[hh, rows, :] = out.astype(o_ref.dtype)


def _inproj_heads_kernel(h_ref, w_ref, o_ref):
    for rows, y in _row_chunks(h_ref, w_ref):
        for hh in range(y.shape[1] // HEAD_DIM):
            o_ref[hh, rows, :] = y[:, hh * HEAD_DIM:(hh + 1) * HEAD_DIM].astype(o_ref.dtype)


CONV_PAD_ROWS = 8


def _inproj_conv_kernel(h_ref, w_ref, cw_ref, cb_ref, o_ref, scr):
    s, tn = h_ref.shape[0], w_ref.shape[1]
    m = s // 2
    half = INPROJ_ROW_CHUNK // 2
    pad = CONV_PAD_ROWS
    zeros = jnp.zeros((pad, LANES), F32)
    for c in range(tn // LANES):
        scr[c, 0:pad, :] = zeros
        scr[c, pad + s:2 * pad + s, :] = zeros

    def finish(r0):
        for c in range(tn // LANES):
            cols = slice(c * LANES, (c + 1) * LANES)
            w0, w1, w2, cb = cw_ref[0:1, cols], cw_ref[1:2, cols], cw_ref[2:3, cols], cb_ref[:, cols]
            ye = scr[c, pl.ds(pad + r0, half, stride=2), :]
            yo = scr[c, pl.ds(pad + r0 + 1, half, stride=2), :]
            yo_prev = scr[c, pl.ds(pad + r0 - 1, half, stride=2), :]
            ye_next = scr[c, pl.ds(pad + r0 + 2, half, stride=2), :]
            o_ref[r0 // 2:r0 // 2 + half, cols] = (yo_prev * w0 + ye * w1 + yo * w2 + cb).astype(o_ref.dtype)
            o_ref[m + r0 // 2:m + r0 // 2 + half, cols] = (ye * w0 + yo * w1 + ye_next * w2 + cb).astype(o_ref.dtype)

    for rows, y in _row_chunks(h_ref, w_ref):
        for c in range(tn // LANES):
            scr[c, pad + rows.start:pad + rows.stop, :] = y[:, c * LANES:(c + 1) * LANES]
        if rows.start > 0:
            finish(rows.start - INPROJ_ROW_CHUNK)
    finish(s - INPROJ_ROW_CHUNK)


def _inproj_gate_kernel(h_ref, w_ref, gb_ref, o_ref):
    for rows, y in _row_chunks(h_ref, w_ref):
        o_ref[rows, :] = jax.nn.sigmoid(y + gb_ref[...]).astype(o_ref.dtype)


def _inproj(h3, w, col0, width, body, extras, extra_specs, tn, scratch=(), head_major=False):
    b, s, d = h3.shape
    tn = min(tn, width)
    nj = width // tn
    jb0 = col0 // tn
    est = (2 * _nbytes((s, d), BF16) + 2 * _nbytes((d, tn), BF16) + 2 * _nbytes((s, tn), BF16)
           + 4 * _nbytes((s, tn), F32) + (8 << 20))
    return pl.pallas_call(
        body,
        grid=(b, nj),
        in_specs=[pl.BlockSpec((None, s, d), lambda i, j: (i, 0, 0)),
                  pl.BlockSpec((d, tn), lambda i, j: (0, jb0 + j))] + extra_specs,
        out_specs=(pl.BlockSpec((None, tn // HEAD_DIM, s, HEAD_DIM), lambda i, j: (i, j, 0, 0)) if head_major
                   else pl.BlockSpec((None, s, tn), lambda i, j: (i, 0, j))),
        out_shape=jax.ShapeDtypeStruct((b, width // HEAD_DIM, s, HEAD_DIM) if head_major else (b, s, width), BF16),
        scratch_shapes=list(scratch),
        compiler_params=_cparams(("parallel", "arbitrary"), est),
        name="inproj_" + getattr(body, "func", body).__name__,
    )(h3, w, *extras)


def _rope_tables(s):
    half = ROPE_DIM // 2
    inv_freq = ROPE_THETA ** (-jnp.arange(half, dtype=F32) / half)
    ang = jnp.arange(s, dtype=F32)[:, None] * inv_freq[None, :]
    cos, sin = jnp.cos(ang), jnp.sin(ang)
    c = jnp.concatenate([cos, cos, jnp.ones((s, HEAD_DIM - ROPE_DIM), F32)], axis=1)
    s1 = jnp.concatenate([-sin, jnp.zeros((s, HEAD_DIM - half), F32)], axis=1)
    s2 = jnp.concatenate([jnp.zeros((s, half), F32), sin, jnp.zeros((s, HEAD_DIM - ROPE_DIM), F32)], axis=1)
    return c, s1, s2


def _band_bias(off, kw):
    col = lax.broadcasted_iota(jnp.int32, (ATT_QBLK, kw), 1)
    row = lax.broadcasted_iota(jnp.int32, (ATT_QBLK, kw), 0)
    return jnp.where(jnp.abs(col - row + off) <= ATT_RADIUS, 0.0, MASK_NEG).astype(F32)


def _band_blocks(seg_len):
    kw = min(ATT_KWIN, seg_len)
    for bi in range(seg_len // ATT_QBLK):
        q0 = bi * ATT_QBLK
        yield q0, min(max(q0 - ATT_RADIUS, 0), seg_len - kw), kw


def _band_attend(blocks):
    scores = [lax.dot_general(qb, kb, (((1,), (1,)), ((), ())), preferred_element_type=F32) + bias
              for qb, kb, _, bias in blocks]
    maxes = [jnp.max(s, axis=-1, keepdims=True) for s in scores]
    probs = [jnp.exp2(s - m).astype(BF16) for s, m in zip(scores, maxes)]
    outs = [jnp.dot(p, blk[2], preferred_element_type=F32) for p, blk in zip(probs, blocks)]
    return [(oa[:, :HEAD_DIM], oa[:, HEAD_DIM:], jnp.broadcast_to(m, (ATT_QBLK, HEAD_DIM)))
            for oa, m in zip(outs, maxes)]


def _softmax_merge(a, b):
    (o1, l1, m1), (o2, l2, m2) = a, b
    m = jnp.maximum(m1, m2)
    a1 = jnp.exp2(m1 - m)
    a2 = jnp.exp2(m2 - m)
    return a1 * o1 + a2 * o2, a1 * l1 + a2 * l2, m


def _attn_kernel(q_ref, k_ref, v_ref, o_ref,
                 qf0, kf0, vf0, qf1, kf1, vf1, qb1, kb1, qb2, kb2, va0, va1, va2,
                 ao, al, am, bo, bl, bm):
    s = q_ref.shape[0]
    c = HEAD_DIM
    n1 = s // 4
    n2 = s // 16
    ones = jnp.ones((s, c), BF16)
    biases = {}

    def bias(off, kw):
        if (off, kw) not in biases:
            biases[(off, kw)] = _band_bias(off, kw)
        return biases[(off, kw)]

    def grouped(items, compute, store, group=ATT_GROUP):
        for g0 in range(0, len(items), group):
            chunk = items[g0:g0 + group]
            for it, val in zip(chunk, compute(chunk)):
                store(*it, val)

    def each(fn):
        return lambda chunk: [fn(*it) for it in chunk]

    qf0[...] = q_ref[...].astype(F32)
    kf0[...] = k_ref[...].astype(F32)
    vf0[...] = v_ref[...].astype(F32)
    va0[:, :c] = v_ref[...]
    va0[:, c:] = ones
    va1[:, c:] = ones
    va2[:, c:] = ones

    def split4_store(r, src, dstf, dstb, x):
        rows = slice(r * n1, (r + 1) * n1)
        dstf[rows, :] = x
        dstb[rows, 0:c] = x.astype(BF16)

    grouped([(r, src, dstf, dstb) for r in range(4)
             for src, dstf, dstb in ((qf0, qf1, qb1), (kf0, kf1, kb1), (vf0, vf1, va1))],
            each(lambda r, src, dstf, dstb: src[pl.ds(r, n1, stride=4), :]), split4_store)

    def split16_store(seg, srcf, dstb, x):
        dstb[seg * n2:(seg + 1) * n2, 0:c] = x

    grouped([(seg, srcf, dstb) for seg in range(16) for srcf, dstb in ((qf1, qb2), (kf1, kb2), (vf1, va2))],
            each(lambda seg, srcf, dstb: srcf[pl.ds((seg // 4) * n1 + seg % 4, n2, stride=4), :].astype(BF16)),
            split16_store)

    def attend(qb, kb, va, chunk):
        return _band_attend([(qb[base + q0:base + q0 + ATT_QBLK, :], kb[base + w0:base + w0 + kw, :],
                              va[base + w0:base + w0 + kw, :], bias(w0 - q0, kw)) for base, q0, w0, kw in chunk])

    def merged(cur, prev_refs, chunk):
        prev = [tuple(ref[base + q0:base + q0 + ATT_QBLK, :] for ref in prev_refs) for base, q0, _, _ in chunk]
        return [_softmax_merge(a, b) for a, b in zip(cur, prev)]

    def store16(base, q0, w0, kw, olm):
        seg = base // n2
        dst = pl.ds((seg // 4) * n1 + 4 * q0 + seg % 4, ATT_QBLK, stride=4)
        for ref, val in zip((ao, al, am), olm):
            ref[dst, :] = val

    grouped([(seg * n2, *blk) for seg in range(16) for blk in _band_blocks(n2)],
            lambda chunk: attend(qb2, kb2, va2, chunk), store16)

    def store4(base, q0, w0, kw, olm):
        dst = pl.ds(4 * q0 + base // n1, ATT_QBLK, stride=4)
        for ref, val in zip((bo, bl, bm), olm):
            ref[dst, :] = val

    grouped([(r4 * n1, *blk) for r4 in range(4) for blk in _band_blocks(n1)],
            lambda chunk: merged(attend(qb1, kb1, va1, chunk), (ao, al, am), chunk), store4)

    def store1(base, q0, w0, kw, olm):
        o, l, _ = olm
        o_ref[q0:q0 + ATT_QBLK, :] = (o / l).astype(o_ref.dtype)

    grouped([(0, *blk) for blk in _band_blocks(s)],
            lambda chunk: merged(attend(q_ref, k_ref, va0, chunk), (bo, bl, bm), chunk), store1)


def _dilated_attention(q, k, v):
    b, nh, s, _ = q.shape
    aw = nh * HEAD_DIM
    assert tuple(w // (2 * d) for w, d in DILATED_PATTERNS) == (ATT_RADIUS,) * 3
    assert tuple(d for _, d in DILATED_PATTERNS) == (1, 4, 16)
    assert s % (16 * ATT_QBLK) == 0
    c = HEAD_DIM
    spec = pl.BlockSpec((None, None, s, c), lambda i, h: (i, h, 0, 0))
    out_spec = pl.BlockSpec((None, s, c), lambda i, h: (i, 0, h))
    f32buf = pltpu.VMEM((s, c), F32)
    bf16buf = pltpu.VMEM((s, c), BF16)
    augbuf = pltpu.VMEM((s, 2 * c), BF16)
    scratch = [f32buf] * 6 + [bf16buf] * 4 + [augbuf] * 3 + [f32buf] * 6
    est = 12 * _nbytes((s, c), F32) + 10 * _nbytes((s, c), BF16) + 8 * _nbytes((s, c), BF16) + (16 << 20)
    return pl.pallas_call(
        _attn_kernel,
        grid=(b, aw // c),
        in_specs=[spec, spec, spec],
        out_specs=out_spec,
        out_shape=jax.ShapeDtypeStruct((b, s, aw), BF16),
        scratch_shapes=scratch,
        compiler_params=_cparams(("parallel", "arbitrary"), est),
        name="dilated_attention",
    )(q, k, v)


def _filter_kernel(z_ref, w1_ref, b1_ref, w2_ref, b2_ref, fr_ref, w3_ref, t_ref, dl_ref,
                   filt_ref, asum_ref, hdn_ref, *, blocks_per_dir):
    @pl.when(pl.program_id(0) == 0)
    def _():
        fr = fr_ref[...]
        hdn = jnp.sin(fr * (jnp.dot(z_ref[...], w1_ref[...], precision=HIGHEST,
                                    preferred_element_type=F32) + b1_ref[...]))
        hdn_ref[...] = jnp.sin(fr * (jnp.dot(hdn, w2_ref[...], precision=HIGHEST,
                                             preferred_element_type=F32) + b2_ref[...]))

    hf = jnp.dot(hdn_ref[...], w3_ref[...], precision=HIGHEST, preferred_element_type=F32)
    hf = hf * jnp.exp(-t_ref[...] * dl_ref[...])
    is_bwd = (pl.program_id(0) // blocks_per_dir) % 2 == 1
    row = lax.broadcasted_iota(jnp.int32, hf.shape, 0)
    hf = jnp.where(jnp.logical_and(row == 0, is_bwd), 0.0, hf)
    filt_ref[...] = hf.astype(filt_ref.dtype)
    asum_ref[...] = jnp.sum(jnp.abs(hf), axis=0, keepdims=True)


def _even_odd_rows(a):
    return jnp.concatenate([a[0::2], a[1::2]], axis=0)


def _hyena_filter_taps(seq_len, w1, b1, w2, b2, sin_freq, w3, hy_w, tn=512):
    ffn = w1.shape[1]
    pad = LANES
    t = jnp.linspace(0.0, 1.0, seq_len, dtype=F32)[:, None]
    wpos = 2.0 * math.pi * jnp.arange(seq_len, dtype=F32) / seq_len
    bands = jnp.linspace(1e-4, HY_BANDS - 1, HY_BANDS, dtype=F32)
    ang = wpos[:, None] * bands[None, :]
    z = jnp.concatenate([t, jnp.cos(ang), -jnp.sin(ang)], axis=-1)
    z = _even_odd_rows(jnp.pad(z, ((0, 0), (0, pad - z.shape[1]))))
    t = _even_odd_rows(t)
    w1p = jnp.pad(w1.astype(F32), ((0, pad - w1.shape[0]), (0, pad - ffn)))
    w2p = jnp.pad(w2.astype(F32), ((0, pad - ffn), (0, pad - ffn)))
    w3p = jnp.pad(w3.astype(F32), ((0, pad - ffn), (0, 0)))
    padv = lambda a: jnp.pad(a.astype(F32), (0, pad - ffn)).reshape(1, pad)
    deltas = jnp.abs(jnp.linspace(HY_MIN_DECAY, HY_MAX_DECAY, hy_w, dtype=F32)).reshape(1, hy_w)
    ncol = w3.shape[1]
    tn = min(tn, hy_w)
    bpd = hy_w // tn
    full = lambda shape: pl.BlockSpec(shape, lambda j: (0, 0))
    return pl.pallas_call(
        functools.partial(_filter_kernel, blocks_per_dir=bpd),
        grid=(ncol // tn,),
        in_specs=[full((seq_len, pad)), full((pad, pad)), full((1, pad)), full((pad, pad)), full((1, pad)),
                  full((1, pad)), pl.BlockSpec((pad, tn), lambda j: (0, j)), full((seq_len, 1)),
                  pl.BlockSpec((1, tn), lambda j: (0, j % bpd))],
        out_specs=[pl.BlockSpec((seq_len, tn), lambda j: (0, j)), pl.BlockSpec((1, tn), lambda j: (0, j))],
        out_shape=[jax.ShapeDtypeStruct((seq_len, ncol), BF16), jax.ShapeDtypeStruct((1, ncol), F32)],
        scratch_shapes=[pltpu.VMEM((seq_len, pad), F32)],
        compiler_params=_cparams(("arbitrary",), 32 << 20),
        name="hyena_filter_taps",
    )(z, w1p, padv(b1), w2p, padv(b2), padv(sin_freq), w3p, t, deltas)


def _dft_matrices(s, n_total, tc):
    n = 2 * s
    k = jnp.arange(s, dtype=jnp.int32)

    def tables(cols):
        ang = ((k[:, None] * cols[None, :]) % n).astype(F32) * (2.0 * math.pi / n)
        return jnp.cos(ang), jnp.sin(ang)

    cl, sl = tables(jnp.arange(LANES, dtype=jnp.int32))
    ch, sh = tables(jnp.arange(s // LANES, dtype=jnp.int32) * LANES)
    cosm = (ch[:, :, None] * cl[:, None, :] - sh[:, :, None] * sl[:, None, :]).reshape(s, s)
    sinm = (sh[:, :, None] * cl[:, None, :] + ch[:, :, None] * sl[:, None, :]).reshape(s, s)
    alt = jnp.where(k % 2 == 0, 1.0, -1.0).astype(F32)
    im = (-sinm).at[0].set(alt)
    fwd = jnp.concatenate([cosm, im], axis=0)
    scale = jnp.full((n,), 2.0 / n_total, F32).at[0].set(1.0 / n_total).at[s].set(1.0 / n_total)
    inv = (fwd * scale[:, None]).T
    r = min(HY_FREQ_CHUNK, s)
    chunked = lambda a: a.reshape(2, s // r, r, -1).transpose(1, 0, 2, 3).reshape(n, -1)
    fwd = chunked(fwd)
    inv = chunked(inv.T).T
    theta = k.astype(F32) * (2.0 * math.pi / n_total)
    cw = jnp.broadcast_to(jnp.cos(theta)[:, None], (s, tc))
    sw = jnp.broadcast_to(jnp.sin(theta)[:, None], (s, tc))
    return fwd.astype(BF16), inv.astype(BF16), cw, sw


HY_FREQ_CHUNK = 256


def _freq_chunks(m):
    r = min(HY_FREQ_CHUNK, m)
    return [slice(r0, r0 + r) for r0 in range(0, m, r)]


def _first_row(shape):
    return lax.broadcasted_iota(jnp.int32, shape, 0) == 0


def _half_spectrum(z_ref, f_ref, cw_ref, sw_ref, rows):
    m = z_ref.shape[0] // 2
    r = rows.stop - rows.start
    frows = slice(2 * rows.start, 2 * rows.stop)
    ef = jnp.dot(f_ref[frows, :], z_ref[0:m, :], preferred_element_type=F32)
    of = jnp.dot(f_ref[frows, :], z_ref[m:2 * m, :], preferred_element_type=F32)
    er, ei, orr, oi = ef[:r], ef[r:], of[:r], of[r:]
    cw, sw = cw_ref[rows, :], sw_ref[rows, :]
    tr = orr * cw + oi * sw
    ti = oi * cw - orr * sw
    pim, qim = ei + ti, ti - ei
    if rows.start == 0:
        first = _first_row(pim.shape)
        pim, qim = jnp.where(first, ei, pim), jnp.where(first, -oi, qim)
    return er + tr, pim, er - tr, qim


def _spectrum_product(h, z, with_first):
    hpre, hpim, hqre, hqim = h
    zpre, zpim, zqre, zqim = z
    a = hpim * zpim
    b = hqim * zqim
    if not with_first:
        return hpre * zpre - a, hpre * zpim + hpim * zpre, hqre * zqre - b, hqre * zqim + hqim * zqre
    first = _first_row(a.shape)
    ypre = hpre * zpre - jnp.where(first, 0.0, a)
    ypim = jnp.where(first, a - b, hpre * zpim + hpim * zpre)
    yqre = hqre * zqre - jnp.where(first, 0.0, b)
    yqim = jnp.where(first, hpim * zqim + hqim * zpim, hqre * zqim + hqim * zqre)
    return ypre, ypim, yqre, yqim


def _inverse_butterfly(y, cw, sw, with_first, dtype):
    ypre, ypim, yqre, yqim = y
    gre = ypre + yqre
    gim = ypim - yqim
    dre = ypre - yqre
    dim = ypim + yqim
    g2re = dre * cw - dim * sw
    g2im = dre * sw + dim * cw
    if with_first:
        first = _first_row(gim.shape)
        gim, g2im = jnp.where(first, 2.0 * ypim, gim), jnp.where(first, -2.0 * yqim, g2im)
    return (jnp.concatenate([gre.astype(dtype), gim.astype(dtype)], axis=0),
            jnp.concatenate([g2re.astype(dtype), g2im.astype(dtype)], axis=0))


def _filter_spectrum_kernel(tf_ref, tb_ref, nf_ref, nb_ref, f_ref, cw_ref, sw_ref, o_ref):
    inv = 1.0 / (nf_ref[...] + nb_ref[...])
    for rows in _freq_chunks(cw_ref.shape[0]):
        fwd = _half_spectrum(tf_ref, f_ref, cw_ref, sw_ref, rows)
        bwd = _half_spectrum(tb_ref, f_ref, cw_ref, sw_ref, rows)
        bwd_pim = -bwd[1]
        if rows.start == 0:
            bwd_pim = jnp.where(_first_row(bwd_pim.shape), bwd[1], bwd_pim)
        o_ref[0, rows, :] = (fwd[0] + bwd[0]) * inv
        o_ref[1, rows, :] = (fwd[1] + bwd_pim) * inv
        o_ref[2, rows, :] = (fwd[2] + bwd[2]) * inv
        o_ref[3, rows, :] = (fwd[3] - bwd[3]) * inv


def _filter_spectrum(taps, asum, fwd, cw, sw, hy_w):
    l = taps.shape[0]
    m, tc = cw.shape
    nb = hy_w // tc
    col = lambda d: (lambda o, c: (0, (2 * o + d) * nb + c))
    const = lambda shape: pl.BlockSpec(shape, lambda o, c: (0,) * len(shape), pipeline_mode=pl.Buffered(1))
    est = (_nbytes((2 * m, m), BF16) + 4 * _nbytes((l, tc), BF16) + 2 * _nbytes((m, tc), F32)
           + 32 * _nbytes((m, tc), F32) + (8 << 20))
    return pl.pallas_call(
        _filter_spectrum_kernel,
        grid=(HY_ORDER, nb),
        in_specs=[pl.BlockSpec((l, tc), col(0)), pl.BlockSpec((l, tc), col(1)),
                  pl.BlockSpec((1, tc), col(0)), pl.BlockSpec((1, tc), col(1)),
                  const((2 * m, m)), const((m, tc)), const((m, tc))],
        out_specs=pl.BlockSpec((None, 4, m, tc), lambda o, c: (o, 0, 0, c)),
        out_shape=jax.ShapeDtypeStruct((HY_ORDER, 4, m, hy_w), F32),
        compiler_params=_cparams(("parallel", "arbitrary"), est),
        name="hyena_filter_spectrum",
    )(taps, taps, asum, asum, fwd, cw, sw)


def _hyena_conv_kernel(z_ref, g_ref, h_ref, b_ref, f_ref, inv_ref, cw_ref, sw_ref, o_ref, ge_ref, go_ref,
                       *scratch, natural_out):
    s, tc = z_ref.shape
    m = s // 2
    for rows in _freq_chunks(m):
        with_first = rows.start == 0
        spec = _half_spectrum(z_ref, f_ref, cw_ref, sw_ref, rows)
        prod = _spectrum_product(tuple(h_ref[i, rows, :] for i in range(4)), spec, with_first)
        ge, go = _inverse_butterfly(prod, cw_ref[rows, :], sw_ref[rows, :], with_first, ge_ref.dtype)
        ge_ref[2 * rows.start:2 * rows.stop, :] = ge
        go_ref[2 * rows.start:2 * rows.stop, :] = go
    bias = b_ref[...]
    for rows in _freq_chunks(m):
        odd = slice(m + rows.start, m + rows.stop)
        conv_e = jnp.dot(inv_ref[rows, :], ge_ref[...], preferred_element_type=F32)
        conv_o = jnp.dot(inv_ref[rows, :], go_ref[...], preferred_element_type=F32)
        out_e = g_ref[rows, :].astype(F32) * (conv_e + bias * z_ref[rows, :].astype(F32))
        out_o = g_ref[odd, :].astype(F32) * (conv_o + bias * z_ref[odd, :].astype(F32))
        if natural_out:
            (nat,) = scratch
            r = rows.stop - rows.start
            both = slice(2 * rows.start, 2 * rows.stop)
            for c in range(tc // LANES):
                cols = slice(c * LANES, (c + 1) * LANES)
                nat[c, pl.ds(2 * rows.start, r, stride=2), :] = out_e[:, cols]
                nat[c, pl.ds(2 * rows.start + 1, r, stride=2), :] = out_o[:, cols]
                o_ref[both, cols] = nat[c, both, :].astype(o_ref.dtype)
        else:
            o_ref[rows, :] = out_e.astype(o_ref.dtype)
            o_ref[odd, :] = out_o.astype(o_ref.dtype)


def _hyena_conv(z, z_col0, gate, gate_col0, filt, order, bias_row, mats, width, natural_out):
    fwd, inv, cw, sw = mats
    b, s, _ = z.shape
    m, tc = cw.shape
    nc = width // tc
    zb0, gb0 = z_col0 // tc, gate_col0 // tc
    const = lambda shape: pl.BlockSpec(shape, lambda c, i: (0,) * len(shape), pipeline_mode=pl.Buffered(1))
    scratch = [pltpu.VMEM((2 * m, tc), BF16)] * 2
    if natural_out:
        scratch.append(pltpu.VMEM((tc // LANES, s, LANES), F32))
    est = (2 * _nbytes((2 * m, m), BF16) + 2 * _nbytes((m, tc), F32) + 2 * _nbytes((4, m, tc), F32)
           + 6 * _nbytes((s, tc), BF16) + 24 * _nbytes((m, tc), F32) + (8 << 20))
    return pl.pallas_call(
        functools.partial(_hyena_conv_kernel, natural_out=natural_out),
        grid=(nc, b),
        in_specs=[pl.BlockSpec((None, s, tc), lambda c, i: (i, 0, zb0 + c)),
                  pl.BlockSpec((None, s, tc), lambda c, i: (i, 0, gb0 + c)),
                  pl.BlockSpec((None, 4, m, tc), lambda c, i: (order, 0, 0, c)),
                  pl.BlockSpec((1, tc), lambda c, i: (0, c)),
                  const((2 * m, m)), const((m, 2 * m)), const((m, tc)), const((m, tc))],
        out_specs=pl.BlockSpec((None, s, tc), lambda c, i: (i, 0, c)),
        out_shape=jax.ShapeDtypeStruct((b, s, width), BF16),
        scratch_shapes=scratch,
        compiler_params=_cparams(("parallel", "arbitrary"), est),
        name="hyena_conv",
    )(z, gate, filt, bias_row, fwd, inv, cw, sw)


def _chunk_rows(n_rows):
    return [slice(r0, r0 + INPROJ_ROW_CHUNK) for r0 in range(0, n_rows, INPROJ_ROW_CHUNK)]


def _merge_kernel(a_ref, hy_ref, wa_ref, wh_ref, g0_ref, g1_ref, o_ref):
    for rows in _chunk_rows(a_ref.shape[0]):
        ya = jnp.dot(a_ref[rows, :], wa_ref[...], preferred_element_type=F32)
        yh = jnp.dot(hy_ref[rows, :], wh_ref[...], preferred_element_type=F32)
        o_ref[rows, :] = (g0_ref[rows, :].astype(F32) * ya + g1_ref[rows, :].astype(F32) * yh).astype(o_ref.dtype)


def _gated_merge(attn, hy, w_att, w_hy, gates, tm=1024, tn=512):
    t, k = attn.shape
    n = w_att.shape[1]
    tm, tn = min(tm, t), min(tn, n)
    g1b0 = n // tn
    est = 4 * _nbytes((tm, k), BF16) + 4 * _nbytes((k, tn), BF16) + 6 * _nbytes((tm, tn), F32) + (8 << 20)
    return pl.pallas_call(
        _merge_kernel,
        grid=(t // tm, n // tn),
        in_specs=[pl.BlockSpec((tm, k), lambda i, j: (i, 0)),
                  pl.BlockSpec((tm, k), lambda i, j: (i, 0)),
                  pl.BlockSpec((k, tn), lambda i, j: (0, j)),
                  pl.BlockSpec((k, tn), lambda i, j: (0, j)),
                  pl.BlockSpec((tm, tn), lambda i, j: (i, j)),
                  pl.BlockSpec((tm, tn), lambda i, j: (i, g1b0 + j))],
        out_specs=pl.BlockSpec((tm, tn), lambda i, j: (i, j)),
        out_shape=jax.ShapeDtypeStruct((t, n), BF16),
        compiler_params=_cparams(("parallel", "arbitrary"), est),
        name="gated_merge",
    )(attn, hy, w_att, w_hy, gates, gates)


def _mm_residual_kernel(a_ref, w_ref, ra_ref, rb_ref, o_ref, *, na):
    def body(r_ref):
        for rows in _chunk_rows(a_ref.shape[0]):
            o_ref[rows, :] = r_ref[rows, :] + jnp.dot(a_ref[rows, :], w_ref[...], preferred_element_type=F32)

    _pick_group(na, ra_ref, rb_ref, body)


def _matmul_residual(a, w, res_a, res_b, tm=1024, tn=1024):
    t, k = a.shape
    n = w.shape[1]
    tm, tn = min(tm, res_a.shape[0]), min(tn, n)
    assert res_a.shape[0] % tm == 0 and res_b.shape[0] % tm == 0
    na = res_a.shape[0] // tm
    est = 2 * _nbytes((tm, k), BF16) + 2 * _nbytes((k, tn), BF16) + 8 * _nbytes((tm, tn), F32) + (8 << 20)
    return pl.pallas_call(
        functools.partial(_mm_residual_kernel, na=na),
        grid=(t // tm, n // tn),
        in_specs=[pl.BlockSpec((tm, k), lambda i, j: (i, 0)),
                  pl.BlockSpec((k, tn), lambda i, j: (0, j)),
                  *_two_group_specs((tm, tn), na, col=lambda j: j)],
        out_specs=pl.BlockSpec((tm, tn), lambda i, j: (i, j)),
        out_shape=jax.ShapeDtypeStruct((t, n), F32),
        compiler_params=_cparams(("parallel", "arbitrary"), est),
        name="matmul_residual",
    )(a, w, res_a, res_b)


def _mlp_kernel(x_ref, g_ref, wu_ref, wd_ref, o_ref, hm_ref):
    @pl.when(pl.program_id(1) == 0)
    def _():
        x = x_ref[...]
        ms = jnp.mean(x * x, axis=-1, keepdims=True)
        hm_ref[...] = (x * lax.rsqrt(ms + EPS) * g_ref[...]).astype(hm_ref.dtype)
        o_ref[...] = x

    for rows in _chunk_rows(hm_ref.shape[0]):
        a = jnp.dot(hm_ref[rows, :], wu_ref[...], preferred_element_type=F32)
        a = jnp.square(jnp.maximum(a, 0.0)).astype(BF16)
        o_ref[rows, :] += jnp.dot(a, wd_ref[...], preferred_element_type=F32)


def _relu2_mlp(x, g, w_up, w_down, tm=1024, tf=1024):
    t, d = x.shape
    ff = w_up.shape[1]
    tm, tf = min(tm, t), min(tf, ff)
    est = (4 * _nbytes((tm, d), F32) + _nbytes((tm, d), BF16) + 4 * _nbytes((d, tf), BF16)
           + 3 * _nbytes((tm, tf), F32) + (8 << 20))
    return pl.pallas_call(
        _mlp_kernel,
        grid=(t // tm, ff // tf),
        in_specs=[pl.BlockSpec((tm, d), lambda i, j: (i, 0)),
                  pl.BlockSpec((1, d), lambda i, j: (0, 0)),
                  pl.BlockSpec((d, tf), lambda i, j: (0, j)),
                  pl.BlockSpec((tf, d), lambda i, j: (j, 0))],
        out_specs=pl.BlockSpec((tm, d), lambda i, j: (i, 0)),
        out_shape=jax.ShapeDtypeStruct((t, d), F32),
        scratch_shapes=[pltpu.VMEM((tm, d), BF16)],
        compiler_params=_cparams(("parallel", "arbitrary"), est),
        name="relu2_mlp",
    )(x, g.reshape(1, d), w_up, w_down)


def _ple_kernel(x_ref, g_ref, wg_ref, p_ref, wp_ref, o_ref, hn_ref):
    j = pl.program_id(1)

    @pl.when(j == 0)
    def _():
        x = x_ref[...]
        ms = jnp.mean(x * x, axis=-1, keepdims=True)
        hn_ref[...] = (x * lax.rsqrt(ms + EPS) * g_ref[...]).astype(hn_ref.dtype)

    tn = o_ref.shape[1]
    cols = pl.ds(pl.multiple_of(j * tn, tn), tn)
    for rows in _chunk_rows(hn_ref.shape[0]):
        gate = jax.nn.sigmoid(jnp.dot(hn_ref[rows, :], wg_ref[...], preferred_element_type=F32))
        proj = jnp.dot(p_ref[rows, :].astype(BF16), wp_ref[...], preferred_element_type=F32)
        o_ref[rows, :] = x_ref[rows, cols] + gate * proj


def _gated_ple(x, row0, p, g, w_gate, w_proj, tm=1024, tn=1024):
    t, pd = p.shape
    d = x.shape[1]
    tm, tn = min(tm, t), min(tn, d)
    rb0 = row0 // tm
    est = (2 * _nbytes((tm, d), F32) + _nbytes((tm, d), BF16) + 2 * _nbytes((d, tn), BF16)
           + 8 * _nbytes((tm, tn), F32) + (8 << 20))
    return pl.pallas_call(
        _ple_kernel,
        grid=(t // tm, d // tn),
        in_specs=[pl.BlockSpec((tm, d), lambda i, j: (rb0 + i, 0)),
                  pl.BlockSpec((1, d), lambda i, j: (0, 0)),
                  pl.BlockSpec((d, tn), lambda i, j: (0, j)),
                  pl.BlockSpec((tm, pd), lambda i, j: (i, 0)),
                  pl.BlockSpec((pd, tn), lambda i, j: (0, j))],
        out_specs=pl.BlockSpec((tm, tn), lambda i, j: (i, j)),
        out_shape=jax.ShapeDtypeStruct((t, d), F32),
        scratch_shapes=[pltpu.VMEM((tm, d), BF16)],
        compiler_params=_cparams(("parallel", "arbitrary"), est),
        name="gated_ple",
    )(x, g.reshape(1, d), w_gate, p, w_proj)


def _encoder_layer(x_list, p_list, norm_mix_g, w_in, gate_b, q_norm_g, k_norm_g, hy_conv_w, hy_conv_b,
                   hy_w1, hy_b1, hy_w2, hy_b2, hy_sin_freq, hy_w3, hy_bias,
                   w_att_out, w_hy_out, w_o, norm_mlp_g, w_up, w_down,
                   norm_ple_g, w_ple_gate, w_ple_proj):
    xa3, xb3 = x_list
    s, d = xa3.shape[1:]
    b = xa3.shape[0] + xb3.shape[0]
    att_w = w_att_out.shape[0]
    hy_w = w_hy_out.shape[0]
    xa, xb = xa3.reshape(-1, d), xb3.reshape(-1, d)
    bf = lambda a: a.astype(BF16)

    h3 = _rmsnorm_cast(xa, xb, norm_mix_g).reshape(b, s, d)
    w_in_b = bf(w_in)
    rope_c, rope_s1, rope_s2 = _rope_tables(s)
    tab_spec = pl.BlockSpec((s, HEAD_DIM), lambda i, j: (0, 0))
    vec_spec = lambda tn, blk0=0: pl.BlockSpec((1, tn), lambda i, j: (0, blk0 + j))
    head_spec = pl.BlockSpec((1, HEAD_DIM), lambda i, j: (0, 0))
    tn = 1024
    qk = lambda g, col0, scale: _inproj(
        h3, w_in_b, col0, att_w, functools.partial(_inproj_qk_kernel, scale=scale),
        (g.reshape(1, HEAD_DIM), rope_c, rope_s1, rope_s2), [head_spec, tab_spec, tab_spec, tab_spec], tn,
        head_major=True)
    q = qk(q_norm_g, 0, HEAD_DIM ** -0.5 * math.log2(math.e))
    k = qk(k_norm_g, att_w, 1.0)
    v = _inproj(h3, w_in_b, 2 * att_w, att_w, _inproj_heads_kernel, (), [], tn, head_major=True)
    tnc = min(tn, hy_w)
    u = _inproj(h3, w_in_b, 3 * att_w, 3 * hy_w, _inproj_conv_kernel,
                (hy_conv_w, hy_conv_b.reshape(1, 3 * hy_w)),
                [pl.BlockSpec((3, tnc), lambda i, j: (0, j)), vec_spec(tnc)], tnc,
                scratch=[pltpu.VMEM((tnc // LANES, s + 2 * CONV_PAD_ROWS, LANES), F32)])
    tng = min(tn, d)
    gates = _inproj(h3, w_in_b, 3 * att_w + 3 * hy_w, N_BRANCH * d, _inproj_gate_kernel,
                    (gate_b.reshape(1, N_BRANCH * d),), [vec_spec(tng)], tng)

    attn = _dilated_attention(q, k, v)

    mats = _dft_matrices(s // 2, 2 * s, min(256, hy_w))
    taps, tap_asum = _hyena_filter_taps(s, hy_w1, hy_b1, hy_w2, hy_b2, hy_sin_freq, hy_w3, hy_w)
    filt = _filter_spectrum(taps, tap_asum, mats[0], mats[2], mats[3], hy_w)
    bias = hy_bias.astype(F32)
    z = _hyena_conv(u, 0, u, hy_w, filt, 0, bias[0:1], mats, hy_w, natural_out=False)
    hy = _hyena_conv(z, 0, u, 2 * hy_w, filt, 1, bias[1:2], mats, hy_w, natural_out=True)

    merged = _gated_merge(attn.reshape(b * s, att_w), hy.reshape(b * s, hy_w),
                          bf(w_att_out), bf(w_hy_out), gates.reshape(b * s, N_BRANCH * d))
    x2 = _matmul_residual(merged, bf(w_o), xa, xb)
    x2 = _relu2_mlp(x2, norm_mlp_g, bf(w_up), bf(w_down))

    outs, row0 = [], 0
    w_pg, w_pp = bf(w_ple_gate), bf(w_ple_proj)
    for p in p_list:
        pb = p.shape[0]
        p2 = p.reshape(pb * s, p.shape[-1])
        outs.append(_gated_ple(x2, row0, p2, norm_ple_g, w_pg, w_pp).reshape(pb, s, d))
        row0 += pb * s
    return outs


def kernel(x_prompt, x_sample, p_prompt, p_sample, norm_mix_g, w_in, gate_b, q_norm_g, k_norm_g, hy_conv_w, hy_conv_b, hy_w1, hy_b1, hy_w2, hy_b2, hy_sin_freq, hy_w3, hy_bias, w_att_out, w_hy_out, w_o, norm_mlp_g, w_up, w_down, norm_ple_g, w_ple_gate, w_ple_proj):
    assert x_prompt.shape[1:] == x_sample.shape[1:]
    y_prompt, y_sample = x_prompt, x_sample
    for i in range(w_in.shape[0]):
        y_prompt, y_sample = _encoder_layer(
            (y_prompt, y_sample), (p_prompt[i], p_sample[i]), norm_mix_g[i], w_in[i], gate_b[i], q_norm_g[i], k_norm_g[i],
            hy_conv_w[i], hy_conv_b[i], hy_w1[i], hy_b1[i], hy_w2[i], hy_b2[i], hy_sin_freq[i], hy_w3[i],
            hy_bias[i], w_att_out[i], w_hy_out[i], w_o[i], norm_mlp_g[i], w_up[i], w_down[i],
            norm_ple_g[i], w_ple_gate[i], w_ple_proj[i])
    return (y_prompt, y_sample)
```

```python
import functools
import math

import jax
import jax.numpy as jnp
from jax import lax
from jax.experimental import pallas as pl
from jax.experimental.pallas import tpu as pltpu

F32 = jnp.float32
BF16 = jnp.bfloat16
HIGHEST = lax.Precision.HIGHEST

EPS = 1e-6
HEAD_DIM = 128
ROPE_DIM = HEAD_DIM // 4
ROPE_THETA = 500000.0
DILATED_PATTERNS = ((128, 1), (512, 4), (2048, 16))
ATT_RADIUS = 64
ATT_QBLK = 128
ATT_KWIN = ATT_QBLK + 2 * ATT_RADIUS
ATT_GROUP = 8
HY_ORDER = 2
HY_EMB_DIM = 33
HY_BANDS = (HY_EMB_DIM - 1) // 2
HY_FAST_DECAY = 0.3
HY_SLOW_DECAY = 1.5
HY_DECAY_TARGET = 1e-2
HY_MIN_DECAY = math.log(HY_DECAY_TARGET) / HY_SLOW_DECAY
HY_MAX_DECAY = math.log(HY_DECAY_TARGET) / HY_FAST_DECAY
N_BRANCH = 2
LANES = 128
MASK_NEG = -1e30

VMEM_CAP_BYTES = 60 * 1024 * 1024


def _cparams(sem, est_bytes):
    limit = int(min(VMEM_CAP_BYTES, max(32 * 1024 * 1024, est_bytes)))
    return pltpu.CompilerParams(dimension_semantics=sem, vmem_limit_bytes=limit)


def _nbytes(shape, dtype):
    return math.prod(shape) * jnp.dtype(dtype).itemsize


def _two_group_specs(block, na, col=lambda *j: 0):
    return (pl.BlockSpec(block, lambda i, *j: (jnp.minimum(i, na - 1), jnp.where(i < na, col(*j), 0))),
            pl.BlockSpec(block, lambda i, *j: (jnp.maximum(i - na, 0), jnp.where(i >= na, col(*j), 0))))


def _pick_group(na, xa_ref, xb_ref, body):
    i = pl.program_id(0)
    pl.when(i < na)(lambda: body(xa_ref))
    pl.when(i >= na)(lambda: body(xb_ref))


def _rmsnorm_kernel(xa_ref, xb_ref, g_ref, o_ref, *, na):
    def body(x_ref):
        x = x_ref[...]
        ms = jnp.mean(x * x, axis=-1, keepdims=True)
        o_ref[...] = (x * lax.rsqrt(ms + EPS) * g_ref[...]).astype(o_ref.dtype)

    _pick_group(na, xa_ref, xb_ref, body)


def _rmsnorm_cast(xa, xb, g, tm=512):
    d = xa.shape[1]
    assert xa.shape[0] % tm == 0 and xb.shape[0] % tm == 0
    na, t = xa.shape[0] // tm, xa.shape[0] + xb.shape[0]
    return pl.pallas_call(
        functools.partial(_rmsnorm_kernel, na=na),
        grid=(t // tm,),
        in_specs=[*_two_group_specs((tm, d), na), pl.BlockSpec((1, d), lambda i: (0, 0))],
        out_specs=pl.BlockSpec((tm, d), lambda i: (i, 0)),
        out_shape=jax.ShapeDtypeStruct((t, d), BF16),
        compiler_params=_cparams(("parallel",), 6 * _nbytes((tm, d), F32)),
        name="rmsnorm_cast",
    )(xa, xb, g.reshape(1, d))


INPROJ_ROW_CHUNK = 256


def _row_chunks(h_ref, w_ref):
    for r0 in range(0, h_ref.shape[0], INPROJ_ROW_CHUNK):
        rows = slice(r0, r0 + INPROJ_ROW_CHUNK)
        yield rows, jnp.dot(h_ref[rows, :], w_ref[...], preferred_element_type=F32)


def _inproj_qk_kernel(h_ref, w_ref, g_ref, c_ref, s1_ref, s2_ref, o_ref, *, scale):
    gs = g_ref[...] * scale
    pw = 2 * HEAD_DIM
    same_head = (lax.broadcasted_iota(jnp.int32, (pw, pw), 0) // HEAD_DIM
                 == lax.broadcasted_iota(jnp.int32, (pw, pw), 1) // HEAD_DIM)
    ones_bd = jnp.where(same_head, 1.0, 0.0).astype(BF16)
    for rows, y in _row_chunks(h_ref, w_ref):
        yy = (y * y).astype(BF16)
        c, s1, s2 = c_ref[rows, :], s1_ref[rows, :], s2_ref[rows, :]
        for hh in range(y.shape[1] // HEAD_DIM):
            sl = slice(hh * HEAD_DIM, (hh + 1) * HEAD_DIM)
            if hh % 2 == 0:
                ss_pair = jnp.dot(yy[:, hh * HEAD_DIM:hh * HEAD_DIM + pw], ones_bd, preferred_element_type=F32)
            ms = ss_pair[:, (hh % 2) * HEAD_DIM:(hh % 2 + 1) * HEAD_DIM] * (1.0 / HEAD_DIM)
            t = y[:, sl] * lax.rsqrt(ms + EPS) * gs
            out = (t * c + pltpu.roll(t, HEAD_DIM - ROPE_DIM // 2, 1) * s1 + pltpu.roll(t, ROPE_DIM // 2, 1) * s2)
            o_ref[hh, rows, :] = out.astype(o_ref.dtype)


def _inproj_heads_kernel(h_ref, w_ref, o_ref):
    for rows, y in _row_chunks(h_ref, w_ref):
        for hh in range(y.shape[1] // HEAD_DIM):
            o_ref[hh, rows, :] = y[:, hh * HEAD_DIM:(hh + 1) * HEAD_DIM].astype(o_ref.dtype)


CONV_PAD_ROWS = 8


def _inproj_conv_kernel(h_ref, w_ref, cw_ref, cb_ref, o_ref, scr):
    s, tn = h_ref.shape[0], w_ref.shape[1]
    m = s // 2
    half = INPROJ_ROW_CHUNK // 2
    pad = CONV_PAD_ROWS
    zeros = jnp.zeros((pad, LANES), F32)
    for c in range(tn // LANES):
        scr[c, 0:pad, :] = zeros
        scr[c, pad + s:2 * pad + s, :] = zeros

    def finish(r0):
        for c in range(tn // LANES):
            cols = slice(c * LANES, (c + 1) * LANES)
            w0, w1, w2, cb = cw_ref[0:1, cols], cw_ref[1:2, cols], cw_ref[2:3, cols], cb_ref[:, cols]
            ye = scr[c, pl.ds(pad + r0, half, stride=2), :]
            yo = scr[c, pl.ds(pad + r0 + 1, half, stride=2), :]
            yo_prev = scr[c, pl.ds(pad + r0 - 1, half, stride=2), :]
            ye_next = scr[c, pl.ds(pad + r0 + 2, half, stride=2), :]
            o_ref[r0 // 2:r0 // 2 + half, cols] = (yo_prev * w0 + ye * w1 + yo * w2 + cb).astype(o_ref.dtype)
            o_ref[m + r0 // 2:m + r0 // 2 + half, cols] = (ye * w0 + yo * w1 + ye_next * w2 + cb).astype(o_ref.dtype)

    for rows, y in _row_chunks(h_ref, w_ref):
        for c in range(tn // LANES):
            scr[c, pad + rows.start:pad + rows.stop, :] = y[:, c * LANES:(c + 1) * LANES]
        if rows.start > 0:
            finish(rows.start - INPROJ_ROW_CHUNK)
    finish(s - INPROJ_ROW_CHUNK)


def _inproj_gate_kernel(h_ref, w_ref, gb_ref, o_ref):
    for rows, y in _row_chunks(h_ref, w_ref):
        o_ref[rows, :] = jax.nn.sigmoid(y + gb_ref[...]).astype(o_ref.dtype)


def _inproj(h3, w, col0, width, body, extras, extra_specs, tn, scratch=(), head_major=False):
    b, s, d = h3.shape
    tn = min(tn, width)
    nj = width // tn
    jb0 = col0 // tn
    est = (2 * _nbytes((s, d), BF16) + 2 * _nbytes((d, tn), BF16) + 2 * _nbytes((s, tn), BF16)
           + 4 * _nbytes((s, tn), F32) + (8 << 20))
    return pl.pallas_call(
        body,
        grid=(b, nj),
        in_specs=[pl.BlockSpec((None, s, d), lambda i, j: (i, 0, 0)),
                  pl.BlockSpec((d, tn), lambda i, j: (0, jb0 + j))] + extra_specs,
        out_specs=(pl.BlockSpec((None, tn // HEAD_DIM, s, HEAD_DIM), lambda i, j: (i, j, 0, 0)) if head_major
                   else pl.BlockSpec((None, s, tn), lambda i, j: (i, 0, j))),
        out_shape=jax.ShapeDtypeStruct((b, width // HEAD_DIM, s, HEAD_DIM) if head_major else (b, s, width), BF16),
        scratch_shapes=list(scratch),
        compiler_params=_cparams(("parallel", "arbitrary"), est),
        name="inproj_" + getattr(body, "func", body).__name__,
    )(h3, w, *extras)


def _rope_tables(s):
    half = ROPE_DIM // 2
    inv_freq = ROPE_THETA ** (-jnp.arange(half, dtype=F32) / half)
    ang = jnp.arange(s, dtype=F32)[:, None] * inv_freq[None, :]
    cos, sin = jnp.cos(ang), jnp.sin(ang)
    c = jnp.concatenate([cos, cos, jnp.ones((s, HEAD_DIM - ROPE_DIM), F32)], axis=1)
    s1 = jnp.concatenate([-sin, jnp.zeros((s, HEAD_DIM - half), F32)], axis=1)
    s2 = jnp.concatenate([jnp.zeros((s, half), F32), sin, jnp.zeros((s, HEAD_DIM - ROPE_DIM), F32)], axis=1)
    return c, s1, s2


def _band_bias(off, kw):
    col = lax.broadcasted_iota(jnp.int32, (ATT_QBLK, kw), 1)
    row = lax.broadcasted_iota(jnp.int32, (ATT_QBLK, kw), 0)
    return jnp.where(jnp.abs(col - row + off) <= ATT_RADIUS, 0.0, MASK_NEG).astype(F32)


def _band_blocks(seg_len):
    kw = min(ATT_KWIN, seg_len)
    for bi in range(seg_len // ATT_QBLK):
        q0 = bi * ATT_QBLK
        yield q0, min(max(q0 - ATT_RADIUS, 0), seg_len - kw), kw


def _band_attend(blocks):
    scores = [lax.dot_general(qb, kb, (((1,), (1,)), ((), ())), preferred_element_type=F32) + bias
              for qb, kb, _, bias in blocks]
    maxes = [jnp.max(s, axis=-1, keepdims=True) for s in scores]
    probs = [jnp.exp2(s - m).astype(BF16) for s, m in zip(scores, maxes)]
    outs = [jnp.dot(p, blk[2], preferred_element_type=F32) for p, blk in zip(probs, blocks)]
    return [(oa[:, :HEAD_DIM], oa[:, HEAD_DIM:], jnp.broadcast_to(m, (ATT_QBLK, HEAD_DIM)))
            for oa, m in zip(outs, maxes)]


def _softmax_merge(a, b):
    (o1, l1, m1), (o2, l2, m2) = a, b
    m = jnp.maximum(m1, m2)
    a1 = jnp.exp2(m1 - m)
    a2 = jnp.exp2(m2 - m)
    return a1 * o1 + a2 * o2, a1 * l1 + a2 * l2, m


def _attn_kernel(q_ref, k_ref, v_ref, o_ref,
                 qf0, kf0, vf0, qf1, kf1, vf1, qb1, kb1, qb2, kb2, va0, va1, va2,
                 ao, al, am, bo, bl, bm):
    s = q_ref.shape[0]
    c = HEAD_DIM
    n1 = s // 4
    n2 = s // 16
    ones = jnp.ones((s, c), BF16)
    biases = {}

    def bias(off, kw):
        if (off, kw) not in biases:
            biases[(off, kw)] = _band_bias(off, kw)
        return biases[(off, kw)]

    def grouped(items, compute, store, group=ATT_GROUP):
        for g0 in range(0, len(items), group):
            chunk = items[g0:g0 + group]
            for it, val in zip(chunk, compute(chunk)):
                store(*it, val)

    def each(fn):
        return lambda chunk: [fn(*it) for it in chunk]

    qf0[...] = q_ref[...].astype(F32)
    kf0[...] = k_ref[...].astype(F32)
    vf0[...] = v_ref[...].astype(F32)
    va0[:, :c] = v_ref[...]
    va0[:, c:] = ones
    va1[:, c:] = ones
    va2[:, c:] = ones

    def split4_store(r, src, dstf, dstb, x):
        rows = slice(r * n1, (r + 1) * n1)
        dstf[rows, :] = x
        dstb[rows, 0:c] = x.astype(BF16)

    grouped([(r, src, dstf, dstb) for r in range(4)
             for src, dstf, dstb in ((qf0, qf1, qb1), (kf0, kf1, kb1), (vf0, vf1, va1))],
            each(lambda r, src, dstf, dstb: src[pl.ds(r, n1, stride=4), :]), split4_store)

    def split16_store(seg, srcf, dstb, x):
        dstb[seg * n2:(seg + 1) * n2, 0:c] = x

    grouped([(seg, srcf, dstb) for seg in range(16) for srcf, dstb in ((qf1, qb2), (kf1, kb2), (vf1, va2))],
            each(lambda seg, srcf, dstb: srcf[pl.ds((seg // 4) * n1 + seg % 4, n2, stride=4), :].astype(BF16)),
            split16_store)

    def attend(qb, kb, va, chunk):
        return _band_attend([(qb[base + q0:base + q0 + ATT_QBLK, :], kb[base + w0:base + w0 + kw, :],
                              va[base + w0:base + w0 + kw, :], bias(w0 - q0, kw)) for base, q0, w0, kw in chunk])

    def merged(cur, prev_refs, chunk):
        prev = [tuple(ref[base + q0:base + q0 + ATT_QBLK, :] for ref in prev_refs) for base, q0, _, _ in chunk]
        return [_softmax_merge(a, b) for a, b in zip(cur, prev)]

    def store16(base, q0, w0, kw, olm):
        seg = base // n2
        dst = pl.ds((seg // 4) * n1 + 4 * q0 + seg % 4, ATT_QBLK, stride=4)
        for ref, val in zip((ao, al, am), olm):
            ref[dst, :] = val

    grouped([(seg * n2, *blk) for seg in range(16) for blk in _band_blocks(n2)],
            lambda chunk: attend(qb2, kb2, va2, chunk), store16)

    def store4(base, q0, w0, kw, olm):
        dst = pl.ds(4 * q0 + base // n1, ATT_QBLK, stride=4)
        for ref, val in zip((bo, bl, bm), olm):
            ref[dst, :] = val

    grouped([(r4 * n1, *blk) for r4 in range(4) for blk in _band_blocks(n1)],
            lambda chunk: merged(attend(qb1, kb1, va1, chunk), (ao, al, am), chunk), store4)

    def store1(base, q0, w0, kw, olm):
        o, l, _ = olm
        o_ref[q0:q0 + ATT_QBLK, :] = (o / l).astype(o_ref.dtype)

    grouped([(0, *blk) for blk in _band_blocks(s)],
            lambda chunk: merged(attend(q_ref, k_ref, va0, chunk), (bo, bl, bm), chunk), store1)


def _dilated_attention(q, k, v):
    b, nh, s, _ = q.shape
    aw = nh * HEAD_DIM
    assert tuple(w // (2 * d) for w, d in DILATED_PATTERNS) == (ATT_RADIUS,) * 3
    assert tuple(d for _, d in DILATED_PATTERNS) == (1, 4, 16)
    assert s % (16 * ATT_QBLK) == 0
    c = HEAD_DIM
    spec = pl.BlockSpec((None, None, s, c), lambda i, h: (i, h, 0, 0))
    out_spec = pl.BlockSpec((None, s, c), lambda i, h: (i, 0, h))
    f32buf = pltpu.VMEM((s, c), F32)
    bf16buf = pltpu.VMEM((s, c), BF16)
    augbuf = pltpu.VMEM((s, 2 * c), BF16)
    scratch = [f32buf] * 6 + [bf16buf] * 4 + [augbuf] * 3 + [f32buf] * 6
    est = 12 * _nbytes((s, c), F32) + 10 * _nbytes((s, c), BF16) + 8 * _nbytes((s, c), BF16) + (16 << 20)
    return pl.pallas_call(
        _attn_kernel,
        grid=(b, aw // c),
        in_specs=[spec, spec, spec],
        out_specs=out_spec,
        out_shape=jax.ShapeDtypeStruct((b, s, aw), BF16),
        scratch_shapes=scratch,
        compiler_params=_cparams(("parallel", "arbitrary"), est),
        name="dilated_attention",
    )(q, k, v)


def _filter_kernel(z_ref, w1_ref, b1_ref, w2_ref, b2_ref, fr_ref, w3_ref, t_ref, dl_ref,
                   filt_ref, asum_ref, hdn_ref, *, blocks_per_dir):
    @pl.when(pl.program_id(0) == 0)
    def _():
        fr = fr_ref[...]
        hdn = jnp.sin(fr * (jnp.dot(z_ref[...], w1_ref[...], precision=HIGHEST,
                                    preferred_element_type=F32) + b1_ref[...]))
        hdn_ref[...] = jnp.sin(fr * (jnp.dot(hdn, w2_ref[...], precision=HIGHEST,
                                             preferred_element_type=F32) + b2_ref[...]))

    hf = jnp.dot(hdn_ref[...], w3_ref[...], precision=HIGHEST, preferred_element_type=F32)
    hf = hf * jnp.exp(-t_ref[...] * dl_ref[...])
    is_bwd = (pl.program_id(0) // blocks_per_dir) % 2 == 1
    row = lax.broadcasted_iota(jnp.int32, hf.shape, 0)
    hf = jnp.where(jnp.logical_and(row == 0, is_bwd), 0.0, hf)
    filt_ref[...] = hf.astype(filt_ref.dtype)
    asum_ref[...] = jnp.sum(jnp.abs(hf), axis=0, keepdims=True)


def _even_odd_rows(a):
    return jnp.concatenate([a[0::2], a[1::2]], axis=0)


def _hyena_filter_taps(seq_len, w1, b1, w2, b2, sin_freq, w3, hy_w, tn=512):
    ffn = w1.shape[1]
    pad = LANES
    t = jnp.linspace(0.0, 1.0, seq_len, dtype=F32)[:, None]
    wpos = 2.0 * math.pi * jnp.arange(seq_len, dtype=F32) / seq_len
    bands = jnp.linspace(1e-4, HY_BANDS - 1, HY_BANDS, dtype=F32)
    ang = wpos[:, None] * bands[None, :]
    z = jnp.concatenate([t, jnp.cos(ang), -jnp.sin(ang)], axis=-1)
    z = _even_odd_rows(jnp.pad(z, ((0, 0), (0, pad - z.shape[1]))))
    t = _even_odd_rows(t)
    w1p = jnp.pad(w1.astype(F32), ((0, pad - w1.shape[0]), (0, pad - ffn)))
    w2p = jnp.pad(w2.astype(F32), ((0, pad - ffn), (0, pad - ffn)))
    w3p = jnp.pad(w3.astype(F32), ((0, pad - ffn), (0, 0)))
    padv = lambda a: jnp.pad(a.astype(F32), (0, pad - ffn)).reshape(1, pad)
    deltas = jnp.abs(jnp.linspace(HY_MIN_DECAY, HY_MAX_DECAY, hy_w, dtype=F32)).reshape(1, hy_w)
    ncol = w3.shape[1]
    tn = min(tn, hy_w)
    bpd = hy_w // tn
    full = lambda shape: pl.BlockSpec(shape, lambda j: (0, 0))
    return pl.pallas_call(
        functools.partial(_filter_kernel, blocks_per_dir=bpd),
        grid=(ncol // tn,),
        in_specs=[full((seq_len, pad)), full((pad, pad)), full((1, pad)), full((pad, pad)), full((1, pad)),
                  full((1, pad)), pl.BlockSpec((pad, tn), lambda j: (0, j)), full((seq_len, 1)),
                  pl.BlockSpec((1, tn), lambda j: (0, j % bpd))],
        out_specs=[pl.BlockSpec((seq_len, tn), lambda j: (0, j)), pl.BlockSpec((1, tn), lambda j: (0, j))],
        out_shape=[jax.ShapeDtypeStruct((seq_len, ncol), BF16), jax.ShapeDtypeStruct((1, ncol), F32)],
        scratch_shapes=[pltpu.VMEM((seq_len, pad), F32)],
        compiler_params=_cparams(("arbitrary",), 32 << 20),
        name="hyena_filter_taps",
    )(z, w1p, padv(b1), w2p, padv(b2), padv(sin_freq), w3p, t, deltas)


def _dft_matrices(s, n_total, tc):
    n = 2 * s
    k = jnp.arange(s, dtype=jnp.int32)

    def tables(cols):
        ang = ((k[:, None] * cols[None, :]) % n).astype(F32) * (2.0 * math.pi / n)
        return jnp.cos(ang), jnp.sin(ang)

    cl, sl = tables(jnp.arange(LANES, dtype=jnp.int32))
    ch, sh = tables(jnp.arange(s // LANES, dtype=jnp.int32) * LANES)
    cosm = (ch[:, :, None] * cl[:, None, :] - sh[:, :, None] * sl[:, None, :]).reshape(s, s)
    sinm = (sh[:, :, None] * cl[:, None, :] + ch[:, :, None] * sl[:, None, :]).reshape(s, s)
    alt = jnp.where(k % 2 == 0, 1.0, -1.0).astype(F32)
    im = (-sinm).at[0].set(alt)
    fwd = jnp.concatenate([cosm, im], axis=0)
    scale = jnp.full((n,), 2.0 / n_total, F32).at[0].set(1.0 / n_total).at[s].set(1.0 / n_total)
    inv = (fwd * scale[:, None]).T
    r = min(HY_FREQ_CHUNK, s)
    chunked = lambda a: a.reshape(2, s // r, r, -1).transpose(1, 0, 2, 3).reshape(n, -1)
    fwd = chunked(fwd)
    inv = chunked(inv.T).T
    theta = k.astype(F32) * (2.0 * math.pi / n_total)
    cw = jnp.broadcast_to(jnp.cos(theta)[:, None], (s, tc))
    sw = jnp.broadcast_to(jnp.sin(theta)[:, None], (s, tc))
    return fwd.astype(BF16), inv.astype(BF16), cw, sw


HY_FREQ_CHUNK = 256


def _freq_chunks(m):
    r = min(HY_FREQ_CHUNK, m)
    return [slice(r0, r0 + r) for r0 in range(0, m, r)]


def _first_row(shape):
    return lax.broadcasted_iota(jnp.int32, shape, 0) == 0


def _half_spectrum(z_ref, f_ref, cw_ref, sw_ref, rows):
    m = z_ref.shape[0] // 2
    r = rows.stop - rows.start
    frows = slice(2 * rows.start, 2 * rows.stop)
    ef = jnp.dot(f_ref[frows, :], z_ref[0:m, :], preferred_element_type=F32)
    of = jnp.dot(f_ref[frows, :], z_ref[m:2 * m, :], preferred_element_type=F32)
    er, ei, orr, oi = ef[:r], ef[r:], of[:r], of[r:]
    cw, sw = cw_ref[rows, :], sw_ref[rows, :]
    tr = orr * cw + oi * sw
    ti = oi * cw - orr * sw
    pim, qim = ei + ti, ti - ei
    if rows.start == 0:
        first = _first_row(pim.shape)
        pim, qim = jnp.where(first, ei, pim), jnp.where(first, -oi, qim)
    return er + tr, pim, er - tr, qim


def _spectrum_product(h, z, with_first):
    hpre, hpim, hqre, hqim = h
    zpre, zpim, zqre, zqim = z
    a = hpim * zpim
    b = hqim * zqim
    if not with_first:
        return hpre * zpre - a, hpre * zpim + hpim * zpre, hqre * zqre - b, hqre * zqim + hqim * zqre
    first = _first_row(a.shape)
    ypre = hpre * zpre - jnp.where(first, 0.0, a)
    ypim = jnp.where(first, a - b, hpre * zpim + hpim * zpre)
    yqre = hqre * zqre - jnp.where(first, 0.0, b)
    yqim = jnp.where(first, hpim * zqim + hqim * zpim, hqre * zqim + hqim * zqre)
    return ypre, ypim, yqre, yqim


def _inverse_butterfly(y, cw, sw, with_first, dtype):
    ypre, ypim, yqre, yqim = y
    gre = ypre + yqre
    gim = ypim - yqim
    dre = ypre - yqre
    dim = ypim + yqim
    g2re = dre * cw - dim * sw
    g2im = dre * sw + dim * cw
    if with_first:
        first = _first_row(gim.shape)
        gim, g2im = jnp.where(first, 2.0 * ypim, gim), jnp.where(first, -2.0 * yqim, g2im)
    return (jnp.concatenate([gre.astype(dtype), gim.astype(dtype)], axis=0),
            jnp.concatenate([g2re.astype(dtype), g2im.astype(dtype)], axis=0))


def _filter_spectrum_kernel(tf_ref, tb_ref, nf_ref, nb_ref, f_ref, cw_ref, sw_ref, o_ref):
    inv = 1.0 / (nf_ref[...] + nb_ref[...])
    for rows in _freq_chunks(cw_ref.shape[0]):
        fwd = _half_spectrum(tf_ref, f_ref, cw_ref, sw_ref, rows)
        bwd = _half_spectrum(tb_ref, f_ref, cw_ref, sw_ref, rows)
        bwd_pim = -bwd[1]
        if rows.start == 0:
            bwd_pim = jnp.where(_first_row(bwd_pim.shape), bwd[1], bwd_pim)
        o_ref[0, rows, :] = (fwd[0] + bwd[0]) * inv
        o_ref[1, rows, :] = (fwd[1] + bwd_pim) * inv
        o_ref[2, rows, :] = (fwd[2] + bwd[2]) * inv
        o_ref[3, rows, :] = (fwd[3] - bwd[3]) * inv


def _filter_spectrum(taps, asum, fwd, cw, sw, hy_w):
    l = taps.shape[0]
    m, tc = cw.shape
    nb = hy_w // tc
    col = lambda d: (lambda o, c: (0, (2 * o + d) * nb + c))
    const = lambda shape: pl.BlockSpec(shape, lambda o, c: (0,) * len(shape), pipeline_mode=pl.Buffered(1))
    est = (_nbytes((2 * m, m), BF16) + 4 * _nbytes((l, tc), BF16) + 2 * _nbytes((m, tc), F32)
           + 32 * _nbytes((m, tc), F32) + (8 << 20))
    return pl.pallas_call(
        _filter_spectrum_kernel,
        grid=(HY_ORDER, nb),
        in_specs=[pl.BlockSpec((l, tc), col(0)), pl.BlockSpec((l, tc), col(1)),
                  pl.BlockSpec((1, tc), col(0)), pl.BlockSpec((1, tc), col(1)),
                  const((2 * m, m)), const((m, tc)), const((m, tc))],
        out_specs=pl.BlockSpec((None, 4, m, tc), lambda o, c: (o, 0, 0, c)),
        out_shape=jax.ShapeDtypeStruct((HY_ORDER, 4, m, hy_w), F32),
        compiler_params=_cparams(("parallel", "arbitrary"), est),
        name="hyena_filter_spectrum",
    )(taps, taps, asum, asum, fwd, cw, sw)


def _hyena_conv_kernel(z_ref, g_ref, h_ref, b_ref, f_ref, inv_ref, cw_ref, sw_ref, o_ref, ge_ref, go_ref,
                       *scratch, natural_out):
    s, tc = z_ref.shape
    m = s // 2
    for rows in _freq_chunks(m):
        with_first = rows.start == 0
        spec = _half_spectrum(z_ref, f_ref, cw_ref, sw_ref, rows)
        prod = _spectrum_product(tuple(h_ref[i, rows, :] for i in range(4)), spec, with_first)
        ge, go = _inverse_butterfly(prod, cw_ref[rows, :], sw_ref[rows, :], with_first, ge_ref.dtype)
        ge_ref[2 * rows.start:2 * rows.stop, :] = ge
        go_ref[2 * rows.start:2 * rows.stop, :] = go
    bias = b_ref[...]
    for rows in _freq_chunks(m):
        odd = slice(m + rows.start, m + rows.stop)
        conv_e = jnp.dot(inv_ref[rows, :], ge_ref[...], preferred_element_type=F32)
        conv_o = jnp.dot(inv_ref[rows, :], go_ref[...], preferred_element_type=F32)
        out_e = g_ref[rows, :].astype(F32) * (conv_e + bias * z_ref[rows, :].astype(F32))
        out_o = g_ref[odd, :].astype(F32) * (conv_o + bias * z_ref[odd, :].astype(F32))
        if natural_out:
            (nat,) = scratch
            r = rows.stop - rows.start
            both = slice(2 * rows.start, 2 * rows.stop)
            for c in range(tc // LANES):
                cols = slice(c * LANES, (c + 1) * LANES)
                nat[c, pl.ds(2 * rows.start, r, stride=2), :] = out_e[:, cols]
                nat[c, pl.ds(2 * rows.start + 1, r, stride=2), :] = out_o[:, cols]
                o_ref[both, cols] = nat[c, both, :].astype(o_ref.dtype)
        else:
            o_ref[rows, :] = out_e.astype(o_ref.dtype)
            o_ref[odd, :] = out_o.astype(o_ref.dtype)


def _hyena_conv(z, z_col0, gate, gate_col0, filt, order, bias_row, mats, width, natural_out):
    fwd, inv, cw, sw = mats
    b, s, _ = z.shape
    m, tc = cw.shape
    nc = width // tc
    zb0, gb0 = z_col0 // tc, gate_col0 // tc
    const = lambda shape: pl.BlockSpec(shape, lambda c, i: (0,) * len(shape), pipeline_mode=pl.Buffered(1))
    scratch = [pltpu.VMEM((2 * m, tc), BF16)] * 2
    if natural_out:
        scratch.append(pltpu.VMEM((tc // LANES, s, LANES), F32))
    est = (2 * _nbytes((2 * m, m), BF16) + 2 * _nbytes((m, tc), F32) + 2 * _nbytes((4, m, tc), F32)
           + 6 * _nbytes((s, tc), BF16) + 24 * _nbytes((m, tc), F32) + (8 << 20))
    return pl.pallas_call(
        functools.partial(_hyena_conv_kernel, natural_out=natural_out),
        grid=(nc, b),
        in_specs=[pl.BlockSpec((None, s, tc), lambda c, i: (i, 0, zb0 + c)),
                  pl.BlockSpec((None, s, tc), lambda c, i: (i, 0, gb0 + c)),
                  pl.BlockSpec((None, 4, m, tc), lambda c, i: (order, 0, 0, c)),
                  pl.BlockSpec((1, tc), lambda c, i: (0, c)),
                  const((2 * m, m)), const((m, 2 * m)), const((m, tc)), const((m, tc))],
        out_specs=pl.BlockSpec((None, s, tc), lambda c, i: (i, 0, c)),
        out_shape=jax.ShapeDtypeStruct((b, s, width), BF16),
        scratch_shapes=scratch,
        compiler_params=_cparams(("parallel", "arbitrary"), est),
        name="hyena_conv",
    )(z, gate, filt, bias_row, fwd, inv, cw, sw)


def _chunk_rows(n_rows):
    return [slice(r0, r0 + INPROJ_ROW_CHUNK) for r0 in range(0, n_rows, INPROJ_ROW_CHUNK)]


def _merge_kernel(a_ref, hy_ref, wa_ref, wh_ref, g0_ref, g1_ref, o_ref):
    for rows in _chunk_rows(a_ref.shape[0]):
        ya = jnp.dot(a_ref[rows, :], wa_ref[...], preferred_element_type=F32)
        yh = jnp.dot(hy_ref[rows, :], wh_ref[...], preferred_element_type=F32)
        o_ref[rows, :] = (g0_ref[rows, :].astype(F32) * ya + g1_ref[rows, :].astype(F32) * yh).astype(o_ref.dtype)


def _weight_spec(k, tn, n):
    if tn == n:
        return pl.BlockSpec((k, n), lambda i, j: (0, 0), pipeline_mode=pl.Buffered(1))
    return pl.BlockSpec((k, tn), lambda i, j: (0, j))


def _gated_merge(attn, hy, w_att, w_hy, gates, tm=512, tn=2048):
    t, k = attn.shape
    n = w_att.shape[1]
    tm, tn = min(tm, t), min(tn, n)
    g1b0 = n // tn
    est = 4 * _nbytes((tm, k), BF16) + 4 * _nbytes((k, tn), BF16) + 6 * _nbytes((tm, tn), F32) + (8 << 20)
    return pl.pallas_call(
        _merge_kernel,
        grid=(t // tm, n // tn),
        in_specs=[pl.BlockSpec((tm, k), lambda i, j: (i, 0)),
                  pl.BlockSpec((tm, k), lambda i, j: (i, 0)),
                  _weight_spec(k, tn, n),
                  _weight_spec(k, tn, n),
                  pl.BlockSpec((tm, tn), lambda i, j: (i, j)),
                  pl.BlockSpec((tm, tn), lambda i, j: (i, g1b0 + j))],
        out_specs=pl.BlockSpec((tm, tn), lambda i, j: (i, j)),
        out_shape=jax.ShapeDtypeStruct((t, n), BF16),
        compiler_params=_cparams(("parallel", "arbitrary"), est),
        name="gated_merge",
    )(attn, hy, w_att, w_hy, gates, gates)


def _mm_residual_kernel(a_ref, w_ref, ra_ref, rb_ref, o_ref, *, na):
    def body(r_ref):
        for rows in _chunk_rows(a_ref.shape[0]):
            o_ref[rows, :] = r_ref[rows, :] + jnp.dot(a_ref[rows, :], w_ref[...], preferred_element_type=F32)

    _pick_group(na, ra_ref, rb_ref, body)


def _matmul_residual(a, w, res_a, res_b, tm=512, tn=2048):
    t, k = a.shape
    n = w.shape[1]
    tm, tn = min(tm, res_a.shape[0]), min(tn, n)
    assert res_a.shape[0] % tm == 0 and res_b.shape[0] % tm == 0
    na = res_a.shape[0] // tm
    est = 2 * _nbytes((tm, k), BF16) + 2 * _nbytes((k, tn), BF16) + 8 * _nbytes((tm, tn), F32) + (8 << 20)
    return pl.pallas_call(
        functools.partial(_mm_residual_kernel, na=na),
        grid=(t // tm, n // tn),
        in_specs=[pl.BlockSpec((tm, k), lambda i, j: (i, 0)),
                  _weight_spec(k, tn, n),
                  *_two_group_specs((tm, tn), na, col=lambda j: j)],
        out_specs=pl.BlockSpec((tm, tn), lambda i, j: (i, j)),
        out_shape=jax.ShapeDtypeStruct((t, n), F32),
        compiler_params=_cparams(("parallel", "arbitrary"), est),
        name="matmul_residual",
    )(a, w, res_a, res_b)


def _mlp_kernel(x_ref, g_ref, wu_ref, wd_ref, o_ref, hm_ref):
    @pl.when(pl.program_id(1) == 0)
    def _():
        x = x_ref[...]
        ms = jnp.mean(x * x, axis=-1, keepdims=True)
        hm_ref[...] = (x * lax.rsqrt(ms + EPS) * g_ref[...]).astype(hm_ref.dtype)
        o_ref[...] = x

    for rows in _chunk_rows(hm_ref.shape[0]):
        a = jnp.dot(hm_ref[rows, :], wu_ref[...], preferred_element_type=F32)
        a = jnp.square(jnp.maximum(a, 0.0)).astype(BF16)
        o_ref[rows, :] += jnp.dot(a, wd_ref[...], preferred_element_type=F32)


def _relu2_mlp(x, g, w_up, w_down, tm=1024, tf=1024):
    t, d = x.shape
    ff = w_up.shape[1]
    tm, tf = min(tm, t), min(tf, ff)
    est = (4 * _nbytes((tm, d), F32) + _nbytes((tm, d), BF16) + 4 * _nbytes((d, tf), BF16)
           + 3 * _nbytes((tm, tf), F32) + (8 << 20))
    return pl.pallas_call(
        _mlp_kernel,
        grid=(t // tm, ff // tf),
        in_specs=[pl.BlockSpec((tm, d), lambda i, j: (i, 0)),
                  pl.BlockSpec((1, d), lambda i, j: (0, 0)),
                  pl.BlockSpec((d, tf), lambda i, j: (0, j)),
                  pl.BlockSpec((tf, d), lambda i, j: (j, 0))],
        out_specs=pl.BlockSpec((tm, d), lambda i, j: (i, 0)),
        out_shape=jax.ShapeDtypeStruct((t, d), F32),
        scratch_shapes=[pltpu.VMEM((tm, d), BF16)],
        compiler_params=_cparams(("parallel", "arbitrary"), est),
        name="relu2_mlp",
    )(x, g.reshape(1, d), w_up, w_down)


def _ple_kernel(x_ref, g_ref, wg_ref, p_ref, wp_ref, o_ref, hn_ref):
    j = pl.program_id(1)

    @pl.when(j == 0)
    def _():
        x = x_ref[...]
        ms = jnp.mean(x * x, axis=-1, keepdims=True)
        hn_ref[...] = (x * lax.rsqrt(ms + EPS) * g_ref[...]).astype(hn_ref.dtype)

    tn = o_ref.shape[1]
    cols = pl.ds(pl.multiple_of(j * tn, tn), tn)
    for rows in _chunk_rows(hn_ref.shape[0]):
        gate = jax.nn.sigmoid(jnp.dot(hn_ref[rows, :], wg_ref[...], preferred_element_type=F32))
        proj = jnp.dot(p_ref[rows, :].astype(BF16), wp_ref[...], preferred_element_type=F32)
        o_ref[rows, :] = x_ref[rows, cols] + gate * proj


def _gated_ple(x, row0, p, g, w_gate, w_proj, tm=512, tn=2048):
    t, pd = p.shape
    d = x.shape[1]
    tm, tn = min(tm, t), min(tn, d)
    rb0 = row0 // tm
    est = (2 * _nbytes((tm, d), F32) + _nbytes((tm, d), BF16) + 2 * _nbytes((d, tn), BF16)
           + 8 * _nbytes((tm, tn), F32) + (8 << 20))
    return pl.pallas_call(
        _ple_kernel,
        grid=(t // tm, d // tn),
        in_specs=[pl.BlockSpec((tm, d), lambda i, j: (rb0 + i, 0)),
                  pl.BlockSpec((1, d), lambda i, j: (0, 0)),
                  _weight_spec(d, tn, d),
                  pl.BlockSpec((tm, pd), lambda i, j: (i, 0)),
                  _weight_spec(pd, tn, d)],
        out_specs=pl.BlockSpec((tm, tn), lambda i, j: (i, j)),
        out_shape=jax.ShapeDtypeStruct((t, d), F32),
        scratch_shapes=[pltpu.VMEM((tm, d), BF16)],
        compiler_params=_cparams(("parallel", "arbitrary"), est),
        name="gated_ple",
    )(x, g.reshape(1, d), w_gate, p, w_proj)


def _encoder_layer(x_list, p_list, norm_mix_g, w_in, gate_b, q_norm_g, k_norm_g, hy_conv_w, hy_conv_b,
                   hy_w1, hy_b1, hy_w2, hy_b2, hy_sin_freq, hy_w3, hy_bias,
                   w_att_out, w_hy_out, w_o, norm_mlp_g, w_up, w_down,
                   norm_ple_g, w_ple_gate, w_ple_proj):
    xa3, xb3 = x_list
    s, d = xa3.shape[1:]
    b = xa3.shape[0] + xb3.shape[0]
    att_w = w_att_out.shape[0]
    hy_w = w_hy_out.shape[0]
    xa, xb = xa3.reshape(-1, d), xb3.reshape(-1, d)
    bf = lambda a: a.astype(BF16)

    h3 = _rmsnorm_cast(xa, xb, norm_mix_g).reshape(b, s, d)
    w_in_b = bf(w_in)
    rope_c, rope_s1, rope_s2 = _rope_tables(s)
    tab_spec = pl.BlockSpec((s, HEAD_DIM), lambda i, j: (0, 0))
    vec_spec = lambda tn, blk0=0: pl.BlockSpec((1, tn), lambda i, j: (0, blk0 + j))
    head_spec = pl.BlockSpec((1, HEAD_DIM), lambda i, j: (0, 0))
    tn = 1024
    qk = lambda g, col0, scale: _inproj(
        h3, w_in_b, col0, att_w, functools.partial(_inproj_qk_kernel, scale=scale),
        (g.reshape(1, HEAD_DIM), rope_c, rope_s1, rope_s2), [head_spec, tab_spec, tab_spec, tab_spec], tn,
        head_major=True)
    q = qk(q_norm_g, 0, HEAD_DIM ** -0.5 * math.log2(math.e))
    k = qk(k_norm_g, att_w, 1.0)
    v = _inproj(h3, w_in_b, 2 * att_w, att_w, _inproj_heads_kernel, (), [], tn, head_major=True)
    tnc = min(tn, hy_w)
    u = _inproj(h3, w_in_b, 3 * att_w, 3 * hy_w, _inproj_conv_kernel,
                (hy_conv_w, hy_conv_b.reshape(1, 3 * hy_w)),
                [pl.BlockSpec((3, tnc), lambda i, j: (0, j)), vec_spec(tnc)], tnc,
                scratch=[pltpu.VMEM((tnc // LANES, s + 2 * CONV_PAD_ROWS, LANES), F32)])
    tng = min(tn, d)
    gates = _inproj(h3, w_in_b, 3 * att_w + 3 * hy_w, N_BRANCH * d, _inproj_gate_kernel,
                    (gate_b.reshape(1, N_BRANCH * d),), [vec_spec(tng)], tng)

    attn = _dilated_attention(q, k, v)

    mats = _dft_matrices(s // 2, 2 * s, min(256, hy_w))
    taps, tap_asum = _hyena_filter_taps(s, hy_w1, hy_b1, hy_w2, hy_b2, hy_sin_freq, hy_w3, hy_w)
    filt = _filter_spectrum(taps, tap_asum, mats[0], mats[2], mats[3], hy_w)
    bias = hy_bias.astype(F32)
    z = _hyena_conv(u, 0, u, hy_w, filt, 0, bias[0:1], mats, hy_w, natural_out=False)
    hy = _hyena_conv(z, 0, u, 2 * hy_w, filt, 1, bias[1:2], mats, hy_w, natural_out=True)

    merged = _gated_merge(attn.reshape(b * s, att_w), hy.reshape(b * s, hy_w),
                          bf(w_att_out), bf(w_hy_out), gates.reshape(b * s, N_BRANCH * d))
    x2 = _matmul_residual(merged, bf(w_o), xa, xb)
    x2 = _relu2_mlp(x2, norm_mlp_g, bf(w_up), bf(w_down))

    outs, row0 = [], 0
    w_pg, w_pp = bf(w_ple_gate), bf(w_ple_proj)
    for p in p_list:
        pb = p.shape[0]
        p2 = p.reshape(pb * s, p.shape[-1])
        outs.append(_gated_ple(x2, row0, p2, norm_ple_g, w_pg, w_pp).reshape(pb, s, d))
        row0 += pb * s
    return outs


def kernel(x_prompt, x_sample, p_prompt, p_sample, norm_mix_g, w_in, gate_b, q_norm_g, k_norm_g, hy_conv_w, hy_conv_b, hy_w1, hy_b1, hy_w2, hy_b2, hy_sin_freq, hy_w3, hy_bias, w_att_out, w_hy_out, w_o, norm_mlp_g, w_up, w_down, norm_ple_g, w_ple_gate, w_ple_proj):
    assert x_prompt.shape[1:] == x_sample.shape[1:]
    y_prompt, y_sample = x_prompt, x_sample
    for i in range(w_in.shape[0]):
        y_prompt, y_sample = _encoder_layer(
            (y_prompt, y_sample), (p_prompt[i], p_sample[i]), norm_mix_g[i], w_in[i], gate_b[i], q_norm_g[i], k_norm_g[i],
            hy_conv_w[i], hy_conv_b[i], hy_w1[i], hy_b1[i], hy_w2[i], hy_b2[i], hy_sin_freq[i], hy_w3[i],
            hy_bias[i], w_att_out[i], w_hy_out[i], w_o[i], norm_mlp_g[i], w_up[i], w_down[i],
            norm_ple_g[i], w_ple_gate[i], w_ple_proj[i])
    return (y_prompt, y_sample)
```

```python
import functools
import math

import jax
import jax.numpy as jnp
from jax import lax
from jax.experimental import pallas as pl
from jax.experimental.pallas import tpu as pltpu

F32 = jnp.float32
BF16 = jnp.bfloat16
HIGHEST = lax.Precision.HIGHEST

EPS = 1e-6
HEAD_DIM = 128
ROPE_DIM = HEAD_DIM // 4
ROPE_THETA = 500000.0
DILATED_PATTERNS = ((128, 1), (512, 4), (2048, 16))
ATT_RADIUS = 64
ATT_QBLK = 128
ATT_KWIN = ATT_QBLK + 2 * ATT_RADIUS
ATT_GROUP = 8
HY_ORDER = 2
HY_EMB_DIM = 33
HY_BANDS = (HY_EMB_DIM - 1) // 2
HY_FAST_DECAY = 0.3
HY_SLOW_DECAY = 1.5
HY_DECAY_TARGET = 1e-2
HY_MIN_DECAY = math.log(HY_DECAY_TARGET) / HY_SLOW_DECAY
HY_MAX_DECAY = math.log(HY_DECAY_TARGET) / HY_FAST_DECAY
N_BRANCH = 2
LANES = 128
MASK_NEG = -1e30

VMEM_CAP_BYTES = 60 * 1024 * 1024


def _cparams(sem, est_bytes):
    limit = int(min(VMEM_CAP_BYTES, max(32 * 1024 * 1024, est_bytes)))
    return pltpu.CompilerParams(dimension_semantics=sem, vmem_limit_bytes=limit)


def _nbytes(shape, dtype):
    return math.prod(shape) * jnp.dtype(dtype).itemsize


def _two_group_specs(block, na, col=lambda *j: 0):
    return (pl.BlockSpec(block, lambda i, *j: (jnp.minimum(i, na - 1), jnp.where(i < na, col(*j), 0))),
            pl.BlockSpec(block, lambda i, *j: (jnp.maximum(i - na, 0), jnp.where(i >= na, col(*j), 0))))


def _pick_group(na, xa_ref, xb_ref, body):
    i = pl.program_id(0)
    pl.when(i < na)(lambda: body(xa_ref))
    pl.when(i >= na)(lambda: body(xb_ref))


def _rmsnorm_kernel(xa_ref, xb_ref, g_ref, o_ref, *, na):
    def body(x_ref):
        x = x_ref[...]
        ms = jnp.mean(x * x, axis=-1, keepdims=True)
        o_ref[...] = (x * lax.rsqrt(ms + EPS) * g_ref[...]).astype(o_ref.dtype)

    _pick_group(na, xa_ref, xb_ref, body)


def _rmsnorm_cast(xa, xb, g, tm=512):
    d = xa.shape[1]
    assert xa.shape[0] % tm == 0 and xb.shape[0] % tm == 0
    na, t = xa.shape[0] // tm, xa.shape[0] + xb.shape[0]
    return pl.pallas_call(
        functools.partial(_rmsnorm_kernel, na=na),
        grid=(t // tm,),
        in_specs=[*_two_group_specs((tm, d), na), pl.BlockSpec((1, d), lambda i: (0, 0))],
        out_specs=pl.BlockSpec((tm, d), lambda i: (i, 0)),
        out_shape=jax.ShapeDtypeStruct((t, d), BF16),
        compiler_params=_cparams(("parallel",), 6 * _nbytes((tm, d), F32)),
        name="rmsnorm_cast",
    )(xa, xb, g.reshape(1, d))


INPROJ_ROW_CHUNK = 256


def _row_chunks(h_ref, w_ref):
    for r0 in range(0, h_ref.shape[0], INPROJ_ROW_CHUNK):
        rows = slice(r0, r0 + INPROJ_ROW_CHUNK)
        yield rows, jnp.dot(h_ref[rows, :], w_ref[...], preferred_element_type=F32)


def _inproj_qk_kernel(h_ref, w_ref, g_ref, c_ref, s1_ref, s2_ref, o_ref, *, scale):
    gs = g_ref[...] * scale
    pw = 2 * HEAD_DIM
    same_head = (lax.broadcasted_iota(jnp.int32, (pw, pw), 0) // HEAD_DIM
                 == lax.broadcasted_iota(jnp.int32, (pw, pw), 1) // HEAD_DIM)
    ones_bd = jnp.where(same_head, 1.0, 0.0).astype(BF16)
    for rows, y in _row_chunks(h_ref, w_ref):
        yy = (y * y).astype(BF16)
        c, s1, s2 = c_ref[rows, :], s1_ref[rows, :], s2_ref[rows, :]
        for hh in range(y.shape[1] // HEAD_DIM):
            sl = slice(hh * HEAD_DIM, (hh + 1) * HEAD_DIM)
            if hh % 2 == 0:
                ss_pair = jnp.dot(yy[:, hh * HEAD_DIM:hh * HEAD_DIM + pw], ones_bd, preferred_element_type=F32)
            ms = ss_pair[:, (hh % 2) * HEAD_DIM:(hh % 2 + 1) * HEAD_DIM] * (1.0 / HEAD_DIM)
            t = y[:, sl] * lax.rsqrt(ms + EPS) * gs
            out = (t * c + pltpu.roll(t, HEAD_DIM - ROPE_DIM // 2, 1) * s1 + pltpu.roll(t, ROPE_DIM // 2, 1) * s2)
            o_ref[hh, rows, :] = out.astype(o_ref.dtype)


def _inproj_heads_kernel(h_ref, w_ref, o_ref):
    for rows, y in _row_chunks(h_ref, w_ref):
        for hh in range(y.shape[1] // HEAD_DIM):
            o_ref[hh, rows, :] = y[:, hh * HEAD_DIM:(hh + 1) * HEAD_DIM].astype(o_ref.dtype)


CONV_PAD_ROWS = 8


def _inproj_conv_kernel(h_ref, w_ref, cw_ref, cb_ref, o_ref, scr):
    s, tn = h_ref.shape[0], w_ref.shape[1]
    m = s // 2
    half = INPROJ_ROW_CHUNK // 2
    pad = CONV_PAD_ROWS
    zeros = jnp.zeros((pad, LANES), F32)
    for c in range(tn // LANES):
        scr[c, 0:pad, :] = zeros
        scr[c, pad + s:2 * pad + s, :] = zeros

    def finish(r0):
        for c in range(tn // LANES):
            cols = slice(c * LANES, (c + 1) * LANES)
            w0, w1, w2, cb = cw_ref[0:1, cols], cw_ref[1:2, cols], cw_ref[2:3, cols], cb_ref[:, cols]
            ye = scr[c, pl.ds(pad + r0, half, stride=2), :]
            yo = scr[c, pl.ds(pad + r0 + 1, half, stride=2), :]
            yo_prev = scr[c, pl.ds(pad + r0 - 1, half, stride=2), :]
            ye_next = scr[c, pl.ds(pad + r0 + 2, half, stride=2), :]
            o_ref[r0 // 2:r0 // 2 + half, cols] = (yo_prev * w0 + ye * w1 + yo * w2 + cb).astype(o_ref.dtype)
            o_ref[m + r0 // 2:m + r0 // 2 + half, cols] = (ye * w0 + yo * w1 + ye_next * w2 + cb).astype(o_ref.dtype)

    for rows, y in _row_chunks(h_ref, w_ref):
        for c in range(tn // LANES):
            scr[c, pad + rows.start:pad + rows.stop, :] = y[:, c * LANES:(c + 1) * LANES]
        if rows.start > 0:
            finish(rows.start - INPROJ_ROW_CHUNK)
    finish(s - INPROJ_ROW_CHUNK)


def _inproj_gate_kernel(h_ref, w_ref, gb_ref, o_ref):
    for rows, y in _row_chunks(h_ref, w_ref):
        o_ref[rows, :] = jax.nn.sigmoid(y + gb_ref[...]).astype(o_ref.dtype)


def _inproj(h3, w, col0, width, body, extras, extra_specs, tn, scratch=(), head_major=False):
    b, s, d = h3.shape
    tn = min(tn, width)
    nj = width // tn
    jb0 = col0 // tn
    est = (2 * _nbytes((s, d), BF16) + 2 * _nbytes((d, tn), BF16) + 2 * _nbytes((s, tn), BF16)
           + 4 * _nbytes((s, tn), F32) + (8 << 20))
    return pl.pallas_call(
        body,
        grid=(b, nj),
        in_specs=[pl.BlockSpec((None, s, d), lambda i, j: (i, 0, 0)),
                  pl.BlockSpec((d, tn), lambda i, j: (0, jb0 + j))] + extra_specs,
        out_specs=(pl.BlockSpec((None, tn // HEAD_DIM, s, HEAD_DIM), lambda i, j: (i, j, 0, 0)) if head_major
                   else pl.BlockSpec((None, s, tn), lambda i, j: (i, 0, j))),
        out_shape=jax.ShapeDtypeStruct((b, width // HEAD_DIM, s, HEAD_DIM) if head_major else (b, s, width), BF16),
        scratch_shapes=list(scratch),
        compiler_params=_cparams(("parallel", "arbitrary"), est),
        name="inproj_" + getattr(body, "func", body).__name__,
    )(h3, w, *extras)


def _rope_tables(s):
    half = ROPE_DIM // 2
    inv_freq = ROPE_THETA ** (-jnp.arange(half, dtype=F32) / half)
    ang = jnp.arange(s, dtype=F32)[:, None] * inv_freq[None, :]
    cos, sin = jnp.cos(ang), jnp.sin(ang)
    c = jnp.concatenate([cos, cos, jnp.ones((s, HEAD_DIM - ROPE_DIM), F32)], axis=1)
    s1 = jnp.concatenate([-sin, jnp.zeros((s, HEAD_DIM - half), F32)], axis=1)
    s2 = jnp.concatenate([jnp.zeros((s, half), F32), sin, jnp.zeros((s, HEAD_DIM - ROPE_DIM), F32)], axis=1)
    return c, s1, s2


def _band_bias(off, kw):
    col = lax.broadcasted_iota(jnp.int32, (ATT_QBLK, kw), 1)
    row = lax.broadcasted_iota(jnp.int32, (ATT_QBLK, kw), 0)
    return jnp.where(jnp.abs(col - row + off) <= ATT_RADIUS, 0.0, MASK_NEG).astype(F32)


def _band_blocks(seg_len):
    kw = min(ATT_KWIN, seg_len)
    for bi in range(seg_len // ATT_QBLK):
        q0 = bi * ATT_QBLK
        yield q0, min(max(q0 - ATT_RADIUS, 0), seg_len - kw), kw


def _band_attend(blocks):
    scores = [lax.dot_general(qb, kb, (((1,), (1,)), ((), ())), preferred_element_type=F32) + bias
              for qb, kb, _, bias in blocks]
    maxes = [jnp.max(s, axis=-1, keepdims=True) for s in scores]
    probs = [jnp.exp2(s - m).astype(BF16) for s, m in zip(scores, maxes)]
    outs = [jnp.dot(p, blk[2], preferred_element_type=F32) for p, blk in zip(probs, blocks)]
    return [(oa[:, :HEAD_DIM], oa[:, HEAD_DIM:], jnp.broadcast_to(m, (ATT_QBLK, HEAD_DIM)))
            for oa, m in zip(outs, maxes)]


def _softmax_merge(a, b):
    (o1, l1, m1), (o2, l2, m2) = a, b
    m = jnp.maximum(m1, m2)
    a1 = jnp.exp2(m1 - m)
    a2 = jnp.exp2(m2 - m)
    return a1 * o1 + a2 * o2, a1 * l1 + a2 * l2, m


def _attn_kernel(q_ref, k_ref, v_ref, o_ref,
                 qf0, kf0, vf0, qf1, kf1, vf1, qb1, kb1, vb1, qb2, kb2, vb2,
                 ao, al, am, bo, bl, bm):
    s = q_ref.shape[0]
    c = HEAD_DIM
    n1 = s // 4
    n2 = s // 16
    biases = {}

    def bias(off, kw):
        if (off, kw) not in biases:
            biases[(off, kw)] = _band_bias(off, kw)
        return biases[(off, kw)]

    def grouped(items, compute, store, group=ATT_GROUP):
        for g0 in range(0, len(items), group):
            chunk = items[g0:g0 + group]
            for it, val in zip(chunk, compute(chunk)):
                store(*it, val)

    def each(fn):
        return lambda chunk: [fn(*it) for it in chunk]

    qf0[...] = q_ref[...].astype(F32)
    kf0[...] = k_ref[...].astype(F32)
    vf0[...] = v_ref[...].astype(F32)

    def split4_store(r, src, dstf, dstb, x):
        rows = slice(r * n1, (r + 1) * n1)
        dstf[rows, :] = x
        dstb[rows, :] = x.astype(BF16)

    grouped([(r, src, dstf, dstb) for r in range(4)
             for src, dstf, dstb in ((qf0, qf1, qb1), (kf0, kf1, kb1), (vf0, vf1, vb1))],
            each(lambda r, src, dstf, dstb: src[pl.ds(r, n1, stride=4), :]), split4_store)

    def split16_store(seg, srcf, dstb, x):
        dstb[seg * n2:(seg + 1) * n2, :] = x

    grouped([(seg, srcf, dstb) for seg in range(16) for srcf, dstb in ((qf1, qb2), (kf1, kb2), (vf1, vb2))],
            each(lambda seg, srcf, dstb: srcf[pl.ds((seg // 4) * n1 + seg % 4, n2, stride=4), :].astype(BF16)),
            split16_store)

    def attend(qb, kb, vb, chunk):
        return _band_attend([(qb[base + q0:base + q0 + ATT_QBLK, :], kb[base + w0:base + w0 + kw, :],
                              jnp.concatenate([vb[base + w0:base + w0 + kw, :], jnp.ones((kw, c), BF16)], axis=1),
                              bias(w0 - q0, kw)) for base, q0, w0, kw in chunk])

    def merged(cur, prev_refs, chunk):
        prev = [tuple(ref[base + q0:base + q0 + ATT_QBLK, :] for ref in prev_refs) for base, q0, _, _ in chunk]
        return [_softmax_merge(a, b) for a, b in zip(cur, prev)]

    def store16(base, q0, w0, kw, olm):
        seg = base // n2
        dst = pl.ds((seg // 4) * n1 + 4 * q0 + seg % 4, ATT_QBLK, stride=4)
        for ref, val in zip((ao, al, am), olm):
            ref[dst, :] = val

    grouped([(seg * n2, *blk) for seg in range(16) for blk in _band_blocks(n2)],
            lambda chunk: attend(qb2, kb2, vb2, chunk), store16)

    def store4(base, q0, w0, kw, olm):
        dst = pl.ds(4 * q0 + base // n1, ATT_QBLK, stride=4)
        for ref, val in zip((bo, bl, bm), olm):
            ref[dst, :] = val

    grouped([(r4 * n1, *blk) for r4 in range(4) for blk in _band_blocks(n1)],
            lambda chunk: merged(attend(qb1, kb1, vb1, chunk), (ao, al, am), chunk), store4)

    def store1(base, q0, w0, kw, olm):
        o, l, _ = olm
        o_ref[q0:q0 + ATT_QBLK, :] = (o / l).astype(o_ref.dtype)

    grouped([(0, *blk) for blk in _band_blocks(s)],
            lambda chunk: merged(attend(q_ref, k_ref, v_ref, chunk), (bo, bl, bm), chunk), store1)


def _dilated_attention(q, k, v):
    b, nh, s, _ = q.shape
    aw = nh * HEAD_DIM
    assert tuple(w // (2 * d) for w, d in DILATED_PATTERNS) == (ATT_RADIUS,) * 3
    assert tuple(d for _, d in DILATED_PATTERNS) == (1, 4, 16)
    assert s % (16 * ATT_QBLK) == 0
    c = HEAD_DIM
    spec = pl.BlockSpec((None, None, s, c), lambda i, h: (i, h, 0, 0))
    out_spec = pl.BlockSpec((None, s, c), lambda i, h: (i, 0, h))
    f32buf = pltpu.VMEM((s, c), F32)
    bf16buf = pltpu.VMEM((s, c), BF16)
    scratch = [f32buf] * 6 + [bf16buf] * 6 + [f32buf] * 6
    est = 12 * _nbytes((s, c), F32) + 10 * _nbytes((s, c), BF16) + 8 * _nbytes((s, c), BF16) + (16 << 20)
    return pl.pallas_call(
        _attn_kernel,
        grid=(b, aw // c),
        in_specs=[spec, spec, spec],
        out_specs=out_spec,
        out_shape=jax.ShapeDtypeStruct((b, s, aw), BF16),
        scratch_shapes=scratch,
        compiler_params=_cparams(("parallel", "arbitrary"), est),
        name="dilated_attention",
    )(q, k, v)


def _filter_kernel(z_ref, w1_ref, b1_ref, w2_ref, b2_ref, fr_ref, w3_ref, t_ref, dl_ref,
                   filt_ref, asum_ref, hi_ref, lo_ref, *, blocks_per_dir):
    def split(a):
        hi = a.astype(BF16)
        return hi, (a - hi.astype(F32)).astype(BF16)

    @pl.when(pl.program_id(0) == 0)
    def _():
        fr = fr_ref[...]
        hdn = jnp.sin(fr * (jnp.dot(z_ref[...], w1_ref[...], precision=HIGHEST,
                                    preferred_element_type=F32) + b1_ref[...]))
        hdn = jnp.sin(fr * (jnp.dot(hdn, w2_ref[...], precision=HIGHEST,
                                    preferred_element_type=F32) + b2_ref[...]))
        hi_ref[...], lo_ref[...] = split(hdn)

    w_hi, w_lo = split(w3_ref[...])
    hf = (jnp.dot(hi_ref[...], w_hi, preferred_element_type=F32)
          + jnp.dot(hi_ref[...], w_lo, preferred_element_type=F32)
          + jnp.dot(lo_ref[...], w_hi, preferred_element_type=F32))
    hf = hf * jnp.exp(-t_ref[...] * dl_ref[...])
    is_bwd = (pl.program_id(0) // blocks_per_dir) % 2 == 1
    row = lax.broadcasted_iota(jnp.int32, hf.shape, 0)
    hf = jnp.where(jnp.logical_and(row == 0, is_bwd), 0.0, hf)
    filt_ref[...] = hf.astype(filt_ref.dtype)
    asum_ref[...] = jnp.sum(jnp.abs(hf), axis=0, keepdims=True)


def _even_odd_rows(a):
    return jnp.concatenate([a[0::2], a[1::2]], axis=0)


def _hyena_filter_taps(seq_len, w1, b1, w2, b2, sin_freq, w3, hy_w, tn=512):
    ffn = w1.shape[1]
    pad = LANES
    t = jnp.linspace(0.0, 1.0, seq_len, dtype=F32)[:, None]
    wpos = 2.0 * math.pi * jnp.arange(seq_len, dtype=F32) / seq_len
    bands = jnp.linspace(1e-4, HY_BANDS - 1, HY_BANDS, dtype=F32)
    ang = wpos[:, None] * bands[None, :]
    z = jnp.concatenate([t, jnp.cos(ang), -jnp.sin(ang)], axis=-1)
    z = _even_odd_rows(jnp.pad(z, ((0, 0), (0, pad - z.shape[1]))))
    t = _even_odd_rows(t)
    w1p = jnp.pad(w1.astype(F32), ((0, pad - w1.shape[0]), (0, pad - ffn)))
    w2p = jnp.pad(w2.astype(F32), ((0, pad - ffn), (0, pad - ffn)))
    w3p = jnp.pad(w3.astype(F32), ((0, pad - ffn), (0, 0)))
    padv = lambda a: jnp.pad(a.astype(F32), (0, pad - ffn)).reshape(1, pad)
    deltas = jnp.abs(jnp.linspace(HY_MIN_DECAY, HY_MAX_DECAY, hy_w, dtype=F32)).reshape(1, hy_w)
    ncol = w3.shape[1]
    tn = min(tn, hy_w)
    bpd = hy_w // tn
    full = lambda shape: pl.BlockSpec(shape, lambda j: (0, 0))
    return pl.pallas_call(
        functools.partial(_filter_kernel, blocks_per_dir=bpd),
        grid=(ncol // tn,),
        in_specs=[full((seq_len, pad)), full((pad, pad)), full((1, pad)), full((pad, pad)), full((1, pad)),
                  full((1, pad)), pl.BlockSpec((pad, tn), lambda j: (0, j)), full((seq_len, 1)),
                  pl.BlockSpec((1, tn), lambda j: (0, j % bpd))],
        out_specs=[pl.BlockSpec((seq_len, tn), lambda j: (0, j)), pl.BlockSpec((1, tn), lambda j: (0, j))],
        out_shape=[jax.ShapeDtypeStruct((seq_len, ncol), BF16), jax.ShapeDtypeStruct((1, ncol), F32)],
        scratch_shapes=[pltpu.VMEM((seq_len, pad), BF16)] * 2,
        compiler_params=_cparams(("arbitrary",), 32 << 20),
        name="hyena_filter_taps",
    )(z, w1p, padv(b1), w2p, padv(b2), padv(sin_freq), w3p, t, deltas)


def _dft_matrices(s, n_total, tc):
    n = 2 * s
    k = jnp.arange(s, dtype=jnp.int32)

    def tables(cols):
        ang = ((k[:, None] * cols[None, :]) % n).astype(F32) * (2.0 * math.pi / n)
        return jnp.cos(ang), jnp.sin(ang)

    cl, sl = tables(jnp.arange(LANES, dtype=jnp.int32))
    ch, sh = tables(jnp.arange(s // LANES, dtype=jnp.int32) * LANES)
    cosm = (ch[:, :, None] * cl[:, None, :] - sh[:, :, None] * sl[:, None, :]).reshape(s, s)
    sinm = (sh[:, :, None] * cl[:, None, :] + ch[:, :, None] * sl[:, None, :]).reshape(s, s)
    alt = jnp.where(k % 2 == 0, 1.0, -1.0).astype(F32)
    im = (-sinm).at[0].set(alt)
    fwd = jnp.concatenate([cosm, im], axis=0)
    scale = jnp.full((n,), 2.0 / n_total, F32).at[0].set(1.0 / n_total).at[s].set(1.0 / n_total)
    inv = (fwd * scale[:, None]).T
    r = min(HY_FREQ_CHUNK, s)
    chunked = lambda a: a.reshape(2, s // r, r, -1).transpose(1, 0, 2, 3).reshape(n, -1)
    fwd = chunked(fwd)
    inv = chunked(inv.T).T
    theta = k.astype(F32) * (2.0 * math.pi / n_total)
    cw = jnp.broadcast_to(jnp.cos(theta)[:, None], (s, tc))
    sw = jnp.broadcast_to(jnp.sin(theta)[:, None], (s, tc))
    return fwd.astype(BF16), inv.astype(BF16), cw, sw


HY_FREQ_CHUNK = 256


def _freq_chunks(m):
    r = min(HY_FREQ_CHUNK, m)
    return [slice(r0, r0 + r) for r0 in range(0, m, r)]


def _first_row(shape):
    return lax.broadcasted_iota(jnp.int32, shape, 0) == 0


def _half_spectrum(z_ref, f_ref, cw_ref, sw_ref, rows):
    m = z_ref.shape[0] // 2
    r = rows.stop - rows.start
    frows = slice(2 * rows.start, 2 * rows.stop)
    ef = jnp.dot(f_ref[frows, :], z_ref[0:m, :], preferred_element_type=F32)
    of = jnp.dot(f_ref[frows, :], z_ref[m:2 * m, :], preferred_element_type=F32)
    er, ei, orr, oi = ef[:r], ef[r:], of[:r], of[r:]
    cw, sw = cw_ref[rows, :], sw_ref[rows, :]
    tr = orr * cw + oi * sw
    ti = oi * cw - orr * sw
    pim, qim = ei + ti, ti - ei
    if rows.start == 0:
        first = _first_row(pim.shape)
        pim, qim = jnp.where(first, ei, pim), jnp.where(first, -oi, qim)
    return er + tr, pim, er - tr, qim


def _spectrum_product(h, z, with_first):
    hpre, hpim, hqre, hqim = h
    zpre, zpim, zqre, zqim = z
    a = hpim * zpim
    b = hqim * zqim
    if not with_first:
        return hpre * zpre - a, hpre * zpim + hpim * zpre, hqre * zqre - b, hqre * zqim + hqim * zqre
    first = _first_row(a.shape)
    ypre = hpre * zpre - jnp.where(first, 0.0, a)
    ypim = jnp.where(first, a - b, hpre * zpim + hpim * zpre)
    yqre = hqre * zqre - jnp.where(first, 0.0, b)
    yqim = jnp.where(first, hpim * zqim + hqim * zpim, hqre * zqim + hqim * zqre)
    return ypre, ypim, yqre, yqim


def _inverse_butterfly(y, cw, sw, with_first, dtype):
    ypre, ypim, yqre, yqim = y
    gre = ypre + yqre
    gim = ypim - yqim
    dre = ypre - yqre
    dim = ypim + yqim
    g2re = dre * cw - dim * sw
    g2im = dre * sw + dim * cw
    if with_first:
        first = _first_row(gim.shape)
        gim, g2im = jnp.where(first, 2.0 * ypim, gim), jnp.where(first, -2.0 * yqim, g2im)
    return (jnp.concatenate([gre.astype(dtype), gim.astype(dtype)], axis=0),
            jnp.concatenate([g2re.astype(dtype), g2im.astype(dtype)], axis=0))


def _filter_spectrum_kernel(tf_ref, tb_ref, nf_ref, nb_ref, f_ref, cw_ref, sw_ref, o_ref):
    inv = 1.0 / (nf_ref[...] + nb_ref[...])
    for rows in _freq_chunks(cw_ref.shape[0]):
        fwd = _half_spectrum(tf_ref, f_ref, cw_ref, sw_ref, rows)
        bwd = _half_spectrum(tb_ref, f_ref, cw_ref, sw_ref, rows)
        bwd_pim = -bwd[1]
        if rows.start == 0:
            bwd_pim = jnp.where(_first_row(bwd_pim.shape), bwd[1], bwd_pim)
        o_ref[0, rows, :] = (fwd[0] + bwd[0]) * inv
        o_ref[1, rows, :] = (fwd[1] + bwd_pim) * inv
        o_ref[2, rows, :] = (fwd[2] + bwd[2]) * inv
        o_ref[3, rows, :] = (fwd[3] - bwd[3]) * inv


def _filter_spectrum(taps, asum, fwd, cw, sw, hy_w):
    l = taps.shape[0]
    m, tc = cw.shape
    nb = hy_w // tc
    col = lambda d: (lambda o, c: (0, (2 * o + d) * nb + c))
    const = lambda shape: pl.BlockSpec(shape, lambda o, c: (0,) * len(shape), pipeline_mode=pl.Buffered(1))
    est = (_nbytes((2 * m, m), BF16) + 4 * _nbytes((l, tc), BF16) + 2 * _nbytes((m, tc), F32)
           + 32 * _nbytes((m, tc), F32) + (8 << 20))
    return pl.pallas_call(
        _filter_spectrum_kernel,
        grid=(HY_ORDER, nb),
        in_specs=[pl.BlockSpec((l, tc), col(0)), pl.BlockSpec((l, tc), col(1)),
                  pl.BlockSpec((1, tc), col(0)), pl.BlockSpec((1, tc), col(1)),
                  const((2 * m, m)), const((m, tc)), const((m, tc))],
        out_specs=pl.BlockSpec((None, 4, m, tc), lambda o, c: (o, 0, 0, c)),
        out_shape=jax.ShapeDtypeStruct((HY_ORDER, 4, m, hy_w), F32),
        compiler_params=_cparams(("parallel", "arbitrary"), est),
        name="hyena_filter_spectrum",
    )(taps, taps, asum, asum, fwd, cw, sw)


def _hyena_conv_kernel(z_ref, g_ref, h_ref, b_ref, f_ref, inv_ref, cw_ref, sw_ref, o_ref, ge_ref, go_ref,
                       *scratch, natural_out):
    s, tc = z_ref.shape
    m = s // 2
    for rows in _freq_chunks(m):
        with_first = rows.start == 0
        spec = _half_spectrum(z_ref, f_ref, cw_ref, sw_ref, rows)
        prod = _spectrum_product(tuple(h_ref[i, rows, :] for i in range(4)), spec, with_first)
        ge, go = _inverse_butterfly(prod, cw_ref[rows, :], sw_ref[rows, :], with_first, ge_ref.dtype)
        ge_ref[2 * rows.start:2 * rows.stop, :] = ge
        go_ref[2 * rows.start:2 * rows.stop, :] = go
    bias = b_ref[...]
    for rows in _freq_chunks(m):
        odd = slice(m + rows.start, m + rows.stop)
        conv_e = jnp.dot(inv_ref[rows, :], ge_ref[...], preferred_element_type=F32)
        conv_o = jnp.dot(inv_ref[rows, :], go_ref[...], preferred_element_type=F32)
        out_e = g_ref[rows, :].astype(F32) * (conv_e + bias * z_ref[rows, :].astype(F32))
        out_o = g_ref[odd, :].astype(F32) * (conv_o + bias * z_ref[odd, :].astype(F32))
        if natural_out:
            (nat,) = scratch
            r = rows.stop - rows.start
            both = slice(2 * rows.start, 2 * rows.stop)
            for c in range(tc // LANES):
                cols = slice(c * LANES, (c + 1) * LANES)
                nat[c, pl.ds(2 * rows.start, r, stride=2), :] = out_e[:, cols]
                nat[c, pl.ds(2 * rows.start + 1, r, stride=2), :] = out_o[:, cols]
                o_ref[both, cols] = nat[c, both, :].astype(o_ref.dtype)
        else:
            o_ref[rows, :] = out_e.astype(o_ref.dtype)
            o_ref[odd, :] = out_o.astype(o_ref.dtype)


def _hyena_conv(z, z_col0, gate, gate_col0, filt, order, bias_row, mats, width, natural_out):
    fwd, inv, cw, sw = mats
    b, s, _ = z.shape
    m, tc = cw.shape
    nc = width // tc
    zb0, gb0 = z_col0 // tc, gate_col0 // tc
    const = lambda shape: pl.BlockSpec(shape, lambda c, i: (0,) * len(shape), pipeline_mode=pl.Buffered(1))
    scratch = [pltpu.VMEM((2 * m, tc), BF16)] * 2
    if natural_out:
        scratch.append(pltpu.VMEM((tc // LANES, s, LANES), F32))
    est = (2 * _nbytes((2 * m, m), BF16) + 2 * _nbytes((m, tc), F32) + 2 * _nbytes((4, m, tc), F32)
           + 6 * _nbytes((s, tc), BF16) + 24 * _nbytes((m, tc), F32) + (8 << 20))
    return pl.pallas_call(
        functools.partial(_hyena_conv_kernel, natural_out=natural_out),
        grid=(nc, b),
        in_specs=[pl.BlockSpec((None, s, tc), lambda c, i: (i, 0, zb0 + c)),
                  pl.BlockSpec((None, s, tc), lambda c, i: (i, 0, gb0 + c)),
                  pl.BlockSpec((None, 4, m, tc), lambda c, i: (order, 0, 0, c)),
                  pl.BlockSpec((1, tc), lambda c, i: (0, c)),
                  const((2 * m, m)), const((m, 2 * m)), const((m, tc)), const((m, tc))],
        out_specs=pl.BlockSpec((None, s, tc), lambda c, i: (i, 0, c)),
        out_shape=jax.ShapeDtypeStruct((b, s, width), BF16),
        scratch_shapes=scratch,
        compiler_params=_cparams(("parallel", "arbitrary"), est),
        name="hyena_conv",
    )(z, gate, filt, bias_row, fwd, inv, cw, sw)


def _chunk_rows(n_rows):
    return [slice(r0, r0 + INPROJ_ROW_CHUNK) for r0 in range(0, n_rows, INPROJ_ROW_CHUNK)]


def _merge_kernel(a_ref, hy_ref, wa_ref, wh_ref, g0_ref, g1_ref, o_ref):
    for rows in _chunk_rows(a_ref.shape[0]):
        ya = jnp.dot(a_ref[rows, :], wa_ref[...], preferred_element_type=F32)
        yh = jnp.dot(hy_ref[rows, :], wh_ref[...], preferred_element_type=F32)
        o_ref[rows, :] = (g0_ref[rows, :].astype(F32) * ya + g1_ref[rows, :].astype(F32) * yh).astype(o_ref.dtype)


def _weight_spec(k, tn, n):
    if tn == n:
        return pl.BlockSpec((k, n), lambda i, j: (0, 0), pipeline_mode=pl.Buffered(1))
    return pl.BlockSpec((k, tn), lambda i, j: (0, j))


def _gated_merge(attn, hy, w_att, w_hy, gates, tm=512, tn=2048):
    t, k = attn.shape
    n = w_att.shape[1]
    tm, tn = min(tm, t), min(tn, n)
    g1b0 = n // tn
    est = 4 * _nbytes((tm, k), BF16) + 4 * _nbytes((k, tn), BF16) + 6 * _nbytes((tm, tn), F32) + (8 << 20)
    return pl.pallas_call(
        _merge_kernel,
        grid=(t // tm, n // tn),
        in_specs=[pl.BlockSpec((tm, k), lambda i, j: (i, 0)),
                  pl.BlockSpec((tm, k), lambda i, j: (i, 0)),
                  _weight_spec(k, tn, n),
                  _weight_spec(k, tn, n),
                  pl.BlockSpec((tm, tn), lambda i, j: (i, j)),
                  pl.BlockSpec((tm, tn), lambda i, j: (i, g1b0 + j))],
        out_specs=pl.BlockSpec((tm, tn), lambda i, j: (i, j)),
        out_shape=jax.ShapeDtypeStruct((t, n), BF16),
        compiler_params=_cparams(("parallel", "arbitrary"), est),
        name="gated_merge",
    )(attn, hy, w_att, w_hy, gates, gates)


def _mm_residual_kernel(a_ref, w_ref, ra_ref, rb_ref, o_ref, *, na):
    def body(r_ref):
        for rows in _chunk_rows(a_ref.shape[0]):
            o_ref[rows, :] = r_ref[rows, :] + jnp.dot(a_ref[rows, :], w_ref[...], preferred_element_type=F32)

    _pick_group(na, ra_ref, rb_ref, body)


def _matmul_residual(a, w, res_a, res_b, tm=512, tn=2048):
    t, k = a.shape
    n = w.shape[1]
    tm, tn = min(tm, res_a.shape[0]), min(tn, n)
    assert res_a.shape[0] % tm == 0 and res_b.shape[0] % tm == 0
    na = res_a.shape[0] // tm
    est = 2 * _nbytes((tm, k), BF16) + 2 * _nbytes((k, tn), BF16) + 8 * _nbytes((tm, tn), F32) + (8 << 20)
    return pl.pallas_call(
        functools.partial(_mm_residual_kernel, na=na),
        grid=(t // tm, n // tn),
        in_specs=[pl.BlockSpec((tm, k), lambda i, j: (i, 0)),
                  _weight_spec(k, tn, n),
                  *_two_group_specs((tm, tn), na, col=lambda j: j)],
        out_specs=pl.BlockSpec((tm, tn), lambda i, j: (i, j)),
        out_shape=jax.ShapeDtypeStruct((t, n), F32),
        compiler_params=_cparams(("parallel", "arbitrary"), est),
        name="matmul_residual",
    )(a, w, res_a, res_b)


def _rmsnorm_rows(x, g):
    ms = jnp.mean(x * x, axis=-1, keepdims=True)
    return x * lax.rsqrt(ms + EPS) * g


def _mlp_kernel(x_ref, g_ref, wu_ref, wd_ref, o_ref, hm_ref):
    def body(first):
        for rows in _chunk_rows(hm_ref.shape[0]):
            if first:
                x = x_ref[rows, :]
                hm = _rmsnorm_rows(x, g_ref[...]).astype(hm_ref.dtype)
                hm_ref[rows, :] = hm
            else:
                hm = hm_ref[rows, :]
            a = jnp.dot(hm, wu_ref[...], preferred_element_type=F32)
            a = jnp.square(jnp.maximum(a, 0.0)).astype(BF16)
            y = jnp.dot(a, wd_ref[...], preferred_element_type=F32)
            o_ref[rows, :] = (x if first else o_ref[rows, :]) + y

    j = pl.program_id(1)
    pl.when(j == 0)(lambda: body(True))
    pl.when(j > 0)(lambda: body(False))


def _relu2_mlp(x, g, w_up, w_down, tm=1024, tf=1024):
    t, d = x.shape
    ff = w_up.shape[1]
    tm, tf = min(tm, t), min(tf, ff)
    est = (4 * _nbytes((tm, d), F32) + _nbytes((tm, d), BF16) + 4 * _nbytes((d, tf), BF16)
           + 3 * _nbytes((tm, tf), F32) + (8 << 20))
    return pl.pallas_call(
        _mlp_kernel,
        grid=(t // tm, ff // tf),
        in_specs=[pl.BlockSpec((tm, d), lambda i, j: (i, 0)),
                  pl.BlockSpec((1, d), lambda i, j: (0, 0)),
                  pl.BlockSpec((d, tf), lambda i, j: (0, j)),
                  pl.BlockSpec((tf, d), lambda i, j: (j, 0))],
        out_specs=pl.BlockSpec((tm, d), lambda i, j: (i, 0)),
        out_shape=jax.ShapeDtypeStruct((t, d), F32),
        scratch_shapes=[pltpu.VMEM((tm, d), BF16)],
        compiler_params=_cparams(("parallel", "arbitrary"), est),
        name="relu2_mlp",
    )(x, g.reshape(1, d), w_up, w_down)


def _ple_kernel(x_ref, g_ref, wg_ref, p_ref, wp_ref, o_ref):
    for rows in _chunk_rows(x_ref.shape[0]):
        x = x_ref[rows, :]
        hn = _rmsnorm_rows(x, g_ref[...]).astype(BF16)
        gate = jax.nn.sigmoid(jnp.dot(hn, wg_ref[...], preferred_element_type=F32))
        proj = jnp.dot(p_ref[rows, :].astype(BF16), wp_ref[...], preferred_element_type=F32)
        o_ref[rows, :] = x + gate * proj


def _gated_ple(x, row0, p, g, w_gate, w_proj, tm=512):
    t, pd = p.shape
    d = x.shape[1]
    tm = min(tm, t)
    rb0 = row0 // tm
    est = (4 * _nbytes((tm, d), F32) + _nbytes((d, d), BF16) + _nbytes((pd, d), BF16)
           + 2 * _nbytes((tm, pd), F32) + 8 * _nbytes((INPROJ_ROW_CHUNK, d), F32) + (8 << 20))
    return pl.pallas_call(
        _ple_kernel,
        grid=(t // tm, 1),
        in_specs=[pl.BlockSpec((tm, d), lambda i, j: (rb0 + i, 0)),
                  pl.BlockSpec((1, d), lambda i, j: (0, 0)),
                  _weight_spec(d, d, d),
                  pl.BlockSpec((tm, pd), lambda i, j: (i, 0)),
                  _weight_spec(pd, d, d)],
        out_specs=pl.BlockSpec((tm, d), lambda i, j: (i, 0)),
        out_shape=jax.ShapeDtypeStruct((t, d), F32),
        compiler_params=_cparams(("parallel", "arbitrary"), est),
        name="gated_ple",
    )(x, g.reshape(1, d), w_gate, p, w_proj)


def _encoder_layer(x_list, p_list, norm_mix_g, w_in, gate_b, q_norm_g, k_norm_g, hy_conv_w, hy_conv_b,
                   hy_w1, hy_b1, hy_w2, hy_b2, hy_sin_freq, hy_w3, hy_bias,
                   w_att_out, w_hy_out, w_o, norm_mlp_g, w_up, w_down,
                   norm_ple_g, w_ple_gate, w_ple_proj):
    xa3, xb3 = x_list
    s, d = xa3.shape[1:]
    b = xa3.shape[0] + xb3.shape[0]
    att_w = w_att_out.shape[0]
    hy_w = w_hy_out.shape[0]
    xa, xb = xa3.reshape(-1, d), xb3.reshape(-1, d)
    bf = lambda a: a.astype(BF16)

    h3 = _rmsnorm_cast(xa, xb, norm_mix_g).reshape(b, s, d)
    w_in_b = bf(w_in)
    rope_c, rope_s1, rope_s2 = _rope_tables(s)
    tab_spec = pl.BlockSpec((s, HEAD_DIM), lambda i, j: (0, 0))
    vec_spec = lambda tn, blk0=0: pl.BlockSpec((1, tn), lambda i, j: (0, blk0 + j))
    head_spec = pl.BlockSpec((1, HEAD_DIM), lambda i, j: (0, 0))
    tn = 1024
    qk = lambda g, col0, scale: _inproj(
        h3, w_in_b, col0, att_w, functools.partial(_inproj_qk_kernel, scale=scale),
        (g.reshape(1, HEAD_DIM), rope_c, rope_s1, rope_s2), [head_spec, tab_spec, tab_spec, tab_spec], tn,
        head_major=True)
    q = qk(q_norm_g, 0, HEAD_DIM ** -0.5 * math.log2(math.e))
    k = qk(k_norm_g, att_w, 1.0)
    v = _inproj(h3, w_in_b, 2 * att_w, att_w, _inproj_heads_kernel, (), [], tn, head_major=True)
    tnc = min(tn, hy_w)
    u = _inproj(h3, w_in_b, 3 * att_w, 3 * hy_w, _inproj_conv_kernel,
                (hy_conv_w, hy_conv_b.reshape(1, 3 * hy_w)),
                [pl.BlockSpec((3, tnc), lambda i, j: (0, j)), vec_spec(tnc)], tnc,
                scratch=[pltpu.VMEM((tnc // LANES, s + 2 * CONV_PAD_ROWS, LANES), F32)])
    tng = min(tn, d)
    gates = _inproj(h3, w_in_b, 3 * att_w + 3 * hy_w, N_BRANCH * d, _inproj_gate_kernel,
                    (gate_b.reshape(1, N_BRANCH * d),), [vec_spec(tng)], tng)

    attn = _dilated_attention(q, k, v)

    mats = _dft_matrices(s // 2, 2 * s, min(256, hy_w))
    taps, tap_asum = _hyena_filter_taps(s, hy_w1, hy_b1, hy_w2, hy_b2, hy_sin_freq, hy_w3, hy_w)
    filt = _filter_spectrum(taps, tap_asum, mats[0], mats[2], mats[3], hy_w)
    bias = hy_bias.astype(F32)
    z = _hyena_conv(u, 0, u, hy_w, filt, 0, bias[0:1], mats, hy_w, natural_out=False)
    hy = _hyena_conv(z, 0, u, 2 * hy_w, filt, 1, bias[1:2], mats, hy_w, natural_out=True)

    merged = _gated_merge(attn.reshape(b * s, att_w), hy.reshape(b * s, hy_w),
                          bf(w_att_out), bf(w_hy_out), gates.reshape(b * s, N_BRANCH * d))
    x2 = _matmul_residual(merged, bf(w_o), xa, xb)
    x2 = _relu2_mlp(x2, norm_mlp_g, bf(w_up), bf(w_down))

    outs, row0 = [], 0
    w_pg, w_pp = bf(w_ple_gate), bf(w_ple_proj)
    for p in p_list:
        pb = p.shape[0]
        p2 = p.reshape(pb * s, p.shape[-1])
        outs.append(_gated_ple(x2, row0, p2, norm_ple_g, w_pg, w_pp).reshape(pb, s, d))
        row0 += pb * s
    return outs


def kernel(x_prompt, x_sample, p_prompt, p_sample, norm_mix_g, w_in, gate_b, q_norm_g, k_norm_g, hy_conv_w, hy_conv_b, hy_w1, hy_b1, hy_w2, hy_b2, hy_sin_freq, hy_w3, hy_bias, w_att_out, w_hy_out, w_o, norm_mlp_g, w_up, w_down, norm_ple_g, w_ple_gate, w_ple_proj):
    assert x_prompt.shape[1:] == x_sample.shape[1:]
    y_prompt, y_sample = x_prompt, x_sample
    for i in range(w_in.shape[0]):
        y_prompt, y_sample = _encoder_layer(
            (y_prompt, y_sample), (p_prompt[i], p_sample[i]), norm_mix_g[i], w_in[i], gate_b[i], q_norm_g[i], k_norm_g[i],
            hy_conv_w[i], hy_conv_b[i], hy_w1[i], hy_b1[i], hy_w2[i], hy_b2[i], hy_sin_freq[i], hy_w3[i],
            hy_bias[i], w_att_out[i], w_hy_out[i], w_o[i], norm_mlp_g[i], w_up[i], w_down[i],
            norm_ple_g[i], w_ple_gate[i], w_ple_proj[i])
    return (y_prompt, y_sample)
```

```python
import functools
import math

import jax
import jax.numpy as jnp
from jax import lax
from jax.experimental import pallas as pl
from jax.experimental.pallas import tpu as pltpu

F32 = jnp.float32
BF16 = jnp.bfloat16
HIGHEST = lax.Precision.HIGHEST

EPS = 1e-6
HEAD_DIM = 128
ROPE_DIM = HEAD_DIM // 4
ROPE_THETA = 500000.0
DILATED_PATTERNS = ((128, 1), (512, 4), (2048, 16))
ATT_RADIUS = 64
ATT_QBLK = 128
ATT_KWIN = ATT_QBLK + 2 * ATT_RADIUS
ATT_GROUP = 8
HY_ORDER = 2
HY_EMB_DIM = 33
HY_BANDS = (HY_EMB_DIM - 1) // 2
HY_FAST_DECAY = 0.3
HY_SLOW_DECAY = 1.5
HY_DECAY_TARGET = 1e-2
HY_MIN_DECAY = math.log(HY_DECAY_TARGET) / HY_SLOW_DECAY
HY_MAX_DECAY = math.log(HY_DECAY_TARGET) / HY_FAST_DECAY
N_BRANCH = 2
LANES = 128
MASK_NEG = -1e30

VMEM_CAP_BYTES = 60 * 1024 * 1024


def _cparams(sem, est_bytes):
    limit = int(min(VMEM_CAP_BYTES, max(32 * 1024 * 1024, est_bytes)))
    return pltpu.CompilerParams(dimension_semantics=sem, vmem_limit_bytes=limit)


def _nbytes(shape, dtype):
    return math.prod(shape) * jnp.dtype(dtype).itemsize


def _two_group_specs(block, na, col=lambda *j: 0):
    return (pl.BlockSpec(block, lambda i, *j: (jnp.minimum(i, na - 1), jnp.where(i < na, col(*j), 0))),
            pl.BlockSpec(block, lambda i, *j: (jnp.maximum(i - na, 0), jnp.where(i >= na, col(*j), 0))))


def _pick_group(na, xa_ref, xb_ref, body):
    i = pl.program_id(0)
    pl.when(i < na)(lambda: body(xa_ref))
    pl.when(i >= na)(lambda: body(xb_ref))


def _rmsnorm_kernel(xa_ref, xb_ref, g_ref, o_ref, *, na):
    def body(x_ref):
        x = x_ref[...]
        ms = jnp.mean(x * x, axis=-1, keepdims=True)
        o_ref[...] = (x * lax.rsqrt(ms + EPS) * g_ref[...]).astype(o_ref.dtype)

    _pick_group(na, xa_ref, xb_ref, body)


def _rmsnorm_cast(xa, xb, g, tm=512):
    d = xa.shape[1]
    assert xa.shape[0] % tm == 0 and xb.shape[0] % tm == 0
    na, t = xa.shape[0] // tm, xa.shape[0] + xb.shape[0]
    return pl.pallas_call(
        functools.partial(_rmsnorm_kernel, na=na),
        grid=(t // tm,),
        in_specs=[*_two_group_specs((tm, d), na), pl.BlockSpec((1, d), lambda i: (0, 0))],
        out_specs=pl.BlockSpec((tm, d), lambda i: (i, 0)),
        out_shape=jax.ShapeDtypeStruct((t, d), BF16),
        compiler_params=_cparams(("parallel",), 6 * _nbytes((tm, d), F32)),
        name="rmsnorm_cast",
    )(xa, xb, g.reshape(1, d))


INPROJ_ROW_CHUNK = 256


def _row_chunks(h_ref, w_ref):
    for r0 in range(0, h_ref.shape[0], INPROJ_ROW_CHUNK):
        rows = slice(r0, r0 + INPROJ_ROW_CHUNK)
        yield rows, jnp.dot(h_ref[rows, :], w_ref[...], preferred_element_type=F32)


def _inproj_qk_kernel(h_ref, w_ref, g_ref, c_ref, s1_ref, s2_ref, o_ref, *, scale):
    gs = g_ref[...] * scale
    pw = 2 * HEAD_DIM
    same_head = (lax.broadcasted_iota(jnp.int32, (pw, pw), 0) // HEAD_DIM
                 == lax.broadcasted_iota(jnp.int32, (pw, pw), 1) // HEAD_DIM)
    ones_bd = jnp.where(same_head, 1.0, 0.0).astype(BF16)
    for rows, y in _row_chunks(h_ref, w_ref):
        yy = (y * y).astype(BF16)
        c, s1, s2 = c_ref[rows, :], s1_ref[rows, :], s2_ref[rows, :]
        for hh in range(y.shape[1] // HEAD_DIM):
            sl = slice(hh * HEAD_DIM, (hh + 1) * HEAD_DIM)
            if hh % 2 == 0:
                ss_pair = jnp.dot(yy[:, hh * HEAD_DIM:hh * HEAD_DIM + pw], ones_bd, preferred_element_type=F32)
            ms = ss_pair[:, (hh % 2) * HEAD_DIM:(hh % 2 + 1) * HEAD_DIM] * (1.0 / HEAD_DIM)
            t = y[:, sl] * lax.rsqrt(ms + EPS) * gs
            out = (t * c + pltpu.roll(t, HEAD_DIM - ROPE_DIM // 2, 1) * s1 + pltpu.roll(t, ROPE_DIM // 2, 1) * s2)
            o_ref[hh, rows, :] = out.astype(o_ref.dtype)


def _inproj_heads_kernel(h_ref, w_ref, o_ref):
    for rows, y in _row_chunks(h_ref, w_ref):
        for hh in range(y.shape[1] // HEAD_DIM):
            o_ref[hh, rows, :] = y[:, hh * HEAD_DIM:(hh + 1) * HEAD_DIM].astype(o_ref.dtype)


CONV_PAD_ROWS = 16


def _inproj_conv_kernel(h_ref, w_ref, cw_ref, cb_ref, o_ref, scr):
    s, tn = h_ref.shape[0], w_ref.shape[1]
    m = s // 2
    half = INPROJ_ROW_CHUNK // 2
    pad = CONV_PAD_ROWS
    zeros = jnp.zeros((pad, LANES), F32)
    for c in range(tn // LANES):
        scr[c, 0:pad, :] = zeros
        scr[c, pad + s:2 * pad + s, :] = zeros

    def finish(r0):
        for c in range(tn // LANES):
            cols = slice(c * LANES, (c + 1) * LANES)
            w0, w1, w2, cb = cw_ref[0:1, cols], cw_ref[1:2, cols], cw_ref[2:3, cols], cb_ref[:, cols]
            ye = scr[c, pl.ds(pad + r0, half, stride=2), :]
            yo = scr[c, pl.ds(pad + r0 + 1, half, stride=2), :]
            yo_prev = scr[c, pl.ds(pad + r0 - 1, half, stride=2), :]
            ye_next = scr[c, pl.ds(pad + r0 + 2, half, stride=2), :]
            o_ref[r0 // 2:r0 // 2 + half, cols] = (yo_prev * w0 + ye * w1 + yo * w2 + cb).astype(o_ref.dtype)
            o_ref[m + r0 // 2:m + r0 // 2 + half, cols] = (ye * w0 + yo * w1 + ye_next * w2 + cb).astype(o_ref.dtype)

    for rows, y in _row_chunks(h_ref, w_ref):
        for c in range(tn // LANES):
            scr[c, pad + rows.start:pad + rows.stop, :] = y[:, c * LANES:(c + 1) * LANES]
        if rows.start > 0:
            finish(rows.start - INPROJ_ROW_CHUNK)
    finish(s - INPROJ_ROW_CHUNK)


def _inproj_gate_kernel(h_ref, w_ref, gb_ref, o_ref):
    for rows, y in _row_chunks(h_ref, w_ref):
        o_ref[rows, :] = jax.nn.sigmoid(y + gb_ref[...]).astype(o_ref.dtype)


def _inproj(h3, w, col0, width, body, extras, extra_specs, tn, scratch=(), head_major=False):
    b, s, d = h3.shape
    tn = min(tn, width)
    nj = width // tn
    jb0 = col0 // tn
    est = (2 * _nbytes((s, d), BF16) + 2 * _nbytes((d, tn), BF16) + 2 * _nbytes((s, tn), BF16)
           + 4 * _nbytes((s, tn), F32) + (8 << 20))
    return pl.pallas_call(
        body,
        grid=(b, nj),
        in_specs=[pl.BlockSpec((None, s, d), lambda i, j: (i, 0, 0)),
                  pl.BlockSpec((d, tn), lambda i, j: (0, jb0 + j))] + extra_specs,
        out_specs=(pl.BlockSpec((None, tn // HEAD_DIM, s, HEAD_DIM), lambda i, j: (i, j, 0, 0)) if head_major
                   else pl.BlockSpec((None, s, tn), lambda i, j: (i, 0, j))),
        out_shape=jax.ShapeDtypeStruct((b, width // HEAD_DIM, s, HEAD_DIM) if head_major else (b, s, width), BF16),
        scratch_shapes=list(scratch),
        compiler_params=_cparams(("parallel", "arbitrary"), est),
        name="inproj_" + getattr(body, "func", body).__name__,
    )(h3, w, *extras)


def _rope_tables(s):
    half = ROPE_DIM // 2
    inv_freq = ROPE_THETA ** (-jnp.arange(half, dtype=F32) / half)
    ang = jnp.arange(s, dtype=F32)[:, None] * inv_freq[None, :]
    cos, sin = jnp.cos(ang), jnp.sin(ang)
    c = jnp.concatenate([cos, cos, jnp.ones((s, HEAD_DIM - ROPE_DIM), F32)], axis=1)
    s1 = jnp.concatenate([-sin, jnp.zeros((s, HEAD_DIM - half), F32)], axis=1)
    s2 = jnp.concatenate([jnp.zeros((s, half), F32), sin, jnp.zeros((s, HEAD_DIM - ROPE_DIM), F32)], axis=1)
    return c, s1, s2


def _band_bias(off, kw):
    col = lax.broadcasted_iota(jnp.int32, (ATT_QBLK, kw), 1)
    row = lax.broadcasted_iota(jnp.int32, (ATT_QBLK, kw), 0)
    return jnp.where(jnp.abs(col - row + off) <= ATT_RADIUS, 0.0, MASK_NEG).astype(F32)


def _band_blocks(seg_len):
    kw = min(ATT_KWIN, seg_len)
    for bi in range(seg_len // ATT_QBLK):
        q0 = bi * ATT_QBLK
        yield q0, min(max(q0 - ATT_RADIUS, 0), seg_len - kw), kw


def _band_attend(blocks):
    scores = [lax.dot_general(qb, kb, (((1,), (1,)), ((), ())), preferred_element_type=F32) + bias
              for qb, kb, _, bias in blocks]
    maxes = [jnp.max(s, axis=-1, keepdims=True) for s in scores]
    probs = [jnp.exp2(s - m).astype(BF16) for s, m in zip(scores, maxes)]
    outs = [jnp.dot(p, blk[2], preferred_element_type=F32) for p, blk in zip(probs, blocks)]
    return [(oa[:, :HEAD_DIM], oa[:, HEAD_DIM:], jnp.broadcast_to(m, (ATT_QBLK, HEAD_DIM)))
            for oa, m in zip(outs, maxes)]


def _softmax_merge(a, b):
    (o1, l1, m1), (o2, l2, m2) = a, b
    m = jnp.maximum(m1, m2)
    a1 = jnp.exp2(m1 - m)
    a2 = jnp.exp2(m2 - m)
    return a1 * o1 + a2 * o2, a1 * l1 + a2 * l2, m


def _attn_kernel(q_ref, k_ref, v_ref, o_ref,
                 qf0, kf0, vf0, qf1, kf1, vf1, qb1, kb1, vb1, qb2, kb2, vb2,
                 ao, al, am, bo, bl, bm):
    s = q_ref.shape[0]
    c = HEAD_DIM
    n1 = s // 4
    n2 = s // 16
    biases = {}

    def bias(off, kw):
        if (off, kw) not in biases:
            biases[(off, kw)] = _band_bias(off, kw)
        return biases[(off, kw)]

    def grouped(items, compute, store, group=ATT_GROUP):
        for g0 in range(0, len(items), group):
            chunk = items[g0:g0 + group]
            for it, val in zip(chunk, compute(chunk)):
                store(*it, val)

    def each(fn):
        return lambda chunk: [fn(*it) for it in chunk]

    qf0[...] = q_ref[...].astype(F32)
    kf0[...] = k_ref[...].astype(F32)
    vf0[...] = v_ref[...].astype(F32)

    def split4_store(r, src, dstf, dstb, x):
        rows = slice(r * n1, (r + 1) * n1)
        dstf[rows, :] = x
        dstb[rows, :] = x.astype(BF16)

    grouped([(r, src, dstf, dstb) for r in range(4)
             for src, dstf, dstb in ((qf0, qf1, qb1), (kf0, kf1, kb1), (vf0, vf1, vb1))],
            each(lambda r, src, dstf, dstb: src[pl.ds(r, n1, stride=4), :]), split4_store)

    def split16_store(seg, srcf, dstb, x):
        dstb[seg * n2:(seg + 1) * n2, :] = x

    grouped([(seg, srcf, dstb) for seg in range(16) for srcf, dstb in ((qf1, qb2), (kf1, kb2), (vf1, vb2))],
            each(lambda seg, srcf, dstb: srcf[pl.ds((seg // 4) * n1 + seg % 4, n2, stride=4), :].astype(BF16)),
            split16_store)

    def attend(qb, kb, vb, chunk):
        return _band_attend([(qb[base + q0:base + q0 + ATT_QBLK, :], kb[base + w0:base + w0 + kw, :],
                              jnp.concatenate([vb[base + w0:base + w0 + kw, :], jnp.ones((kw, c), BF16)], axis=1),
                              bias(w0 - q0, kw)) for base, q0, w0, kw in chunk])

    def merged(cur, prev_refs, chunk):
        prev = [tuple(ref[base + q0:base + q0 + ATT_QBLK, :] for ref in prev_refs) for base, q0, _, _ in chunk]
        return [_softmax_merge(a, b) for a, b in zip(cur, prev)]

    def store16(base, q0, w0, kw, olm):
        seg = base // n2
        dst = pl.ds((seg // 4) * n1 + 4 * q0 + seg % 4, ATT_QBLK, stride=4)
        for ref, val in zip((ao, al, am), olm):
            ref[dst, :] = val

    grouped([(seg * n2, *blk) for seg in range(16) for blk in _band_blocks(n2)],
            lambda chunk: attend(qb2, kb2, vb2, chunk), store16)

    def store4(base, q0, w0, kw, olm):
        dst = pl.ds(4 * q0 + base // n1, ATT_QBLK, stride=4)
        for ref, val in zip((bo, bl, bm), olm):
            ref[dst, :] = val

    grouped([(r4 * n1, *blk) for r4 in range(4) for blk in _band_blocks(n1)],
            lambda chunk: merged(attend(qb1, kb1, vb1, chunk), (ao, al, am), chunk), store4)

    def store1(base, q0, w0, kw, olm):
        o, l, _ = olm
        o_ref[q0:q0 + ATT_QBLK, :] = (o / l).astype(o_ref.dtype)

    grouped([(0, *blk) for blk in _band_blocks(s)],
            lambda chunk: merged(attend(q_ref, k_ref, v_ref, chunk), (bo, bl, bm), chunk), store1)


def _dilated_attention(q, k, v):
    b, nh, s, _ = q.shape
    aw = nh * HEAD_DIM
    assert tuple(w // (2 * d) for w, d in DILATED_PATTERNS) == (ATT_RADIUS,) * 3
    assert tuple(d for _, d in DILATED_PATTERNS) == (1, 4, 16)
    assert s % (16 * ATT_QBLK) == 0
    c = HEAD_DIM
    spec = pl.BlockSpec((None, None, s, c), lambda i, h: (i, h, 0, 0))
    out_spec = pl.BlockSpec((None, s, c), lambda i, h: (i, 0, h))
    f32buf = pltpu.VMEM((s, c), F32)
    bf16buf = pltpu.VMEM((s, c), BF16)
    scratch = [f32buf] * 6 + [bf16buf] * 6 + [f32buf] * 6
    est = 12 * _nbytes((s, c), F32) + 10 * _nbytes((s, c), BF16) + 8 * _nbytes((s, c), BF16) + (16 << 20)
    return pl.pallas_call(
        _attn_kernel,
        grid=(b, aw // c),
        in_specs=[spec, spec, spec],
        out_specs=out_spec,
        out_shape=jax.ShapeDtypeStruct((b, s, aw), BF16),
        scratch_shapes=scratch,
        compiler_params=_cparams(("parallel", "arbitrary"), est),
        name="dilated_attention",
    )(q, k, v)


def _filter_kernel(z_ref, w1_ref, b1_ref, w2_ref, b2_ref, fr_ref, w3_ref, t_ref, dl_ref,
                   filt_ref, asum_ref, hi_ref, lo_ref, *, blocks_per_dir):
    def split(a):
        hi = a.astype(BF16)
        return hi, (a - hi.astype(F32)).astype(BF16)

    @pl.when(pl.program_id(0) == 0)
    def _():
        fr = fr_ref[...]
        hdn = jnp.sin(fr * (jnp.dot(z_ref[...], w1_ref[...], precision=HIGHEST,
                                    preferred_element_type=F32) + b1_ref[...]))
        hdn = jnp.sin(fr * (jnp.dot(hdn, w2_ref[...], precision=HIGHEST,
                                    preferred_element_type=F32) + b2_ref[...]))
        hi_ref[...], lo_ref[...] = split(hdn)

    w_hi, w_lo = split(w3_ref[...])
    hf = (jnp.dot(hi_ref[...], w_hi, preferred_element_type=F32)
          + jnp.dot(hi_ref[...], w_lo, preferred_element_type=F32)
          + jnp.dot(lo_ref[...], w_hi, preferred_element_type=F32))
    hf = hf * jnp.exp(-t_ref[...] * dl_ref[...])
    is_bwd = (pl.program_id(0) // blocks_per_dir) % 2 == 1
    row = lax.broadcasted_iota(jnp.int32, hf.shape, 0)
    hf = jnp.where(jnp.logical_and(row == 0, is_bwd), 0.0, hf)
    filt_ref[...] = hf.astype(filt_ref.dtype)
    asum_ref[...] = jnp.sum(jnp.abs(hf), axis=0, keepdims=True)


def _even_odd_rows(a):
    return jnp.concatenate([a[0::2], a[1::2]], axis=0)


def _hyena_filter_taps(seq_len, w1, b1, w2, b2, sin_freq, w3, hy_w, tn=512):
    ffn = w1.shape[1]
    pad = LANES
    t = jnp.linspace(0.0, 1.0, seq_len, dtype=F32)[:, None]
    wpos = 2.0 * math.pi * jnp.arange(seq_len, dtype=F32) / seq_len
    bands = jnp.linspace(1e-4, HY_BANDS - 1, HY_BANDS, dtype=F32)
    ang = wpos[:, None] * bands[None, :]
    z = jnp.concatenate([t, jnp.cos(ang), -jnp.sin(ang)], axis=-1)
    z = _even_odd_rows(jnp.pad(z, ((0, 0), (0, pad - z.shape[1]))))
    t = _even_odd_rows(t)
    w1p = jnp.pad(w1.astype(F32), ((0, pad - w1.shape[0]), (0, pad - ffn)))
    w2p = jnp.pad(w2.astype(F32), ((0, pad - ffn), (0, pad - ffn)))
    w3p = jnp.pad(w3.astype(F32), ((0, pad - ffn), (0, 0)))
    padv = lambda a: jnp.pad(a.astype(F32), (0, pad - ffn)).reshape(1, pad)
    deltas = jnp.abs(jnp.linspace(HY_MIN_DECAY, HY_MAX_DECAY, hy_w, dtype=F32)).reshape(1, hy_w)
    ncol = w3.shape[1]
    tn = min(tn, hy_w)
    bpd = hy_w // tn
    full = lambda shape: pl.BlockSpec(shape, lambda j: (0, 0))
    return pl.pallas_call(
        functools.partial(_filter_kernel, blocks_per_dir=bpd),
        grid=(ncol // tn,),
        in_specs=[full((seq_len, pad)), full((pad, pad)), full((1, pad)), full((pad, pad)), full((1, pad)),
                  full((1, pad)), pl.BlockSpec((pad, tn), lambda j: (0, j)), full((seq_len, 1)),
                  pl.BlockSpec((1, tn), lambda j: (0, j % bpd))],
        out_specs=[pl.BlockSpec((seq_len, tn), lambda j: (0, j)), pl.BlockSpec((1, tn), lambda j: (0, j))],
        out_shape=[jax.ShapeDtypeStruct((seq_len, ncol), BF16), jax.ShapeDtypeStruct((1, ncol), F32)],
        scratch_shapes=[pltpu.VMEM((seq_len, pad), BF16)] * 2,
        compiler_params=_cparams(("arbitrary",), 32 << 20),
        name="hyena_filter_taps",
    )(z, w1p, padv(b1), w2p, padv(b2), padv(sin_freq), w3p, t, deltas)


def _dft_matrices(s, n_total, tc):
    n = 2 * s
    k = jnp.arange(s, dtype=jnp.int32)

    def tables(cols):
        ang = ((k[:, None] * cols[None, :]) % n).astype(F32) * (2.0 * math.pi / n)
        return jnp.cos(ang), jnp.sin(ang)

    cl, sl = tables(jnp.arange(LANES, dtype=jnp.int32))
    ch, sh = tables(jnp.arange(s // LANES, dtype=jnp.int32) * LANES)
    cosm = (ch[:, :, None] * cl[:, None, :] - sh[:, :, None] * sl[:, None, :]).reshape(s, s)
    sinm = (sh[:, :, None] * cl[:, None, :] + ch[:, :, None] * sl[:, None, :]).reshape(s, s)
    alt = jnp.where(k % 2 == 0, 1.0, -1.0).astype(F32)
    im = (-sinm).at[0].set(alt)
    fwd = jnp.concatenate([cosm, im], axis=0)
    scale = jnp.full((n,), 2.0 / n_total, F32).at[0].set(1.0 / n_total).at[s].set(1.0 / n_total)
    inv = (fwd * scale[:, None]).T
    r = min(HY_FREQ_CHUNK, s)
    chunked = lambda a: a.reshape(2, s // r, r, -1).transpose(1, 0, 2, 3).reshape(n, -1)
    fwd = chunked(fwd)
    inv = chunked(inv.T).T
    theta = k.astype(F32) * (2.0 * math.pi / n_total)
    cw = jnp.broadcast_to(jnp.cos(theta)[:, None], (s, tc))
    sw = jnp.broadcast_to(jnp.sin(theta)[:, None], (s, tc))
    return fwd.astype(BF16), inv.astype(BF16), cw, sw


HY_FREQ_CHUNK = 256


def _freq_chunks(m):
    r = min(HY_FREQ_CHUNK, m)
    return [slice(r0, r0 + r) for r0 in range(0, m, r)]


def _first_row(shape):
    return lax.broadcasted_iota(jnp.int32, shape, 0) == 0


def _half_spectrum(z_ref, f_ref, cw_ref, sw_ref, rows):
    m = z_ref.shape[0] // 2
    r = rows.stop - rows.start
    frows = slice(2 * rows.start, 2 * rows.stop)
    ef = jnp.dot(f_ref[frows, :], z_ref[0:m, :], preferred_element_type=F32)
    of = jnp.dot(f_ref[frows, :], z_ref[m:2 * m, :], preferred_element_type=F32)
    er, ei, orr, oi = ef[:r], ef[r:], of[:r], of[r:]
    cw, sw = cw_ref[rows, :], sw_ref[rows, :]
    tr = orr * cw + oi * sw
    ti = oi * cw - orr * sw
    pim, qim = ei + ti, ti - ei
    if rows.start == 0:
        first = _first_row(pim.shape)
        pim, qim = jnp.where(first, ei, pim), jnp.where(first, -oi, qim)
    return er + tr, pim, er - tr, qim


def _spectrum_product(h, z, with_first):
    hpre, hpim, hqre, hqim = h
    zpre, zpim, zqre, zqim = z
    a = hpim * zpim
    b = hqim * zqim
    if not with_first:
        return hpre * zpre - a, hpre * zpim + hpim * zpre, hqre * zqre - b, hqre * zqim + hqim * zqre
    first = _first_row(a.shape)
    ypre = hpre * zpre - jnp.where(first, 0.0, a)
    ypim = jnp.where(first, a - b, hpre * zpim + hpim * zpre)
    yqre = hqre * zqre - jnp.where(first, 0.0, b)
    yqim = jnp.where(first, hpim * zqim + hqim * zpim, hqre * zqim + hqim * zqre)
    return ypre, ypim, yqre, yqim


def _inverse_butterfly(y, cw, sw, with_first, dtype):
    ypre, ypim, yqre, yqim = y
    gre = ypre + yqre
    gim = ypim - yqim
    dre = ypre - yqre
    dim = ypim + yqim
    g2re = dre * cw - dim * sw
    g2im = dre * sw + dim * cw
    if with_first:
        first = _first_row(gim.shape)
        gim, g2im = jnp.where(first, 2.0 * ypim, gim), jnp.where(first, -2.0 * yqim, g2im)
    return (jnp.concatenate([gre.astype(dtype), gim.astype(dtype)], axis=0),
            jnp.concatenate([g2re.astype(dtype), g2im.astype(dtype)], axis=0))


def _filter_spectrum_kernel(tf_ref, tb_ref, nf_ref, nb_ref, f_ref, cw_ref, sw_ref, o_ref):
    inv = 1.0 / (nf_ref[...] + nb_ref[...])
    for rows in _freq_chunks(cw_ref.shape[0]):
        fwd = _half_spectrum(tf_ref, f_ref, cw_ref, sw_ref, rows)
        bwd = _half_spectrum(tb_ref, f_ref, cw_ref, sw_ref, rows)
        bwd_pim = -bwd[1]
        if rows.start == 0:
            bwd_pim = jnp.where(_first_row(bwd_pim.shape), bwd[1], bwd_pim)
        o_ref[0, rows, :] = (fwd[0] + bwd[0]) * inv
        o_ref[1, rows, :] = (fwd[1] + bwd_pim) * inv
        o_ref[2, rows, :] = (fwd[2] + bwd[2]) * inv
        o_ref[3, rows, :] = (fwd[3] - bwd[3]) * inv


def _filter_spectrum(taps, asum, fwd, cw, sw, hy_w):
    l = taps.shape[0]
    m, tc = cw.shape
    nb = hy_w // tc
    col = lambda d: (lambda o, c: (0, (2 * o + d) * nb + c))
    const = lambda shape: pl.BlockSpec(shape, lambda o, c: (0,) * len(shape), pipeline_mode=pl.Buffered(1))
    est = (_nbytes((2 * m, m), BF16) + 4 * _nbytes((l, tc), BF16) + 2 * _nbytes((m, tc), F32)
           + 32 * _nbytes((m, tc), F32) + (8 << 20))
    return pl.pallas_call(
        _filter_spectrum_kernel,
        grid=(HY_ORDER, nb),
        in_specs=[pl.BlockSpec((l, tc), col(0)), pl.BlockSpec((l, tc), col(1)),
                  pl.BlockSpec((1, tc), col(0)), pl.BlockSpec((1, tc), col(1)),
                  const((2 * m, m)), const((m, tc)), const((m, tc))],
        out_specs=pl.BlockSpec((None, 4, m, tc), lambda o, c: (o, 0, 0, c)),
        out_shape=jax.ShapeDtypeStruct((HY_ORDER, 4, m, hy_w), F32),
        compiler_params=_cparams(("parallel", "arbitrary"), est),
        name="hyena_filter_spectrum",
    )(taps, taps, asum, asum, fwd, cw, sw)


def _hyena_conv_kernel(z_ref, g_ref, h_ref, b_ref, f_ref, inv_ref, cw_ref, sw_ref, o_ref, ge_ref, go_ref,
                       *scratch, natural_out):
    s, tc = z_ref.shape
    m = s // 2
    for rows in _freq_chunks(m):
        with_first = rows.start == 0
        spec = _half_spectrum(z_ref, f_ref, cw_ref, sw_ref, rows)
        prod = _spectrum_product(tuple(h_ref[i, rows, :] for i in range(4)), spec, with_first)
        ge, go = _inverse_butterfly(prod, cw_ref[rows, :], sw_ref[rows, :], with_first, ge_ref.dtype)
        ge_ref[2 * rows.start:2 * rows.stop, :] = ge
        go_ref[2 * rows.start:2 * rows.stop, :] = go
    bias = b_ref[...]
    for rows in _freq_chunks(m):
        odd = slice(m + rows.start, m + rows.stop)
        conv_e = jnp.dot(inv_ref[rows, :], ge_ref[...], preferred_element_type=F32)
        conv_o = jnp.dot(inv_ref[rows, :], go_ref[...], preferred_element_type=F32)
        out_e = g_ref[rows, :].astype(F32) * (conv_e + bias * z_ref[rows, :].astype(F32))
        out_o = g_ref[odd, :].astype(F32) * (conv_o + bias * z_ref[odd, :].astype(F32))
        if natural_out:
            (nat,) = scratch
            r = rows.stop - rows.start
            both = slice(2 * rows.start, 2 * rows.stop)
            for c in range(tc // LANES):
                cols = slice(c * LANES, (c + 1) * LANES)
                nat[c, pl.ds(2 * rows.start, r, stride=2), :] = out_e[:, cols]
                nat[c, pl.ds(2 * rows.start + 1, r, stride=2), :] = out_o[:, cols]
                o_ref[both, cols] = nat[c, both, :].astype(o_ref.dtype)
        else:
            o_ref[rows, :] = out_e.astype(o_ref.dtype)
            o_ref[odd, :] = out_o.astype(o_ref.dtype)


def _hyena_conv(z, z_col0, gate, gate_col0, filt, order, bias_row, mats, width, natural_out):
    fwd, inv, cw, sw = mats
    b, s, _ = z.shape
    m, tc = cw.shape
    nc = width // tc
    zb0, gb0 = z_col0 // tc, gate_col0 // tc
    const = lambda shape: pl.BlockSpec(shape, lambda c, i: (0,) * len(shape), pipeline_mode=pl.Buffered(1))
    scratch = [pltpu.VMEM((2 * m, tc), BF16)] * 2
    if natural_out:
        scratch.append(pltpu.VMEM((tc // LANES, s, LANES), F32))
    est = (2 * _nbytes((2 * m, m), BF16) + 2 * _nbytes((m, tc), F32) + 2 * _nbytes((4, m, tc), F32)
           + 6 * _nbytes((s, tc), BF16) + 24 * _nbytes((m, tc), F32) + (8 << 20))
    return pl.pallas_call(
        functools.partial(_hyena_conv_kernel, natural_out=natural_out),
        grid=(nc, b),
        in_specs=[pl.BlockSpec((None, s, tc), lambda c, i: (i, 0, zb0 + c)),
                  pl.BlockSpec((None, s, tc), lambda c, i: (i, 0, gb0 + c)),
                  pl.BlockSpec((None, 4, m, tc), lambda c, i: (order, 0, 0, c)),
                  pl.BlockSpec((1, tc), lambda c, i: (0, c)),
                  const((2 * m, m)), const((m, 2 * m)), const((m, tc)), const((m, tc))],
        out_specs=pl.BlockSpec((None, s, tc), lambda c, i: (i, 0, c)),
        out_shape=jax.ShapeDtypeStruct((b, s, width), BF16),
        scratch_shapes=scratch,
        compiler_params=_cparams(("parallel", "arbitrary"), est),
        name="hyena_conv",
    )(z, gate, filt, bias_row, fwd, inv, cw, sw)


def _chunk_rows(n_rows):
    return [slice(r0, r0 + INPROJ_ROW_CHUNK) for r0 in range(0, n_rows, INPROJ_ROW_CHUNK)]


def _merge_kernel(a_ref, hy_ref, wa_ref, wh_ref, g0_ref, g1_ref, o_ref):
    for rows in _chunk_rows(a_ref.shape[0]):
        ya = jnp.dot(a_ref[rows, :], wa_ref[...], preferred_element_type=F32)
        yh = jnp.dot(hy_ref[rows, :], wh_ref[...], preferred_element_type=F32)
        o_ref[rows, :] = (g0_ref[rows, :].astype(F32) * ya + g1_ref[rows, :].astype(F32) * yh).astype(o_ref.dtype)


def _weight_spec(k, tn, n):
    if tn == n:
        return pl.BlockSpec((k, n), lambda i, j: (0, 0), pipeline_mode=pl.Buffered(1))
    return pl.BlockSpec((k, tn), lambda i, j: (0, j))


def _gated_merge(attn, hy, w_att, w_hy, gates, tm=512, tn=2048):
    t, k = attn.shape
    n = w_att.shape[1]
    tm, tn = min(tm, t), min(tn, n)
    g1b0 = n // tn
    est = 4 * _nbytes((tm, k), BF16) + 4 * _nbytes((k, tn), BF16) + 6 * _nbytes((tm, tn), F32) + (8 << 20)
    return pl.pallas_call(
        _merge_kernel,
        grid=(t // tm, n // tn),
        in_specs=[pl.BlockSpec((tm, k), lambda i, j: (i, 0)),
                  pl.BlockSpec((tm, k), lambda i, j: (i, 0)),
                  _weight_spec(k, tn, n),
                  _weight_spec(k, tn, n),
                  pl.BlockSpec((tm, tn), lambda i, j: (i, j)),
                  pl.BlockSpec((tm, tn), lambda i, j: (i, g1b0 + j))],
        out_specs=pl.BlockSpec((tm, tn), lambda i, j: (i, j)),
        out_shape=jax.ShapeDtypeStruct((t, n), BF16),
        compiler_params=_cparams(("parallel", "arbitrary"), est),
        name="gated_merge",
    )(attn, hy, w_att, w_hy, gates, gates)


def _mm_residual_kernel(a_ref, w_ref, ra_ref, rb_ref, o_ref, *, na):
    def body(r_ref):
        for rows in _chunk_rows(a_ref.shape[0]):
            o_ref[rows, :] = r_ref[rows, :] + jnp.dot(a_ref[rows, :], w_ref[...], preferred_element_type=F32)

    _pick_group(na, ra_ref, rb_ref, body)


def _matmul_residual(a, w, res_a, res_b, tm=512, tn=2048):
    t, k = a.shape
    n = w.shape[1]
    tm, tn = min(tm, res_a.shape[0]), min(tn, n)
    assert res_a.shape[0] % tm == 0 and res_b.shape[0] % tm == 0
    na = res_a.shape[0] // tm
    est = 2 * _nbytes((tm, k), BF16) + 2 * _nbytes((k, tn), BF16) + 8 * _nbytes((tm, tn), F32) + (8 << 20)
    return pl.pallas_call(
        functools.partial(_mm_residual_kernel, na=na),
        grid=(t // tm, n // tn),
        in_specs=[pl.BlockSpec((tm, k), lambda i, j: (i, 0)),
                  _weight_spec(k, tn, n),
                  *_two_group_specs((tm, tn), na, col=lambda j: j)],
        out_specs=pl.BlockSpec((tm, tn), lambda i, j: (i, j)),
        out_shape=jax.ShapeDtypeStruct((t, n), F32),
        compiler_params=_cparams(("parallel", "arbitrary"), est),
        name="matmul_residual",
    )(a, w, res_a, res_b)


def _rmsnorm_rows(x, g):
    ms = jnp.mean(x * x, axis=-1, keepdims=True)
    return x * lax.rsqrt(ms + EPS) * g


def _mlp_kernel(x_ref, g_ref, wu_ref, wd_ref, o_ref, hm_ref):
    def body(first):
        for rows in _chunk_rows(hm_ref.shape[0]):
            if first:
                x = x_ref[rows, :]
                hm = _rmsnorm_rows(x, g_ref[...]).astype(hm_ref.dtype)
                hm_ref[rows, :] = hm
            else:
                hm = hm_ref[rows, :]
            a = jnp.dot(hm, wu_ref[...], preferred_element_type=F32)
            a = jnp.square(jnp.maximum(a, 0.0)).astype(BF16)
            y = jnp.dot(a, wd_ref[...], preferred_element_type=F32)
            o_ref[rows, :] = (x if first else o_ref[rows, :]) + y

    j = pl.program_id(1)
    pl.when(j == 0)(lambda: body(True))
    pl.when(j > 0)(lambda: body(False))


def _relu2_mlp(x, g, w_up, w_down, tm=1024, tf=1024):
    t, d = x.shape
    ff = w_up.shape[1]
    tm, tf = min(tm, t), min(tf, ff)
    est = (4 * _nbytes((tm, d), F32) + _nbytes((tm, d), BF16) + 4 * _nbytes((d, tf), BF16)
           + 3 * _nbytes((tm, tf), F32) + (8 << 20))
    return pl.pallas_call(
        _mlp_kernel,
        grid=(t // tm, ff // tf),
        in_specs=[pl.BlockSpec((tm, d), lambda i, j: (i, 0)),
                  pl.BlockSpec((1, d), lambda i, j: (0, 0)),
                  pl.BlockSpec((d, tf), lambda i, j: (0, j)),
                  pl.BlockSpec((tf, d), lambda i, j: (j, 0))],
        out_specs=pl.BlockSpec((tm, d), lambda i, j: (i, 0)),
        out_shape=jax.ShapeDtypeStruct((t, d), F32),
        scratch_shapes=[pltpu.VMEM((tm, d), BF16)],
        compiler_params=_cparams(("parallel", "arbitrary"), est),
        name="relu2_mlp",
    )(x, g.reshape(1, d), w_up, w_down)


def _ple_kernel(x_ref, g_ref, wg_ref, p_ref, wp_ref, o_ref):
    for rows in _chunk_rows(x_ref.shape[0]):
        x = x_ref[rows, :]
        hn = _rmsnorm_rows(x, g_ref[...]).astype(BF16)
        gate = jax.nn.sigmoid(jnp.dot(hn, wg_ref[...], preferred_element_type=F32))
        proj = jnp.dot(p_ref[rows, :].astype(BF16), wp_ref[...], preferred_element_type=F32)
        o_ref[rows, :] = x + gate * proj


def _gated_ple(x, row0, p, g, w_gate, w_proj, tm=512):
    t, pd = p.shape
    d = x.shape[1]
    tm = min(tm, t)
    rb0 = row0 // tm
    est = (4 * _nbytes((tm, d), F32) + _nbytes((d, d), BF16) + _nbytes((pd, d), BF16)
           + 2 * _nbytes((tm, pd), F32) + 8 * _nbytes((INPROJ_ROW_CHUNK, d), F32) + (8 << 20))
    return pl.pallas_call(
        _ple_kernel,
        grid=(t // tm, 1),
        in_specs=[pl.BlockSpec((tm, d), lambda i, j: (rb0 + i, 0)),
                  pl.BlockSpec((1, d), lambda i, j: (0, 0)),
                  _weight_spec(d, d, d),
                  pl.BlockSpec((tm, pd), lambda i, j: (i, 0)),
                  _weight_spec(pd, d, d)],
        out_specs=pl.BlockSpec((tm, d), lambda i, j: (i, 0)),
        out_shape=jax.ShapeDtypeStruct((t, d), F32),
        compiler_params=_cparams(("parallel", "arbitrary"), est),
        name="gated_ple",
    )(x, g.reshape(1, d), w_gate, p, w_proj)


def _encoder_layer(x_list, p_list, norm_mix_g, w_in, gate_b, q_norm_g, k_norm_g, hy_conv_w, hy_conv_b,
                   hy_w1, hy_b1, hy_w2, hy_b2, hy_sin_freq, hy_w3, hy_bias,
                   w_att_out, w_hy_out, w_o, norm_mlp_g, w_up, w_down,
                   norm_ple_g, w_ple_gate, w_ple_proj):
    xa3, xb3 = x_list
    s, d = xa3.shape[1:]
    b = xa3.shape[0] + xb3.shape[0]
    att_w = w_att_out.shape[0]
    hy_w = w_hy_out.shape[0]
    xa, xb = xa3.reshape(-1, d), xb3.reshape(-1, d)
    bf = lambda a: a.astype(BF16)

    h3 = _rmsnorm_cast(xa, xb, norm_mix_g).reshape(b, s, d)
    w_in_b = bf(w_in)
    rope_c, rope_s1, rope_s2 = _rope_tables(s)
    tab_spec = pl.BlockSpec((s, HEAD_DIM), lambda i, j: (0, 0))
    vec_spec = lambda tn, blk0=0: pl.BlockSpec((1, tn), lambda i, j: (0, blk0 + j))
    head_spec = pl.BlockSpec((1, HEAD_DIM), lambda i, j: (0, 0))
    tn = 1024
    qk = lambda g, col0, scale: _inproj(
        h3, w_in_b, col0, att_w, functools.partial(_inproj_qk_kernel, scale=scale),
        (g.reshape(1, HEAD_DIM), rope_c, rope_s1, rope_s2), [head_spec, tab_spec, tab_spec, tab_spec], tn,
        head_major=True)
    q = qk(q_norm_g, 0, HEAD_DIM ** -0.5 * math.log2(math.e))
    k = qk(k_norm_g, att_w, 1.0)
    v = _inproj(h3, w_in_b, 2 * att_w, att_w, _inproj_heads_kernel, (), [], tn, head_major=True)
    tnc = min(tn, hy_w)
    u = _inproj(h3, w_in_b, 3 * att_w, 3 * hy_w, _inproj_conv_kernel,
                (hy_conv_w, hy_conv_b.reshape(1, 3 * hy_w)),
                [pl.BlockSpec((3, tnc), lambda i, j: (0, j)), vec_spec(tnc)], tnc,
                scratch=[pltpu.VMEM((tnc // LANES, s + 2 * CONV_PAD_ROWS, LANES), F32)])
    tng = min(tn, d)
    gates = _inproj(h3, w_in_b, 3 * att_w + 3 * hy_w, N_BRANCH * d, _inproj_gate_kernel,
                    (gate_b.reshape(1, N_BRANCH * d),), [vec_spec(tng)], tng)

    attn = _dilated_attention(q, k, v)

    mats = _dft_matrices(s // 2, 2 * s, min(256, hy_w))
    taps, tap_asum = _hyena_filter_taps(s, hy_w1, hy_b1, hy_w2, hy_b2, hy_sin_freq, hy_w3, hy_w)
    filt = _filter_spectrum(taps, tap_asum, mats[0], mats[2], mats[3], hy_w)
    bias = hy_bias.astype(F32)
    z = _hyena_conv(u, 0, u, hy_w, filt, 0, bias[0:1], mats, hy_w, natural_out=False)
    hy = _hyena_conv(z, 0, u, 2 * hy_w, filt, 1, bias[1:2], mats, hy_w, natural_out=True)

    merged = _gated_merge(attn.reshape(b * s, att_w), hy.reshape(b * s, hy_w),
                          bf(w_att_out), bf(w_hy_out), gates.reshape(b * s, N_BRANCH * d))
    x2 = _matmul_residual(merged, bf(w_o), xa, xb)
    x2 = _relu2_mlp(x2, norm_mlp_g, bf(w_up), bf(w_down))

    outs, row0 = [], 0
    w_pg, w_pp = bf(w_ple_gate), bf(w_ple_proj)
    for p in p_list:
        pb = p.shape[0]
        p2 = p.reshape(pb * s, p.shape[-1])
        outs.append(_gated_ple(x2, row0, p2, norm_ple_g, w_pg, w_pp).reshape(pb, s, d))
        row0 += pb * s
    return outs


def kernel(x_prompt, x_sample, p_prompt, p_sample, norm_mix_g, w_in, gate_b, q_norm_g, k_norm_g, hy_conv_w, hy_conv_b, hy_w1, hy_b1, hy_w2, hy_b2, hy_sin_freq, hy_w3, hy_bias, w_att_out, w_hy_out, w_o, norm_mlp_g, w_up, w_down, norm_ple_g, w_ple_gate, w_ple_proj):
    assert x_prompt.shape[1:] == x_sample.shape[1:]
    y_prompt, y_sample = x_prompt, x_sample
    for i in range(w_in.shape[0]):
        y_prompt, y_sample = _encoder_layer(
            (y_prompt, y_sample), (p_prompt[i], p_sample[i]), norm_mix_g[i], w_in[i], gate_b[i], q_norm_g[i], k_norm_g[i],
            hy_conv_w[i], hy_conv_b[i], hy_w1[i], hy_b1[i], hy_w2[i], hy_b2[i], hy_sin_freq[i], hy_w3[i],
            hy_bias[i], w_att_out[i], w_hy_out[i], w_o[i], norm_mlp_g[i], w_up[i], w_down[i],
            norm_ple_g[i], w_ple_gate[i], w_ple_proj[i])
    return (y_prompt, y_sample)
```

```python
import functools
import math

import jax
import jax.numpy as jnp
from jax import lax
from jax.experimental import pallas as pl
from jax.experimental.pallas import tpu as pltpu

F32 = jnp.float32
BF16 = jnp.bfloat16
HIGHEST = lax.Precision.HIGHEST

EPS = 1e-6
HEAD_DIM = 128
ROPE_DIM = HEAD_DIM // 4
ROPE_THETA = 500000.0
DILATED_PATTERNS = ((128, 1), (512, 4), (2048, 16))
ATT_RADIUS = 64
ATT_QBLK = 128
ATT_KWIN = ATT_QBLK + 2 * ATT_RADIUS
ATT_GROUP = 8
HY_ORDER = 2
HY_EMB_DIM = 33
HY_BANDS = (HY_EMB_DIM - 1) // 2
HY_FAST_DECAY = 0.3
HY_SLOW_DECAY = 1.5
HY_DECAY_TARGET = 1e-2
HY_MIN_DECAY = math.log(HY_DECAY_TARGET) / HY_SLOW_DECAY
HY_MAX_DECAY = math.log(HY_DECAY_TARGET) / HY_FAST_DECAY
N_BRANCH = 2
LANES = 128
MASK_NEG = -1e30

VMEM_CAP_BYTES = 60 * 1024 * 1024


def _cparams(sem, est_bytes):
    limit = int(min(VMEM_CAP_BYTES, max(32 * 1024 * 1024, est_bytes)))
    return pltpu.CompilerParams(dimension_semantics=sem, vmem_limit_bytes=limit)


def _nbytes(shape, dtype):
    return math.prod(shape) * jnp.dtype(dtype).itemsize


def _two_group_specs(block, na, col=lambda *j: 0):
    return (pl.BlockSpec(block, lambda i, *j: (jnp.minimum(i, na - 1), jnp.where(i < na, col(*j), 0))),
            pl.BlockSpec(block, lambda i, *j: (jnp.maximum(i - na, 0), jnp.where(i >= na, col(*j), 0))))


def _pick_group(na, xa_ref, xb_ref, body):
    i = pl.program_id(0)
    pl.when(i < na)(lambda: body(xa_ref))
    pl.when(i >= na)(lambda: body(xb_ref))


def _rmsnorm_kernel(xa_ref, xb_ref, g_ref, o_ref, *, na):
    def body(x_ref):
        x = x_ref[...]
        ms = jnp.mean(x * x, axis=-1, keepdims=True)
        o_ref[...] = (x * lax.rsqrt(ms + EPS) * g_ref[...]).astype(o_ref.dtype)

    _pick_group(na, xa_ref, xb_ref, body)


def _rmsnorm_cast(xa, xb, g, tm=512):
    d = xa.shape[1]
    assert xa.shape[0] % tm == 0 and xb.shape[0] % tm == 0
    na, t = xa.shape[0] // tm, xa.shape[0] + xb.shape[0]
    return pl.pallas_call(
        functools.partial(_rmsnorm_kernel, na=na),
        grid=(t // tm,),
        in_specs=[*_two_group_specs((tm, d), na), pl.BlockSpec((1, d), lambda i: (0, 0))],
        out_specs=pl.BlockSpec((tm, d), lambda i: (i, 0)),
        out_shape=jax.ShapeDtypeStruct((t, d), BF16),
        compiler_params=_cparams(("parallel",), 6 * _nbytes((tm, d), F32)),
        name="rmsnorm_cast",
    )(xa, xb, g.reshape(1, d))


INPROJ_ROW_CHUNK = 256


def _row_chunks(h_ref, w_ref):
    for r0 in range(0, h_ref.shape[0], INPROJ_ROW_CHUNK):
        rows = slice(r0, r0 + INPROJ_ROW_CHUNK)
        yield rows, jnp.dot(h_ref[rows, :], w_ref[...], preferred_element_type=F32)


def _inproj_qk_kernel(h_ref, w_ref, g_ref, c_ref, s1_ref, s2_ref, o_ref, *, scale):
    gs = g_ref[...] * scale
    pw = 2 * HEAD_DIM
    same_head = (lax.broadcasted_iota(jnp.int32, (pw, pw), 0) // HEAD_DIM
                 == lax.broadcasted_iota(jnp.int32, (pw, pw), 1) // HEAD_DIM)
    ones_bd = jnp.where(same_head, 1.0, 0.0).astype(BF16)
    for rows, y in _row_chunks(h_ref, w_ref):
        yy = (y * y).astype(BF16)
        c, s1, s2 = c_ref[rows, :], s1_ref[rows, :], s2_ref[rows, :]
        for hh in range(y.shape[1] // HEAD_DIM):
            sl = slice(hh * HEAD_DIM, (hh + 1) * HEAD_DIM)
            if hh % 2 == 0:
                ss_pair = jnp.dot(yy[:, hh * HEAD_DIM:hh * HEAD_DIM + pw], ones_bd, preferred_element_type=F32)
            ms = ss_pair[:, (hh % 2) * HEAD_DIM:(hh % 2 + 1) * HEAD_DIM] * (1.0 / HEAD_DIM)
            t = y[:, sl] * lax.rsqrt(ms + EPS) * gs
            out = (t * c + pltpu.roll(t, HEAD_DIM - ROPE_DIM // 2, 1) * s1 + pltpu.roll(t, ROPE_DIM // 2, 1) * s2)
            o_ref[hh, rows, :] = out.astype(o_ref.dtype)


def _inproj_heads_kernel(h_ref, w_ref, o_ref):
    for rows, y in _row_chunks(h_ref, w_ref):
        for hh in range(y.shape[1] // HEAD_DIM):
            o_ref[hh, rows, :] = y[:, hh * HEAD_DIM:(hh + 1) * HEAD_DIM].astype(o_ref.dtype)


CONV_PAD_ROWS = 16


def _inproj_conv_kernel(h_ref, w_ref, cw_ref, cb_ref, o_ref, scr):
    s, tn = h_ref.shape[0], w_ref.shape[1]
    q = s // 4
    part = INPROJ_ROW_CHUNK // 4
    pad = CONV_PAD_ROWS
    zeros = jnp.zeros((pad, LANES), F32)
    for c in range(tn // LANES):
        scr[c, 0:pad, :] = zeros
        scr[c, pad + s:2 * pad + s, :] = zeros

    def finish(r0):
        for c in range(tn // LANES):
            cols = slice(c * LANES, (c + 1) * LANES)
            w0, w1, w2, cb = cw_ref[0:1, cols], cw_ref[1:2, cols], cw_ref[2:3, cols], cb_ref[:, cols]
            y = [scr[c, pl.ds(pad + r0 + off, part, stride=4), :] for off in range(-1, 5)]
            for cls in range(4):
                rows = slice(cls * q + r0 // 4, cls * q + r0 // 4 + part)
                o_ref[rows, cols] = (y[cls] * w0 + y[cls + 1] * w1 + y[cls + 2] * w2 + cb).astype(o_ref.dtype)

    for rows, y in _row_chunks(h_ref, w_ref):
        for c in range(tn // LANES):
            scr[c, pad + rows.start:pad + rows.stop, :] = y[:, c * LANES:(c + 1) * LANES]
        if rows.start > 0:
            finish(rows.start - INPROJ_ROW_CHUNK)
    finish(s - INPROJ_ROW_CHUNK)


def _inproj_gate_kernel(h_ref, w_ref, gb_ref, o_ref):
    for rows, y in _row_chunks(h_ref, w_ref):
        o_ref[rows, :] = jax.nn.sigmoid(y + gb_ref[...]).astype(o_ref.dtype)


def _inproj(h3, w, col0, width, body, extras, extra_specs, tn, scratch=(), head_major=False):
    b, s, d = h3.shape
    tn = min(tn, width)
    nj = width // tn
    jb0 = col0 // tn
    est = (2 * _nbytes((s, d), BF16) + 2 * _nbytes((d, tn), BF16) + 2 * _nbytes((s, tn), BF16)
           + 4 * _nbytes((s, tn), F32) + (8 << 20))
    return pl.pallas_call(
        body,
        grid=(b, nj),
        in_specs=[pl.BlockSpec((None, s, d), lambda i, j: (i, 0, 0)),
                  pl.BlockSpec((d, tn), lambda i, j: (0, jb0 + j))] + extra_specs,
        out_specs=(pl.BlockSpec((None, tn // HEAD_DIM, s, HEAD_DIM), lambda i, j: (i, j, 0, 0)) if head_major
                   else pl.BlockSpec((None, s, tn), lambda i, j: (i, 0, j))),
        out_shape=jax.ShapeDtypeStruct((b, width // HEAD_DIM, s, HEAD_DIM) if head_major else (b, s, width), BF16),
        scratch_shapes=list(scratch),
        compiler_params=_cparams(("parallel", "arbitrary"), est),
        name="inproj_" + getattr(body, "func", body).__name__,
    )(h3, w, *extras)


def _rope_tables(s):
    half = ROPE_DIM // 2
    inv_freq = ROPE_THETA ** (-jnp.arange(half, dtype=F32) / half)
    ang = jnp.arange(s, dtype=F32)[:, None] * inv_freq[None, :]
    cos, sin = jnp.cos(ang), jnp.sin(ang)
    c = jnp.concatenate([cos, cos, jnp.ones((s, HEAD_DIM - ROPE_DIM), F32)], axis=1)
    s1 = jnp.concatenate([-sin, jnp.zeros((s, HEAD_DIM - half), F32)], axis=1)
    s2 = jnp.concatenate([jnp.zeros((s, half), F32), sin, jnp.zeros((s, HEAD_DIM - ROPE_DIM), F32)], axis=1)
    return c, s1, s2


def _band_bias(off, kw):
    col = lax.broadcasted_iota(jnp.int32, (ATT_QBLK, kw), 1)
    row = lax.broadcasted_iota(jnp.int32, (ATT_QBLK, kw), 0)
    return jnp.where(jnp.abs(col - row + off) <= ATT_RADIUS, 0.0, MASK_NEG).astype(F32)


def _band_blocks(seg_len):
    kw = min(ATT_KWIN, seg_len)
    for bi in range(seg_len // ATT_QBLK):
        q0 = bi * ATT_QBLK
        yield q0, min(max(q0 - ATT_RADIUS, 0), seg_len - kw), kw


def _band_attend(blocks):
    scores = [lax.dot_general(qb, kb, (((1,), (1,)), ((), ())), preferred_element_type=F32) + bias
              for qb, kb, _, bias in blocks]
    maxes = [jnp.max(s, axis=-1, keepdims=True) for s in scores]
    probs = [jnp.exp2(s - m).astype(BF16) for s, m in zip(scores, maxes)]
    outs = [jnp.dot(p, blk[2], preferred_element_type=F32) for p, blk in zip(probs, blocks)]
    return [(oa[:, :HEAD_DIM], oa[:, HEAD_DIM:], jnp.broadcast_to(m, (ATT_QBLK, HEAD_DIM)))
            for oa, m in zip(outs, maxes)]


def _softmax_merge(a, b):
    (o1, l1, m1), (o2, l2, m2) = a, b
    m = jnp.maximum(m1, m2)
    a1 = jnp.exp2(m1 - m)
    a2 = jnp.exp2(m2 - m)
    return a1 * o1 + a2 * o2, a1 * l1 + a2 * l2, m


def _attn_kernel(q_ref, k_ref, v_ref, o_ref,
                 qf0, kf0, vf0, qf1, kf1, vf1, qb1, kb1, vb1, qb2, kb2, vb2,
                 ao, al, am, bo, bl, bm):
    s = q_ref.shape[0]
    c = HEAD_DIM
    n1 = s // 4
    n2 = s // 16
    biases = {}

    def bias(off, kw):
        if (off, kw) not in biases:
            biases[(off, kw)] = _band_bias(off, kw)
        return biases[(off, kw)]

    def grouped(items, compute, store, group=ATT_GROUP):
        for g0 in range(0, len(items), group):
            chunk = items[g0:g0 + group]
            for it, val in zip(chunk, compute(chunk)):
                store(*it, val)

    def each(fn):
        return lambda chunk: [fn(*it) for it in chunk]

    qf0[...] = q_ref[...].astype(F32)
    kf0[...] = k_ref[...].astype(F32)
    vf0[...] = v_ref[...].astype(F32)

    def split4_store(r, src, dstf, dstb, x):
        rows = slice(r * n1, (r + 1) * n1)
        dstf[rows, :] = x
        dstb[rows, :] = x.astype(BF16)

    grouped([(r, src, dstf, dstb) for r in range(4)
             for src, dstf, dstb in ((qf0, qf1, qb1), (kf0, kf1, kb1), (vf0, vf1, vb1))],
            each(lambda r, src, dstf, dstb: src[pl.ds(r, n1, stride=4), :]), split4_store)

    def split16_store(seg, srcf, dstb, x):
        dstb[seg * n2:(seg + 1) * n2, :] = x

    grouped([(seg, srcf, dstb) for seg in range(16) for srcf, dstb in ((qf1, qb2), (kf1, kb2), (vf1, vb2))],
            each(lambda seg, srcf, dstb: srcf[pl.ds((seg // 4) * n1 + seg % 4, n2, stride=4), :].astype(BF16)),
            split16_store)

    def attend(qb, kb, vb, chunk):
        return _band_attend([(qb[base + q0:base + q0 + ATT_QBLK, :], kb[base + w0:base + w0 + kw, :],
                              jnp.concatenate([vb[base + w0:base + w0 + kw, :], jnp.ones((kw, c), BF16)], axis=1),
                              bias(w0 - q0, kw)) for base, q0, w0, kw in chunk])

    def merged(cur, prev_refs, chunk):
        prev = [tuple(ref[base + q0:base + q0 + ATT_QBLK, :] for ref in prev_refs) for base, q0, _, _ in chunk]
        return [_softmax_merge(a, b) for a, b in zip(cur, prev)]

    def store16(base, q0, w0, kw, olm):
        seg = base // n2
        dst = pl.ds((seg // 4) * n1 + 4 * q0 + seg % 4, ATT_QBLK, stride=4)
        for ref, val in zip((ao, al, am), olm):
            ref[dst, :] = val

    grouped([(seg * n2, *blk) for seg in range(16) for blk in _band_blocks(n2)],
            lambda chunk: attend(qb2, kb2, vb2, chunk), store16)

    def store4(base, q0, w0, kw, olm):
        dst = pl.ds(4 * q0 + base // n1, ATT_QBLK, stride=4)
        for ref, val in zip((bo, bl, bm), olm):
            ref[dst, :] = val

    grouped([(r4 * n1, *blk) for r4 in range(4) for blk in _band_blocks(n1)],
            lambda chunk: merged(attend(qb1, kb1, vb1, chunk), (ao, al, am), chunk), store4)

    def store1(base, q0, w0, kw, olm):
        o, l, _ = olm
        o_ref[q0:q0 + ATT_QBLK, :] = (o / l).astype(o_ref.dtype)

    grouped([(0, *blk) for blk in _band_blocks(s)],
            lambda chunk: merged(attend(q_ref, k_ref, v_ref, chunk), (bo, bl, bm), chunk), store1)


def _dilated_attention(q, k, v):
    b, nh, s, _ = q.shape
    aw = nh * HEAD_DIM
    assert tuple(w // (2 * d) for w, d in DILATED_PATTERNS) == (ATT_RADIUS,) * 3
    assert tuple(d for _, d in DILATED_PATTERNS) == (1, 4, 16)
    assert s % (16 * ATT_QBLK) == 0
    c = HEAD_DIM
    spec = pl.BlockSpec((None, None, s, c), lambda i, h: (i, h, 0, 0))
    out_spec = pl.BlockSpec((None, s, c), lambda i, h: (i, 0, h))
    f32buf = pltpu.VMEM((s, c), F32)
    bf16buf = pltpu.VMEM((s, c), BF16)
    scratch = [f32buf] * 6 + [bf16buf] * 6 + [f32buf] * 6
    est = 12 * _nbytes((s, c), F32) + 10 * _nbytes((s, c), BF16) + 8 * _nbytes((s, c), BF16) + (16 << 20)
    return pl.pallas_call(
        _attn_kernel,
        grid=(b, aw // c),
        in_specs=[spec, spec, spec],
        out_specs=out_spec,
        out_shape=jax.ShapeDtypeStruct((b, s, aw), BF16),
        scratch_shapes=scratch,
        compiler_params=_cparams(("parallel", "arbitrary"), est),
        name="dilated_attention",
    )(q, k, v)


def _filter_kernel(z_ref, w1_ref, b1_ref, w2_ref, b2_ref, fr_ref, w3_ref, t_ref, dl_ref,
                   filt_ref, asum_ref, hi_ref, lo_ref, *, blocks_per_dir):
    def split(a):
        hi = a.astype(BF16)
        return hi, (a - hi.astype(F32)).astype(BF16)

    @pl.when(pl.program_id(0) == 0)
    def _():
        fr = fr_ref[...]
        hdn = jnp.sin(fr * (jnp.dot(z_ref[...], w1_ref[...], precision=HIGHEST,
                                    preferred_element_type=F32) + b1_ref[...]))
        hdn = jnp.sin(fr * (jnp.dot(hdn, w2_ref[...], precision=HIGHEST,
                                    preferred_element_type=F32) + b2_ref[...]))
        hi_ref[...], lo_ref[...] = split(hdn)

    w_hi, w_lo = split(w3_ref[...])
    hf = (jnp.dot(hi_ref[...], w_hi, preferred_element_type=F32)
          + jnp.dot(hi_ref[...], w_lo, preferred_element_type=F32)
          + jnp.dot(lo_ref[...], w_hi, preferred_element_type=F32))
    hf = hf * jnp.exp(-t_ref[...] * dl_ref[...])
    is_bwd = (pl.program_id(0) // blocks_per_dir) % 2 == 1
    row = lax.broadcasted_iota(jnp.int32, hf.shape, 0)
    hf = jnp.where(jnp.logical_and(row == 0, is_bwd), 0.0, hf)
    filt_ref[...] = hf.astype(filt_ref.dtype)
    asum_ref[...] = jnp.sum(jnp.abs(hf), axis=0, keepdims=True)


def _mod4_rows(a):
    return jnp.concatenate([a[c::4] for c in range(4)], axis=0)


def _hyena_filter_taps(seq_len, w1, b1, w2, b2, sin_freq, w3, hy_w, tn=512):
    ffn = w1.shape[1]
    pad = LANES
    t = jnp.linspace(0.0, 1.0, seq_len, dtype=F32)[:, None]
    wpos = 2.0 * math.pi * jnp.arange(seq_len, dtype=F32) / seq_len
    bands = jnp.linspace(1e-4, HY_BANDS - 1, HY_BANDS, dtype=F32)
    ang = wpos[:, None] * bands[None, :]
    z = jnp.concatenate([t, jnp.cos(ang), -jnp.sin(ang)], axis=-1)
    z = _mod4_rows(jnp.pad(z, ((0, 0), (0, pad - z.shape[1]))))
    t = _mod4_rows(t)
    w1p = jnp.pad(w1.astype(F32), ((0, pad - w1.shape[0]), (0, pad - ffn)))
    w2p = jnp.pad(w2.astype(F32), ((0, pad - ffn), (0, pad - ffn)))
    w3p = jnp.pad(w3.astype(F32), ((0, pad - ffn), (0, 0)))
    padv = lambda a: jnp.pad(a.astype(F32), (0, pad - ffn)).reshape(1, pad)
    deltas = jnp.abs(jnp.linspace(HY_MIN_DECAY, HY_MAX_DECAY, hy_w, dtype=F32)).reshape(1, hy_w)
    ncol = w3.shape[1]
    tn = min(tn, hy_w)
    bpd = hy_w // tn
    full = lambda shape: pl.BlockSpec(shape, lambda j: (0, 0))
    return pl.pallas_call(
        functools.partial(_filter_kernel, blocks_per_dir=bpd),
        grid=(ncol // tn,),
        in_specs=[full((seq_len, pad)), full((pad, pad)), full((1, pad)), full((pad, pad)), full((1, pad)),
                  full((1, pad)), pl.BlockSpec((pad, tn), lambda j: (0, j)), full((seq_len, 1)),
                  pl.BlockSpec((1, tn), lambda j: (0, j % bpd))],
        out_specs=[pl.BlockSpec((seq_len, tn), lambda j: (0, j)), pl.BlockSpec((1, tn), lambda j: (0, j))],
        out_shape=[jax.ShapeDtypeStruct((seq_len, ncol), BF16), jax.ShapeDtypeStruct((1, ncol), F32)],
        scratch_shapes=[pltpu.VMEM((seq_len, pad), BF16)] * 2,
        compiler_params=_cparams(("arbitrary",), 32 << 20),
        name="hyena_filter_taps",
    )(z, w1p, padv(b1), w2p, padv(b2), padv(sin_freq), w3p, t, deltas)


def _dft_matrices(s, n_total):
    n = 2 * s
    k = jnp.arange(s, dtype=jnp.int32)

    def tables(cols):
        ang = ((k[:, None] * cols[None, :]) % n).astype(F32) * (2.0 * math.pi / n)
        return jnp.cos(ang), jnp.sin(ang)

    cl, sl = tables(jnp.arange(LANES, dtype=jnp.int32))
    ch, sh = tables(jnp.arange(s // LANES, dtype=jnp.int32) * LANES)
    cosm = (ch[:, :, None] * cl[:, None, :] - sh[:, :, None] * sl[:, None, :]).reshape(s, s)
    sinm = (sh[:, :, None] * cl[:, None, :] + ch[:, :, None] * sl[:, None, :]).reshape(s, s)
    alt = jnp.where(k % 2 == 0, 1.0, -1.0).astype(F32)
    im = (-sinm).at[0].set(alt)
    scale_re = jnp.full((s,), 2.0 / n_total, F32).at[0].set(1.0 / n_total)[:, None]
    r = min(HY_FREQ_CHUNK, s)
    chunked = lambda a: a.reshape(2, s // r, r, -1).transpose(1, 0, 2, 3).reshape(n, -1)
    fwds, invs = [], []
    for c in range(4):
        theta = ((c * k) % n_total).astype(F32) * (2.0 * math.pi / n_total)
        cs, sn = jnp.cos(theta)[:, None], jnp.sin(theta)[:, None]
        fwds.append(chunked(jnp.concatenate([cs * cosm + sn * im, cs * im - sn * cosm], axis=0)))
        invs.append(chunked(jnp.concatenate([(cs * cosm + sn * im) * scale_re, (cs * im - sn * cosm) * scale_re],
                                            axis=0)).T)
    return jnp.stack(fwds).astype(BF16), jnp.stack(invs).astype(BF16)


HY_FREQ_CHUNK = 256


def _freq_chunks(m):
    r = min(HY_FREQ_CHUNK, m)
    return [slice(r0, r0 + r) for r0 in range(0, m, r)]


def _first_row(shape):
    return lax.broadcasted_iota(jnp.int32, shape, 0) == 0


COS_PI_4 = math.sqrt(0.5)


def _quarter_spectrum(z_ref, f_ref, rows):
    q = z_ref.shape[0] // 4
    r = rows.stop - rows.start
    frows = slice(2 * rows.start, 2 * rows.stop)
    sub = []
    for c in range(4):
        f = jnp.dot(f_ref[c, frows, :], z_ref[c * q:(c + 1) * q, :], preferred_element_type=F32)
        sub.append((f[:r], f[r:]))
    (t0r, t0i), (t1r, t1i), (t2r, t2i), (t3r, t3i) = sub
    ur, ui, vr, vi = t0r + t2r, t0i + t2i, t0r - t2r, t0i - t2i
    wr, wi, xr, xi = t1r + t3r, t1i + t3i, t1r - t3r, t1i - t3i
    slots = [(ur + wr, ui + wi), (vr - xi, vi + xr), (vr + xi, vi - xr), (ur - wr, ui - wi)]
    if rows.start == 0:
        s0, s1, s2, s3 = (sub[c][0][0:1, :] for c in range(4))
        n0, n1, n2, n3 = (sub[c][1][0:1, :] for c in range(4))
        k4 = COS_PI_4
        row0 = [(s0 + s1 + s2 + s3, s0 - s1 + s2 - s3),
                (s0 - s2, s3 - s1),
                (n0 + k4 * (n1 - n3), -n2 - k4 * (n1 + n3)),
                (n0 - k4 * (n1 - n3), n2 - k4 * (n1 + n3))]
        first = _first_row(ur.shape)
        slots = [(jnp.where(first, p0[0], p[0]), jnp.where(first, p0[1], p[1])) for p, p0 in zip(slots, row0)]
    return slots


def _spectrum_product(h, z, with_first):
    out = []
    for i, ((hr, hi), (zr, zi)) in enumerate(zip(h, z)):
        rr, ii = hr * zr, hi * zi
        yr, yi = rr - ii, hr * zi + hi * zr
        if i == 0 and with_first:
            first = _first_row(rr.shape)
            yr, yi = jnp.where(first, rr, yr), jnp.where(first, ii, yi)
        out.append((yr, yi))
    return out


def _inverse_butterfly(y, rows, dtype):
    (ar, ai), (br, bi), (cr, ci), (dr, di) = y
    g = [(ar + br + cr + dr, ai + bi + ci + di),
         (ar + bi - ci - dr, ai - br + cr - di),
         (ar - br - cr + dr, ai - bi - ci + di),
         (ar - bi + ci - dr, ai + br - cr - di)]
    if rows.start == 0:
        y0, yn, qr, qi = ar[0:1, :], ai[0:1, :], br[0:1, :], bi[0:1, :]
        a, b, a2, b2 = cr[0:1, :], ci[0:1, :], dr[0:1, :], di[0:1, :]
        k4 = 2.0 * COS_PI_4
        dc = [y0 + yn + 2.0 * qr, y0 - yn - 2.0 * qi, y0 + yn - 2.0 * qr, y0 - yn + 2.0 * qi]
        half = [2.0 * (a + a2), k4 * (a - b) - k4 * (a2 + b2), 2.0 * (b2 - b), k4 * (a2 - b2) - k4 * (a + b)]
        first = _first_row(ar.shape)
        g = [(jnp.where(first, d0, gr), jnp.where(first, h0, gi)) for (gr, gi), d0, h0 in zip(g, dc, half)]
    return [jnp.concatenate([gr.astype(dtype), gi.astype(dtype)], axis=0) for gr, gi in g]


def _filter_spectrum_kernel(tf_ref, tb_ref, nf_ref, nb_ref, f_ref, o_ref):
    inv = 1.0 / (nf_ref[...] + nb_ref[...])
    for rows in _freq_chunks(o_ref.shape[1]):
        fwd = _quarter_spectrum(tf_ref, f_ref, rows)
        bwd = _quarter_spectrum(tb_ref, f_ref, rows)
        for i, ((fr, fi), (gr, gi)) in enumerate(zip(fwd, bwd)):
            im = fi - gi
            if i == 0 and rows.start == 0:
                im = jnp.where(_first_row(im.shape), fi + gi, im)
            o_ref[2 * i, rows, :] = (fr + gr) * inv
            o_ref[2 * i + 1, rows, :] = im * inv


def _filter_spectrum(taps, asum, fwd, hy_w, tc):
    l = taps.shape[0]
    q = fwd.shape[2]
    nb = hy_w // tc
    col = lambda d: (lambda o, c: (0, (2 * o + d) * nb + c))
    const = lambda shape: pl.BlockSpec(shape, lambda o, c: (0,) * len(shape), pipeline_mode=pl.Buffered(1))
    est = (_nbytes((4, 2 * q, q), BF16) + 4 * _nbytes((l, tc), BF16)
           + 2 * _nbytes((8, q, tc), F32) + 48 * _nbytes((HY_FREQ_CHUNK, tc), F32) + (8 << 20))
    return pl.pallas_call(
        _filter_spectrum_kernel,
        grid=(HY_ORDER, nb),
        in_specs=[pl.BlockSpec((l, tc), col(0)), pl.BlockSpec((l, tc), col(1)),
                  pl.BlockSpec((1, tc), col(0)), pl.BlockSpec((1, tc), col(1)),
                  const((4, 2 * q, q))],
        out_specs=pl.BlockSpec((None, 8, q, tc), lambda o, c: (o, 0, 0, c)),
        out_shape=jax.ShapeDtypeStruct((HY_ORDER, 8, q, hy_w), F32),
        compiler_params=_cparams(("parallel", "arbitrary"), est),
        name="hyena_filter_spectrum",
    )(taps, taps, asum, asum, fwd)


def _hyena_conv_kernel(z_ref, g_ref, h_ref, b_ref, f_ref, inv_ref, o_ref, g0_ref, g1_ref, g2_ref, g3_ref,
                       *scratch, natural_out):
    s, tc = z_ref.shape
    q = s // 4
    spectra = (g0_ref, g1_ref, g2_ref, g3_ref)
    for rows in _freq_chunks(q):
        spec = _quarter_spectrum(z_ref, f_ref, rows)
        filt = [(h_ref[2 * i, rows, :], h_ref[2 * i + 1, rows, :]) for i in range(4)]
        prod = _spectrum_product(filt, spec, rows.start == 0)
        for ref, packed in zip(spectra, _inverse_butterfly(prod, rows, g0_ref.dtype)):
            ref[2 * rows.start:2 * rows.stop, :] = packed
    bias = b_ref[...]
    for rows in _freq_chunks(q):
        r = rows.stop - rows.start
        outs = []
        for c in range(4):
            cls = slice(c * q + rows.start, c * q + rows.stop)
            conv = jnp.dot(inv_ref[c, rows, :], spectra[c][...], preferred_element_type=F32)
            outs.append((cls, g_ref[cls, :].astype(F32) * (conv + bias * z_ref[cls, :].astype(F32))))
        if natural_out:
            (nat,) = scratch
            span = slice(4 * rows.start, 4 * rows.stop)
            for lane in range(tc // LANES):
                cols = slice(lane * LANES, (lane + 1) * LANES)
                for c, (_, val) in enumerate(outs):
                    nat[lane, pl.ds(4 * rows.start + c, r, stride=4), :] = val[:, cols]
                o_ref[span, cols] = nat[lane, span, :].astype(o_ref.dtype)
        else:
            for cls, val in outs:
                o_ref[cls, :] = val.astype(o_ref.dtype)


def _hyena_conv(z, z_col0, gate, gate_col0, filt, order, bias_row, mats, width, natural_out):
    fwd, inv, tc = mats
    b, s, _ = z.shape
    q = s // 4
    nc = width // tc
    zb0, gb0 = z_col0 // tc, gate_col0 // tc
    const = lambda shape: pl.BlockSpec(shape, lambda c, i: (0,) * len(shape), pipeline_mode=pl.Buffered(1))
    scratch = [pltpu.VMEM((2 * q, tc), BF16)] * 4
    if natural_out:
        scratch.append(pltpu.VMEM((tc // LANES, s, LANES), F32))
    est = (2 * _nbytes((4, 2 * q, q), BF16) + 2 * _nbytes((8, q, tc), F32)
           + 6 * _nbytes((s, tc), BF16) + 4 * _nbytes((2 * q, tc), BF16) + _nbytes((s, tc), F32)
           + 64 * _nbytes((HY_FREQ_CHUNK, tc), F32) + (8 << 20))
    return pl.pallas_call(
        functools.partial(_hyena_conv_kernel, natural_out=natural_out),
        grid=(nc, b),
        in_specs=[pl.BlockSpec((None, s, tc), lambda c, i: (i, 0, zb0 + c)),
                  pl.BlockSpec((None, s, tc), lambda c, i: (i, 0, gb0 + c)),
                  pl.BlockSpec((None, 8, q, tc), lambda c, i: (order, 0, 0, c)),
                  pl.BlockSpec((1, tc), lambda c, i: (0, c)),
                  const((4, 2 * q, q)), const((4, q, 2 * q))],
        out_specs=pl.BlockSpec((None, s, tc), lambda c, i: (i, 0, c)),
        out_shape=jax.ShapeDtypeStruct((b, s, width), BF16),
        scratch_shapes=scratch,
        compiler_params=_cparams(("parallel", "arbitrary"), est),
        name="hyena_conv",
    )(z, gate, filt, bias_row, fwd, inv)


def _chunk_rows(n_rows):
    return [slice(r0, r0 + INPROJ_ROW_CHUNK) for r0 in range(0, n_rows, INPROJ_ROW_CHUNK)]


def _merge_kernel(a_ref, hy_ref, wa_ref, wh_ref, g0_ref, g1_ref, o_ref):
    for rows in _chunk_rows(a_ref.shape[0]):
        ya = jnp.dot(a_ref[rows, :], wa_ref[...], preferred_element_type=F32)
        yh = jnp.dot(hy_ref[rows, :], wh_ref[...], preferred_element_type=F32)
        o_ref[rows, :] = (g0_ref[rows, :].astype(F32) * ya + g1_ref[rows, :].astype(F32) * yh).astype(o_ref.dtype)


def _weight_spec(k, tn, n):
    if tn == n:
        return pl.BlockSpec((k, n), lambda i, j: (0, 0), pipeline_mode=pl.Buffered(1))
    return pl.BlockSpec((k, tn), lambda i, j: (0, j))


def _gated_merge(attn, hy, w_att, w_hy, gates, tm=512, tn=2048):
    t, k = attn.shape
    n = w_att.shape[1]
    tm, tn = min(tm, t), min(tn, n)
    g1b0 = n // tn
    est = 4 * _nbytes((tm, k), BF16) + 4 * _nbytes((k, tn), BF16) + 6 * _nbytes((tm, tn), F32) + (8 << 20)
    return pl.pallas_call(
        _merge_kernel,
        grid=(t // tm, n // tn),
        in_specs=[pl.BlockSpec((tm, k), lambda i, j: (i, 0)),
                  pl.BlockSpec((tm, k), lambda i, j: (i, 0)),
                  _weight_spec(k, tn, n),
                  _weight_spec(k, tn, n),
                  pl.BlockSpec((tm, tn), lambda i, j: (i, j)),
                  pl.BlockSpec((tm, tn), lambda i, j: (i, g1b0 + j))],
        out_specs=pl.BlockSpec((tm, tn), lambda i, j: (i, j)),
        out_shape=jax.ShapeDtypeStruct((t, n), BF16),
        compiler_params=_cparams(("parallel", "arbitrary"), est),
        name="gated_merge",
    )(attn, hy, w_att, w_hy, gates, gates)


def _mm_residual_kernel(a_ref, w_ref, ra_ref, rb_ref, o_ref, *, na):
    def body(r_ref):
        for rows in _chunk_rows(a_ref.shape[0]):
            o_ref[rows, :] = r_ref[rows, :] + jnp.dot(a_ref[rows, :], w_ref[...], preferred_element_type=F32)

    _pick_group(na, ra_ref, rb_ref, body)


def _matmul_residual(a, w, res_a, res_b, tm=512, tn=2048):
    t, k = a.shape
    n = w.shape[1]
    tm, tn = min(tm, res_a.shape[0]), min(tn, n)
    assert res_a.shape[0] % tm == 0 and res_b.shape[0] % tm == 0
    na = res_a.shape[0] // tm
    est = 2 * _nbytes((tm, k), BF16) + 2 * _nbytes((k, tn), BF16) + 8 * _nbytes((tm, tn), F32) + (8 << 20)
    return pl.pallas_call(
        functools.partial(_mm_residual_kernel, na=na),
        grid=(t // tm, n // tn),
        in_specs=[pl.BlockSpec((tm, k), lambda i, j: (i, 0)),
                  _weight_spec(k, tn, n),
                  *_two_group_specs((tm, tn), na, col=lambda j: j)],
        out_specs=pl.BlockSpec((tm, tn), lambda i, j: (i, j)),
        out_shape=jax.ShapeDtypeStruct((t, n), F32),
        compiler_params=_cparams(("parallel", "arbitrary"), est),
        name="matmul_residual",
    )(a, w, res_a, res_b)


def _rmsnorm_rows(x, g):
    ms = jnp.mean(x * x, axis=-1, keepdims=True)
    return x * lax.rsqrt(ms + EPS) * g


def _mlp_kernel(x_ref, g_ref, wu_ref, wd_ref, o_ref, hm_ref):
    def body(first):
        for rows in _chunk_rows(hm_ref.shape[0]):
            if first:
                x = x_ref[rows, :]
                hm = _rmsnorm_rows(x, g_ref[...]).astype(hm_ref.dtype)
                hm_ref[rows, :] = hm
            else:
                hm = hm_ref[rows, :]
            a = jnp.dot(hm, wu_ref[...], preferred_element_type=F32)
            a = jnp.square(jnp.maximum(a, 0.0)).astype(BF16)
            y = jnp.dot(a, wd_ref[...], preferred_element_type=F32)
            o_ref[rows, :] = (x if first else o_ref[rows, :]) + y

    j = pl.program_id(1)
    pl.when(j == 0)(lambda: body(True))
    pl.when(j > 0)(lambda: body(False))


def _relu2_mlp(x, g, w_up, w_down, tm=1024, tf=1024):
    t, d = x.shape
    ff = w_up.shape[1]
    tm, tf = min(tm, t), min(tf, ff)
    est = (4 * _nbytes((tm, d), F32) + _nbytes((tm, d), BF16) + 4 * _nbytes((d, tf), BF16)
           + 3 * _nbytes((tm, tf), F32) + (8 << 20))
    return pl.pallas_call(
        _mlp_kernel,
        grid=(t // tm, ff // tf),
        in_specs=[pl.BlockSpec((tm, d), lambda i, j: (i, 0)),
                  pl.BlockSpec((1, d), lambda i, j: (0, 0)),
                  pl.BlockSpec((d, tf), lambda i, j: (0, j)),
                  pl.BlockSpec((tf, d), lambda i, j: (j, 0))],
        out_specs=pl.BlockSpec((tm, d), lambda i, j: (i, 0)),
        out_shape=jax.ShapeDtypeStruct((t, d), F32),
        scratch_shapes=[pltpu.VMEM((tm, d), BF16)],
        compiler_params=_cparams(("parallel", "arbitrary"), est),
        name="relu2_mlp",
    )(x, g.reshape(1, d), w_up, w_down)


def _ple_kernel(x_ref, g_ref, wg_ref, p_ref, wp_ref, o_ref):
    for rows in _chunk_rows(x_ref.shape[0]):
        x = x_ref[rows, :]
        hn = _rmsnorm_rows(x, g_ref[...]).astype(BF16)
        gate = jax.nn.sigmoid(jnp.dot(hn, wg_ref[...], preferred_element_type=F32))
        proj = jnp.dot(p_ref[rows, :].astype(BF16), wp_ref[...], preferred_element_type=F32)
        o_ref[rows, :] = x + gate * proj


def _gated_ple(x, row0, p, g, w_gate, w_proj, tm=512):
    t, pd = p.shape
    d = x.shape[1]
    tm = min(tm, t)
    rb0 = row0 // tm
    est = (4 * _nbytes((tm, d), F32) + _nbytes((d, d), BF16) + _nbytes((pd, d), BF16)
           + 2 * _nbytes((tm, pd), F32) + 8 * _nbytes((INPROJ_ROW_CHUNK, d), F32) + (8 << 20))
    return pl.pallas_call(
        _ple_kernel,
        grid=(t // tm, 1),
        in_specs=[pl.BlockSpec((tm, d), lambda i, j: (rb0 + i, 0)),
                  pl.BlockSpec((1, d), lambda i, j: (0, 0)),
                  _weight_spec(d, d, d),
                  pl.BlockSpec((tm, pd), lambda i, j: (i, 0)),
                  _weight_spec(pd, d, d)],
        out_specs=pl.BlockSpec((tm, d), lambda i, j: (i, 0)),
        out_shape=jax.ShapeDtypeStruct((t, d), F32),
        compiler_params=_cparams(("parallel", "arbitrary"), est),
        name="gated_ple",
    )(x, g.reshape(1, d), w_gate, p, w_proj)


def _encoder_layer(x_list, p_list, norm_mix_g, w_in, gate_b, q_norm_g, k_norm_g, hy_conv_w, hy_conv_b,
                   hy_w1, hy_b1, hy_w2, hy_b2, hy_sin_freq, hy_w3, hy_bias,
                   w_att_out, w_hy_out, w_o, norm_mlp_g, w_up, w_down,
                   norm_ple_g, w_ple_gate, w_ple_proj):
    xa3, xb3 = x_list
    s, d = xa3.shape[1:]
    b = xa3.shape[0] + xb3.shape[0]
    att_w = w_att_out.shape[0]
    hy_w = w_hy_out.shape[0]
    xa, xb = xa3.reshape(-1, d), xb3.reshape(-1, d)
    bf = lambda a: a.astype(BF16)

    h3 = _rmsnorm_cast(xa, xb, norm_mix_g).reshape(b, s, d)
    w_in_b = bf(w_in)
    rope_c, rope_s1, rope_s2 = _rope_tables(s)
    tab_spec = pl.BlockSpec((s, HEAD_DIM), lambda i, j: (0, 0))
    vec_spec = lambda tn, blk0=0: pl.BlockSpec((1, tn), lambda i, j: (0, blk0 + j))
    head_spec = pl.BlockSpec((1, HEAD_DIM), lambda i, j: (0, 0))
    tn = 1024
    qk = lambda g, col0, scale: _inproj(
        h3, w_in_b, col0, att_w, functools.partial(_inproj_qk_kernel, scale=scale),
        (g.reshape(1, HEAD_DIM), rope_c, rope_s1, rope_s2), [head_spec, tab_spec, tab_spec, tab_spec], tn,
        head_major=True)
    q = qk(q_norm_g, 0, HEAD_DIM ** -0.5 * math.log2(math.e))
    k = qk(k_norm_g, att_w, 1.0)
    v = _inproj(h3, w_in_b, 2 * att_w, att_w, _inproj_heads_kernel, (), [], tn, head_major=True)
    tnc = min(tn, hy_w)
    u = _inproj(h3, w_in_b, 3 * att_w, 3 * hy_w, _inproj_conv_kernel,
                (hy_conv_w, hy_conv_b.reshape(1, 3 * hy_w)),
                [pl.BlockSpec((3, tnc), lambda i, j: (0, j)), vec_spec(tnc)], tnc,
                scratch=[pltpu.VMEM((tnc // LANES, s + 2 * CONV_PAD_ROWS, LANES), F32)])
    tng = min(tn, d)
    gates = _inproj(h3, w_in_b, 3 * att_w + 3 * hy_w, N_BRANCH * d, _inproj_gate_kernel,
                    (gate_b.reshape(1, N_BRANCH * d),), [vec_spec(tng)], tng)

    attn = _dilated_attention(q, k, v)

    hy_tc = min(256, hy_w)
    mats = (*_dft_matrices(s // 4, 2 * s), hy_tc)
    taps, tap_asum = _hyena_filter_taps(s, hy_w1, hy_b1, hy_w2, hy_b2, hy_sin_freq, hy_w3, hy_w)
    filt = _filter_spectrum(taps, tap_asum, mats[0], hy_w, hy_tc)
    bias = hy_bias.astype(F32)
    z = _hyena_conv(u, 0, u, hy_w, filt, 0, bias[0:1], mats, hy_w, natural_out=False)
    hy = _hyena_conv(z, 0, u, 2 * hy_w, filt, 1, bias[1:2], mats, hy_w, natural_out=True)

    merged = _gated_merge(attn.reshape(b * s, att_w), hy.reshape(b * s, hy_w),
                          bf(w_att_out), bf(w_hy_out), gates.reshape(b * s, N_BRANCH * d))
    x2 = _matmul_residual(merged, bf(w_o), xa, xb)
    x2 = _relu2_mlp(x2, norm_mlp_g, bf(w_up), bf(w_down))

    outs, row0 = [], 0
    w_pg, w_pp = bf(w_ple_gate), bf(w_ple_proj)
    for p in p_list:
        pb = p.shape[0]
        p2 = p.reshape(pb * s, p.shape[-1])
        outs.append(_gated_ple(x2, row0, p2, norm_ple_g, w_pg, w_pp).reshape(pb, s, d))
        row0 += pb * s
    return outs


def kernel(x_prompt, x_sample, p_prompt, p_sample, norm_mix_g, w_in, gate_b, q_norm_g, k_norm_g, hy_conv_w, hy_conv_b, hy_w1, hy_b1, hy_w2, hy_b2, hy_sin_freq, hy_w3, hy_bias, w_att_out, w_hy_out, w_o, norm_mlp_g, w_up, w_down, norm_ple_g, w_ple_gate, w_ple_proj):
    assert x_prompt.shape[1:] == x_sample.shape[1:]
    y_prompt, y_sample = x_prompt, x_sample
    for i in range(w_in.shape[0]):
        y_prompt, y_sample = _encoder_layer(
            (y_prompt, y_sample), (p_prompt[i], p_sample[i]), norm_mix_g[i], w_in[i], gate_b[i], q_norm_g[i], k_norm_g[i],
            hy_conv_w[i], hy_conv_b[i], hy_w1[i], hy_b1[i], hy_w2[i], hy_b2[i], hy_sin_freq[i], hy_w3[i],
            hy_bias[i], w_att_out[i], w_hy_out[i], w_o[i], norm_mlp_g[i], w_up[i], w_down[i],
            norm_ple_g[i], w_ple_gate[i], w_ple_proj[i])
    return (y_prompt, y_sample)
```

```python
import functools
import math

import jax
import jax.numpy as jnp
from jax import lax
from jax.experimental import pallas as pl
from jax.experimental.pallas import tpu as pltpu

F32 = jnp.float32
BF16 = jnp.bfloat16
HIGHEST = lax.Precision.HIGHEST

EPS = 1e-6
HEAD_DIM = 128
ROPE_DIM = HEAD_DIM // 4
ROPE_THETA = 500000.0
DILATED_PATTERNS = ((128, 1), (512, 4), (2048, 16))
ATT_RADIUS = 64
ATT_QBLK = 128
ATT_KWIN = ATT_QBLK + 2 * ATT_RADIUS
ATT_GROUP = 8
ATT_DEPTH = 3
HY_ORDER = 2
HY_EMB_DIM = 33
HY_BANDS = (HY_EMB_DIM - 1) // 2
HY_FAST_DECAY = 0.3
HY_SLOW_DECAY = 1.5
HY_DECAY_TARGET = 1e-2
HY_MIN_DECAY = math.log(HY_DECAY_TARGET) / HY_SLOW_DECAY
HY_MAX_DECAY = math.log(HY_DECAY_TARGET) / HY_FAST_DECAY
N_BRANCH = 2
LANES = 128
MASK_NEG = -1e30

VMEM_CAP_BYTES = 60 * 1024 * 1024


def _cparams(sem, est_bytes):
    limit = int(min(VMEM_CAP_BYTES, max(32 * 1024 * 1024, est_bytes)))
    return pltpu.CompilerParams(dimension_semantics=sem, vmem_limit_bytes=limit)


def _nbytes(shape, dtype):
    return math.prod(shape) * jnp.dtype(dtype).itemsize


def _two_group_specs(block, na, col=lambda *j: 0):
    return (pl.BlockSpec(block, lambda i, *j: (jnp.minimum(i, na - 1), jnp.where(i < na, col(*j), 0))),
            pl.BlockSpec(block, lambda i, *j: (jnp.maximum(i - na, 0), jnp.where(i >= na, col(*j), 0))))


def _pick_group(na, xa_ref, xb_ref, body):
    i = pl.program_id(0)
    pl.when(i < na)(lambda: body(xa_ref))
    pl.when(i >= na)(lambda: body(xb_ref))


def _rmsnorm_kernel(xa_ref, xb_ref, g_ref, o_ref, *, na):
    def body(x_ref):
        x = x_ref[...]
        ms = jnp.mean(x * x, axis=-1, keepdims=True)
        o_ref[...] = (x * lax.rsqrt(ms + EPS) * g_ref[...]).astype(o_ref.dtype)

    _pick_group(na, xa_ref, xb_ref, body)


def _rmsnorm_cast(xa, xb, g, tm=512):
    d = xa.shape[1]
    assert xa.shape[0] % tm == 0 and xb.shape[0] % tm == 0
    na, t = xa.shape[0] // tm, xa.shape[0] + xb.shape[0]
    return pl.pallas_call(
        functools.partial(_rmsnorm_kernel, na=na),
        grid=(t // tm,),
        in_specs=[*_two_group_specs((tm, d), na), pl.BlockSpec((1, d), lambda i: (0, 0))],
        out_specs=pl.BlockSpec((tm, d), lambda i: (i, 0)),
        out_shape=jax.ShapeDtypeStruct((t, d), BF16),
        compiler_params=_cparams(("parallel",), 6 * _nbytes((tm, d), F32)),
        name="rmsnorm_cast",
    )(xa, xb, g.reshape(1, d))


INPROJ_ROW_CHUNK = 256


def _row_chunks(h_ref, w_ref):
    for r0 in range(0, h_ref.shape[0], INPROJ_ROW_CHUNK):
        rows = slice(r0, r0 + INPROJ_ROW_CHUNK)
        yield rows, jnp.dot(h_ref[rows, :], w_ref[...], preferred_element_type=F32)


def _inproj_qk_kernel(h_ref, w_ref, g_ref, c_ref, s1_ref, s2_ref, o_ref, *, scale):
    gs = g_ref[...] * scale
    pw = 2 * HEAD_DIM
    same_head = (lax.broadcasted_iota(jnp.int32, (pw, pw), 0) // HEAD_DIM
                 == lax.broadcasted_iota(jnp.int32, (pw, pw), 1) // HEAD_DIM)
    ones_bd = jnp.where(same_head, 1.0, 0.0).astype(BF16)
    for rows, y in _row_chunks(h_ref, w_ref):
        yy = (y * y).astype(BF16)
        c, s1, s2 = c_ref[rows, :], s1_ref[rows, :], s2_ref[rows, :]
        for hh in range(y.shape[1] // HEAD_DIM):
            sl = slice(hh * HEAD_DIM, (hh + 1) * HEAD_DIM)
            if hh % 2 == 0:
                ss_pair = jnp.dot(yy[:, hh * HEAD_DIM:hh * HEAD_DIM + pw], ones_bd, preferred_element_type=F32)
            ms = ss_pair[:, (hh % 2) * HEAD_DIM:(hh % 2 + 1) * HEAD_DIM] * (1.0 / HEAD_DIM)
            t = y[:, sl] * lax.rsqrt(ms + EPS) * gs
            out = (t * c + pltpu.roll(t, HEAD_DIM - ROPE_DIM // 2, 1) * s1 + pltpu.roll(t, ROPE_DIM // 2, 1) * s2)
            o_ref[hh, rows, :] = out.astype(o_ref.dtype)


def _inproj_heads_kernel(h_ref, w_ref, o_ref):
    for rows, y in _row_chunks(h_ref, w_ref):
        for hh in range(y.shape[1] // HEAD_DIM):
            o_ref[hh, rows, :] = y[:, hh * HEAD_DIM:(hh + 1) * HEAD_DIM].astype(o_ref.dtype)


CONV_PAD_ROWS = 16


def _inproj_conv_kernel(h_ref, w_ref, cw_ref, cb_ref, o_ref, scr):
    s, tn = h_ref.shape[0], w_ref.shape[1]
    q = s // 4
    part = INPROJ_ROW_CHUNK // 4
    pad = CONV_PAD_ROWS
    zeros = jnp.zeros((pad, LANES), F32)
    for c in range(tn // LANES):
        scr[c, 0:pad, :] = zeros
        scr[c, pad + s:2 * pad + s, :] = zeros

    def finish(r0):
        for c in range(tn // LANES):
            cols = slice(c * LANES, (c + 1) * LANES)
            w0, w1, w2, cb = cw_ref[0:1, cols], cw_ref[1:2, cols], cw_ref[2:3, cols], cb_ref[:, cols]
            y = [scr[c, pl.ds(pad + r0 + off, part, stride=4), :] for off in range(-1, 5)]
            for cls in range(4):
                rows = slice(cls * q + r0 // 4, cls * q + r0 // 4 + part)
                o_ref[rows, cols] = (y[cls] * w0 + y[cls + 1] * w1 + y[cls + 2] * w2 + cb).astype(o_ref.dtype)

    for rows, y in _row_chunks(h_ref, w_ref):
        for c in range(tn // LANES):
            scr[c, pad + rows.start:pad + rows.stop, :] = y[:, c * LANES:(c + 1) * LANES]
        if rows.start > 0:
            finish(rows.start - INPROJ_ROW_CHUNK)
    finish(s - INPROJ_ROW_CHUNK)


def _inproj_gate_kernel(h_ref, w_ref, gb_ref, o_ref):
    for rows, y in _row_chunks(h_ref, w_ref):
        o_ref[rows, :] = jax.nn.sigmoid(y + gb_ref[...]).astype(o_ref.dtype)


def _inproj(h3, w, col0, width, body, extras, extra_specs, tn, scratch=(), head_major=False):
    b, s, d = h3.shape
    tn = min(tn, width)
    nj = width // tn
    jb0 = col0 // tn
    est = (2 * _nbytes((s, d), BF16) + 2 * _nbytes((d, tn), BF16) + 2 * _nbytes((s, tn), BF16)
           + 4 * _nbytes((s, tn), F32) + (8 << 20))
    return pl.pallas_call(
        body,
        grid=(b, nj),
        in_specs=[pl.BlockSpec((None, s, d), lambda i, j: (i, 0, 0)),
                  pl.BlockSpec((d, tn), lambda i, j: (0, jb0 + j))] + extra_specs,
        out_specs=(pl.BlockSpec((None, tn // HEAD_DIM, s, HEAD_DIM), lambda i, j: (i, j, 0, 0)) if head_major
                   else pl.BlockSpec((None, s, tn), lambda i, j: (i, 0, j))),
        out_shape=jax.ShapeDtypeStruct((b, width // HEAD_DIM, s, HEAD_DIM) if head_major else (b, s, width), BF16),
        scratch_shapes=list(scratch),
        compiler_params=_cparams(("parallel", "arbitrary"), est),
        name="inproj_" + getattr(body, "func", body).__name__,
    )(h3, w, *extras)


def _rope_tables(s):
    half = ROPE_DIM // 2
    inv_freq = ROPE_THETA ** (-jnp.arange(half, dtype=F32) / half)
    ang = jnp.arange(s, dtype=F32)[:, None] * inv_freq[None, :]
    cos, sin = jnp.cos(ang), jnp.sin(ang)
    c = jnp.concatenate([cos, cos, jnp.ones((s, HEAD_DIM - ROPE_DIM), F32)], axis=1)
    s1 = jnp.concatenate([-sin, jnp.zeros((s, HEAD_DIM - half), F32)], axis=1)
    s2 = jnp.concatenate([jnp.zeros((s, half), F32), sin, jnp.zeros((s, HEAD_DIM - ROPE_DIM), F32)], axis=1)
    return c, s1, s2


def _band_bias(off, kw):
    col = lax.broadcasted_iota(jnp.int32, (ATT_QBLK, kw), 1)
    row = lax.broadcasted_iota(jnp.int32, (ATT_QBLK, kw), 0)
    return jnp.where(jnp.abs(col - row + off) <= ATT_RADIUS, 0.0, MASK_NEG).astype(F32)


def _band_blocks(seg_len):
    kw = min(ATT_KWIN, seg_len)
    for bi in range(seg_len // ATT_QBLK):
        q0 = bi * ATT_QBLK
        yield q0, min(max(q0 - ATT_RADIUS, 0), seg_len - kw), kw


def _band_attend(blocks, finish):
    pending = {}
    for i in range(len(blocks) + ATT_DEPTH):
        if i < len(blocks):
            qb, kb, bias = blocks[i][0]()
            pending[i] = lax.dot_general(qb, kb, (((1,), (1,)), ((), ())), preferred_element_type=F32) + bias
        j = i - ATT_DEPTH
        if j >= 0:
            s = pending.pop(j)
            m = jnp.max(s, axis=-1, keepdims=True)
            p = jnp.exp2(s - m).astype(BF16)
            oa = jnp.dot(p, blocks[j][1](), preferred_element_type=F32)
            finish(j, (oa[:, :HEAD_DIM], oa[:, HEAD_DIM:], jnp.broadcast_to(m, (ATT_QBLK, HEAD_DIM))))


def _softmax_merge(a, b):
    (o1, l1, m1), (o2, l2, m2) = a, b
    m = jnp.maximum(m1, m2)
    a1 = jnp.exp2(m1 - m)
    a2 = jnp.exp2(m2 - m)
    return a1 * o1 + a2 * o2, a1 * l1 + a2 * l2, m


def _attn_kernel(q_ref, k_ref, v_ref, o_ref,
                 qf0, kf0, vf0, qf1, kf1, vf1, qb1, kb1, vb1, qb2, kb2, vb2,
                 ao, al, am, bo, bl, bm):
    s = q_ref.shape[0]
    c = HEAD_DIM
    n1 = s // 4
    n2 = s // 16
    biases = {}

    def bias(off, kw):
        if (off, kw) not in biases:
            biases[(off, kw)] = _band_bias(off, kw)
        return biases[(off, kw)]

    def grouped(items, compute, store, group=ATT_GROUP):
        for g0 in range(0, len(items), group):
            chunk = items[g0:g0 + group]
            for it, val in zip(chunk, compute(chunk)):
                store(*it, val)

    def each(fn):
        return lambda chunk: [fn(*it) for it in chunk]

    qf0[...] = q_ref[...].astype(F32)
    kf0[...] = k_ref[...].astype(F32)
    vf0[...] = v_ref[...].astype(F32)

    def split4_store(r, src, dstf, dstb, x):
        rows = slice(r * n1, (r + 1) * n1)
        dstf[rows, :] = x
        dstb[rows, :] = x.astype(BF16)

    grouped([(r, src, dstf, dstb) for r in range(4)
             for src, dstf, dstb in ((qf0, qf1, qb1), (kf0, kf1, kb1), (vf0, vf1, vb1))],
            each(lambda r, src, dstf, dstb: src[pl.ds(r, n1, stride=4), :]), split4_store)

    def split16_store(seg, srcf, dstb, x):
        dstb[seg * n2:(seg + 1) * n2, :] = x

    grouped([(seg, srcf, dstb) for seg in range(16) for srcf, dstb in ((qf1, qb2), (kf1, kb2), (vf1, vb2))],
            each(lambda seg, srcf, dstb: srcf[pl.ds((seg // 4) * n1 + seg % 4, n2, stride=4), :].astype(BF16)),
            split16_store)

    def attend(qb, kb, vb, items, finish):
        def loaders(base, q0, w0, kw):
            return (lambda: (qb[base + q0:base + q0 + ATT_QBLK, :], kb[base + w0:base + w0 + kw, :],
                             bias(w0 - q0, kw)),
                    lambda: jnp.concatenate([vb[base + w0:base + w0 + kw, :], jnp.ones((kw, c), BF16)], axis=1))

        _band_attend([loaders(*it) for it in items], lambda i, olm: finish(*items[i], olm))

    def merged(olm, prev_refs, base, q0):
        return _softmax_merge(olm, tuple(ref[base + q0:base + q0 + ATT_QBLK, :] for ref in prev_refs))

    def finish16(base, q0, w0, kw, olm):
        seg = base // n2
        dst = pl.ds((seg // 4) * n1 + 4 * q0 + seg % 4, ATT_QBLK, stride=4)
        for ref, val in zip((ao, al, am), olm):
            ref[dst, :] = val

    attend(qb2, kb2, vb2, [(seg * n2, *blk) for seg in range(16) for blk in _band_blocks(n2)], finish16)

    def finish4(base, q0, w0, kw, olm):
        dst = pl.ds(4 * q0 + base // n1, ATT_QBLK, stride=4)
        for ref, val in zip((bo, bl, bm), merged(olm, (ao, al, am), base, q0)):
            ref[dst, :] = val

    attend(qb1, kb1, vb1, [(r4 * n1, *blk) for r4 in range(4) for blk in _band_blocks(n1)], finish4)

    def finish1(base, q0, w0, kw, olm):
        o, l, _ = merged(olm, (bo, bl, bm), base, q0)
        o_ref[q0:q0 + ATT_QBLK, :] = (o / l).astype(o_ref.dtype)

    attend(q_ref, k_ref, v_ref, [(0, *blk) for blk in _band_blocks(s)], finish1)


def _dilated_attention(q, k, v):
    b, nh, s, _ = q.shape
    aw = nh * HEAD_DIM
    assert tuple(w // (2 * d) for w, d in DILATED_PATTERNS) == (ATT_RADIUS,) * 3
    assert tuple(d for _, d in DILATED_PATTERNS) == (1, 4, 16)
    assert s % (16 * ATT_QBLK) == 0
    c = HEAD_DIM
    spec = pl.BlockSpec((None, None, s, c), lambda i, h: (i, h, 0, 0))
    out_spec = pl.BlockSpec((None, s, c), lambda i, h: (i, 0, h))
    f32buf = pltpu.VMEM((s, c), F32)
    bf16buf = pltpu.VMEM((s, c), BF16)
    scratch = [f32buf] * 6 + [bf16buf] * 6 + [f32buf] * 6
    est = 12 * _nbytes((s, c), F32) + 10 * _nbytes((s, c), BF16) + 8 * _nbytes((s, c), BF16) + (16 << 20)
    return pl.pallas_call(
        _attn_kernel,
        grid=(b, aw // c),
        in_specs=[spec, spec, spec],
        out_specs=out_spec,
        out_shape=jax.ShapeDtypeStruct((b, s, aw), BF16),
        scratch_shapes=scratch,
        compiler_params=_cparams(("parallel", "arbitrary"), est),
        name="dilated_attention",
    )(q, k, v)


def _filter_kernel(z_ref, w1_ref, b1_ref, w2_ref, b2_ref, fr_ref, w3_ref, t_ref, dl_ref,
                   filt_ref, asum_ref, hi_ref, lo_ref, *, blocks_per_dir):
    def split(a):
        hi = a.astype(BF16)
        return hi, (a - hi.astype(F32)).astype(BF16)

    @pl.when(pl.program_id(0) == 0)
    def _():
        fr = fr_ref[...]
        hdn = jnp.sin(fr * (jnp.dot(z_ref[...], w1_ref[...], precision=HIGHEST,
                                    preferred_element_type=F32) + b1_ref[...]))
        hdn = jnp.sin(fr * (jnp.dot(hdn, w2_ref[...], precision=HIGHEST,
                                    preferred_element_type=F32) + b2_ref[...]))
        hi_ref[...], lo_ref[...] = split(hdn)

    w_hi, w_lo = split(w3_ref[...])
    hf = (jnp.dot(hi_ref[...], w_hi, preferred_element_type=F32)
          + jnp.dot(hi_ref[...], w_lo, preferred_element_type=F32)
          + jnp.dot(lo_ref[...], w_hi, preferred_element_type=F32))
    hf = hf * jnp.exp(-t_ref[...] * dl_ref[...])
    is_bwd = (pl.program_id(0) // blocks_per_dir) % 2 == 1
    row = lax.broadcasted_iota(jnp.int32, hf.shape, 0)
    hf = jnp.where(jnp.logical_and(row == 0, is_bwd), 0.0, hf)
    filt_ref[...] = hf.astype(filt_ref.dtype)
    asum_ref[...] = jnp.sum(jnp.abs(hf), axis=0, keepdims=True)


def _mod4_rows(a):
    return jnp.concatenate([a[c::4] for c in range(4)], axis=0)


def _hyena_filter_taps(seq_len, w1, b1, w2, b2, sin_freq, w3, hy_w, tn=512):
    ffn = w1.shape[1]
    pad = LANES
    t = jnp.linspace(0.0, 1.0, seq_len, dtype=F32)[:, None]
    wpos = 2.0 * math.pi * jnp.arange(seq_len, dtype=F32) / seq_len
    bands = jnp.linspace(1e-4, HY_BANDS - 1, HY_BANDS, dtype=F32)
    ang = wpos[:, None] * bands[None, :]
    z = jnp.concatenate([t, jnp.cos(ang), -jnp.sin(ang)], axis=-1)
    z = _mod4_rows(jnp.pad(z, ((0, 0), (0, pad - z.shape[1]))))
    t = _mod4_rows(t)
    w1p = jnp.pad(w1.astype(F32), ((0, pad - w1.shape[0]), (0, pad - ffn)))
    w2p = jnp.pad(w2.astype(F32), ((0, pad - ffn), (0, pad - ffn)))
    w3p = jnp.pad(w3.astype(F32), ((0, pad - ffn), (0, 0)))
    padv = lambda a: jnp.pad(a.astype(F32), (0, pad - ffn)).reshape(1, pad)
    deltas = jnp.abs(jnp.linspace(HY_MIN_DECAY, HY_MAX_DECAY, hy_w, dtype=F32)).reshape(1, hy_w)
    ncol = w3.shape[1]
    tn = min(tn, hy_w)
    bpd = hy_w // tn
    full = lambda shape: pl.BlockSpec(shape, lambda j: (0, 0))
    return pl.pallas_call(
        functools.partial(_filter_kernel, blocks_per_dir=bpd),
        grid=(ncol // tn,),
        in_specs=[full((seq_len, pad)), full((pad, pad)), full((1, pad)), full((pad, pad)), full((1, pad)),
                  full((1, pad)), pl.BlockSpec((pad, tn), lambda j: (0, j)), full((seq_len, 1)),
                  pl.BlockSpec((1, tn), lambda j: (0, j % bpd))],
        out_specs=[pl.BlockSpec((seq_len, tn), lambda j: (0, j)), pl.BlockSpec((1, tn), lambda j: (0, j))],
        out_shape=[jax.ShapeDtypeStruct((seq_len, ncol), BF16), jax.ShapeDtypeStruct((1, ncol), F32)],
        scratch_shapes=[pltpu.VMEM((seq_len, pad), BF16)] * 2,
        compiler_params=_cparams(("arbitrary",), 32 << 20),
        name="hyena_filter_taps",
    )(z, w1p, padv(b1), w2p, padv(b2), padv(sin_freq), w3p, t, deltas)


def _dft_matrices(s, n_total):
    n = 2 * s
    k = jnp.arange(s, dtype=jnp.int32)

    def tables(cols):
        ang = ((k[:, None] * cols[None, :]) % n).astype(F32) * (2.0 * math.pi / n)
        return jnp.cos(ang), jnp.sin(ang)

    cl, sl = tables(jnp.arange(LANES, dtype=jnp.int32))
    ch, sh = tables(jnp.arange(s // LANES, dtype=jnp.int32) * LANES)
    cosm = (ch[:, :, None] * cl[:, None, :] - sh[:, :, None] * sl[:, None, :]).reshape(s, s)
    sinm = (sh[:, :, None] * cl[:, None, :] + ch[:, :, None] * sl[:, None, :]).reshape(s, s)
    alt = jnp.where(k % 2 == 0, 1.0, -1.0).astype(F32)
    im = (-sinm).at[0].set(alt)
    scale_re = jnp.full((s,), 2.0 / n_total, F32).at[0].set(1.0 / n_total)[:, None]
    r = min(HY_FREQ_CHUNK, s)
    chunked = lambda a: a.reshape(2, s // r, r, -1).transpose(1, 0, 2, 3).reshape(n, -1)
    fwds, invs = [], []
    for c in range(4):
        theta = ((c * k) % n_total).astype(F32) * (2.0 * math.pi / n_total)
        cs, sn = jnp.cos(theta)[:, None], jnp.sin(theta)[:, None]
        fwds.append(chunked(jnp.concatenate([cs * cosm + sn * im, cs * im - sn * cosm], axis=0)))
        invs.append(chunked(jnp.concatenate([(cs * cosm + sn * im) * scale_re, (cs * im - sn * cosm) * scale_re],
                                            axis=0)).T)
    return jnp.stack(fwds).astype(BF16), jnp.stack(invs).astype(BF16)


HY_FREQ_CHUNK = 256


def _freq_chunks(m):
    r = min(HY_FREQ_CHUNK, m)
    return [slice(r0, r0 + r) for r0 in range(0, m, r)]


def _first_row(shape):
    return lax.broadcasted_iota(jnp.int32, shape, 0) == 0


COS_PI_4 = math.sqrt(0.5)


def _quarter_spectrum(z_ref, f_ref, rows):
    q = z_ref.shape[0] // 4
    r = rows.stop - rows.start
    frows = slice(2 * rows.start, 2 * rows.stop)
    sub = []
    for c in range(4):
        f = jnp.dot(f_ref[c, frows, :], z_ref[c * q:(c + 1) * q, :], preferred_element_type=F32)
        sub.append((f[:r], f[r:]))
    (t0r, t0i), (t1r, t1i), (t2r, t2i), (t3r, t3i) = sub
    ur, ui, vr, vi = t0r + t2r, t0i + t2i, t0r - t2r, t0i - t2i
    wr, wi, xr, xi = t1r + t3r, t1i + t3i, t1r - t3r, t1i - t3i
    slots = [(ur + wr, ui + wi), (vr - xi, vi + xr), (vr + xi, vi - xr), (ur - wr, ui - wi)]
    if rows.start == 0:
        s0, s1, s2, s3 = (sub[c][0][0:1, :] for c in range(4))
        n0, n1, n2, n3 = (sub[c][1][0:1, :] for c in range(4))
        k4 = COS_PI_4
        row0 = [(s0 + s1 + s2 + s3, s0 - s1 + s2 - s3),
                (s0 - s2, s3 - s1),
                (n0 + k4 * (n1 - n3), -n2 - k4 * (n1 + n3)),
                (n0 - k4 * (n1 - n3), n2 - k4 * (n1 + n3))]
        first = _first_row(ur.shape)
        slots = [(jnp.where(first, p0[0], p[0]), jnp.where(first, p0[1], p[1])) for p, p0 in zip(slots, row0)]
    return slots


def _spectrum_product(h, z, with_first):
    out = []
    for i, ((hr, hi), (zr, zi)) in enumerate(zip(h, z)):
        rr, ii = hr * zr, hi * zi
        yr, yi = rr - ii, hr * zi + hi * zr
        if i == 0 and with_first:
            first = _first_row(rr.shape)
            yr, yi = jnp.where(first, rr, yr), jnp.where(first, ii, yi)
        out.append((yr, yi))
    return out


def _inverse_butterfly(y, rows, dtype):
    (ar, ai), (br, bi), (cr, ci), (dr, di) = y
    g = [(ar + br + cr + dr, ai + bi + ci + di),
         (ar + bi - ci - dr, ai - br + cr - di),
         (ar - br - cr + dr, ai - bi - ci + di),
         (ar - bi + ci - dr, ai + br - cr - di)]
    if rows.start == 0:
        y0, yn, qr, qi = ar[0:1, :], ai[0:1, :], br[0:1, :], bi[0:1, :]
        a, b, a2, b2 = cr[0:1, :], ci[0:1, :], dr[0:1, :], di[0:1, :]
        k4 = 2.0 * COS_PI_4
        dc = [y0 + yn + 2.0 * qr, y0 - yn - 2.0 * qi, y0 + yn - 2.0 * qr, y0 - yn + 2.0 * qi]
        half = [2.0 * (a + a2), k4 * (a - b) - k4 * (a2 + b2), 2.0 * (b2 - b), k4 * (a2 - b2) - k4 * (a + b)]
        first = _first_row(ar.shape)
        g = [(jnp.where(first, d0, gr), jnp.where(first, h0, gi)) for (gr, gi), d0, h0 in zip(g, dc, half)]
    return [jnp.concatenate([gr.astype(dtype), gi.astype(dtype)], axis=0) for gr, gi in g]


def _filter_spectrum_kernel(tf_ref, tb_ref, nf_ref, nb_ref, f_ref, o_ref):
    inv = 1.0 / (nf_ref[...] + nb_ref[...])
    for rows in _freq_chunks(o_ref.shape[1]):
        fwd = _quarter_spectrum(tf_ref, f_ref, rows)
        bwd = _quarter_spectrum(tb_ref, f_ref, rows)
        for i, ((fr, fi), (gr, gi)) in enumerate(zip(fwd, bwd)):
            im = fi - gi
            if i == 0 and rows.start == 0:
                im = jnp.where(_first_row(im.shape), fi + gi, im)
            o_ref[2 * i, rows, :] = (fr + gr) * inv
            o_ref[2 * i + 1, rows, :] = im * inv


def _filter_spectrum(taps, asum, fwd, hy_w, tc):
    l = taps.shape[0]
    q = fwd.shape[2]
    nb = hy_w // tc
    col = lambda d: (lambda o, c: (0, (2 * o + d) * nb + c))
    const = lambda shape: pl.BlockSpec(shape, lambda o, c: (0,) * len(shape), pipeline_mode=pl.Buffered(1))
    est = (_nbytes((4, 2 * q, q), BF16) + 4 * _nbytes((l, tc), BF16)
           + 2 * _nbytes((8, q, tc), F32) + 48 * _nbytes((HY_FREQ_CHUNK, tc), F32) + (8 << 20))
    return pl.pallas_call(
        _filter_spectrum_kernel,
        grid=(HY_ORDER, nb),
        in_specs=[pl.BlockSpec((l, tc), col(0)), pl.BlockSpec((l, tc), col(1)),
                  pl.BlockSpec((1, tc), col(0)), pl.BlockSpec((1, tc), col(1)),
                  const((4, 2 * q, q))],
        out_specs=pl.BlockSpec((None, 8, q, tc), lambda o, c: (o, 0, 0, c)),
        out_shape=jax.ShapeDtypeStruct((HY_ORDER, 8, q, hy_w), F32),
        compiler_params=_cparams(("parallel", "arbitrary"), est),
        name="hyena_filter_spectrum",
    )(taps, taps, asum, asum, fwd)


def _hyena_conv_kernel(z_ref, g_ref, h_ref, b_ref, f_ref, inv_ref, o_ref, g0_ref, g1_ref, g2_ref, g3_ref,
                       *scratch, natural_out):
    s, tc = z_ref.shape
    q = s // 4
    spectra = (g0_ref, g1_ref, g2_ref, g3_ref)
    for rows in _freq_chunks(q):
        spec = _quarter_spectrum(z_ref, f_ref, rows)
        filt = [(h_ref[2 * i, rows, :], h_ref[2 * i + 1, rows, :]) for i in range(4)]
        prod = _spectrum_product(filt, spec, rows.start == 0)
        for ref, packed in zip(spectra, _inverse_butterfly(prod, rows, g0_ref.dtype)):
            ref[2 * rows.start:2 * rows.stop, :] = packed
    bias = b_ref[...]
    for rows in _freq_chunks(q):
        r = rows.stop - rows.start
        outs = []
        for c in range(4):
            cls = slice(c * q + rows.start, c * q + rows.stop)
            conv = jnp.dot(inv_ref[c, rows, :], spectra[c][...], preferred_element_type=F32)
            outs.append((cls, g_ref[cls, :].astype(F32) * (conv + bias * z_ref[cls, :].astype(F32))))
        if natural_out:
            (nat,) = scratch
            span = slice(4 * rows.start, 4 * rows.stop)
            for lane in range(tc // LANES):
                cols = slice(lane * LANES, (lane + 1) * LANES)
                for c, (_, val) in enumerate(outs):
                    nat[lane, pl.ds(4 * rows.start + c, r, stride=4), :] = val[:, cols]
                o_ref[span, cols] = nat[lane, span, :].astype(o_ref.dtype)
        else:
            for cls, val in outs:
                o_ref[cls, :] = val.astype(o_ref.dtype)


def _hyena_conv(z, z_col0, gate, gate_col0, filt, order, bias_row, mats, width, natural_out):
    fwd, inv, tc = mats
    b, s, _ = z.shape
    q = s // 4
    nc = width // tc
    zb0, gb0 = z_col0 // tc, gate_col0 // tc
    const = lambda shape: pl.BlockSpec(shape, lambda c, i: (0,) * len(shape), pipeline_mode=pl.Buffered(1))
    scratch = [pltpu.VMEM((2 * q, tc), BF16)] * 4
    if natural_out:
        scratch.append(pltpu.VMEM((tc // LANES, s, LANES), F32))
    est = (2 * _nbytes((4, 2 * q, q), BF16) + 2 * _nbytes((8, q, tc), F32)
           + 6 * _nbytes((s, tc), BF16) + 4 * _nbytes((2 * q, tc), BF16) + _nbytes((s, tc), F32)
           + 64 * _nbytes((HY_FREQ_CHUNK, tc), F32) + (8 << 20))
    return pl.pallas_call(
        functools.partial(_hyena_conv_kernel, natural_out=natural_out),
        grid=(nc, b),
        in_specs=[pl.BlockSpec((None, s, tc), lambda c, i: (i, 0, zb0 + c)),
                  pl.BlockSpec((None, s, tc), lambda c, i: (i, 0, gb0 + c)),
                  pl.BlockSpec((None, 8, q, tc), lambda c, i: (order, 0, 0, c)),
                  pl.BlockSpec((1, tc), lambda c, i: (0, c)),
                  const((4, 2 * q, q)), const((4, q, 2 * q))],
        out_specs=pl.BlockSpec((None, s, tc), lambda c, i: (i, 0, c)),
        out_shape=jax.ShapeDtypeStruct((b, s, width), BF16),
        scratch_shapes=scratch,
        compiler_params=_cparams(("parallel", "arbitrary"), est),
        name="hyena_conv",
    )(z, gate, filt, bias_row, fwd, inv)


def _chunk_rows(n_rows):
    return [slice(r0, r0 + INPROJ_ROW_CHUNK) for r0 in range(0, n_rows, INPROJ_ROW_CHUNK)]


def _merge_kernel(a_ref, hy_ref, wa_ref, wh_ref, g0_ref, g1_ref, o_ref):
    for rows in _chunk_rows(a_ref.shape[0]):
        ya = jnp.dot(a_ref[rows, :], wa_ref[...], preferred_element_type=F32)
        yh = jnp.dot(hy_ref[rows, :], wh_ref[...], preferred_element_type=F32)
        o_ref[rows, :] = (g0_ref[rows, :].astype(F32) * ya + g1_ref[rows, :].astype(F32) * yh).astype(o_ref.dtype)


def _weight_spec(k, tn, n):
    if tn == n:
        return pl.BlockSpec((k, n), lambda i, j: (0, 0), pipeline_mode=pl.Buffered(1))
    return pl.BlockSpec((k, tn), lambda i, j: (0, j))


def _gated_merge(attn, hy, w_att, w_hy, gates, tm=512, tn=2048):
    t, k = attn.shape
    n = w_att.shape[1]
    tm, tn = min(tm, t), min(tn, n)
    g1b0 = n // tn
    est = 4 * _nbytes((tm, k), BF16) + 4 * _nbytes((k, tn), BF16) + 6 * _nbytes((tm, tn), F32) + (8 << 20)
    return pl.pallas_call(
        _merge_kernel,
        grid=(t // tm, n // tn),
        in_specs=[pl.BlockSpec((tm, k), lambda i, j: (i, 0)),
                  pl.BlockSpec((tm, k), lambda i, j: (i, 0)),
                  _weight_spec(k, tn, n),
                  _weight_spec(k, tn, n),
                  pl.BlockSpec((tm, tn), lambda i, j: (i, j)),
                  pl.BlockSpec((tm, tn), lambda i, j: (i, g1b0 + j))],
        out_specs=pl.BlockSpec((tm, tn), lambda i, j: (i, j)),
        out_shape=jax.ShapeDtypeStruct((t, n), BF16),
        compiler_params=_cparams(("parallel", "arbitrary"), est),
        name="gated_merge",
    )(attn, hy, w_att, w_hy, gates, gates)


def _mm_residual_kernel(a_ref, w_ref, ra_ref, rb_ref, o_ref, *, na):
    def body(r_ref):
        for rows in _chunk_rows(a_ref.shape[0]):
            o_ref[rows, :] = r_ref[rows, :] + jnp.dot(a_ref[rows, :], w_ref[...], preferred_element_type=F32)

    _pick_group(na, ra_ref, rb_ref, body)


def _matmul_residual(a, w, res_a, res_b, tm=512, tn=2048):
    t, k = a.shape
    n = w.shape[1]
    tm, tn = min(tm, res_a.shape[0]), min(tn, n)
    assert res_a.shape[0] % tm == 0 and res_b.shape[0] % tm == 0
    na = res_a.shape[0] // tm
    est = 2 * _nbytes((tm, k), BF16) + 2 * _nbytes((k, tn), BF16) + 8 * _nbytes((tm, tn), F32) + (8 << 20)
    return pl.pallas_call(
        functools.partial(_mm_residual_kernel, na=na),
        grid=(t // tm, n // tn),
        in_specs=[pl.BlockSpec((tm, k), lambda i, j: (i, 0)),
                  _weight_spec(k, tn, n),
                  *_two_group_specs((tm, tn), na, col=lambda j: j)],
        out_specs=pl.BlockSpec((tm, tn), lambda i, j: (i, j)),
        out_shape=jax.ShapeDtypeStruct((t, n), F32),
        compiler_params=_cparams(("parallel", "arbitrary"), est),
        name="matmul_residual",
    )(a, w, res_a, res_b)


def _rmsnorm_rows(x, g):
    ms = jnp.mean(x * x, axis=-1, keepdims=True)
    return x * lax.rsqrt(ms + EPS) * g


def _mlp_kernel(x_ref, g_ref, wu_ref, wd_ref, o_ref, hm_ref):
    def body(first):
        for rows in _chunk_rows(hm_ref.shape[0]):
            if first:
                x = x_ref[rows, :]
                hm = _rmsnorm_rows(x, g_ref[...]).astype(hm_ref.dtype)
                hm_ref[rows, :] = hm
            else:
                hm = hm_ref[rows, :]
            a = jnp.dot(hm, wu_ref[...], preferred_element_type=F32)
            a = jnp.square(jnp.maximum(a, 0.0)).astype(BF16)
            y = jnp.dot(a, wd_ref[...], preferred_element_type=F32)
            o_ref[rows, :] = (x if first else o_ref[rows, :]) + y

    j = pl.program_id(1)
    pl.when(j == 0)(lambda: body(True))
    pl.when(j > 0)(lambda: body(False))


def _relu2_mlp(x, g, w_up, w_down, tm=1024, tf=1024):
    t, d = x.shape
    ff = w_up.shape[1]
    tm, tf = min(tm, t), min(tf, ff)
    est = (4 * _nbytes((tm, d), F32) + _nbytes((tm, d), BF16) + 4 * _nbytes((d, tf), BF16)
           + 3 * _nbytes((tm, tf), F32) + (8 << 20))
    return pl.pallas_call(
        _mlp_kernel,
        grid=(t // tm, ff // tf),
        in_specs=[pl.BlockSpec((tm, d), lambda i, j: (i, 0)),
                  pl.BlockSpec((1, d), lambda i, j: (0, 0)),
                  pl.BlockSpec((d, tf), lambda i, j: (0, j)),
                  pl.BlockSpec((tf, d), lambda i, j: (j, 0))],
        out_specs=pl.BlockSpec((tm, d), lambda i, j: (i, 0)),
        out_shape=jax.ShapeDtypeStruct((t, d), F32),
        scratch_shapes=[pltpu.VMEM((tm, d), BF16)],
        compiler_params=_cparams(("parallel", "arbitrary"), est),
        name="relu2_mlp",
    )(x, g.reshape(1, d), w_up, w_down)


def _ple_kernel(x_ref, g_ref, wg_ref, p_ref, wp_ref, o_ref):
    for rows in _chunk_rows(x_ref.shape[0]):
        x = x_ref[rows, :]
        hn = _rmsnorm_rows(x, g_ref[...]).astype(BF16)
        gate = jax.nn.sigmoid(jnp.dot(hn, wg_ref[...], preferred_element_type=F32))
        proj = jnp.dot(p_ref[rows, :].astype(BF16), wp_ref[...], preferred_element_type=F32)
        o_ref[rows, :] = x + gate * proj


def _gated_ple(x, row0, p, g, w_gate, w_proj, tm=512):
    t, pd = p.shape
    d = x.shape[1]
    tm = min(tm, t)
    rb0 = row0 // tm
    est = (4 * _nbytes((tm, d), F32) + _nbytes((d, d), BF16) + _nbytes((pd, d), BF16)
           + 2 * _nbytes((tm, pd), F32) + 8 * _nbytes((INPROJ_ROW_CHUNK, d), F32) + (8 << 20))
    return pl.pallas_call(
        _ple_kernel,
        grid=(t // tm, 1),
        in_specs=[pl.BlockSpec((tm, d), lambda i, j: (rb0 + i, 0)),
                  pl.BlockSpec((1, d), lambda i, j: (0, 0)),
                  _weight_spec(d, d, d),
                  pl.BlockSpec((tm, pd), lambda i, j: (i, 0)),
                  _weight_spec(pd, d, d)],
        out_specs=pl.BlockSpec((tm, d), lambda i, j: (i, 0)),
        out_shape=jax.ShapeDtypeStruct((t, d), F32),
        compiler_params=_cparams(("parallel", "arbitrary"), est),
        name="gated_ple",
    )(x, g.reshape(1, d), w_gate, p, w_proj)


def _encoder_layer(x_list, p_list, norm_mix_g, w_in, gate_b, q_norm_g, k_norm_g, hy_conv_w, hy_conv_b,
                   hy_w1, hy_b1, hy_w2, hy_b2, hy_sin_freq, hy_w3, hy_bias,
                   w_att_out, w_hy_out, w_o, norm_mlp_g, w_up, w_down,
                   norm_ple_g, w_ple_gate, w_ple_proj):
    xa3, xb3 = x_list
    s, d = xa3.shape[1:]
    b = xa3.shape[0] + xb3.shape[0]
    att_w = w_att_out.shape[0]
    hy_w = w_hy_out.shape[0]
    xa, xb = xa3.reshape(-1, d), xb3.reshape(-1, d)
    bf = lambda a: a.astype(BF16)

    h3 = _rmsnorm_cast(xa, xb, norm_mix_g).reshape(b, s, d)
    w_in_b = bf(w_in)
    rope_c, rope_s1, rope_s2 = _rope_tables(s)
    tab_spec = pl.BlockSpec((s, HEAD_DIM), lambda i, j: (0, 0))
    vec_spec = lambda tn, blk0=0: pl.BlockSpec((1, tn), lambda i, j: (0, blk0 + j))
    head_spec = pl.BlockSpec((1, HEAD_DIM), lambda i, j: (0, 0))
    tn = 1024
    qk = lambda g, col0, scale: _inproj(
        h3, w_in_b, col0, att_w, functools.partial(_inproj_qk_kernel, scale=scale),
        (g.reshape(1, HEAD_DIM), rope_c, rope_s1, rope_s2), [head_spec, tab_spec, tab_spec, tab_spec], tn,
        head_major=True)
    q = qk(q_norm_g, 0, HEAD_DIM ** -0.5 * math.log2(math.e))
    k = qk(k_norm_g, att_w, 1.0)
    v = _inproj(h3, w_in_b, 2 * att_w, att_w, _inproj_heads_kernel, (), [], tn, head_major=True)
    tnc = min(tn, hy_w)
    u = _inproj(h3, w_in_b, 3 * att_w, 3 * hy_w, _inproj_conv_kernel,
                (hy_conv_w, hy_conv_b.reshape(1, 3 * hy_w)),
                [pl.BlockSpec((3, tnc), lambda i, j: (0, j)), vec_spec(tnc)], tnc,
                scratch=[pltpu.VMEM((tnc // LANES, s + 2 * CONV_PAD_ROWS, LANES), F32)])
    tng = min(tn, d)
    gates = _inproj(h3, w_in_b, 3 * att_w + 3 * hy_w, N_BRANCH * d, _inproj_gate_kernel,
                    (gate_b.reshape(1, N_BRANCH * d),), [vec_spec(tng)], tng)

    attn = _dilated_attention(q, k, v)

    hy_tc = min(256, hy_w)
    mats = (*_dft_matrices(s // 4, 2 * s), hy_tc)
    taps, tap_asum = _hyena_filter_taps(s, hy_w1, hy_b1, hy_w2, hy_b2, hy_sin_freq, hy_w3, hy_w)
    filt = _filter_spectrum(taps, tap_asum, mats[0], hy_w, hy_tc)
    bias = hy_bias.astype(F32)
    z = _hyena_conv(u, 0, u, hy_w, filt, 0, bias[0:1], mats, hy_w, natural_out=False)
    hy = _hyena_conv(z, 0, u, 2 * hy_w, filt, 1, bias[1:2], mats, hy_w, natural_out=True)

    merged = _gated_merge(attn.reshape(b * s, att_w), hy.reshape(b * s, hy_w),
                          bf(w_att_out), bf(w_hy_out), gates.reshape(b * s, N_BRANCH * d))
    x2 = _matmul_residual(merged, bf(w_o), xa, xb)
    x2 = _relu2_mlp(x2, norm_mlp_g, bf(w_up), bf(w_down))

    outs, row0 = [], 0
    w_pg, w_pp = bf(w_ple_gate), bf(w_ple_proj)
    for p in p_list:
        pb = p.shape[0]
        p2 = p.reshape(pb * s, p.shape[-1])
        outs.append(_gated_ple(x2, row0, p2, norm_ple_g, w_pg, w_pp).reshape(pb, s, d))
        row0 += pb * s
    return outs


def kernel(x_prompt, x_sample, p_prompt, p_sample, norm_mix_g, w_in, gate_b, q_norm_g, k_norm_g, hy_conv_w, hy_conv_b, hy_w1, hy_b1, hy_w2, hy_b2, hy_sin_freq, hy_w3, hy_bias, w_att_out, w_hy_out, w_o, norm_mlp_g, w_up, w_down, norm_ple_g, w_ple_gate, w_ple_proj):
    assert x_prompt.shape[1:] == x_sample.shape[1:]
    y_prompt, y_sample = x_prompt, x_sample
    for i in range(w_in.shape[0]):
        y_prompt, y_sample = _encoder_layer(
            (y_prompt, y_sample), (p_prompt[i], p_sample[i]), norm_mix_g[i], w_in[i], gate_b[i], q_norm_g[i], k_norm_g[i],
            hy_conv_w[i], hy_conv_b[i], hy_w1[i], hy_b1[i], hy_w2[i], hy_b2[i], hy_sin_freq[i], hy_w3[i],
            hy_bias[i], w_att_out[i], w_hy_out[i], w_o[i], norm_mlp_g[i], w_up[i], w_down[i],
            norm_ple_g[i], w_ple_gate[i], w_ple_proj[i])
    return (y_prompt, y_sample)
```

```python
import functools
import math

import jax
import jax.numpy as jnp
from jax import lax
from jax.experimental import pallas as pl
from jax.experimental.pallas import tpu as pltpu

F32 = jnp.float32
BF16 = jnp.bfloat16
HIGHEST = lax.Precision.HIGHEST

EPS = 1e-6
HEAD_DIM = 128
ROPE_DIM = HEAD_DIM // 4
ROPE_THETA = 500000.0
DILATED_PATTERNS = ((128, 1), (512, 4), (2048, 16))
ATT_RADIUS = 64
ATT_QBLK = 128
ATT_KWIN = ATT_QBLK + 2 * ATT_RADIUS
ATT_GROUP = 8
ATT_DEPTH = 3
HY_ORDER = 2
HY_EMB_DIM = 33
HY_BANDS = (HY_EMB_DIM - 1) // 2
HY_FAST_DECAY = 0.3
HY_SLOW_DECAY = 1.5
HY_DECAY_TARGET = 1e-2
HY_MIN_DECAY = math.log(HY_DECAY_TARGET) / HY_SLOW_DECAY
HY_MAX_DECAY = math.log(HY_DECAY_TARGET) / HY_FAST_DECAY
N_BRANCH = 2
LANES = 128
MASK_NEG = -1e30

VMEM_CAP_BYTES = 60 * 1024 * 1024


def _cparams(sem, est_bytes):
    limit = int(min(VMEM_CAP_BYTES, max(32 * 1024 * 1024, est_bytes)))
    return pltpu.CompilerParams(dimension_semantics=sem, vmem_limit_bytes=limit)


def _nbytes(shape, dtype):
    return math.prod(shape) * jnp.dtype(dtype).itemsize


def _two_group_specs(block, na, col=lambda *j: 0):
    return (pl.BlockSpec(block, lambda i, *j: (jnp.minimum(i, na - 1), jnp.where(i < na, col(*j), 0))),
            pl.BlockSpec(block, lambda i, *j: (jnp.maximum(i - na, 0), jnp.where(i >= na, col(*j), 0))))


def _pick_group(na, xa_ref, xb_ref, body):
    i = pl.program_id(0)
    pl.when(i < na)(lambda: body(xa_ref))
    pl.when(i >= na)(lambda: body(xb_ref))


def _rmsnorm_kernel(xa_ref, xb_ref, g_ref, o_ref, *, na):
    def body(x_ref):
        x = x_ref[...]
        ms = jnp.mean(x * x, axis=-1, keepdims=True)
        o_ref[...] = (x * lax.rsqrt(ms + EPS) * g_ref[...]).astype(o_ref.dtype)

    _pick_group(na, xa_ref, xb_ref, body)


def _rmsnorm_cast(xa, xb, g, tm=512):
    d = xa.shape[1]
    assert xa.shape[0] % tm == 0 and xb.shape[0] % tm == 0
    na, t = xa.shape[0] // tm, xa.shape[0] + xb.shape[0]
    return pl.pallas_call(
        functools.partial(_rmsnorm_kernel, na=na),
        grid=(t // tm,),
        in_specs=[*_two_group_specs((tm, d), na), pl.BlockSpec((1, d), lambda i: (0, 0))],
        out_specs=pl.BlockSpec((tm, d), lambda i: (i, 0)),
        out_shape=jax.ShapeDtypeStruct((t, d), BF16),
        compiler_params=_cparams(("parallel",), 6 * _nbytes((tm, d), F32)),
        name="rmsnorm_cast",
    )(xa, xb, g.reshape(1, d))


INPROJ_ROW_CHUNK = 256


def _row_chunks(h_ref, w_ref):
    for r0 in range(0, h_ref.shape[0], INPROJ_ROW_CHUNK):
        rows = slice(r0, r0 + INPROJ_ROW_CHUNK)
        yield rows, jnp.dot(h_ref[rows, :], w_ref[...], preferred_element_type=F32)


def _inproj_qk_kernel(h_ref, w_ref, g_ref, c_ref, s1_ref, s2_ref, o_ref, *, scale):
    gs = g_ref[...] * scale
    pw = 2 * HEAD_DIM
    same_head = (lax.broadcasted_iota(jnp.int32, (pw, pw), 0) // HEAD_DIM
                 == lax.broadcasted_iota(jnp.int32, (pw, pw), 1) // HEAD_DIM)
    ones_bd = jnp.where(same_head, 1.0, 0.0).astype(BF16)
    for rows, y in _row_chunks(h_ref, w_ref):
        yy = (y * y).astype(BF16)
        c, s1, s2 = c_ref[rows, :], s1_ref[rows, :], s2_ref[rows, :]
        for hh in range(y.shape[1] // HEAD_DIM):
            sl = slice(hh * HEAD_DIM, (hh + 1) * HEAD_DIM)
            if hh % 2 == 0:
                ss_pair = jnp.dot(yy[:, hh * HEAD_DIM:hh * HEAD_DIM + pw], ones_bd, preferred_element_type=F32)
            ms = ss_pair[:, (hh % 2) * HEAD_DIM:(hh % 2 + 1) * HEAD_DIM] * (1.0 / HEAD_DIM)
            t = y[:, sl] * lax.rsqrt(ms + EPS) * gs
            out = (t * c + pltpu.roll(t, HEAD_DIM - ROPE_DIM // 2, 1) * s1 + pltpu.roll(t, ROPE_DIM // 2, 1) * s2)
            o_ref[hh, rows, :] = out.astype(o_ref.dtype)


def _inproj_heads_kernel(h_ref, w_ref, o_ref):
    for rows, y in _row_chunks(h_ref, w_ref):
        for hh in range(y.shape[1] // HEAD_DIM):
            o_ref[hh, rows, :] = y[:, hh * HEAD_DIM:(hh + 1) * HEAD_DIM].astype(o_ref.dtype)


CONV_PAD_ROWS = 16


def _inproj_conv_kernel(h_ref, w_ref, cw_ref, cb_ref, o_ref, scr):
    s, tn = h_ref.shape[0], w_ref.shape[1]
    q = s // 4
    part = INPROJ_ROW_CHUNK // 4
    pad = CONV_PAD_ROWS
    zeros = jnp.zeros((pad, LANES), F32)
    for c in range(tn // LANES):
        scr[c, 0:pad, :] = zeros
        scr[c, pad + s:2 * pad + s, :] = zeros

    def finish(r0):
        for c in range(tn // LANES):
            cols = slice(c * LANES, (c + 1) * LANES)
            w0, w1, w2, cb = cw_ref[0:1, cols], cw_ref[1:2, cols], cw_ref[2:3, cols], cb_ref[:, cols]
            y = [scr[c, pl.ds(pad + r0 + off, part, stride=4), :] for off in range(-1, 5)]
            for cls in range(4):
                rows = slice(cls * q + r0 // 4, cls * q + r0 // 4 + part)
                o_ref[rows, cols] = (y[cls] * w0 + y[cls + 1] * w1 + y[cls + 2] * w2 + cb).astype(o_ref.dtype)

    for rows, y in _row_chunks(h_ref, w_ref):
        for c in range(tn // LANES):
            scr[c, pad + rows.start:pad + rows.stop, :] = y[:, c * LANES:(c + 1) * LANES]
        if rows.start > 0:
            finish(rows.start - INPROJ_ROW_CHUNK)
    finish(s - INPROJ_ROW_CHUNK)


def _inproj_gate_kernel(h_ref, w_ref, gb_ref, o_ref):
    for rows, y in _row_chunks(h_ref, w_ref):
        o_ref[rows, :] = jax.nn.sigmoid(y + gb_ref[...]).astype(o_ref.dtype)


def _inproj(h3, w, col0, width, body, extras, extra_specs, tn, scratch=(), head_major=False):
    b, s, d = h3.shape
    tn = min(tn, width)
    nj = width // tn
    jb0 = col0 // tn
    est = (2 * _nbytes((s, d), BF16) + 2 * _nbytes((d, tn), BF16) + 2 * _nbytes((s, tn), BF16)
           + 4 * _nbytes((s, tn), F32) + (8 << 20))
    return pl.pallas_call(
        body,
        grid=(b, nj),
        in_specs=[pl.BlockSpec((None, s, d), lambda i, j: (i, 0, 0)),
                  pl.BlockSpec((d, tn), lambda i, j: (0, jb0 + j))] + extra_specs,
        out_specs=(pl.BlockSpec((None, tn // HEAD_DIM, s, HEAD_DIM), lambda i, j: (i, j, 0, 0)) if head_major
                   else pl.BlockSpec((None, s, tn), lambda i, j: (i, 0, j))),
        out_shape=jax.ShapeDtypeStruct((b, width // HEAD_DIM, s, HEAD_DIM) if head_major else (b, s, width), BF16),
        scratch_shapes=list(scratch),
        compiler_params=_cparams(("parallel", "arbitrary"), est),
        name="inproj_" + getattr(body, "func", body).__name__,
    )(h3, w, *extras)


def _rope_tables(s):
    half = ROPE_DIM // 2
    inv_freq = ROPE_THETA ** (-jnp.arange(half, dtype=F32) / half)
    ang = jnp.arange(s, dtype=F32)[:, None] * inv_freq[None, :]
    cos, sin = jnp.cos(ang), jnp.sin(ang)
    c = jnp.concatenate([cos, cos, jnp.ones((s, HEAD_DIM - ROPE_DIM), F32)], axis=1)
    s1 = jnp.concatenate([-sin, jnp.zeros((s, HEAD_DIM - half), F32)], axis=1)
    s2 = jnp.concatenate([jnp.zeros((s, half), F32), sin, jnp.zeros((s, HEAD_DIM - ROPE_DIM), F32)], axis=1)
    return c, s1, s2


def _band_bias(off, kw):
    col = lax.broadcasted_iota(jnp.int32, (ATT_QBLK, kw), 1)
    row = lax.broadcasted_iota(jnp.int32, (ATT_QBLK, kw), 0)
    return jnp.where(jnp.abs(col - row + off) <= ATT_RADIUS, 0.0, MASK_NEG).astype(F32)


def _band_blocks(seg_len):
    kw = min(ATT_KWIN, seg_len)
    for bi in range(seg_len // ATT_QBLK):
        q0 = bi * ATT_QBLK
        yield q0, min(max(q0 - ATT_RADIUS, 0), seg_len - kw), kw


def _band_attend(blocks, finish):
    pending = {}
    for i in range(len(blocks) + ATT_DEPTH):
        if i < len(blocks):
            qb, kb, bias = blocks[i][0]()
            pending[i] = lax.dot_general(qb, kb, (((1,), (1,)), ((), ())), preferred_element_type=F32) + bias
        j = i - ATT_DEPTH
        if j >= 0:
            s = pending.pop(j)
            m = jnp.max(s, axis=-1, keepdims=True)
            p = jnp.exp2(s - m).astype(BF16)
            oa = jnp.dot(p, blocks[j][1](), preferred_element_type=F32)
            finish(j, (oa[:, :HEAD_DIM], oa[:, HEAD_DIM:], jnp.broadcast_to(m, (ATT_QBLK, HEAD_DIM))))


def _softmax_merge(a, b):
    (o1, l1, m1), (o2, l2, m2) = a, b
    m = jnp.maximum(m1, m2)
    a1 = jnp.exp2(m1 - m)
    a2 = jnp.exp2(m2 - m)
    return a1 * o1 + a2 * o2, a1 * l1 + a2 * l2, m


def _attn_kernel(q_ref, k_ref, v_ref, o_ref,
                 qf0, kf0, vf0, qf1, kf1, vf1, qb1, kb1, vb1, qb2, kb2, vb2,
                 ao, al, am, bo, bl, bm):
    s = q_ref.shape[0]
    c = HEAD_DIM
    n1 = s // 4
    n2 = s // 16
    biases = {}

    def bias(off, kw):
        if (off, kw) not in biases:
            biases[(off, kw)] = _band_bias(off, kw)
        return biases[(off, kw)]

    def grouped(items, compute, store, group=ATT_GROUP):
        for g0 in range(0, len(items), group):
            chunk = items[g0:g0 + group]
            for it, val in zip(chunk, compute(chunk)):
                store(*it, val)

    def each(fn):
        return lambda chunk: [fn(*it) for it in chunk]

    qf0[...] = q_ref[...].astype(F32)
    kf0[...] = k_ref[...].astype(F32)
    vf0[...] = v_ref[...].astype(F32)

    def split4_store(r, src, dstf, dstb, x):
        rows = slice(r * n1, (r + 1) * n1)
        dstf[rows, :] = x
        dstb[rows, :] = x.astype(BF16)

    grouped([(r, src, dstf, dstb) for r in range(4)
             for src, dstf, dstb in ((qf0, qf1, qb1), (kf0, kf1, kb1), (vf0, vf1, vb1))],
            each(lambda r, src, dstf, dstb: src[pl.ds(r, n1, stride=4), :]), split4_store)

    def split16_store(seg, srcf, dstb, x):
        dstb[seg * n2:(seg + 1) * n2, :] = x

    grouped([(seg, srcf, dstb) for seg in range(16) for srcf, dstb in ((qf1, qb2), (kf1, kb2), (vf1, vb2))],
            each(lambda seg, srcf, dstb: srcf[pl.ds((seg // 4) * n1 + seg % 4, n2, stride=4), :].astype(BF16)),
            split16_store)

    def attend(qb, kb, vb, items, finish):
        def loaders(base, q0, w0, kw):
            return (lambda: (qb[base + q0:base + q0 + ATT_QBLK, :], kb[base + w0:base + w0 + kw, :],
                             bias(w0 - q0, kw)),
                    lambda: jnp.concatenate([vb[base + w0:base + w0 + kw, :], jnp.ones((kw, c), BF16)], axis=1))

        _band_attend([loaders(*it) for it in items], lambda i, olm: finish(*items[i], olm))

    def merged(olm, prev_refs, base, q0):
        return _softmax_merge(olm, tuple(ref[base + q0:base + q0 + ATT_QBLK, :] for ref in prev_refs))

    def finish16(base, q0, w0, kw, olm):
        seg = base // n2
        dst = pl.ds((seg // 4) * n1 + 4 * q0 + seg % 4, ATT_QBLK, stride=4)
        for ref, val in zip((ao, al, am), olm):
            ref[dst, :] = val

    attend(qb2, kb2, vb2, [(seg * n2, *blk) for seg in range(16) for blk in _band_blocks(n2)], finish16)

    def finish4(base, q0, w0, kw, olm):
        dst = pl.ds(4 * q0 + base // n1, ATT_QBLK, stride=4)
        for ref, val in zip((bo, bl, bm), merged(olm, (ao, al, am), base, q0)):
            ref[dst, :] = val

    attend(qb1, kb1, vb1, [(r4 * n1, *blk) for r4 in range(4) for blk in _band_blocks(n1)], finish4)

    def finish1(base, q0, w0, kw, olm):
        o, l, _ = merged(olm, (bo, bl, bm), base, q0)
        o_ref[q0:q0 + ATT_QBLK, :] = (o / l).astype(o_ref.dtype)

    attend(q_ref, k_ref, v_ref, [(0, *blk) for blk in _band_blocks(s)], finish1)


def _dilated_attention(q, k, v):
    b, nh, s, _ = q.shape
    aw = nh * HEAD_DIM
    assert tuple(w // (2 * d) for w, d in DILATED_PATTERNS) == (ATT_RADIUS,) * 3
    assert tuple(d for _, d in DILATED_PATTERNS) == (1, 4, 16)
    assert s % (16 * ATT_QBLK) == 0
    c = HEAD_DIM
    spec = pl.BlockSpec((None, None, s, c), lambda i, h: (i, h, 0, 0))
    out_spec = pl.BlockSpec((None, s, c), lambda i, h: (i, 0, h))
    f32buf = pltpu.VMEM((s, c), F32)
    bf16buf = pltpu.VMEM((s, c), BF16)
    scratch = [f32buf] * 6 + [bf16buf] * 6 + [f32buf] * 6
    est = 12 * _nbytes((s, c), F32) + 10 * _nbytes((s, c), BF16) + 8 * _nbytes((s, c), BF16) + (16 << 20)
    return pl.pallas_call(
        _attn_kernel,
        grid=(b, aw // c),
        in_specs=[spec, spec, spec],
        out_specs=out_spec,
        out_shape=jax.ShapeDtypeStruct((b, s, aw), BF16),
        scratch_shapes=scratch,
        compiler_params=_cparams(("parallel", "arbitrary"), est),
        name="dilated_attention",
    )(q, k, v)


def _filter_kernel(z_ref, w1_ref, b1_ref, w2_ref, b2_ref, fr_ref, w3_ref, t_ref, dl_ref,
                   filt_ref, asum_ref, hi_ref, lo_ref, *, blocks_per_dir):
    def split(a):
        hi = a.astype(BF16)
        return hi, (a - hi.astype(F32)).astype(BF16)

    @pl.when(pl.program_id(0) == 0)
    def _():
        fr = fr_ref[...]
        hdn = jnp.sin(fr * (jnp.dot(z_ref[...], w1_ref[...], precision=HIGHEST,
                                    preferred_element_type=F32) + b1_ref[...]))
        hdn = jnp.sin(fr * (jnp.dot(hdn, w2_ref[...], precision=HIGHEST,
                                    preferred_element_type=F32) + b2_ref[...]))
        hi_ref[...], lo_ref[...] = split(hdn)

    w_hi, w_lo = split(w3_ref[...])
    hf = (jnp.dot(hi_ref[...], w_hi, preferred_element_type=F32)
          + jnp.dot(hi_ref[...], w_lo, preferred_element_type=F32)
          + jnp.dot(lo_ref[...], w_hi, preferred_element_type=F32))
    hf = hf * jnp.exp(-t_ref[...] * dl_ref[...])
    is_bwd = (pl.program_id(0) // blocks_per_dir) % 2 == 1
    row = lax.broadcasted_iota(jnp.int32, hf.shape, 0)
    hf = jnp.where(jnp.logical_and(row == 0, is_bwd), 0.0, hf)
    filt_ref[...] = hf.astype(filt_ref.dtype)
    asum_ref[...] = jnp.sum(jnp.abs(hf), axis=0, keepdims=True)


def _mod4_rows(a):
    return jnp.concatenate([a[c::4] for c in range(4)], axis=0)


def _hyena_filter_taps(seq_len, w1, b1, w2, b2, sin_freq, w3, hy_w, tn=512):
    ffn = w1.shape[1]
    pad = LANES
    t = jnp.linspace(0.0, 1.0, seq_len, dtype=F32)[:, None]
    wpos = 2.0 * math.pi * jnp.arange(seq_len, dtype=F32) / seq_len
    bands = jnp.linspace(1e-4, HY_BANDS - 1, HY_BANDS, dtype=F32)
    ang = wpos[:, None] * bands[None, :]
    z = jnp.concatenate([t, jnp.cos(ang), -jnp.sin(ang)], axis=-1)
    z = _mod4_rows(jnp.pad(z, ((0, 0), (0, pad - z.shape[1]))))
    t = _mod4_rows(t)
    w1p = jnp.pad(w1.astype(F32), ((0, pad - w1.shape[0]), (0, pad - ffn)))
    w2p = jnp.pad(w2.astype(F32), ((0, pad - ffn), (0, pad - ffn)))
    w3p = jnp.pad(w3.astype(F32), ((0, pad - ffn), (0, 0)))
    padv = lambda a: jnp.pad(a.astype(F32), (0, pad - ffn)).reshape(1, pad)
    deltas = jnp.abs(jnp.linspace(HY_MIN_DECAY, HY_MAX_DECAY, hy_w, dtype=F32)).reshape(1, hy_w)
    ncol = w3.shape[1]
    tn = min(tn, hy_w)
    bpd = hy_w // tn
    full = lambda shape: pl.BlockSpec(shape, lambda j: (0, 0))
    return pl.pallas_call(
        functools.partial(_filter_kernel, blocks_per_dir=bpd),
        grid=(ncol // tn,),
        in_specs=[full((seq_len, pad)), full((pad, pad)), full((1, pad)), full((pad, pad)), full((1, pad)),
                  full((1, pad)), pl.BlockSpec((pad, tn), lambda j: (0, j)), full((seq_len, 1)),
                  pl.BlockSpec((1, tn), lambda j: (0, j % bpd))],
        out_specs=[pl.BlockSpec((seq_len, tn), lambda j: (0, j)), pl.BlockSpec((1, tn), lambda j: (0, j))],
        out_shape=[jax.ShapeDtypeStruct((seq_len, ncol), BF16), jax.ShapeDtypeStruct((1, ncol), F32)],
        scratch_shapes=[pltpu.VMEM((seq_len, pad), BF16)] * 2,
        compiler_params=_cparams(("arbitrary",), 32 << 20),
        name="hyena_filter_taps",
    )(z, w1p, padv(b1), w2p, padv(b2), padv(sin_freq), w3p, t, deltas)


def _dft_matrices(s, n_total):
    n = 2 * s
    k = jnp.arange(s, dtype=jnp.int32)

    def tables(cols):
        ang = ((k[:, None] * cols[None, :]) % n).astype(F32) * (2.0 * math.pi / n)
        return jnp.cos(ang), jnp.sin(ang)

    cl, sl = tables(jnp.arange(LANES, dtype=jnp.int32))
    ch, sh = tables(jnp.arange(s // LANES, dtype=jnp.int32) * LANES)
    cosm = (ch[:, :, None] * cl[:, None, :] - sh[:, :, None] * sl[:, None, :]).reshape(s, s)
    sinm = (sh[:, :, None] * cl[:, None, :] + ch[:, :, None] * sl[:, None, :]).reshape(s, s)
    alt = jnp.where(k % 2 == 0, 1.0, -1.0).astype(F32)
    im = (-sinm).at[0].set(alt)
    scale_re = jnp.full((s,), 2.0 / n_total, F32).at[0].set(1.0 / n_total)[:, None]
    r = min(HY_FREQ_CHUNK, s)
    chunked = lambda a: a.reshape(2, s // r, r, -1).transpose(1, 0, 2, 3).reshape(n, -1)
    fwds, invs = [], []
    for c in range(4):
        theta = ((c * k) % n_total).astype(F32) * (2.0 * math.pi / n_total)
        cs, sn = jnp.cos(theta)[:, None], jnp.sin(theta)[:, None]
        fwds.append(chunked(jnp.concatenate([cs * cosm + sn * im, cs * im - sn * cosm], axis=0)))
        invs.append(chunked(jnp.concatenate([(cs * cosm + sn * im) * scale_re, (cs * im - sn * cosm) * scale_re],
                                            axis=0)).T)
    return jnp.stack(fwds).astype(BF16), jnp.stack(invs).astype(BF16)


HY_FREQ_CHUNK = 256


def _freq_chunks(m):
    r = min(HY_FREQ_CHUNK, m)
    return [slice(r0, r0 + r) for r0 in range(0, m, r)]


def _first_row(shape):
    return lax.broadcasted_iota(jnp.int32, shape, 0) == 0


COS_PI_4 = math.sqrt(0.5)


def _quarter_spectrum(z_ref, f_ref, rows):
    q = z_ref.shape[0] // 4
    r = rows.stop - rows.start
    frows = slice(2 * rows.start, 2 * rows.stop)
    sub = []
    for c in range(4):
        f = jnp.dot(f_ref[c, frows, :], z_ref[c * q:(c + 1) * q, :], preferred_element_type=F32)
        sub.append((f[:r], f[r:]))
    (t0r, t0i), (t1r, t1i), (t2r, t2i), (t3r, t3i) = sub
    ur, ui, vr, vi = t0r + t2r, t0i + t2i, t0r - t2r, t0i - t2i
    wr, wi, xr, xi = t1r + t3r, t1i + t3i, t1r - t3r, t1i - t3i
    slots = [(ur + wr, ui + wi), (vr - xi, vi + xr), (vr + xi, vi - xr), (ur - wr, ui - wi)]
    if rows.start == 0:
        s0, s1, s2, s3 = (sub[c][0][0:1, :] for c in range(4))
        n0, n1, n2, n3 = (sub[c][1][0:1, :] for c in range(4))
        k4 = COS_PI_4
        row0 = [(s0 + s1 + s2 + s3, s0 - s1 + s2 - s3),
                (s0 - s2, s3 - s1),
                (n0 + k4 * (n1 - n3), -n2 - k4 * (n1 + n3)),
                (n0 - k4 * (n1 - n3), n2 - k4 * (n1 + n3))]
        first = _first_row(ur.shape)
        slots = [(jnp.where(first, p0[0], p[0]), jnp.where(first, p0[1], p[1])) for p, p0 in zip(slots, row0)]
    return slots


def _spectrum_product(h, z, with_first):
    out = []
    for i, ((hr, hi), (zr, zi)) in enumerate(zip(h, z)):
        rr, ii = hr * zr, hi * zi
        yr, yi = rr - ii, hr * zi + hi * zr
        if i == 0 and with_first:
            first = _first_row(rr.shape)
            yr, yi = jnp.where(first, rr, yr), jnp.where(first, ii, yi)
        out.append((yr, yi))
    return out


def _inverse_butterfly(y, rows, dtype):
    (ar, ai), (br, bi), (cr, ci), (dr, di) = y
    g = [(ar + br + cr + dr, ai + bi + ci + di),
         (ar + bi - ci - dr, ai - br + cr - di),
         (ar - br - cr + dr, ai - bi - ci + di),
         (ar - bi + ci - dr, ai + br - cr - di)]
    if rows.start == 0:
        y0, yn, qr, qi = ar[0:1, :], ai[0:1, :], br[0:1, :], bi[0:1, :]
        a, b, a2, b2 = cr[0:1, :], ci[0:1, :], dr[0:1, :], di[0:1, :]
        k4 = 2.0 * COS_PI_4
        dc = [y0 + yn + 2.0 * qr, y0 - yn - 2.0 * qi, y0 + yn - 2.0 * qr, y0 - yn + 2.0 * qi]
        half = [2.0 * (a + a2), k4 * (a - b) - k4 * (a2 + b2), 2.0 * (b2 - b), k4 * (a2 - b2) - k4 * (a + b)]
        first = _first_row(ar.shape)
        g = [(jnp.where(first, d0, gr), jnp.where(first, h0, gi)) for (gr, gi), d0, h0 in zip(g, dc, half)]
    return [jnp.concatenate([gr.astype(dtype), gi.astype(dtype)], axis=0) for gr, gi in g]


def _filter_spectrum_kernel(tf_ref, tb_ref, nf_ref, nb_ref, f_ref, o_ref):
    inv = 1.0 / (nf_ref[...] + nb_ref[...])
    for rows in _freq_chunks(o_ref.shape[1]):
        fwd = _quarter_spectrum(tf_ref, f_ref, rows)
        bwd = _quarter_spectrum(tb_ref, f_ref, rows)
        for i, ((fr, fi), (gr, gi)) in enumerate(zip(fwd, bwd)):
            im = fi - gi
            if i == 0 and rows.start == 0:
                im = jnp.where(_first_row(im.shape), fi + gi, im)
            o_ref[2 * i, rows, :] = (fr + gr) * inv
            o_ref[2 * i + 1, rows, :] = im * inv


def _filter_spectrum(taps, asum, fwd, hy_w, tc):
    l = taps.shape[0]
    q = fwd.shape[2]
    nb = hy_w // tc
    col = lambda d: (lambda o, c: (0, (2 * o + d) * nb + c))
    const = lambda shape: pl.BlockSpec(shape, lambda o, c: (0,) * len(shape), pipeline_mode=pl.Buffered(1))
    est = (_nbytes((4, 2 * q, q), BF16) + 4 * _nbytes((l, tc), BF16)
           + 2 * _nbytes((8, q, tc), F32) + 48 * _nbytes((HY_FREQ_CHUNK, tc), F32) + (8 << 20))
    return pl.pallas_call(
        _filter_spectrum_kernel,
        grid=(HY_ORDER, nb),
        in_specs=[pl.BlockSpec((l, tc), col(0)), pl.BlockSpec((l, tc), col(1)),
                  pl.BlockSpec((1, tc), col(0)), pl.BlockSpec((1, tc), col(1)),
                  const((4, 2 * q, q))],
        out_specs=pl.BlockSpec((None, 8, q, tc), lambda o, c: (o, 0, 0, c)),
        out_shape=jax.ShapeDtypeStruct((HY_ORDER, 8, q, hy_w), F32),
        compiler_params=_cparams(("parallel", "arbitrary"), est),
        name="hyena_filter_spectrum",
    )(taps, taps, asum, asum, fwd)


HY_SEQS_PER_STEP = 2


def _hyena_conv_kernel(z_ref, g_ref, h_ref, b_ref, f_ref, inv_ref, o_ref, spec_ref, *scratch, natural_out):
    nseq, s, tc = z_ref.shape
    q = s // 4
    for b in range(nseq):
        for rows in _freq_chunks(q):
            spec = _quarter_spectrum(z_ref.at[b], f_ref, rows)
            filt = [(h_ref[2 * i, rows, :], h_ref[2 * i + 1, rows, :]) for i in range(4)]
            prod = _spectrum_product(filt, spec, rows.start == 0)
            for c, packed in enumerate(_inverse_butterfly(prod, rows, spec_ref.dtype)):
                spec_ref[b, c, 2 * rows.start:2 * rows.stop, :] = packed
    bias = b_ref[...]
    for b in range(nseq):
        for rows in _freq_chunks(q):
            r = rows.stop - rows.start
            outs = []
            for c in range(4):
                cls = slice(c * q + rows.start, c * q + rows.stop)
                conv = jnp.dot(inv_ref[c, rows, :], spec_ref[b, c], preferred_element_type=F32)
                outs.append((cls, g_ref[b, cls, :].astype(F32) * (conv + bias * z_ref[b, cls, :].astype(F32))))
            if natural_out:
                (nat,) = scratch
                span = slice(4 * rows.start, 4 * rows.stop)
                for lane in range(tc // LANES):
                    cols = slice(lane * LANES, (lane + 1) * LANES)
                    for c, (_, val) in enumerate(outs):
                        nat[b, lane, pl.ds(4 * rows.start + c, r, stride=4), :] = val[:, cols]
                    o_ref[b, span, cols] = nat[b, lane, span, :].astype(o_ref.dtype)
            else:
                for cls, val in outs:
                    o_ref[b, cls, :] = val.astype(o_ref.dtype)


def _hyena_conv(z, z_col0, gate, gate_col0, filt, order, bias_row, mats, width, natural_out):
    fwd, inv, tc = mats
    b, s, _ = z.shape
    q = s // 4
    nc = width // tc
    nseq = HY_SEQS_PER_STEP if b % HY_SEQS_PER_STEP == 0 else 1
    zb0, gb0 = z_col0 // tc, gate_col0 // tc
    const = lambda shape: pl.BlockSpec(shape, lambda c, i: (0,) * len(shape), pipeline_mode=pl.Buffered(1))
    scratch = [pltpu.VMEM((nseq, 4, 2 * q, tc), BF16)]
    if natural_out:
        scratch.append(pltpu.VMEM((nseq, tc // LANES, s, LANES), F32))
    est = (2 * _nbytes((4, 2 * q, q), BF16) + 2 * _nbytes((8, q, tc), F32)
           + nseq * (6 * _nbytes((s, tc), BF16) + 4 * _nbytes((2 * q, tc), BF16) + _nbytes((s, tc), F32))
           + 64 * _nbytes((HY_FREQ_CHUNK, tc), F32) + (8 << 20))
    return pl.pallas_call(
        functools.partial(_hyena_conv_kernel, natural_out=natural_out),
        grid=(nc, b // nseq),
        in_specs=[pl.BlockSpec((nseq, s, tc), lambda c, i: (i, 0, zb0 + c)),
                  pl.BlockSpec((nseq, s, tc), lambda c, i: (i, 0, gb0 + c)),
                  pl.BlockSpec((None, 8, q, tc), lambda c, i: (order, 0, 0, c)),
                  pl.BlockSpec((1, tc), lambda c, i: (0, c)),
                  const((4, 2 * q, q)), const((4, q, 2 * q))],
        out_specs=pl.BlockSpec((nseq, s, tc), lambda c, i: (i, 0, c)),
        out_shape=jax.ShapeDtypeStruct((b, s, width), BF16),
        scratch_shapes=scratch,
        compiler_params=_cparams(("parallel", "arbitrary"), est),
        name="hyena_conv",
    )(z, gate, filt, bias_row, fwd, inv)


def _chunk_rows(n_rows):
    return [slice(r0, r0 + INPROJ_ROW_CHUNK) for r0 in range(0, n_rows, INPROJ_ROW_CHUNK)]


def _merge_kernel(a_ref, hy_ref, wa_ref, wh_ref, g0_ref, g1_ref, o_ref):
    for rows in _chunk_rows(a_ref.shape[0]):
        ya = jnp.dot(a_ref[rows, :], wa_ref[...], preferred_element_type=F32)
        yh = jnp.dot(hy_ref[rows, :], wh_ref[...], preferred_element_type=F32)
        o_ref[rows, :] = (g0_ref[rows, :].astype(F32) * ya + g1_ref[rows, :].astype(F32) * yh).astype(o_ref.dtype)


def _weight_spec(k, tn, n):
    if tn == n:
        return pl.BlockSpec((k, n), lambda i, j: (0, 0), pipeline_mode=pl.Buffered(1))
    return pl.BlockSpec((k, tn), lambda i, j: (0, j))


def _gated_merge(attn, hy, w_att, w_hy, gates, tm=512, tn=2048):
    t, k = attn.shape
    n = w_att.shape[1]
    tm, tn = min(tm, t), min(tn, n)
    g1b0 = n // tn
    est = 4 * _nbytes((tm, k), BF16) + 4 * _nbytes((k, tn), BF16) + 6 * _nbytes((tm, tn), F32) + (8 << 20)
    return pl.pallas_call(
        _merge_kernel,
        grid=(t // tm, n // tn),
        in_specs=[pl.BlockSpec((tm, k), lambda i, j: (i, 0)),
                  pl.BlockSpec((tm, k), lambda i, j: (i, 0)),
                  _weight_spec(k, tn, n),
                  _weight_spec(k, tn, n),
                  pl.BlockSpec((tm, tn), lambda i, j: (i, j)),
                  pl.BlockSpec((tm, tn), lambda i, j: (i, g1b0 + j))],
        out_specs=pl.BlockSpec((tm, tn), lambda i, j: (i, j)),
        out_shape=jax.ShapeDtypeStruct((t, n), BF16),
        compiler_params=_cparams(("parallel", "arbitrary"), est),
        name="gated_merge",
    )(attn, hy, w_att, w_hy, gates, gates)


def _mm_residual_kernel(a_ref, w_ref, ra_ref, rb_ref, o_ref, *, na):
    def body(r_ref):
        for rows in _chunk_rows(a_ref.shape[0]):
            o_ref[rows, :] = r_ref[rows, :] + jnp.dot(a_ref[rows, :], w_ref[...], preferred_element_type=F32)

    _pick_group(na, ra_ref, rb_ref, body)


def _matmul_residual(a, w, res_a, res_b, tm=512, tn=2048):
    t, k = a.shape
    n = w.shape[1]
    tm, tn = min(tm, res_a.shape[0]), min(tn, n)
    assert res_a.shape[0] % tm == 0 and res_b.shape[0] % tm == 0
    na = res_a.shape[0] // tm
    est = 2 * _nbytes((tm, k), BF16) + 2 * _nbytes((k, tn), BF16) + 8 * _nbytes((tm, tn), F32) + (8 << 20)
    return pl.pallas_call(
        functools.partial(_mm_residual_kernel, na=na),
        grid=(t // tm, n // tn),
        in_specs=[pl.BlockSpec((tm, k), lambda i, j: (i, 0)),
                  _weight_spec(k, tn, n),
                  *_two_group_specs((tm, tn), na, col=lambda j: j)],
        out_specs=pl.BlockSpec((tm, tn), lambda i, j: (i, j)),
        out_shape=jax.ShapeDtypeStruct((t, n), F32),
        compiler_params=_cparams(("parallel", "arbitrary"), est),
        name="matmul_residual",
    )(a, w, res_a, res_b)


def _rmsnorm_rows(x, g):
    ms = jnp.mean(x * x, axis=-1, keepdims=True)
    return x * lax.rsqrt(ms + EPS) * g


def _mlp_kernel(x_ref, g_ref, wu_ref, wd_ref, o_ref, hm_ref):
    def body(first):
        for rows in _chunk_rows(hm_ref.shape[0]):
            if first:
                x = x_ref[rows, :]
                hm = _rmsnorm_rows(x, g_ref[...]).astype(hm_ref.dtype)
                hm_ref[rows, :] = hm
            else:
                hm = hm_ref[rows, :]
            a = jnp.dot(hm, wu_ref[...], preferred_element_type=F32)
            a = jnp.square(jnp.maximum(a, 0.0)).astype(BF16)
            y = jnp.dot(a, wd_ref[...], preferred_element_type=F32)
            o_ref[rows, :] = (x if first else o_ref[rows, :]) + y

    j = pl.program_id(1)
    pl.when(j == 0)(lambda: body(True))
    pl.when(j > 0)(lambda: body(False))


def _relu2_mlp(x, g, w_up, w_down, tm=1024, tf=1024):
    t, d = x.shape
    ff = w_up.shape[1]
    tm, tf = min(tm, t), min(tf, ff)
    est = (4 * _nbytes((tm, d), F32) + _nbytes((tm, d), BF16) + 4 * _nbytes((d, tf), BF16)
           + 3 * _nbytes((tm, tf), F32) + (8 << 20))
    return pl.pallas_call(
        _mlp_kernel,
        grid=(t // tm, ff // tf),
        in_specs=[pl.BlockSpec((tm, d), lambda i, j: (i, 0)),
                  pl.BlockSpec((1, d), lambda i, j: (0, 0)),
                  pl.BlockSpec((d, tf), lambda i, j: (0, j)),
                  pl.BlockSpec((tf, d), lambda i, j: (j, 0))],
        out_specs=pl.BlockSpec((tm, d), lambda i, j: (i, 0)),
        out_shape=jax.ShapeDtypeStruct((t, d), F32),
        scratch_shapes=[pltpu.VMEM((tm, d), BF16)],
        compiler_params=_cparams(("parallel", "arbitrary"), est),
        name="relu2_mlp",
    )(x, g.reshape(1, d), w_up, w_down)


def _ple_kernel(x_ref, g_ref, wg_ref, p_ref, wp_ref, o_ref):
    for rows in _chunk_rows(x_ref.shape[0]):
        x = x_ref[rows, :]
        hn = _rmsnorm_rows(x, g_ref[...]).astype(BF16)
        gate = jax.nn.sigmoid(jnp.dot(hn, wg_ref[...], preferred_element_type=F32))
        proj = jnp.dot(p_ref[rows, :].astype(BF16), wp_ref[...], preferred_element_type=F32)
        o_ref[rows, :] = x + gate * proj


def _gated_ple(x, row0, p, g, w_gate, w_proj, tm=512):
    t, pd = p.shape
    d = x.shape[1]
    tm = min(tm, t)
    rb0 = row0 // tm
    est = (4 * _nbytes((tm, d), F32) + _nbytes((d, d), BF16) + _nbytes((pd, d), BF16)
           + 2 * _nbytes((tm, pd), F32) + 8 * _nbytes((INPROJ_ROW_CHUNK, d), F32) + (8 << 20))
    return pl.pallas_call(
        _ple_kernel,
        grid=(t // tm, 1),
        in_specs=[pl.BlockSpec((tm, d), lambda i, j: (rb0 + i, 0)),
                  pl.BlockSpec((1, d), lambda i, j: (0, 0)),
                  _weight_spec(d, d, d),
                  pl.BlockSpec((tm, pd), lambda i, j: (i, 0)),
                  _weight_spec(pd, d, d)],
        out_specs=pl.BlockSpec((tm, d), lambda i, j: (i, 0)),
        out_shape=jax.ShapeDtypeStruct((t, d), F32),
        compiler_params=_cparams(("parallel", "arbitrary"), est),
        name="gated_ple",
    )(x, g.reshape(1, d), w_gate, p, w_proj)


def _encoder_layer(x_list, p_list, norm_mix_g, w_in, gate_b, q_norm_g, k_norm_g, hy_conv_w, hy_conv_b,
                   hy_w1, hy_b1, hy_w2, hy_b2, hy_sin_freq, hy_w3, hy_bias,
                   w_att_out, w_hy_out, w_o, norm_mlp_g, w_up, w_down,
                   norm_ple_g, w_ple_gate, w_ple_proj):
    xa3, xb3 = x_list
    s, d = xa3.shape[1:]
    b = xa3.shape[0] + xb3.shape[0]
    att_w = w_att_out.shape[0]
    hy_w = w_hy_out.shape[0]
    xa, xb = xa3.reshape(-1, d), xb3.reshape(-1, d)
    bf = lambda a: a.astype(BF16)

    h3 = _rmsnorm_cast(xa, xb, norm_mix_g).reshape(b, s, d)
    w_in_b = bf(w_in)
    rope_c, rope_s1, rope_s2 = _rope_tables(s)
    tab_spec = pl.BlockSpec((s, HEAD_DIM), lambda i, j: (0, 0))
    vec_spec = lambda tn, blk0=0: pl.BlockSpec((1, tn), lambda i, j: (0, blk0 + j))
    head_spec = pl.BlockSpec((1, HEAD_DIM), lambda i, j: (0, 0))
    tn = 1024
    qk = lambda g, col0, scale: _inproj(
        h3, w_in_b, col0, att_w, functools.partial(_inproj_qk_kernel, scale=scale),
        (g.reshape(1, HEAD_DIM), rope_c, rope_s1, rope_s2), [head_spec, tab_spec, tab_spec, tab_spec], tn,
        head_major=True)
    q = qk(q_norm_g, 0, HEAD_DIM ** -0.5 * math.log2(math.e))
    k = qk(k_norm_g, att_w, 1.0)
    v = _inproj(h3, w_in_b, 2 * att_w, att_w, _inproj_heads_kernel, (), [], tn, head_major=True)
    tnc = min(tn, hy_w)
    u = _inproj(h3, w_in_b, 3 * att_w, 3 * hy_w, _inproj_conv_kernel,
                (hy_conv_w, hy_conv_b.reshape(1, 3 * hy_w)),
                [pl.BlockSpec((3, tnc), lambda i, j: (0, j)), vec_spec(tnc)], tnc,
                scratch=[pltpu.VMEM((tnc // LANES, s + 2 * CONV_PAD_ROWS, LANES), F32)])
    tng = min(tn, d)
    gates = _inproj(h3, w_in_b, 3 * att_w + 3 * hy_w, N_BRANCH * d, _inproj_gate_kernel,
                    (gate_b.reshape(1, N_BRANCH * d),), [vec_spec(tng)], tng)

    attn = _dilated_attention(q, k, v)

    hy_tc = min(256, hy_w)
    mats = (*_dft_matrices(s // 4, 2 * s), hy_tc)
    taps, tap_asum = _hyena_filter_taps(s, hy_w1, hy_b1, hy_w2, hy_b2, hy_sin_freq, hy_w3, hy_w)
    filt = _filter_spectrum(taps, tap_asum, mats[0], hy_w, hy_tc)
    bias = hy_bias.astype(F32)
    z = _hyena_conv(u, 0, u, hy_w, filt, 0, bias[0:1], mats, hy_w, natural_out=False)
    hy = _hyena_conv(z, 0, u, 2 * hy_w, filt, 1, bias[1:2], mats, hy_w, natural_out=True)

    merged = _gated_merge(attn.reshape(b * s, att_w), hy.reshape(b * s, hy_w),
                          bf(w_att_out), bf(w_hy_out), gates.reshape(b * s, N_BRANCH * d))
    x2 = _matmul_residual(merged, bf(w_o), xa, xb)
    x2 = _relu2_mlp(x2, norm_mlp_g, bf(w_up), bf(w_down))

    outs, row0 = [], 0
    w_pg, w_pp = bf(w_ple_gate), bf(w_ple_proj)
    for p in p_list:
        pb = p.shape[0]
        p2 = p.reshape(pb * s, p.shape[-1])
        outs.append(_gated_ple(x2, row0, p2, norm_ple_g, w_pg, w_pp).reshape(pb, s, d))
        row0 += pb * s
    return outs


def kernel(x_prompt, x_sample, p_prompt, p_sample, norm_mix_g, w_in, gate_b, q_norm_g, k_norm_g, hy_conv_w, hy_conv_b, hy_w1, hy_b1, hy_w2, hy_b2, hy_sin_freq, hy_w3, hy_bias, w_att_out, w_hy_out, w_o, norm_mlp_g, w_up, w_down, norm_ple_g, w_ple_gate, w_ple_proj):
    assert x_prompt.shape[1:] == x_sample.shape[1:]
    y_prompt, y_sample = x_prompt, x_sample
    for i in range(w_in.shape[0]):
        y_prompt, y_sample = _encoder_layer(
            (y_prompt, y_sample), (p_prompt[i], p_sample[i]), norm_mix_g[i], w_in[i], gate_b[i], q_norm_g[i], k_norm_g[i],
            hy_conv_w[i], hy_conv_b[i], hy_w1[i], hy_b1[i], hy_w2[i], hy_b2[i], hy_sin_freq[i], hy_w3[i],
            hy_bias[i], w_att_out[i], w_hy_out[i], w_o[i], norm_mlp_g[i], w_up[i], w_down[i],
            norm_ple_g[i], w_ple_gate[i], w_ple_proj[i])
    return (y_prompt, y_sample)
```

```python
import functools
import math

import jax
import jax.numpy as jnp
from jax import lax
from jax.experimental import pallas as pl
from jax.experimental.pallas import tpu as pltpu

F32 = jnp.float32
BF16 = jnp.bfloat16
HIGHEST = lax.Precision.HIGHEST

EPS = 1e-6
HEAD_DIM = 128
ROPE_DIM = HEAD_DIM // 4
ROPE_THETA = 500000.0
DILATED_PATTERNS = ((128, 1), (512, 4), (2048, 16))
ATT_RADIUS = 64
ATT_QBLK = 128
ATT_KWIN = ATT_QBLK + 2 * ATT_RADIUS
ATT_GROUP = 8
ATT_DEPTH = 3
HY_ORDER = 2
HY_EMB_DIM = 33
HY_BANDS = (HY_EMB_DIM - 1) // 2
HY_FAST_DECAY = 0.3
HY_SLOW_DECAY = 1.5
HY_DECAY_TARGET = 1e-2
HY_MIN_DECAY = math.log(HY_DECAY_TARGET) / HY_SLOW_DECAY
HY_MAX_DECAY = math.log(HY_DECAY_TARGET) / HY_FAST_DECAY
N_BRANCH = 2
LANES = 128
MASK_NEG = -1e30

VMEM_CAP_BYTES = 60 * 1024 * 1024


def _cparams(sem, est_bytes):
    limit = int(min(VMEM_CAP_BYTES, max(32 * 1024 * 1024, est_bytes)))
    return pltpu.CompilerParams(dimension_semantics=sem, vmem_limit_bytes=limit)


def _nbytes(shape, dtype):
    return math.prod(shape) * jnp.dtype(dtype).itemsize


def _two_group_specs(block, na, col=lambda *j: 0):
    return (pl.BlockSpec(block, lambda i, *j: (jnp.minimum(i, na - 1), jnp.where(i < na, col(*j), 0))),
            pl.BlockSpec(block, lambda i, *j: (jnp.maximum(i - na, 0), jnp.where(i >= na, col(*j), 0))))


def _pick_group(na, xa_ref, xb_ref, body):
    i = pl.program_id(0)
    pl.when(i < na)(lambda: body(xa_ref))
    pl.when(i >= na)(lambda: body(xb_ref))


def _rmsnorm_kernel(xa_ref, xb_ref, g_ref, o_ref, *, na):
    def body(x_ref):
        x = x_ref[...]
        ms = jnp.mean(x * x, axis=-1, keepdims=True)
        o_ref[...] = (x * lax.rsqrt(ms + EPS) * g_ref[...]).astype(o_ref.dtype)

    _pick_group(na, xa_ref, xb_ref, body)


def _rmsnorm_cast(xa, xb, g, tm=512):
    d = xa.shape[1]
    assert xa.shape[0] % tm == 0 and xb.shape[0] % tm == 0
    na, t = xa.shape[0] // tm, xa.shape[0] + xb.shape[0]
    return pl.pallas_call(
        functools.partial(_rmsnorm_kernel, na=na),
        grid=(t // tm,),
        in_specs=[*_two_group_specs((tm, d), na), pl.BlockSpec((1, d), lambda i: (0, 0))],
        out_specs=pl.BlockSpec((tm, d), lambda i: (i, 0)),
        out_shape=jax.ShapeDtypeStruct((t, d), BF16),
        compiler_params=_cparams(("parallel",), 6 * _nbytes((tm, d), F32)),
        name="rmsnorm_cast",
    )(xa, xb, g.reshape(1, d))


INPROJ_ROW_CHUNK = 256


def _row_chunks(h_ref, w_ref, chunk=INPROJ_ROW_CHUNK):
    for r0 in range(0, h_ref.shape[0], chunk):
        rows = slice(r0, r0 + chunk)
        yield rows, jnp.dot(h_ref[rows, :], w_ref[...], preferred_element_type=F32)


def _inproj_qk_kernel(h_ref, w_ref, g_ref, c_ref, s1_ref, s2_ref, o_ref, *, scale):
    gs = g_ref[...] * scale
    pw = 2 * HEAD_DIM
    same_head = (lax.broadcasted_iota(jnp.int32, (pw, pw), 0) // HEAD_DIM
                 == lax.broadcasted_iota(jnp.int32, (pw, pw), 1) // HEAD_DIM)
    ones_bd = jnp.where(same_head, 1.0, 0.0).astype(BF16)
    for rows, y in _row_chunks(h_ref, w_ref):
        yy = (y * y).astype(BF16)
        c, s1, s2 = c_ref[rows, :], s1_ref[rows, :], s2_ref[rows, :]
        for hh in range(y.shape[1] // HEAD_DIM):
            sl = slice(hh * HEAD_DIM, (hh + 1) * HEAD_DIM)
            if hh % 2 == 0:
                ss_pair = jnp.dot(yy[:, hh * HEAD_DIM:hh * HEAD_DIM + pw], ones_bd, preferred_element_type=F32)
            ms = ss_pair[:, (hh % 2) * HEAD_DIM:(hh % 2 + 1) * HEAD_DIM] * (1.0 / HEAD_DIM)
            t = y[:, sl] * lax.rsqrt(ms + EPS) * gs
            out = (t * c + pltpu.roll(t, HEAD_DIM - ROPE_DIM // 2, 1) * s1 + pltpu.roll(t, ROPE_DIM // 2, 1) * s2)
            o_ref[hh, rows, :] = out.astype(o_ref.dtype)


def _inproj_heads_kernel(h_ref, w_ref, o_ref):
    for rows, y in _row_chunks(h_ref, w_ref):
        for hh in range(y.shape[1] // HEAD_DIM):
            o_ref[hh, rows, :] = y[:, hh * HEAD_DIM:(hh + 1) * HEAD_DIM].astype(o_ref.dtype)


CONV_PAD_ROWS = 16
CONV_ROW_CHUNK = 1024


def _inproj_conv_kernel(h_ref, w_ref, cw_ref, cb_ref, o_ref, scr):
    s, tn = h_ref.shape[0], w_ref.shape[1]
    q = s // 4
    chunk = CONV_ROW_CHUNK
    part = chunk // 4
    pad = CONV_PAD_ROWS
    zeros = jnp.zeros((pad, LANES), F32)
    for c in range(tn // LANES):
        scr[c, 0:pad, :] = zeros
        scr[c, pad + s:2 * pad + s, :] = zeros

    def finish(r0):
        for c in range(tn // LANES):
            cols = slice(c * LANES, (c + 1) * LANES)
            w0, w1, w2, cb = cw_ref[0:1, cols], cw_ref[1:2, cols], cw_ref[2:3, cols], cb_ref[:, cols]
            y = [scr[c, pl.ds(pad + r0 + off, part, stride=4), :] for off in range(-1, 5)]
            for cls in range(4):
                rows = slice(cls * q + r0 // 4, cls * q + r0 // 4 + part)
                o_ref[rows, cols] = (y[cls] * w0 + y[cls + 1] * w1 + y[cls + 2] * w2 + cb).astype(o_ref.dtype)

    for rows, y in _row_chunks(h_ref, w_ref, chunk):
        for c in range(tn // LANES):
            scr[c, pad + rows.start:pad + rows.stop, :] = y[:, c * LANES:(c + 1) * LANES]
        if rows.start > 0:
            finish(rows.start - chunk)
    finish(s - chunk)


def _inproj_gate_kernel(h_ref, w_ref, gb_ref, o_ref):
    for rows, y in _row_chunks(h_ref, w_ref):
        o_ref[rows, :] = jax.nn.sigmoid(y + gb_ref[...]).astype(o_ref.dtype)


def _inproj(h3, w, col0, width, body, extras, extra_specs, tn, scratch=(), head_major=False):
    b, s, d = h3.shape
    tn = min(tn, width)
    nj = width // tn
    jb0 = col0 // tn
    est = (2 * _nbytes((s, d), BF16) + 2 * _nbytes((d, tn), BF16) + 2 * _nbytes((s, tn), BF16)
           + 4 * _nbytes((s, tn), F32) + (8 << 20))
    return pl.pallas_call(
        body,
        grid=(b, nj),
        in_specs=[pl.BlockSpec((None, s, d), lambda i, j: (i, 0, 0)),
                  pl.BlockSpec((d, tn), lambda i, j: (0, jb0 + j))] + extra_specs,
        out_specs=(pl.BlockSpec((None, tn // HEAD_DIM, s, HEAD_DIM), lambda i, j: (i, j, 0, 0)) if head_major
                   else pl.BlockSpec((None, s, tn), lambda i, j: (i, 0, j))),
        out_shape=jax.ShapeDtypeStruct((b, width // HEAD_DIM, s, HEAD_DIM) if head_major else (b, s, width), BF16),
        scratch_shapes=list(scratch),
        compiler_params=_cparams(("parallel", "arbitrary"), est),
        name="inproj_" + getattr(body, "func", body).__name__,
    )(h3, w, *extras)


def _rope_tables(s):
    half = ROPE_DIM // 2
    inv_freq = ROPE_THETA ** (-jnp.arange(half, dtype=F32) / half)
    ang = jnp.arange(s, dtype=F32)[:, None] * inv_freq[None, :]
    cos, sin = jnp.cos(ang), jnp.sin(ang)
    c = jnp.concatenate([cos, cos, jnp.ones((s, HEAD_DIM - ROPE_DIM), F32)], axis=1)
    s1 = jnp.concatenate([-sin, jnp.zeros((s, HEAD_DIM - half), F32)], axis=1)
    s2 = jnp.concatenate([jnp.zeros((s, half), F32), sin, jnp.zeros((s, HEAD_DIM - ROPE_DIM), F32)], axis=1)
    return c, s1, s2


def _band_bias(off, kw):
    col = lax.broadcasted_iota(jnp.int32, (ATT_QBLK, kw), 1)
    row = lax.broadcasted_iota(jnp.int32, (ATT_QBLK, kw), 0)
    return jnp.where(jnp.abs(col - row + off) <= ATT_RADIUS, 0.0, MASK_NEG).astype(F32)


def _band_blocks(seg_len):
    kw = min(ATT_KWIN, seg_len)
    for bi in range(seg_len // ATT_QBLK):
        q0 = bi * ATT_QBLK
        yield q0, min(max(q0 - ATT_RADIUS, 0), seg_len - kw), kw


def _band_attend(blocks, finish):
    pending = {}
    for i in range(len(blocks) + ATT_DEPTH):
        if i < len(blocks):
            qb, kb, bias = blocks[i][0]()
            pending[i] = lax.dot_general(qb, kb, (((1,), (1,)), ((), ())), preferred_element_type=F32) + bias
        j = i - ATT_DEPTH
        if j >= 0:
            s = pending.pop(j)
            m = jnp.max(s, axis=-1, keepdims=True)
            p = jnp.exp2(s - m).astype(BF16)
            oa = jnp.dot(p, blocks[j][1](), preferred_element_type=F32)
            finish(j, (oa[:, :HEAD_DIM], oa[:, HEAD_DIM:], jnp.broadcast_to(m, (ATT_QBLK, HEAD_DIM))))


def _softmax_merge(a, b):
    (o1, l1, m1), (o2, l2, m2) = a, b
    m = jnp.maximum(m1, m2)
    a1 = jnp.exp2(m1 - m)
    a2 = jnp.exp2(m2 - m)
    return a1 * o1 + a2 * o2, a1 * l1 + a2 * l2, m


def _attn_kernel(q_ref, k_ref, v_ref, o_ref,
                 qf0, kf0, vf0, qf1, kf1, vf1, qb1, kb1, vb1, qb2, kb2, vb2,
                 ao, al, am, bo, bl, bm):
    s = q_ref.shape[0]
    c = HEAD_DIM
    n1 = s // 4
    n2 = s // 16
    biases = {}

    def bias(off, kw):
        if (off, kw) not in biases:
            biases[(off, kw)] = _band_bias(off, kw)
        return biases[(off, kw)]

    def grouped(items, compute, store, group=ATT_GROUP):
        for g0 in range(0, len(items), group):
            chunk = items[g0:g0 + group]
            for it, val in zip(chunk, compute(chunk)):
                store(*it, val)

    def each(fn):
        return lambda chunk: [fn(*it) for it in chunk]

    qf0[...] = q_ref[...].astype(F32)
    kf0[...] = k_ref[...].astype(F32)
    vf0[...] = v_ref[...].astype(F32)

    def split4_store(r, src, dstf, dstb, x):
        rows = slice(r * n1, (r + 1) * n1)
        dstf[rows, :] = x
        dstb[rows, :] = x.astype(BF16)

    grouped([(r, src, dstf, dstb) for r in range(4)
             for src, dstf, dstb in ((qf0, qf1, qb1), (kf0, kf1, kb1), (vf0, vf1, vb1))],
            each(lambda r, src, dstf, dstb: src[pl.ds(r, n1, stride=4), :]), split4_store)

    def split16_store(seg, srcf, dstb, x):
        dstb[seg * n2:(seg + 1) * n2, :] = x

    grouped([(seg, srcf, dstb) for seg in range(16) for srcf, dstb in ((qf1, qb2), (kf1, kb2), (vf1, vb2))],
            each(lambda seg, srcf, dstb: srcf[pl.ds((seg // 4) * n1 + seg % 4, n2, stride=4), :].astype(BF16)),
            split16_store)

    def attend(qb, kb, vb, items, finish):
        def loaders(base, q0, w0, kw):
            return (lambda: (qb[base + q0:base + q0 + ATT_QBLK, :], kb[base + w0:base + w0 + kw, :],
                             bias(w0 - q0, kw)),
                    lambda: jnp.concatenate([vb[base + w0:base + w0 + kw, :], jnp.ones((kw, c), BF16)], axis=1))

        _band_attend([loaders(*it) for it in items], lambda i, olm: finish(*items[i], olm))

    def merged(olm, prev_refs, base, q0):
        return _softmax_merge(olm, tuple(ref[base + q0:base + q0 + ATT_QBLK, :] for ref in prev_refs))

    def finish16(base, q0, w0, kw, olm):
        seg = base // n2
        dst = pl.ds((seg // 4) * n1 + 4 * q0 + seg % 4, ATT_QBLK, stride=4)
        for ref, val in zip((ao, al, am), olm):
            ref[dst, :] = val

    attend(qb2, kb2, vb2, [(seg * n2, *blk) for seg in range(16) for blk in _band_blocks(n2)], finish16)

    def finish4(base, q0, w0, kw, olm):
        dst = pl.ds(4 * q0 + base // n1, ATT_QBLK, stride=4)
        for ref, val in zip((bo, bl, bm), merged(olm, (ao, al, am), base, q0)):
            ref[dst, :] = val

    attend(qb1, kb1, vb1, [(r4 * n1, *blk) for r4 in range(4) for blk in _band_blocks(n1)], finish4)

    def finish1(base, q0, w0, kw, olm):
        o, l, _ = merged(olm, (bo, bl, bm), base, q0)
        o_ref[q0:q0 + ATT_QBLK, :] = (o / l).astype(o_ref.dtype)

    attend(q_ref, k_ref, v_ref, [(0, *blk) for blk in _band_blocks(s)], finish1)


def _dilated_attention(q, k, v):
    b, nh, s, _ = q.shape
    aw = nh * HEAD_DIM
    assert tuple(w // (2 * d) for w, d in DILATED_PATTERNS) == (ATT_RADIUS,) * 3
    assert tuple(d for _, d in DILATED_PATTERNS) == (1, 4, 16)
    assert s % (16 * ATT_QBLK) == 0
    c = HEAD_DIM
    spec = pl.BlockSpec((None, None, s, c), lambda i, h: (i, h, 0, 0))
    out_spec = pl.BlockSpec((None, s, c), lambda i, h: (i, 0, h))
    f32buf = pltpu.VMEM((s, c), F32)
    bf16buf = pltpu.VMEM((s, c), BF16)
    scratch = [f32buf] * 6 + [bf16buf] * 6 + [f32buf] * 6
    est = 12 * _nbytes((s, c), F32) + 10 * _nbytes((s, c), BF16) + 8 * _nbytes((s, c), BF16) + (16 << 20)
    return pl.pallas_call(
        _attn_kernel,
        grid=(b, aw // c),
        in_specs=[spec, spec, spec],
        out_specs=out_spec,
        out_shape=jax.ShapeDtypeStruct((b, s, aw), BF16),
        scratch_shapes=scratch,
        compiler_params=_cparams(("parallel", "arbitrary"), est),
        name="dilated_attention",
    )(q, k, v)


def _filter_kernel(z_ref, w1_ref, b1_ref, w2_ref, b2_ref, fr_ref, w3_ref, t_ref, dl_ref,
                   filt_ref, asum_ref, hi_ref, lo_ref, *, blocks_per_dir):
    def split(a):
        hi = a.astype(BF16)
        return hi, (a - hi.astype(F32)).astype(BF16)

    @pl.when(pl.program_id(0) == 0)
    def _():
        fr = fr_ref[...]
        hdn = jnp.sin(fr * (jnp.dot(z_ref[...], w1_ref[...], precision=HIGHEST,
                                    preferred_element_type=F32) + b1_ref[...]))
        hdn = jnp.sin(fr * (jnp.dot(hdn, w2_ref[...], precision=HIGHEST,
                                    preferred_element_type=F32) + b2_ref[...]))
        hi_ref[...], lo_ref[...] = split(hdn)

    w_hi, w_lo = split(w3_ref[...])
    hf = (jnp.dot(hi_ref[...], w_hi, preferred_element_type=F32)
          + jnp.dot(hi_ref[...], w_lo, preferred_element_type=F32)
          + jnp.dot(lo_ref[...], w_hi, preferred_element_type=F32))
    hf = hf * jnp.exp(-t_ref[...] * dl_ref[...])
    is_bwd = (pl.program_id(0) // blocks_per_dir) % 2 == 1
    row = lax.broadcasted_iota(jnp.int32, hf.shape, 0)
    hf = jnp.where(jnp.logical_and(row == 0, is_bwd), 0.0, hf)
    filt_ref[...] = hf.astype(filt_ref.dtype)
    asum_ref[...] = jnp.sum(jnp.abs(hf), axis=0, keepdims=True)


def _mod4_rows(a):
    return jnp.concatenate([a[c::4] for c in range(4)], axis=0)


def _hyena_filter_taps(seq_len, w1, b1, w2, b2, sin_freq, w3, hy_w, tn=512):
    ffn = w1.shape[1]
    pad = LANES
    t = jnp.linspace(0.0, 1.0, seq_len, dtype=F32)[:, None]
    wpos = 2.0 * math.pi * jnp.arange(seq_len, dtype=F32) / seq_len
    bands = jnp.linspace(1e-4, HY_BANDS - 1, HY_BANDS, dtype=F32)
    ang = wpos[:, None] * bands[None, :]
    z = jnp.concatenate([t, jnp.cos(ang), -jnp.sin(ang)], axis=-1)
    z = _mod4_rows(jnp.pad(z, ((0, 0), (0, pad - z.shape[1]))))
    t = _mod4_rows(t)
    w1p = jnp.pad(w1.astype(F32), ((0, pad - w1.shape[0]), (0, pad - ffn)))
    w2p = jnp.pad(w2.astype(F32), ((0, pad - ffn), (0, pad - ffn)))
    w3p = jnp.pad(w3.astype(F32), ((0, pad - ffn), (0, 0)))
    padv = lambda a: jnp.pad(a.astype(F32), (0, pad - ffn)).reshape(1, pad)
    deltas = jnp.abs(jnp.linspace(HY_MIN_DECAY, HY_MAX_DECAY, hy_w, dtype=F32)).reshape(1, hy_w)
    ncol = w3.shape[1]
    tn = min(tn, hy_w)
    bpd = hy_w // tn
    full = lambda shape: pl.BlockSpec(shape, lambda j: (0, 0))
    return pl.pallas_call(
        functools.partial(_filter_kernel, blocks_per_dir=bpd),
        grid=(ncol // tn,),
        in_specs=[full((seq_len, pad)), full((pad, pad)), full((1, pad)), full((pad, pad)), full((1, pad)),
                  full((1, pad)), pl.BlockSpec((pad, tn), lambda j: (0, j)), full((seq_len, 1)),
                  pl.BlockSpec((1, tn), lambda j: (0, j % bpd))],
        out_specs=[pl.BlockSpec((seq_len, tn), lambda j: (0, j)), pl.BlockSpec((1, tn), lambda j: (0, j))],
        out_shape=[jax.ShapeDtypeStruct((seq_len, ncol), BF16), jax.ShapeDtypeStruct((1, ncol), F32)],
        scratch_shapes=[pltpu.VMEM((seq_len, pad), BF16)] * 2,
        compiler_params=_cparams(("arbitrary",), 32 << 20),
        name="hyena_filter_taps",
    )(z, w1p, padv(b1), w2p, padv(b2), padv(sin_freq), w3p, t, deltas)


def _dft_matrices(s, n_total):
    n = 2 * s
    k = jnp.arange(s, dtype=jnp.int32)

    def tables(cols):
        ang = ((k[:, None] * cols[None, :]) % n).astype(F32) * (2.0 * math.pi / n)
        return jnp.cos(ang), jnp.sin(ang)

    cl, sl = tables(jnp.arange(LANES, dtype=jnp.int32))
    ch, sh = tables(jnp.arange(s // LANES, dtype=jnp.int32) * LANES)
    cosm = (ch[:, :, None] * cl[:, None, :] - sh[:, :, None] * sl[:, None, :]).reshape(s, s)
    sinm = (sh[:, :, None] * cl[:, None, :] + ch[:, :, None] * sl[:, None, :]).reshape(s, s)
    alt = jnp.where(k % 2 == 0, 1.0, -1.0).astype(F32)
    im = (-sinm).at[0].set(alt)
    scale_re = jnp.full((s,), 2.0 / n_total, F32).at[0].set(1.0 / n_total)[:, None]
    r = min(HY_FREQ_CHUNK, s)
    chunked = lambda a: a.reshape(2, s // r, r, -1).transpose(1, 0, 2, 3).reshape(n, -1)
    fwds, invs = [], []
    for c in range(4):
        theta = ((c * k) % n_total).astype(F32) * (2.0 * math.pi / n_total)
        cs, sn = jnp.cos(theta)[:, None], jnp.sin(theta)[:, None]
        fwds.append(chunked(jnp.concatenate([cs * cosm + sn * im, cs * im - sn * cosm], axis=0)))
        invs.append(chunked(jnp.concatenate([(cs * cosm + sn * im) * scale_re, (cs * im - sn * cosm) * scale_re],
                                            axis=0)).T)
    return jnp.stack(fwds).astype(BF16), jnp.stack(invs).astype(BF16)


HY_FREQ_CHUNK = 256


def _freq_chunks(m):
    r = min(HY_FREQ_CHUNK, m)
    return [slice(r0, r0 + r) for r0 in range(0, m, r)]


def _first_row(shape):
    return lax.broadcasted_iota(jnp.int32, shape, 0) == 0


COS_PI_4 = math.sqrt(0.5)


def _quarter_spectrum(z_ref, f_ref, rows):
    q = z_ref.shape[0] // 4
    r = rows.stop - rows.start
    frows = slice(2 * rows.start, 2 * rows.stop)
    sub = []
    for c in range(4):
        f = jnp.dot(f_ref[c, frows, :], z_ref[c * q:(c + 1) * q, :], preferred_element_type=F32)
        sub.append((f[:r], f[r:]))
    (t0r, t0i), (t1r, t1i), (t2r, t2i), (t3r, t3i) = sub
    ur, ui, vr, vi = t0r + t2r, t0i + t2i, t0r - t2r, t0i - t2i
    wr, wi, xr, xi = t1r + t3r, t1i + t3i, t1r - t3r, t1i - t3i
    slots = [(ur + wr, ui + wi), (vr - xi, vi + xr), (vr + xi, vi - xr), (ur - wr, ui - wi)]
    if rows.start == 0:
        s0, s1, s2, s3 = (sub[c][0][0:1, :] for c in range(4))
        n0, n1, n2, n3 = (sub[c][1][0:1, :] for c in range(4))
        k4 = COS_PI_4
        row0 = [(s0 + s1 + s2 + s3, s0 - s1 + s2 - s3),
                (s0 - s2, s3 - s1),
                (n0 + k4 * (n1 - n3), -n2 - k4 * (n1 + n3)),
                (n0 - k4 * (n1 - n3), n2 - k4 * (n1 + n3))]
        first = _first_row(ur.shape)
        slots = [(jnp.where(first, p0[0], p[0]), jnp.where(first, p0[1], p[1])) for p, p0 in zip(slots, row0)]
    return slots


def _spectrum_product(h, z, with_first):
    out = []
    for i, ((hr, hi), (zr, zi)) in enumerate(zip(h, z)):
        rr, ii = hr * zr, hi * zi
        yr, yi = rr - ii, hr * zi + hi * zr
        if i == 0 and with_first:
            first = _first_row(rr.shape)
            yr, yi = jnp.where(first, rr, yr), jnp.where(first, ii, yi)
        out.append((yr, yi))
    return out


def _inverse_butterfly(y, rows, dtype):
    (ar, ai), (br, bi), (cr, ci), (dr, di) = y
    g = [(ar + br + cr + dr, ai + bi + ci + di),
         (ar + bi - ci - dr, ai - br + cr - di),
         (ar - br - cr + dr, ai - bi - ci + di),
         (ar - bi + ci - dr, ai + br - cr - di)]
    if rows.start == 0:
        y0, yn, qr, qi = ar[0:1, :], ai[0:1, :], br[0:1, :], bi[0:1, :]
        a, b, a2, b2 = cr[0:1, :], ci[0:1, :], dr[0:1, :], di[0:1, :]
        k4 = 2.0 * COS_PI_4
        dc = [y0 + yn + 2.0 * qr, y0 - yn - 2.0 * qi, y0 + yn - 2.0 * qr, y0 - yn + 2.0 * qi]
        half = [2.0 * (a + a2), k4 * (a - b) - k4 * (a2 + b2), 2.0 * (b2 - b), k4 * (a2 - b2) - k4 * (a + b)]
        first = _first_row(ar.shape)
        g = [(jnp.where(first, d0, gr), jnp.where(first, h0, gi)) for (gr, gi), d0, h0 in zip(g, dc, half)]
    return [jnp.concatenate([gr.astype(dtype), gi.astype(dtype)], axis=0) for gr, gi in g]


def _filter_spectrum_kernel(tf_ref, tb_ref, nf_ref, nb_ref, f_ref, o_ref):
    inv = 1.0 / (nf_ref[...] + nb_ref[...])
    for rows in _freq_chunks(o_ref.shape[1]):
        fwd = _quarter_spectrum(tf_ref, f_ref, rows)
        bwd = _quarter_spectrum(tb_ref, f_ref, rows)
        for i, ((fr, fi), (gr, gi)) in enumerate(zip(fwd, bwd)):
            im = fi - gi
            if i == 0 and rows.start == 0:
                im = jnp.where(_first_row(im.shape), fi + gi, im)
            o_ref[2 * i, rows, :] = (fr + gr) * inv
            o_ref[2 * i + 1, rows, :] = im * inv


def _filter_spectrum(taps, asum, fwd, hy_w, tc):
    l = taps.shape[0]
    q = fwd.shape[2]
    nb = hy_w // tc
    col = lambda d: (lambda o, c: (0, (2 * o + d) * nb + c))
    const = lambda shape: pl.BlockSpec(shape, lambda o, c: (0,) * len(shape), pipeline_mode=pl.Buffered(1))
    est = (_nbytes((4, 2 * q, q), BF16) + 4 * _nbytes((l, tc), BF16)
           + 2 * _nbytes((8, q, tc), F32) + 48 * _nbytes((HY_FREQ_CHUNK, tc), F32) + (8 << 20))
    return pl.pallas_call(
        _filter_spectrum_kernel,
        grid=(HY_ORDER, nb),
        in_specs=[pl.BlockSpec((l, tc), col(0)), pl.BlockSpec((l, tc), col(1)),
                  pl.BlockSpec((1, tc), col(0)), pl.BlockSpec((1, tc), col(1)),
                  const((4, 2 * q, q))],
        out_specs=pl.BlockSpec((None, 8, q, tc), lambda o, c: (o, 0, 0, c)),
        out_shape=jax.ShapeDtypeStruct((HY_ORDER, 8, q, hy_w), F32),
        compiler_params=_cparams(("parallel", "arbitrary"), est),
        name="hyena_filter_spectrum",
    )(taps, taps, asum, asum, fwd)


HY_SEQS_PER_STEP = 2


def _hyena_conv_kernel(z_ref, g_ref, h_ref, b_ref, f_ref, inv_ref, o_ref, spec_ref, *scratch, natural_out):
    nseq, s, tc = z_ref.shape
    q = s // 4
    for b in range(nseq):
        for rows in _freq_chunks(q):
            spec = _quarter_spectrum(z_ref.at[b], f_ref, rows)
            filt = [(h_ref[2 * i, rows, :], h_ref[2 * i + 1, rows, :]) for i in range(4)]
            prod = _spectrum_product(filt, spec, rows.start == 0)
            for c, packed in enumerate(_inverse_butterfly(prod, rows, spec_ref.dtype)):
                spec_ref[b, c, 2 * rows.start:2 * rows.stop, :] = packed
    bias = b_ref[...]
    for b in range(nseq):
        for rows in _freq_chunks(q):
            r = rows.stop - rows.start
            outs = []
            for c in range(4):
                cls = slice(c * q + rows.start, c * q + rows.stop)
                conv = jnp.dot(inv_ref[c, rows, :], spec_ref[b, c], preferred_element_type=F32)
                outs.append((cls, g_ref[b, cls, :].astype(F32) * (conv + bias * z_ref[b, cls, :].astype(F32))))
            if natural_out:
                (nat,) = scratch
                span = slice(4 * rows.start, 4 * rows.stop)
                for lane in range(tc // LANES):
                    cols = slice(lane * LANES, (lane + 1) * LANES)
                    for c, (_, val) in enumerate(outs):
                        nat[b, lane, pl.ds(4 * rows.start + c, r, stride=4), :] = val[:, cols]
                    o_ref[b, span, cols] = nat[b, lane, span, :].astype(o_ref.dtype)
            else:
                for cls, val in outs:
                    o_ref[b, cls, :] = val.astype(o_ref.dtype)


def _hyena_conv(z, z_col0, gate, gate_col0, filt, order, bias_row, mats, width, natural_out):
    fwd, inv, tc = mats
    b, s, _ = z.shape
    q = s // 4
    nc = width // tc
    nseq = HY_SEQS_PER_STEP if b % HY_SEQS_PER_STEP == 0 else 1
    zb0, gb0 = z_col0 // tc, gate_col0 // tc
    const = lambda shape: pl.BlockSpec(shape, lambda c, i: (0,) * len(shape), pipeline_mode=pl.Buffered(1))
    scratch = [pltpu.VMEM((nseq, 4, 2 * q, tc), BF16)]
    if natural_out:
        scratch.append(pltpu.VMEM((nseq, tc // LANES, s, LANES), F32))
    est = (2 * _nbytes((4, 2 * q, q), BF16) + 2 * _nbytes((8, q, tc), F32)
           + nseq * (6 * _nbytes((s, tc), BF16) + 4 * _nbytes((2 * q, tc), BF16) + _nbytes((s, tc), F32))
           + 64 * _nbytes((HY_FREQ_CHUNK, tc), F32) + (8 << 20))
    return pl.pallas_call(
        functools.partial(_hyena_conv_kernel, natural_out=natural_out),
        grid=(nc, b // nseq),
        in_specs=[pl.BlockSpec((nseq, s, tc), lambda c, i: (i, 0, zb0 + c)),
                  pl.BlockSpec((nseq, s, tc), lambda c, i: (i, 0, gb0 + c)),
                  pl.BlockSpec((None, 8, q, tc), lambda c, i: (order, 0, 0, c)),
                  pl.BlockSpec((1, tc), lambda c, i: (0, c)),
                  const((4, 2 * q, q)), const((4, q, 2 * q))],
        out_specs=pl.BlockSpec((nseq, s, tc), lambda c, i: (i, 0, c)),
        out_shape=jax.ShapeDtypeStruct((b, s, width), BF16),
        scratch_shapes=scratch,
        compiler_params=_cparams(("parallel", "arbitrary"), est),
        name="hyena_conv",
    )(z, gate, filt, bias_row, fwd, inv)


MATMUL_ROW_CHUNK = 512


def _chunk_rows(n_rows):
    chunk = min(MATMUL_ROW_CHUNK, n_rows)
    return [slice(r0, r0 + chunk) for r0 in range(0, n_rows, chunk)]


def _merge_kernel(a_ref, hy_ref, wa_ref, wh_ref, g0_ref, g1_ref, o_ref):
    for rows in _chunk_rows(a_ref.shape[0]):
        ya = jnp.dot(a_ref[rows, :], wa_ref[...], preferred_element_type=F32)
        yh = jnp.dot(hy_ref[rows, :], wh_ref[...], preferred_element_type=F32)
        o_ref[rows, :] = (g0_ref[rows, :].astype(F32) * ya + g1_ref[rows, :].astype(F32) * yh).astype(o_ref.dtype)


def _weight_spec(k, tn, n):
    if tn == n:
        return pl.BlockSpec((k, n), lambda i, j: (0, 0), pipeline_mode=pl.Buffered(1))
    return pl.BlockSpec((k, tn), lambda i, j: (0, j))


def _gated_merge(attn, hy, w_att, w_hy, gates, tm=512, tn=2048):
    t, k = attn.shape
    n = w_att.shape[1]
    tm, tn = min(tm, t), min(tn, n)
    g1b0 = n // tn
    est = 4 * _nbytes((tm, k), BF16) + 4 * _nbytes((k, tn), BF16) + 6 * _nbytes((tm, tn), F32) + (8 << 20)
    return pl.pallas_call(
        _merge_kernel,
        grid=(t // tm, n // tn),
        in_specs=[pl.BlockSpec((tm, k), lambda i, j: (i, 0)),
                  pl.BlockSpec((tm, k), lambda i, j: (i, 0)),
                  _weight_spec(k, tn, n),
                  _weight_spec(k, tn, n),
                  pl.BlockSpec((tm, tn), lambda i, j: (i, j)),
                  pl.BlockSpec((tm, tn), lambda i, j: (i, g1b0 + j))],
        out_specs=pl.BlockSpec((tm, tn), lambda i, j: (i, j)),
        out_shape=jax.ShapeDtypeStruct((t, n), BF16),
        compiler_params=_cparams(("parallel", "arbitrary"), est),
        name="gated_merge",
    )(attn, hy, w_att, w_hy, gates, gates)


def _mm_residual_kernel(a_ref, w_ref, ra_ref, rb_ref, o_ref, *, na):
    def body(r_ref):
        for rows in _chunk_rows(a_ref.shape[0]):
            o_ref[rows, :] = r_ref[rows, :] + jnp.dot(a_ref[rows, :], w_ref[...], preferred_element_type=F32)

    _pick_group(na, ra_ref, rb_ref, body)


def _matmul_residual(a, w, res_a, res_b, tm=512, tn=2048):
    t, k = a.shape
    n = w.shape[1]
    tm, tn = min(tm, res_a.shape[0]), min(tn, n)
    assert res_a.shape[0] % tm == 0 and res_b.shape[0] % tm == 0
    na = res_a.shape[0] // tm
    est = 2 * _nbytes((tm, k), BF16) + 2 * _nbytes((k, tn), BF16) + 8 * _nbytes((tm, tn), F32) + (8 << 20)
    return pl.pallas_call(
        functools.partial(_mm_residual_kernel, na=na),
        grid=(t // tm, n // tn),
        in_specs=[pl.BlockSpec((tm, k), lambda i, j: (i, 0)),
                  _weight_spec(k, tn, n),
                  *_two_group_specs((tm, tn), na, col=lambda j: j)],
        out_specs=pl.BlockSpec((tm, tn), lambda i, j: (i, j)),
        out_shape=jax.ShapeDtypeStruct((t, n), F32),
        compiler_params=_cparams(("parallel", "arbitrary"), est),
        name="matmul_residual",
    )(a, w, res_a, res_b)


def _rmsnorm_rows(x, g):
    ms = jnp.mean(x * x, axis=-1, keepdims=True)
    return x * lax.rsqrt(ms + EPS) * g


def _mlp_kernel(x_ref, g_ref, wu_ref, wd_ref, o_ref, hm_ref):
    def body(first):
        for rows in _chunk_rows(hm_ref.shape[0]):
            if first:
                x = x_ref[rows, :]
                hm = _rmsnorm_rows(x, g_ref[...]).astype(hm_ref.dtype)
                hm_ref[rows, :] = hm
            else:
                hm = hm_ref[rows, :]
            a = jnp.dot(hm, wu_ref[...], preferred_element_type=F32)
            a = jnp.square(jnp.maximum(a, 0.0)).astype(BF16)
            y = jnp.dot(a, wd_ref[...], preferred_element_type=F32)
            o_ref[rows, :] = (x if first else o_ref[rows, :]) + y

    j = pl.program_id(1)
    pl.when(j == 0)(lambda: body(True))
    pl.when(j > 0)(lambda: body(False))


def _relu2_mlp(x, g, w_up, w_down, tm=1024, tf=1024):
    t, d = x.shape
    ff = w_up.shape[1]
    tm, tf = min(tm, t), min(tf, ff)
    est = (4 * _nbytes((tm, d), F32) + _nbytes((tm, d), BF16) + 4 * _nbytes((d, tf), BF16)
           + 3 * _nbytes((tm, tf), F32) + (8 << 20))
    return pl.pallas_call(
        _mlp_kernel,
        grid=(t // tm, ff // tf),
        in_specs=[pl.BlockSpec((tm, d), lambda i, j: (i, 0)),
                  pl.BlockSpec((1, d), lambda i, j: (0, 0)),
                  pl.BlockSpec((d, tf), lambda i, j: (0, j)),
                  pl.BlockSpec((tf, d), lambda i, j: (j, 0))],
        out_specs=pl.BlockSpec((tm, d), lambda i, j: (i, 0)),
        out_shape=jax.ShapeDtypeStruct((t, d), F32),
        scratch_shapes=[pltpu.VMEM((tm, d), BF16)],
        compiler_params=_cparams(("parallel", "arbitrary"), est),
        name="relu2_mlp",
    )(x, g.reshape(1, d), w_up, w_down)


def _ple_kernel(x_ref, g_ref, wg_ref, p_ref, wp_ref, o_ref):
    for rows in _chunk_rows(x_ref.shape[0]):
        x = x_ref[rows, :]
        hn = _rmsnorm_rows(x, g_ref[...]).astype(BF16)
        gate = jax.nn.sigmoid(jnp.dot(hn, wg_ref[...], preferred_element_type=F32))
        proj = jnp.dot(p_ref[rows, :].astype(BF16), wp_ref[...], preferred_element_type=F32)
        o_ref[rows, :] = x + gate * proj


def _gated_ple(x, row0, p, g, w_gate, w_proj, tm=512):
    t, pd = p.shape
    d = x.shape[1]
    tm = min(tm, t)
    rb0 = row0 // tm
    est = (4 * _nbytes((tm, d), F32) + _nbytes((d, d), BF16) + _nbytes((pd, d), BF16)
           + 2 * _nbytes((tm, pd), F32) + 8 * _nbytes((MATMUL_ROW_CHUNK, d), F32) + (8 << 20))
    return pl.pallas_call(
        _ple_kernel,
        grid=(t // tm, 1),
        in_specs=[pl.BlockSpec((tm, d), lambda i, j: (rb0 + i, 0)),
                  pl.BlockSpec((1, d), lambda i, j: (0, 0)),
                  _weight_spec(d, d, d),
                  pl.BlockSpec((tm, pd), lambda i, j: (i, 0)),
                  _weight_spec(pd, d, d)],
        out_specs=pl.BlockSpec((tm, d), lambda i, j: (i, 0)),
        out_shape=jax.ShapeDtypeStruct((t, d), F32),
        compiler_params=_cparams(("parallel", "arbitrary"), est),
        name="gated_ple",
    )(x, g.reshape(1, d), w_gate, p, w_proj)


def _encoder_layer(x_list, p_list, norm_mix_g, w_in, gate_b, q_norm_g, k_norm_g, hy_conv_w, hy_conv_b,
                   hy_w1, hy_b1, hy_w2, hy_b2, hy_sin_freq, hy_w3, hy_bias,
                   w_att_out, w_hy_out, w_o, norm_mlp_g, w_up, w_down,
                   norm_ple_g, w_ple_gate, w_ple_proj):
    xa3, xb3 = x_list
    s, d = xa3.shape[1:]
    b = xa3.shape[0] + xb3.shape[0]
    att_w = w_att_out.shape[0]
    hy_w = w_hy_out.shape[0]
    xa, xb = xa3.reshape(-1, d), xb3.reshape(-1, d)
    bf = lambda a: a.astype(BF16)

    h3 = _rmsnorm_cast(xa, xb, norm_mix_g).reshape(b, s, d)
    w_in_b = bf(w_in)
    rope_c, rope_s1, rope_s2 = _rope_tables(s)
    tab_spec = pl.BlockSpec((s, HEAD_DIM), lambda i, j: (0, 0))
    vec_spec = lambda tn, blk0=0: pl.BlockSpec((1, tn), lambda i, j: (0, blk0 + j))
    head_spec = pl.BlockSpec((1, HEAD_DIM), lambda i, j: (0, 0))
    tn = 1024
    qk = lambda g, col0, scale: _inproj(
        h3, w_in_b, col0, att_w, functools.partial(_inproj_qk_kernel, scale=scale),
        (g.reshape(1, HEAD_DIM), rope_c, rope_s1, rope_s2), [head_spec, tab_spec, tab_spec, tab_spec], tn,
        head_major=True)
    q = qk(q_norm_g, 0, HEAD_DIM ** -0.5 * math.log2(math.e))
    k = qk(k_norm_g, att_w, 1.0)
    v = _inproj(h3, w_in_b, 2 * att_w, att_w, _inproj_heads_kernel, (), [], tn, head_major=True)
    tnc = min(tn, hy_w)
    u = _inproj(h3, w_in_b, 3 * att_w, 3 * hy_w, _inproj_conv_kernel,
                (hy_conv_w, hy_conv_b.reshape(1, 3 * hy_w)),
                [pl.BlockSpec((3, tnc), lambda i, j: (0, j)), vec_spec(tnc)], tnc,
                scratch=[pltpu.VMEM((tnc // LANES, s + 2 * CONV_PAD_ROWS, LANES), F32)])
    tng = min(tn, d)
    gates = _inproj(h3, w_in_b, 3 * att_w + 3 * hy_w, N_BRANCH * d, _inproj_gate_kernel,
                    (gate_b.reshape(1, N_BRANCH * d),), [vec_spec(tng)], tng)

    attn = _dilated_attention(q, k, v)

    hy_tc = min(256, hy_w)
    mats = (*_dft_matrices(s // 4, 2 * s), hy_tc)
    taps, tap_asum = _hyena_filter_taps(s, hy_w1, hy_b1, hy_w2, hy_b2, hy_sin_freq, hy_w3, hy_w)
    filt = _filter_spectrum(taps, tap_asum, mats[0], hy_w, hy_tc)
    bias = hy_bias.astype(F32)
    z = _hyena_conv(u, 0, u, hy_w, filt, 0, bias[0:1], mats, hy_w, natural_out=False)
    hy = _hyena_conv(z, 0, u, 2 * hy_w, filt, 1, bias[1:2], mats, hy_w, natural_out=True)

    merged = _gated_merge(attn.reshape(b * s, att_w), hy.reshape(b * s, hy_w),
                          bf(w_att_out), bf(w_hy_out), gates.reshape(b * s, N_BRANCH * d))
    x2 = _matmul_residual(merged, bf(w_o), xa, xb)
    x2 = _relu2_mlp(x2, norm_mlp_g, bf(w_up), bf(w_down))

    outs, row0 = [], 0
    w_pg, w_pp = bf(w_ple_gate), bf(w_ple_proj)
    for p in p_list:
        pb = p.shape[0]
        p2 = p.reshape(pb * s, p.shape[-1])
        outs.append(_gated_ple(x2, row0, p2, norm_ple_g, w_pg, w_pp).reshape(pb, s, d))
        row0 += pb * s
    return outs


def kernel(x_prompt, x_sample, p_prompt, p_sample, norm_mix_g, w_in, gate_b, q_norm_g, k_norm_g, hy_conv_w, hy_conv_b, hy_w1, hy_b1, hy_w2, hy_b2, hy_sin_freq, hy_w3, hy_bias, w_att_out, w_hy_out, w_o, norm_mlp_g, w_up, w_down, norm_ple_g, w_ple_gate, w_ple_proj):
    assert x_prompt.shape[1:] == x_sample.shape[1:]
    y_prompt, y_sample = x_prompt, x_sample
    for i in range(w_in.shape[0]):
        y_prompt, y_sample = _encoder_layer(
            (y_prompt, y_sample), (p_prompt[i], p_sample[i]), norm_mix_g[i], w_in[i], gate_b[i], q_norm_g[i], k_norm_g[i],
            hy_conv_w[i], hy_conv_b[i], hy_w1[i], hy_b1[i], hy_w2[i], hy_b2[i], hy_sin_freq[i], hy_w3[i],
            hy_bias[i], w_att_out[i], w_hy_out[i], w_o[i], norm_mlp_g[i], w_up[i], w_down[i],
            norm_ple_g[i], w_ple_gate[i], w_ple_proj[i])
    return (y_prompt, y_sample)
```

```python
import functools
import math

import jax
import jax.numpy as jnp
from jax import lax
from jax.experimental import pallas as pl
from jax.experimental.pallas import tpu as pltpu

F32 = jnp.float32
BF16 = jnp.bfloat16
HIGHEST = lax.Precision.HIGHEST

EPS = 1e-6
HEAD_DIM = 128
ROPE_DIM = HEAD_DIM // 4
ROPE_THETA = 500000.0
DILATED_PATTERNS = ((128, 1), (512, 4), (2048, 16))
ATT_RADIUS = 64
ATT_QBLK = 128
ATT_KWIN = ATT_QBLK + 2 * ATT_RADIUS
ATT_GROUP = 8
ATT_DEPTH = 3
HY_ORDER = 2
HY_EMB_DIM = 33
HY_BANDS = (HY_EMB_DIM - 1) // 2
HY_FAST_DECAY = 0.3
HY_SLOW_DECAY = 1.5
HY_DECAY_TARGET = 1e-2
HY_MIN_DECAY = math.log(HY_DECAY_TARGET) / HY_SLOW_DECAY
HY_MAX_DECAY = math.log(HY_DECAY_TARGET) / HY_FAST_DECAY
N_BRANCH = 2
LANES = 128
MASK_NEG = -1e30

VMEM_CAP_BYTES = 60 * 1024 * 1024


def _cparams(sem, est_bytes):
    limit = int(min(VMEM_CAP_BYTES, max(32 * 1024 * 1024, est_bytes)))
    return pltpu.CompilerParams(dimension_semantics=sem, vmem_limit_bytes=limit)


def _nbytes(shape, dtype):
    return math.prod(shape) * jnp.dtype(dtype).itemsize


def _two_group_specs(block, na, col=lambda *j: 0):
    return (pl.BlockSpec(block, lambda i, *j: (jnp.minimum(i, na - 1), jnp.where(i < na, col(*j), 0))),
            pl.BlockSpec(block, lambda i, *j: (jnp.maximum(i - na, 0), jnp.where(i >= na, col(*j), 0))))


def _pick_group(na, xa_ref, xb_ref, body):
    i = pl.program_id(0)
    pl.when(i < na)(lambda: body(xa_ref))
    pl.when(i >= na)(lambda: body(xb_ref))


def _rmsnorm_kernel(xa_ref, xb_ref, g_ref, o_ref, *, na):
    def body(x_ref):
        x = x_ref[...]
        ms = jnp.mean(x * x, axis=-1, keepdims=True)
        o_ref[...] = (x * lax.rsqrt(ms + EPS) * g_ref[...]).astype(o_ref.dtype)

    _pick_group(na, xa_ref, xb_ref, body)


def _rmsnorm_cast(xa, xb, g, tm=512):
    d = xa.shape[1]
    assert xa.shape[0] % tm == 0 and xb.shape[0] % tm == 0
    na, t = xa.shape[0] // tm, xa.shape[0] + xb.shape[0]
    return pl.pallas_call(
        functools.partial(_rmsnorm_kernel, na=na),
        grid=(t // tm,),
        in_specs=[*_two_group_specs((tm, d), na), pl.BlockSpec((1, d), lambda i: (0, 0))],
        out_specs=pl.BlockSpec((tm, d), lambda i: (i, 0)),
        out_shape=jax.ShapeDtypeStruct((t, d), BF16),
        compiler_params=_cparams(("parallel",), 6 * _nbytes((tm, d), F32)),
        name="rmsnorm_cast",
    )(xa, xb, g.reshape(1, d))


INPROJ_ROW_CHUNK = 256


def _row_chunks(h_ref, w_ref, chunk=INPROJ_ROW_CHUNK):
    for r0 in range(0, h_ref.shape[0], chunk):
        rows = slice(r0, r0 + chunk)
        yield rows, jnp.dot(h_ref[rows, :], w_ref[...], preferred_element_type=F32)


def _inproj_qk_kernel(h_ref, w_ref, g_ref, c_ref, s1_ref, s2_ref, o_ref, *, scale):
    gs = g_ref[...] * scale
    pw = 2 * HEAD_DIM
    same_head = (lax.broadcasted_iota(jnp.int32, (pw, pw), 0) // HEAD_DIM
                 == lax.broadcasted_iota(jnp.int32, (pw, pw), 1) // HEAD_DIM)
    ones_bd = jnp.where(same_head, 1.0, 0.0).astype(BF16)
    for rows, y in _row_chunks(h_ref, w_ref):
        yy = (y * y).astype(BF16)
        c, s1, s2 = c_ref[rows, :], s1_ref[rows, :], s2_ref[rows, :]
        for hh in range(y.shape[1] // HEAD_DIM):
            sl = slice(hh * HEAD_DIM, (hh + 1) * HEAD_DIM)
            if hh % 2 == 0:
                ss_pair = jnp.dot(yy[:, hh * HEAD_DIM:hh * HEAD_DIM + pw], ones_bd, preferred_element_type=F32)
            ms = ss_pair[:, (hh % 2) * HEAD_DIM:(hh % 2 + 1) * HEAD_DIM] * (1.0 / HEAD_DIM)
            t = y[:, sl] * lax.rsqrt(ms + EPS) * gs
            out = (t * c + pltpu.roll(t, HEAD_DIM - ROPE_DIM // 2, 1) * s1 + pltpu.roll(t, ROPE_DIM // 2, 1) * s2)
            o_ref[hh, rows, :] = out.astype(o_ref.dtype)


def _inproj_heads_kernel(h_ref, w_ref, o_ref):
    for rows, y in _row_chunks(h_ref, w_ref):
        for hh in range(y.shape[1] // HEAD_DIM):
            o_ref[hh, rows, :] = y[:, hh * HEAD_DIM:(hh + 1) * HEAD_DIM].astype(o_ref.dtype)


CONV_PAD_ROWS = 16
CONV_ROW_CHUNK = 1024


def _inproj_conv_kernel(h_ref, w_ref, cw_ref, cb_ref, o_ref, scr):
    s, tn = h_ref.shape[0], w_ref.shape[1]
    q = s // 4
    chunk = CONV_ROW_CHUNK
    part = chunk // 4
    pad = CONV_PAD_ROWS
    zeros = jnp.zeros((pad, LANES), F32)
    for c in range(tn // LANES):
        scr[c, 0:pad, :] = zeros
        scr[c, pad + s:2 * pad + s, :] = zeros

    def finish(r0):
        for c in range(tn // LANES):
            cols = slice(c * LANES, (c + 1) * LANES)
            w0, w1, w2, cb = cw_ref[0:1, cols], cw_ref[1:2, cols], cw_ref[2:3, cols], cb_ref[:, cols]
            y = [scr[c, pl.ds(pad + r0 + off, part, stride=4), :] for off in range(-1, 5)]
            for cls in range(4):
                rows = slice(cls * q + r0 // 4, cls * q + r0 // 4 + part)
                o_ref[rows, cols] = (y[cls] * w0 + y[cls + 1] * w1 + y[cls + 2] * w2 + cb).astype(o_ref.dtype)

    for rows, y in _row_chunks(h_ref, w_ref, chunk):
        for c in range(tn // LANES):
            scr[c, pad + rows.start:pad + rows.stop, :] = y[:, c * LANES:(c + 1) * LANES]
        if rows.start > 0:
            finish(rows.start - chunk)
    finish(s - chunk)


def _inproj_gate_kernel(h_ref, w_ref, gb_ref, o_ref):
    for rows, y in _row_chunks(h_ref, w_ref):
        o_ref[rows, :] = jax.nn.sigmoid(y + gb_ref[...]).astype(o_ref.dtype)


def _inproj(h3, w, col0, width, body, extras, extra_specs, tn, scratch=(), head_major=False):
    b, s, d = h3.shape
    tn = min(tn, width)
    nj = width // tn
    jb0 = col0 // tn
    est = (2 * _nbytes((s, d), BF16) + 2 * _nbytes((d, tn), BF16) + 2 * _nbytes((s, tn), BF16)
           + 4 * _nbytes((s, tn), F32) + (8 << 20))
    return pl.pallas_call(
        body,
        grid=(b, nj),
        in_specs=[pl.BlockSpec((None, s, d), lambda i, j: (i, 0, 0)),
                  pl.BlockSpec((d, tn), lambda i, j: (0, jb0 + j))] + extra_specs,
        out_specs=(pl.BlockSpec((None, tn // HEAD_DIM, s, HEAD_DIM), lambda i, j: (i, j, 0, 0)) if head_major
                   else pl.BlockSpec((None, s, tn), lambda i, j: (i, 0, j))),
        out_shape=jax.ShapeDtypeStruct((b, width // HEAD_DIM, s, HEAD_DIM) if head_major else (b, s, width), BF16),
        scratch_shapes=list(scratch),
        compiler_params=_cparams(("parallel", "arbitrary"), est),
        name="inproj_" + getattr(body, "func", body).__name__,
    )(h3, w, *extras)


def _rope_tables(s):
    half = ROPE_DIM // 2
    inv_freq = ROPE_THETA ** (-jnp.arange(half, dtype=F32) / half)
    ang = jnp.arange(s, dtype=F32)[:, None] * inv_freq[None, :]
    cos, sin = jnp.cos(ang), jnp.sin(ang)
    c = jnp.concatenate([cos, cos, jnp.ones((s, HEAD_DIM - ROPE_DIM), F32)], axis=1)
    s1 = jnp.concatenate([-sin, jnp.zeros((s, HEAD_DIM - half), F32)], axis=1)
    s2 = jnp.concatenate([jnp.zeros((s, half), F32), sin, jnp.zeros((s, HEAD_DIM - ROPE_DIM), F32)], axis=1)
    return c, s1, s2


def _band_bias(off, kw):
    col = lax.broadcasted_iota(jnp.int32, (ATT_QBLK, kw), 1)
    row = lax.broadcasted_iota(jnp.int32, (ATT_QBLK, kw), 0)
    return jnp.where(jnp.abs(col - row + off) <= ATT_RADIUS, 0.0, MASK_NEG).astype(F32)


def _band_blocks(seg_len):
    kw = min(ATT_KWIN, seg_len)
    for bi in range(seg_len // ATT_QBLK):
        q0 = bi * ATT_QBLK
        yield q0, min(max(q0 - ATT_RADIUS, 0), seg_len - kw), kw


def _band_attend(blocks, finish):
    pending = {}
    for i in range(len(blocks) + ATT_DEPTH):
        if i < len(blocks):
            qb, kb, bias = blocks[i][0]()
            pending[i] = lax.dot_general(qb, kb, (((1,), (1,)), ((), ())), preferred_element_type=F32) + bias
        j = i - ATT_DEPTH
        if j >= 0:
            s = pending.pop(j)
            m = jnp.max(s, axis=-1, keepdims=True)
            p = jnp.exp2(s - m).astype(BF16)
            oa = jnp.dot(p, blocks[j][1](), preferred_element_type=F32)
            finish(j, (oa[:, :HEAD_DIM], oa[:, HEAD_DIM:], jnp.broadcast_to(m, (ATT_QBLK, HEAD_DIM))))


def _softmax_merge(a, b):
    (o1, l1, m1), (o2, l2, m2) = a, b
    m = jnp.maximum(m1, m2)
    a1 = jnp.exp2(m1 - m)
    a2 = jnp.exp2(m2 - m)
    return a1 * o1 + a2 * o2, a1 * l1 + a2 * l2, m


def _attn_kernel(q_ref, k_ref, v_ref, o_ref,
                 qf0, kf0, vf0, qf1, kf1, vf1, qb1, kb1, vb1, qb2, kb2, vb2,
                 ao, al, am, bo, bl, bm):
    s = q_ref.shape[0]
    c = HEAD_DIM
    n1 = s // 4
    n2 = s // 16
    biases = {}

    def bias(off, kw):
        if (off, kw) not in biases:
            biases[(off, kw)] = _band_bias(off, kw)
        return biases[(off, kw)]

    def grouped(items, compute, store, group=ATT_GROUP):
        for g0 in range(0, len(items), group):
            chunk = items[g0:g0 + group]
            for it, val in zip(chunk, compute(chunk)):
                store(*it, val)

    def each(fn):
        return lambda chunk: [fn(*it) for it in chunk]

    qf0[...] = q_ref[...].astype(F32)
    kf0[...] = k_ref[...].astype(F32)
    vf0[...] = v_ref[...].astype(F32)

    def split4_store(r, src, dstf, dstb, x):
        rows = slice(r * n1, (r + 1) * n1)
        dstf[rows, :] = x
        dstb[rows, :] = x.astype(BF16)

    grouped([(r, src, dstf, dstb) for r in range(4)
             for src, dstf, dstb in ((qf0, qf1, qb1), (kf0, kf1, kb1), (vf0, vf1, vb1))],
            each(lambda r, src, dstf, dstb: src[pl.ds(r, n1, stride=4), :]), split4_store)

    def split16_store(seg, srcf, dstb, x):
        dstb[seg * n2:(seg + 1) * n2, :] = x

    grouped([(seg, srcf, dstb) for seg in range(16) for srcf, dstb in ((qf1, qb2), (kf1, kb2), (vf1, vb2))],
            each(lambda seg, srcf, dstb: srcf[pl.ds((seg // 4) * n1 + seg % 4, n2, stride=4), :].astype(BF16)),
            split16_store)

    def attend(qb, kb, vb, items, finish):
        def loaders(base, q0, w0, kw):
            return (lambda: (qb[base + q0:base + q0 + ATT_QBLK, :], kb[base + w0:base + w0 + kw, :],
                             bias(w0 - q0, kw)),
                    lambda: jnp.concatenate([vb[base + w0:base + w0 + kw, :], jnp.ones((kw, c), BF16)], axis=1))

        _band_attend([loaders(*it) for it in items], lambda i, olm: finish(*items[i], olm))

    def merged(olm, prev_refs, base, q0):
        return _softmax_merge(olm, tuple(ref[base + q0:base + q0 + ATT_QBLK, :] for ref in prev_refs))

    def finish16(base, q0, w0, kw, olm):
        seg = base // n2
        dst = pl.ds((seg // 4) * n1 + 4 * q0 + seg % 4, ATT_QBLK, stride=4)
        for ref, val in zip((ao, al, am), olm):
            ref[dst, :] = val

    attend(qb2, kb2, vb2, [(seg * n2, *blk) for seg in range(16) for blk in _band_blocks(n2)], finish16)

    def finish4(base, q0, w0, kw, olm):
        dst = pl.ds(4 * q0 + base // n1, ATT_QBLK, stride=4)
        for ref, val in zip((bo, bl, bm), merged(olm, (ao, al, am), base, q0)):
            ref[dst, :] = val

    attend(qb1, kb1, vb1, [(r4 * n1, *blk) for r4 in range(4) for blk in _band_blocks(n1)], finish4)

    def finish1(base, q0, w0, kw, olm):
        o, l, _ = merged(olm, (bo, bl, bm), base, q0)
        o_ref[q0:q0 + ATT_QBLK, :] = (o / l).astype(o_ref.dtype)

    attend(q_ref, k_ref, v_ref, [(0, *blk) for blk in _band_blocks(s)], finish1)


def _dilated_attention(q, k, v):
    b, nh, s, _ = q.shape
    aw = nh * HEAD_DIM
    assert tuple(w // (2 * d) for w, d in DILATED_PATTERNS) == (ATT_RADIUS,) * 3
    assert tuple(d for _, d in DILATED_PATTERNS) == (1, 4, 16)
    assert s % (16 * ATT_QBLK) == 0
    c = HEAD_DIM
    spec = pl.BlockSpec((None, None, s, c), lambda i, h: (i, h, 0, 0))
    out_spec = pl.BlockSpec((None, s, c), lambda i, h: (i, 0, h))
    f32buf = pltpu.VMEM((s, c), F32)
    bf16buf = pltpu.VMEM((s, c), BF16)
    scratch = [f32buf] * 6 + [bf16buf] * 6 + [f32buf] * 6
    est = 12 * _nbytes((s, c), F32) + 10 * _nbytes((s, c), BF16) + 8 * _nbytes((s, c), BF16) + (16 << 20)
    return pl.pallas_call(
        _attn_kernel,
        grid=(b, aw // c),
        in_specs=[spec, spec, spec],
        out_specs=out_spec,
        out_shape=jax.ShapeDtypeStruct((b, s, aw), BF16),
        scratch_shapes=scratch,
        compiler_params=_cparams(("parallel", "arbitrary"), est),
        name="dilated_attention",
    )(q, k, v)


def _filter_kernel(z_ref, w1_ref, b1_ref, w2_ref, b2_ref, fr_ref, w3_ref, t_ref, dl_ref,
                   filt_ref, asum_ref, hi_ref, lo_ref, *, blocks_per_dir):
    def split(a):
        hi = a.astype(BF16)
        return hi, (a - hi.astype(F32)).astype(BF16)

    @pl.when(pl.program_id(0) == 0)
    def _():
        fr = fr_ref[...]
        hdn = jnp.sin(fr * (jnp.dot(z_ref[...], w1_ref[...], precision=HIGHEST,
                                    preferred_element_type=F32) + b1_ref[...]))
        hdn = jnp.sin(fr * (jnp.dot(hdn, w2_ref[...], precision=HIGHEST,
                                    preferred_element_type=F32) + b2_ref[...]))
        hi_ref[...], lo_ref[...] = split(hdn)

    w_hi, w_lo = split(w3_ref[...])
    hf = (jnp.dot(hi_ref[...], w_hi, preferred_element_type=F32)
          + jnp.dot(hi_ref[...], w_lo, preferred_element_type=F32)
          + jnp.dot(lo_ref[...], w_hi, preferred_element_type=F32))
    hf = hf * jnp.exp(-t_ref[...] * dl_ref[...])
    is_bwd = (pl.program_id(0) // blocks_per_dir) % 2 == 1
    row = lax.broadcasted_iota(jnp.int32, hf.shape, 0)
    hf = jnp.where(jnp.logical_and(row == 0, is_bwd), 0.0, hf)
    filt_ref[...] = hf.astype(filt_ref.dtype)
    asum_ref[...] = jnp.sum(jnp.abs(hf), axis=0, keepdims=True)


def _mod4_rows(a):
    return jnp.concatenate([a[c::4] for c in range(4)], axis=0)


def _hyena_filter_taps(seq_len, w1, b1, w2, b2, sin_freq, w3, hy_w, tn=512):
    ffn = w1.shape[1]
    pad = LANES
    t = jnp.linspace(0.0, 1.0, seq_len, dtype=F32)[:, None]
    wpos = 2.0 * math.pi * jnp.arange(seq_len, dtype=F32) / seq_len
    bands = jnp.linspace(1e-4, HY_BANDS - 1, HY_BANDS, dtype=F32)
    ang = wpos[:, None] * bands[None, :]
    z = jnp.concatenate([t, jnp.cos(ang), -jnp.sin(ang)], axis=-1)
    z = _mod4_rows(jnp.pad(z, ((0, 0), (0, pad - z.shape[1]))))
    t = _mod4_rows(t)
    w1p = jnp.pad(w1.astype(F32), ((0, pad - w1.shape[0]), (0, pad - ffn)))
    w2p = jnp.pad(w2.astype(F32), ((0, pad - ffn), (0, pad - ffn)))
    w3p = jnp.pad(w3.astype(F32), ((0, pad - ffn), (0, 0)))
    padv = lambda a: jnp.pad(a.astype(F32), (0, pad - ffn)).reshape(1, pad)
    deltas = jnp.abs(jnp.linspace(HY_MIN_DECAY, HY_MAX_DECAY, hy_w, dtype=F32)).reshape(1, hy_w)
    ncol = w3.shape[1]
    tn = min(tn, hy_w)
    bpd = hy_w // tn
    full = lambda shape: pl.BlockSpec(shape, lambda j: (0, 0))
    return pl.pallas_call(
        functools.partial(_filter_kernel, blocks_per_dir=bpd),
        grid=(ncol // tn,),
        in_specs=[full((seq_len, pad)), full((pad, pad)), full((1, pad)), full((pad, pad)), full((1, pad)),
                  full((1, pad)), pl.BlockSpec((pad, tn), lambda j: (0, j)), full((seq_len, 1)),
                  pl.BlockSpec((1, tn), lambda j: (0, j % bpd))],
        out_specs=[pl.BlockSpec((seq_len, tn), lambda j: (0, j)), pl.BlockSpec((1, tn), lambda j: (0, j))],
        out_shape=[jax.ShapeDtypeStruct((seq_len, ncol), BF16), jax.ShapeDtypeStruct((1, ncol), F32)],
        scratch_shapes=[pltpu.VMEM((seq_len, pad), BF16)] * 2,
        compiler_params=_cparams(("arbitrary",), 32 << 20),
        name="hyena_filter_taps",
    )(z, w1p, padv(b1), w2p, padv(b2), padv(sin_freq), w3p, t, deltas)


def _dft_matrices(s, n_total):
    n = 2 * s
    k = jnp.arange(s, dtype=jnp.int32)

    def tables(cols):
        ang = ((k[:, None] * cols[None, :]) % n).astype(F32) * (2.0 * math.pi / n)
        return jnp.cos(ang), jnp.sin(ang)

    cl, sl = tables(jnp.arange(LANES, dtype=jnp.int32))
    ch, sh = tables(jnp.arange(s // LANES, dtype=jnp.int32) * LANES)
    cosm = (ch[:, :, None] * cl[:, None, :] - sh[:, :, None] * sl[:, None, :]).reshape(s, s)
    sinm = (sh[:, :, None] * cl[:, None, :] + ch[:, :, None] * sl[:, None, :]).reshape(s, s)
    alt = jnp.where(k % 2 == 0, 1.0, -1.0).astype(F32)
    im = (-sinm).at[0].set(alt)
    scale_re = jnp.full((s,), 2.0 / n_total, F32).at[0].set(1.0 / n_total)[:, None]
    r = min(HY_FREQ_CHUNK, s)
    chunked = lambda a: a.reshape(2, s // r, r, -1).transpose(1, 0, 2, 3).reshape(n, -1)
    fwds, invs = [], []
    for c in range(4):
        theta = ((c * k) % n_total).astype(F32) * (2.0 * math.pi / n_total)
        cs, sn = jnp.cos(theta)[:, None], jnp.sin(theta)[:, None]
        fwds.append(chunked(jnp.concatenate([cs * cosm + sn * im, cs * im - sn * cosm], axis=0)))
        invs.append(chunked(jnp.concatenate([(cs * cosm + sn * im) * scale_re, (cs * im - sn * cosm) * scale_re],
                                            axis=0)).T)
    return jnp.stack(fwds).astype(BF16), jnp.stack(invs).astype(BF16)


HY_FREQ_CHUNK = 256


def _freq_chunks(m):
    r = min(HY_FREQ_CHUNK, m)
    return [slice(r0, r0 + r) for r0 in range(0, m, r)]


def _first_row(shape):
    return lax.broadcasted_iota(jnp.int32, shape, 0) == 0


COS_PI_4 = math.sqrt(0.5)


def _quarter_spectrum(z_ref, f_ref, rows):
    q = z_ref.shape[0] // 4
    r = rows.stop - rows.start
    frows = slice(2 * rows.start, 2 * rows.stop)
    sub = []
    for c in range(4):
        f = jnp.dot(f_ref[c, frows, :], z_ref[c * q:(c + 1) * q, :], preferred_element_type=F32)
        sub.append((f[:r], f[r:]))
    (t0r, t0i), (t1r, t1i), (t2r, t2i), (t3r, t3i) = sub
    ur, ui, vr, vi = t0r + t2r, t0i + t2i, t0r - t2r, t0i - t2i
    wr, wi, xr, xi = t1r + t3r, t1i + t3i, t1r - t3r, t1i - t3i
    slots = [(ur + wr, ui + wi), (vr - xi, vi + xr), (vr + xi, vi - xr), (ur - wr, ui - wi)]
    if rows.start == 0:
        s0, s1, s2, s3 = (sub[c][0][0:1, :] for c in range(4))
        n0, n1, n2, n3 = (sub[c][1][0:1, :] for c in range(4))
        k4 = COS_PI_4
        row0 = [(s0 + s1 + s2 + s3, s0 - s1 + s2 - s3),
                (s0 - s2, s3 - s1),
                (n0 + k4 * (n1 - n3), -n2 - k4 * (n1 + n3)),
                (n0 - k4 * (n1 - n3), n2 - k4 * (n1 + n3))]
        first = _first_row(ur.shape)
        slots = [(jnp.where(first, p0[0], p[0]), jnp.where(first, p0[1], p[1])) for p, p0 in zip(slots, row0)]
    return slots


def _spectrum_product(h, z, with_first):
    out = []
    for i, ((hr, hi), (zr, zi)) in enumerate(zip(h, z)):
        rr, ii = hr * zr, hi * zi
        yr, yi = rr - ii, hr * zi + hi * zr
        if i == 0 and with_first:
            first = _first_row(rr.shape)
            yr, yi = jnp.where(first, rr, yr), jnp.where(first, ii, yi)
        out.append((yr, yi))
    return out


def _inverse_butterfly(y, rows, dtype):
    (ar, ai), (br, bi), (cr, ci), (dr, di) = y
    g = [(ar + br + cr + dr, ai + bi + ci + di),
         (ar + bi - ci - dr, ai - br + cr - di),
         (ar - br - cr + dr, ai - bi - ci + di),
         (ar - bi + ci - dr, ai + br - cr - di)]
    if rows.start == 0:
        y0, yn, qr, qi = ar[0:1, :], ai[0:1, :], br[0:1, :], bi[0:1, :]
        a, b, a2, b2 = cr[0:1, :], ci[0:1, :], dr[0:1, :], di[0:1, :]
        k4 = 2.0 * COS_PI_4
        dc = [y0 + yn + 2.0 * qr, y0 - yn - 2.0 * qi, y0 + yn - 2.0 * qr, y0 - yn + 2.0 * qi]
        half = [2.0 * (a + a2), k4 * (a - b) - k4 * (a2 + b2), 2.0 * (b2 - b), k4 * (a2 - b2) - k4 * (a + b)]
        first = _first_row(ar.shape)
        g = [(jnp.where(first, d0, gr), jnp.where(first, h0, gi)) for (gr, gi), d0, h0 in zip(g, dc, half)]
    return [jnp.concatenate([gr.astype(dtype), gi.astype(dtype)], axis=0) for gr, gi in g]


def _filter_spectrum_kernel(tf_ref, tb_ref, nf_ref, nb_ref, f_ref, o_ref):
    inv = 1.0 / (nf_ref[...] + nb_ref[...])
    for rows in _freq_chunks(o_ref.shape[1]):
        fwd = _quarter_spectrum(tf_ref, f_ref, rows)
        bwd = _quarter_spectrum(tb_ref, f_ref, rows)
        for i, ((fr, fi), (gr, gi)) in enumerate(zip(fwd, bwd)):
            im = fi - gi
            if i == 0 and rows.start == 0:
                im = jnp.where(_first_row(im.shape), fi + gi, im)
            o_ref[2 * i, rows, :] = (fr + gr) * inv
            o_ref[2 * i + 1, rows, :] = im * inv


def _filter_spectrum(taps, asum, fwd, hy_w, tc):
    l = taps.shape[0]
    q = fwd.shape[2]
    nb = hy_w // tc
    col = lambda d: (lambda o, c: (0, (2 * o + d) * nb + c))
    const = lambda shape: pl.BlockSpec(shape, lambda o, c: (0,) * len(shape), pipeline_mode=pl.Buffered(1))
    est = (_nbytes((4, 2 * q, q), BF16) + 4 * _nbytes((l, tc), BF16)
           + 2 * _nbytes((8, q, tc), F32) + 48 * _nbytes((HY_FREQ_CHUNK, tc), F32) + (8 << 20))
    return pl.pallas_call(
        _filter_spectrum_kernel,
        grid=(HY_ORDER, nb),
        in_specs=[pl.BlockSpec((l, tc), col(0)), pl.BlockSpec((l, tc), col(1)),
                  pl.BlockSpec((1, tc), col(0)), pl.BlockSpec((1, tc), col(1)),
                  const((4, 2 * q, q))],
        out_specs=pl.BlockSpec((None, 8, q, tc), lambda o, c: (o, 0, 0, c)),
        out_shape=jax.ShapeDtypeStruct((HY_ORDER, 8, q, hy_w), F32),
        compiler_params=_cparams(("parallel", "arbitrary"), est),
        name="hyena_filter_spectrum",
    )(taps, taps, asum, asum, fwd)


HY_SEQS_PER_STEP = 3


def _hyena_conv_kernel(z_ref, g_ref, h_ref, b_ref, f_ref, inv_ref, o_ref, spec_ref, *scratch, natural_out):
    nseq, s, tc = z_ref.shape
    q = s // 4
    for b in range(nseq):
        for rows in _freq_chunks(q):
            spec = _quarter_spectrum(z_ref.at[b], f_ref, rows)
            filt = [(h_ref[2 * i, rows, :], h_ref[2 * i + 1, rows, :]) for i in range(4)]
            prod = _spectrum_product(filt, spec, rows.start == 0)
            for c, packed in enumerate(_inverse_butterfly(prod, rows, spec_ref.dtype)):
                spec_ref[b, c, 2 * rows.start:2 * rows.stop, :] = packed
    bias = b_ref[...]
    for b in range(nseq):
        for rows in _freq_chunks(q):
            r = rows.stop - rows.start
            outs = []
            for c in range(4):
                cls = slice(c * q + rows.start, c * q + rows.stop)
                conv = jnp.dot(inv_ref[c, rows, :], spec_ref[b, c], preferred_element_type=F32)
                outs.append((cls, g_ref[b, cls, :].astype(F32) * (conv + bias * z_ref[b, cls, :].astype(F32))))
            if natural_out:
                (nat,) = scratch
                span = slice(4 * rows.start, 4 * rows.stop)
                for lane in range(tc // LANES):
                    cols = slice(lane * LANES, (lane + 1) * LANES)
                    for c, (_, val) in enumerate(outs):
                        nat[b, lane, pl.ds(4 * rows.start + c, r, stride=4), :] = val[:, cols]
                    o_ref[b, span, cols] = nat[b, lane, span, :].astype(o_ref.dtype)
            else:
                for cls, val in outs:
                    o_ref[b, cls, :] = val.astype(o_ref.dtype)


def _hyena_conv(z, z_col0, gate, gate_col0, filt, order, bias_row, mats, width, natural_out):
    fwd, inv, tc = mats
    b, s, _ = z.shape
    q = s // 4
    nc = width // tc
    nseq = max(n for n in range(1, HY_SEQS_PER_STEP + 1) if b % n == 0)
    zb0, gb0 = z_col0 // tc, gate_col0 // tc
    const = lambda shape: pl.BlockSpec(shape, lambda c, i: (0,) * len(shape), pipeline_mode=pl.Buffered(1))
    scratch = [pltpu.VMEM((nseq, 4, 2 * q, tc), BF16)]
    if natural_out:
        scratch.append(pltpu.VMEM((nseq, tc // LANES, s, LANES), F32))
    est = (2 * _nbytes((4, 2 * q, q), BF16) + 2 * _nbytes((8, q, tc), F32)
           + nseq * (6 * _nbytes((s, tc), BF16) + 4 * _nbytes((2 * q, tc), BF16) + _nbytes((s, tc), F32))
           + 64 * _nbytes((HY_FREQ_CHUNK, tc), F32) + (8 << 20))
    return pl.pallas_call(
        functools.partial(_hyena_conv_kernel, natural_out=natural_out),
        grid=(nc, b // nseq),
        in_specs=[pl.BlockSpec((nseq, s, tc), lambda c, i: (i, 0, zb0 + c)),
                  pl.BlockSpec((nseq, s, tc), lambda c, i: (i, 0, gb0 + c)),
                  pl.BlockSpec((None, 8, q, tc), lambda c, i: (order, 0, 0, c)),
                  pl.BlockSpec((1, tc), lambda c, i: (0, c)),
                  const((4, 2 * q, q)), const((4, q, 2 * q))],
        out_specs=pl.BlockSpec((nseq, s, tc), lambda c, i: (i, 0, c)),
        out_shape=jax.ShapeDtypeStruct((b, s, width), BF16),
        scratch_shapes=scratch,
        compiler_params=_cparams(("parallel", "arbitrary"), est),
        name="hyena_conv",
    )(z, gate, filt, bias_row, fwd, inv)


MATMUL_ROW_CHUNK = 512


def _chunk_rows(n_rows):
    chunk = min(MATMUL_ROW_CHUNK, n_rows)
    return [slice(r0, r0 + chunk) for r0 in range(0, n_rows, chunk)]


def _merge_kernel(a_ref, hy_ref, wa_ref, wh_ref, g0_ref, g1_ref, o_ref):
    for rows in _chunk_rows(a_ref.shape[0]):
        ya = jnp.dot(a_ref[rows, :], wa_ref[...], preferred_element_type=F32)
        yh = jnp.dot(hy_ref[rows, :], wh_ref[...], preferred_element_type=F32)
        o_ref[rows, :] = (g0_ref[rows, :].astype(F32) * ya + g1_ref[rows, :].astype(F32) * yh).astype(o_ref.dtype)


def _weight_spec(k, tn, n):
    if tn == n:
        return pl.BlockSpec((k, n), lambda i, j: (0, 0), pipeline_mode=pl.Buffered(1))
    return pl.BlockSpec((k, tn), lambda i, j: (0, j))


def _gated_merge(attn, hy, w_att, w_hy, gates, tm=512, tn=2048):
    t, k = attn.shape
    n = w_att.shape[1]
    tm, tn = min(tm, t), min(tn, n)
    g1b0 = n // tn
    est = 4 * _nbytes((tm, k), BF16) + 4 * _nbytes((k, tn), BF16) + 6 * _nbytes((tm, tn), F32) + (8 << 20)
    return pl.pallas_call(
        _merge_kernel,
        grid=(t // tm, n // tn),
        in_specs=[pl.BlockSpec((tm, k), lambda i, j: (i, 0)),
                  pl.BlockSpec((tm, k), lambda i, j: (i, 0)),
                  _weight_spec(k, tn, n),
                  _weight_spec(k, tn, n),
                  pl.BlockSpec((tm, tn), lambda i, j: (i, j)),
                  pl.BlockSpec((tm, tn), lambda i, j: (i, g1b0 + j))],
        out_specs=pl.BlockSpec((tm, tn), lambda i, j: (i, j)),
        out_shape=jax.ShapeDtypeStruct((t, n), BF16),
        compiler_params=_cparams(("parallel", "arbitrary"), est),
        name="gated_merge",
    )(attn, hy, w_att, w_hy, gates, gates)


def _mm_residual_kernel(a_ref, w_ref, ra_ref, rb_ref, o_ref, *, na):
    def body(r_ref):
        for rows in _chunk_rows(a_ref.shape[0]):
            o_ref[rows, :] = r_ref[rows, :] + jnp.dot(a_ref[rows, :], w_ref[...], preferred_element_type=F32)

    _pick_group(na, ra_ref, rb_ref, body)


def _matmul_residual(a, w, res_a, res_b, tm=512, tn=2048):
    t, k = a.shape
    n = w.shape[1]
    tm, tn = min(tm, res_a.shape[0]), min(tn, n)
    assert res_a.shape[0] % tm == 0 and res_b.shape[0] % tm == 0
    na = res_a.shape[0] // tm
    est = 2 * _nbytes((tm, k), BF16) + 2 * _nbytes((k, tn), BF16) + 8 * _nbytes((tm, tn), F32) + (8 << 20)
    return pl.pallas_call(
        functools.partial(_mm_residual_kernel, na=na),
        grid=(t // tm, n // tn),
        in_specs=[pl.BlockSpec((tm, k), lambda i, j: (i, 0)),
                  _weight_spec(k, tn, n),
                  *_two_group_specs((tm, tn), na, col=lambda j: j)],
        out_specs=pl.BlockSpec((tm, tn), lambda i, j: (i, j)),
        out_shape=jax.ShapeDtypeStruct((t, n), F32),
        compiler_params=_cparams(("parallel", "arbitrary"), est),
        name="matmul_residual",
    )(a, w, res_a, res_b)


def _rmsnorm_rows(x, g):
    ms = jnp.mean(x * x, axis=-1, keepdims=True)
    return x * lax.rsqrt(ms + EPS) * g


def _mlp_kernel(x_ref, g_ref, wu_ref, wd_ref, o_ref, hm_ref):
    def body(first):
        for rows in _chunk_rows(hm_ref.shape[0]):
            if first:
                x = x_ref[rows, :]
                hm = _rmsnorm_rows(x, g_ref[...]).astype(hm_ref.dtype)
                hm_ref[rows, :] = hm
            else:
                hm = hm_ref[rows, :]
            a = jnp.dot(hm, wu_ref[...], preferred_element_type=F32)
            a = jnp.square(jnp.maximum(a, 0.0)).astype(BF16)
            y = jnp.dot(a, wd_ref[...], preferred_element_type=F32)
            o_ref[rows, :] = (x if first else o_ref[rows, :]) + y

    j = pl.program_id(1)
    pl.when(j == 0)(lambda: body(True))
    pl.when(j > 0)(lambda: body(False))


def _relu2_mlp(x, g, w_up, w_down, tm=1024, tf=1024):
    t, d = x.shape
    ff = w_up.shape[1]
    tm, tf = min(tm, t), min(tf, ff)
    est = (4 * _nbytes((tm, d), F32) + _nbytes((tm, d), BF16) + 4 * _nbytes((d, tf), BF16)
           + 3 * _nbytes((tm, tf), F32) + (8 << 20))
    return pl.pallas_call(
        _mlp_kernel,
        grid=(t // tm, ff // tf),
        in_specs=[pl.BlockSpec((tm, d), lambda i, j: (i, 0)),
                  pl.BlockSpec((1, d), lambda i, j: (0, 0)),
                  pl.BlockSpec((d, tf), lambda i, j: (0, j)),
                  pl.BlockSpec((tf, d), lambda i, j: (j, 0))],
        out_specs=pl.BlockSpec((tm, d), lambda i, j: (i, 0)),
        out_shape=jax.ShapeDtypeStruct((t, d), F32),
        scratch_shapes=[pltpu.VMEM((tm, d), BF16)],
        compiler_params=_cparams(("parallel", "arbitrary"), est),
        name="relu2_mlp",
    )(x, g.reshape(1, d), w_up, w_down)


def _ple_kernel(x_ref, g_ref, wg_ref, p_ref, wp_ref, o_ref):
    for rows in _chunk_rows(x_ref.shape[0]):
        x = x_ref[rows, :]
        hn = _rmsnorm_rows(x, g_ref[...]).astype(BF16)
        gate = jax.nn.sigmoid(jnp.dot(hn, wg_ref[...], preferred_element_type=F32))
        proj = jnp.dot(p_ref[rows, :].astype(BF16), wp_ref[...], preferred_element_type=F32)
        o_ref[rows, :] = x + gate * proj


def _gated_ple(x, row0, p, g, w_gate, w_proj, tm=512):
    t, pd = p.shape
    d = x.shape[1]
    tm = min(tm, t)
    rb0 = row0 // tm
    est = (4 * _nbytes((tm, d), F32) + _nbytes((d, d), BF16) + _nbytes((pd, d), BF16)
           + 2 * _nbytes((tm, pd), F32) + 8 * _nbytes((MATMUL_ROW_CHUNK, d), F32) + (8 << 20))
    return pl.pallas_call(
        _ple_kernel,
        grid=(t // tm, 1),
        in_specs=[pl.BlockSpec((tm, d), lambda i, j: (rb0 + i, 0)),
                  pl.BlockSpec((1, d), lambda i, j: (0, 0)),
                  _weight_spec(d, d, d),
                  pl.BlockSpec((tm, pd), lambda i, j: (i, 0)),
                  _weight_spec(pd, d, d)],
        out_specs=pl.BlockSpec((tm, d), lambda i, j: (i, 0)),
        out_shape=jax.ShapeDtypeStruct((t, d), F32),
        compiler_params=_cparams(("parallel", "arbitrary"), est),
        name="gated_ple",
    )(x, g.reshape(1, d), w_gate, p, w_proj)


def _encoder_layer(x_list, p_list, norm_mix_g, w_in, gate_b, q_norm_g, k_norm_g, hy_conv_w, hy_conv_b,
                   hy_w1, hy_b1, hy_w2, hy_b2, hy_sin_freq, hy_w3, hy_bias,
                   w_att_out, w_hy_out, w_o, norm_mlp_g, w_up, w_down,
                   norm_ple_g, w_ple_gate, w_ple_proj):
    xa3, xb3 = x_list
    s, d = xa3.shape[1:]
    b = xa3.shape[0] + xb3.shape[0]
    att_w = w_att_out.shape[0]
    hy_w = w_hy_out.shape[0]
    xa, xb = xa3.reshape(-1, d), xb3.reshape(-1, d)
    bf = lambda a: a.astype(BF16)

    h3 = _rmsnorm_cast(xa, xb, norm_mix_g).reshape(b, s, d)
    w_in_b = bf(w_in)
    rope_c, rope_s1, rope_s2 = _rope_tables(s)
    tab_spec = pl.BlockSpec((s, HEAD_DIM), lambda i, j: (0, 0))
    vec_spec = lambda tn, blk0=0: pl.BlockSpec((1, tn), lambda i, j: (0, blk0 + j))
    head_spec = pl.BlockSpec((1, HEAD_DIM), lambda i, j: (0, 0))
    tn = 1024
    qk = lambda g, col0, scale: _inproj(
        h3, w_in_b, col0, att_w, functools.partial(_inproj_qk_kernel, scale=scale),
        (g.reshape(1, HEAD_DIM), rope_c, rope_s1, rope_s2), [head_spec, tab_spec, tab_spec, tab_spec], tn,
        head_major=True)
    q = qk(q_norm_g, 0, HEAD_DIM ** -0.5 * math.log2(math.e))
    k = qk(k_norm_g, att_w, 1.0)
    v = _inproj(h3, w_in_b, 2 * att_w, att_w, _inproj_heads_kernel, (), [], tn, head_major=True)
    tnc = min(tn, hy_w)
    u = _inproj(h3, w_in_b, 3 * att_w, 3 * hy_w, _inproj_conv_kernel,
                (hy_conv_w, hy_conv_b.reshape(1, 3 * hy_w)),
                [pl.BlockSpec((3, tnc), lambda i, j: (0, j)), vec_spec(tnc)], tnc,
                scratch=[pltpu.VMEM((tnc // LANES, s + 2 * CONV_PAD_ROWS, LANES), F32)])
    tng = min(tn, d)
    gates = _inproj(h3, w_in_b, 3 * att_w + 3 * hy_w, N_BRANCH * d, _inproj_gate_kernel,
                    (gate_b.reshape(1, N_BRANCH * d),), [vec_spec(tng)], tng)

    attn = _dilated_attention(q, k, v)

    hy_tc = min(256, hy_w)
    mats = (*_dft_matrices(s // 4, 2 * s), hy_tc)
    taps, tap_asum = _hyena_filter_taps(s, hy_w1, hy_b1, hy_w2, hy_b2, hy_sin_freq, hy_w3, hy_w)
    filt = _filter_spectrum(taps, tap_asum, mats[0], hy_w, hy_tc)
    bias = hy_bias.astype(F32)
    z = _hyena_conv(u, 0, u, hy_w, filt, 0, bias[0:1], mats, hy_w, natural_out=False)
    hy = _hyena_conv(z, 0, u, 2 * hy_w, filt, 1, bias[1:2], mats, hy_w, natural_out=True)

    merged = _gated_merge(attn.reshape(b * s, att_w), hy.reshape(b * s, hy_w),
                          bf(w_att_out), bf(w_hy_out), gates.reshape(b * s, N_BRANCH * d))
    x2 = _matmul_residual(merged, bf(w_o), xa, xb)
    x2 = _relu2_mlp(x2, norm_mlp_g, bf(w_up), bf(w_down))

    outs, row0 = [], 0
    w_pg, w_pp = bf(w_ple_gate), bf(w_ple_proj)
    for p in p_list:
        pb = p.shape[0]
        p2 = p.reshape(pb * s, p.shape[-1])
        outs.append(_gated_ple(x2, row0, p2, norm_ple_g, w_pg, w_pp).reshape(pb, s, d))
        row0 += pb * s
    return outs


def kernel(x_prompt, x_sample, p_prompt, p_sample, norm_mix_g, w_in, gate_b, q_norm_g, k_norm_g, hy_conv_w, hy_conv_b, hy_w1, hy_b1, hy_w2, hy_b2, hy_sin_freq, hy_w3, hy_bias, w_att_out, w_hy_out, w_o, norm_mlp_g, w_up, w_down, norm_ple_g, w_ple_gate, w_ple_proj):
    assert x_prompt.shape[1:] == x_sample.shape[1:]
    y_prompt, y_sample = x_prompt, x_sample
    for i in range(w_in.shape[0]):
        y_prompt, y_sample = _encoder_layer(
            (y_prompt, y_sample), (p_prompt[i], p_sample[i]), norm_mix_g[i], w_in[i], gate_b[i], q_norm_g[i], k_norm_g[i],
            hy_conv_w[i], hy_conv_b[i], hy_w1[i], hy_b1[i], hy_w2[i], hy_b2[i], hy_sin_freq[i], hy_w3[i],
            hy_bias[i], w_att_out[i], w_hy_out[i], w_o[i], norm_mlp_g[i], w_up[i], w_down[i],
            norm_ple_g[i], w_ple_gate[i], w_ple_proj[i])
    return (y_prompt, y_sample)
```

```python
import functools
import math

import jax
import jax.numpy as jnp
from jax import lax
from jax.experimental import pallas as pl
from jax.experimental.pallas import tpu as pltpu

F32 = jnp.float32
BF16 = jnp.bfloat16
HIGHEST = lax.Precision.HIGHEST

EPS = 1e-6
HEAD_DIM = 128
ROPE_DIM = HEAD_DIM // 4
ROPE_THETA = 500000.0
DILATED_PATTERNS = ((128, 1), (512, 4), (2048, 16))
ATT_RADIUS = 64
ATT_QBLK = 128
ATT_KWIN = ATT_QBLK + 2 * ATT_RADIUS
ATT_GROUP = 8
ATT_DEPTH = 3
HY_ORDER = 2
HY_EMB_DIM = 33
HY_BANDS = (HY_EMB_DIM - 1) // 2
HY_FAST_DECAY = 0.3
HY_SLOW_DECAY = 1.5
HY_DECAY_TARGET = 1e-2
HY_MIN_DECAY = math.log(HY_DECAY_TARGET) / HY_SLOW_DECAY
HY_MAX_DECAY = math.log(HY_DECAY_TARGET) / HY_FAST_DECAY
N_BRANCH = 2
LANES = 128
MASK_NEG = -1e30

VMEM_CAP_BYTES = 60 * 1024 * 1024


def _cparams(sem, est_bytes):
    limit = int(min(VMEM_CAP_BYTES, max(32 * 1024 * 1024, est_bytes)))
    return pltpu.CompilerParams(dimension_semantics=sem, vmem_limit_bytes=limit)


def _nbytes(shape, dtype):
    return math.prod(shape) * jnp.dtype(dtype).itemsize


def _two_group_specs(block, na, col=lambda *j: 0):
    return (pl.BlockSpec(block, lambda i, *j: (jnp.minimum(i, na - 1), jnp.where(i < na, col(*j), 0))),
            pl.BlockSpec(block, lambda i, *j: (jnp.maximum(i - na, 0), jnp.where(i >= na, col(*j), 0))))


def _pick_group(na, xa_ref, xb_ref, body):
    i = pl.program_id(0)
    pl.when(i < na)(lambda: body(xa_ref))
    pl.when(i >= na)(lambda: body(xb_ref))


def _rmsnorm_kernel(xa_ref, xb_ref, g_ref, o_ref, *, na):
    def body(x_ref):
        x = x_ref[...]
        ms = jnp.mean(x * x, axis=-1, keepdims=True)
        o_ref[...] = (x * lax.rsqrt(ms + EPS) * g_ref[...]).astype(o_ref.dtype)

    _pick_group(na, xa_ref, xb_ref, body)


def _rmsnorm_cast(xa, xb, g, tm=512):
    d = xa.shape[1]
    assert xa.shape[0] % tm == 0 and xb.shape[0] % tm == 0
    na, t = xa.shape[0] // tm, xa.shape[0] + xb.shape[0]
    return pl.pallas_call(
        functools.partial(_rmsnorm_kernel, na=na),
        grid=(t // tm,),
        in_specs=[*_two_group_specs((tm, d), na), pl.BlockSpec((1, d), lambda i: (0, 0))],
        out_specs=pl.BlockSpec((tm, d), lambda i: (i, 0)),
        out_shape=jax.ShapeDtypeStruct((t, d), BF16),
        compiler_params=_cparams(("parallel",), 6 * _nbytes((tm, d), F32)),
        name="rmsnorm_cast",
    )(xa, xb, g.reshape(1, d))


INPROJ_ROW_CHUNK = 256


def _row_chunks(h_ref, w_ref, chunk=INPROJ_ROW_CHUNK):
    for r0 in range(0, h_ref.shape[0], chunk):
        rows = slice(r0, r0 + chunk)
        yield rows, jnp.dot(h_ref[rows, :], w_ref[...], preferred_element_type=F32)


def _inproj_qk_kernel(h_ref, w_ref, g_ref, c_ref, s1_ref, s2_ref, o_ref, *, scale):
    gs = g_ref[...] * scale
    pw = 2 * HEAD_DIM
    same_head = (lax.broadcasted_iota(jnp.int32, (pw, pw), 0) // HEAD_DIM
                 == lax.broadcasted_iota(jnp.int32, (pw, pw), 1) // HEAD_DIM)
    ones_bd = jnp.where(same_head, 1.0, 0.0).astype(BF16)
    for rows, y in _row_chunks(h_ref, w_ref):
        yy = (y * y).astype(BF16)
        c, s1, s2 = c_ref[rows, :], s1_ref[rows, :], s2_ref[rows, :]
        for hh in range(y.shape[1] // HEAD_DIM):
            sl = slice(hh * HEAD_DIM, (hh + 1) * HEAD_DIM)
            if hh % 2 == 0:
                ss_pair = jnp.dot(yy[:, hh * HEAD_DIM:hh * HEAD_DIM + pw], ones_bd, preferred_element_type=F32)
            ms = ss_pair[:, (hh % 2) * HEAD_DIM:(hh % 2 + 1) * HEAD_DIM] * (1.0 / HEAD_DIM)
            t = y[:, sl] * lax.rsqrt(ms + EPS) * gs
            out = (t * c + pltpu.roll(t, HEAD_DIM - ROPE_DIM // 2, 1) * s1 + pltpu.roll(t, ROPE_DIM // 2, 1) * s2)
            o_ref[hh, rows, :] = out.astype(o_ref.dtype)


def _inproj_heads_kernel(h_ref, w_ref, o_ref):
    for rows, y in _row_chunks(h_ref, w_ref):
        for hh in range(y.shape[1] // HEAD_DIM):
            o_ref[hh, rows, :] = y[:, hh * HEAD_DIM:(hh + 1) * HEAD_DIM].astype(o_ref.dtype)


CONV_PAD_ROWS = 16
CONV_ROW_CHUNK = 1024


def _inproj_conv_kernel(h_ref, w_ref, cw_ref, cb_ref, o_ref, scr):
    s, tn = h_ref.shape[0], w_ref.shape[1]
    q = s // 4
    chunk = CONV_ROW_CHUNK
    part = chunk // 4
    pad = CONV_PAD_ROWS
    zeros = jnp.zeros((pad, LANES), F32)
    for c in range(tn // LANES):
        scr[c, 0:pad, :] = zeros
        scr[c, pad + s:2 * pad + s, :] = zeros

    def finish(r0):
        for c in range(tn // LANES):
            cols = slice(c * LANES, (c + 1) * LANES)
            w0, w1, w2, cb = cw_ref[0:1, cols], cw_ref[1:2, cols], cw_ref[2:3, cols], cb_ref[:, cols]
            y = [scr[c, pl.ds(pad + r0 + off, part, stride=4), :] for off in range(-1, 5)]
            for cls in range(4):
                rows = slice(cls * q + r0 // 4, cls * q + r0 // 4 + part)
                o_ref[rows, cols] = (y[cls] * w0 + y[cls + 1] * w1 + y[cls + 2] * w2 + cb).astype(o_ref.dtype)

    for rows, y in _row_chunks(h_ref, w_ref, chunk):
        for c in range(tn // LANES):
            scr[c, pad + rows.start:pad + rows.stop, :] = y[:, c * LANES:(c + 1) * LANES]
        if rows.start > 0:
            finish(rows.start - chunk)
    finish(s - chunk)


def _inproj_gate_kernel(h_ref, w_ref, gb_ref, o_ref):
    for rows, y in _row_chunks(h_ref, w_ref):
        o_ref[rows, :] = jax.nn.sigmoid(y + gb_ref[...]).astype(o_ref.dtype)


def _inproj(h3, w, col0, width, body, extras, extra_specs, tn, scratch=(), head_major=False):
    b, s, d = h3.shape
    tn = min(tn, width)
    nj = width // tn
    jb0 = col0 // tn
    est = (2 * _nbytes((s, d), BF16) + 2 * _nbytes((d, tn), BF16) + 2 * _nbytes((s, tn), BF16)
           + 4 * _nbytes((s, tn), F32) + (8 << 20))
    return pl.pallas_call(
        body,
        grid=(b, nj),
        in_specs=[pl.BlockSpec((None, s, d), lambda i, j: (i, 0, 0)),
                  pl.BlockSpec((d, tn), lambda i, j: (0, jb0 + j))] + extra_specs,
        out_specs=(pl.BlockSpec((None, tn // HEAD_DIM, s, HEAD_DIM), lambda i, j: (i, j, 0, 0)) if head_major
                   else pl.BlockSpec((None, s, tn), lambda i, j: (i, 0, j))),
        out_shape=jax.ShapeDtypeStruct((b, width // HEAD_DIM, s, HEAD_DIM) if head_major else (b, s, width), BF16),
        scratch_shapes=list(scratch),
        compiler_params=_cparams(("parallel", "arbitrary"), est),
        name="inproj_" + getattr(body, "func", body).__name__,
    )(h3, w, *extras)


def _rope_tables(s):
    half = ROPE_DIM // 2
    inv_freq = ROPE_THETA ** (-jnp.arange(half, dtype=F32) / half)
    ang = jnp.arange(s, dtype=F32)[:, None] * inv_freq[None, :]
    cos, sin = jnp.cos(ang), jnp.sin(ang)
    c = jnp.concatenate([cos, cos, jnp.ones((s, HEAD_DIM - ROPE_DIM), F32)], axis=1)
    s1 = jnp.concatenate([-sin, jnp.zeros((s, HEAD_DIM - half), F32)], axis=1)
    s2 = jnp.concatenate([jnp.zeros((s, half), F32), sin, jnp.zeros((s, HEAD_DIM - ROPE_DIM), F32)], axis=1)
    return c, s1, s2


def _band_bias(off, kw):
    col = lax.broadcasted_iota(jnp.int32, (ATT_QBLK, kw), 1)
    row = lax.broadcasted_iota(jnp.int32, (ATT_QBLK, kw), 0)
    return jnp.where(jnp.abs(col - row + off) <= ATT_RADIUS, 0.0, MASK_NEG).astype(F32)


def _band_blocks(seg_len):
    kw = min(ATT_KWIN, seg_len)
    for bi in range(seg_len // ATT_QBLK):
        q0 = bi * ATT_QBLK
        yield q0, min(max(q0 - ATT_RADIUS, 0), seg_len - kw), kw


def _band_attend(blocks, finish):
    pending = {}
    for i in range(len(blocks) + ATT_DEPTH):
        if i < len(blocks):
            qb, kb, bias = blocks[i][0]()
            pending[i] = lax.dot_general(qb, kb, (((1,), (1,)), ((), ())), preferred_element_type=F32) + bias
        j = i - ATT_DEPTH
        if j >= 0:
            s = pending.pop(j)
            m = jnp.max(s, axis=-1, keepdims=True)
            p = jnp.exp2(s - m).astype(BF16)
            oa = jnp.dot(p, blocks[j][1](), preferred_element_type=F32)
            finish(j, (oa[:, :HEAD_DIM], oa[:, HEAD_DIM:], jnp.broadcast_to(m, (ATT_QBLK, HEAD_DIM))))


def _softmax_merge(a, b):
    (o1, l1, m1), (o2, l2, m2) = a, b
    m = jnp.maximum(m1, m2)
    a1 = jnp.exp2(m1 - m)
    a2 = jnp.exp2(m2 - m)
    return a1 * o1 + a2 * o2, a1 * l1 + a2 * l2, m


def _attn_kernel(q_ref, k_ref, v_ref, o_ref,
                 qf0, kf0, vf0, qf1, kf1, vf1, qb1, kb1, vb1, qb2, kb2, vb2,
                 ao, al, am, bo, bl, bm):
    s = q_ref.shape[0]
    c = HEAD_DIM
    n1 = s // 4
    n2 = s // 16
    biases = {}

    def bias(off, kw):
        if (off, kw) not in biases:
            biases[(off, kw)] = _band_bias(off, kw)
        return biases[(off, kw)]

    def grouped(items, compute, store, group=ATT_GROUP):
        for g0 in range(0, len(items), group):
            chunk = items[g0:g0 + group]
            for it, val in zip(chunk, compute(chunk)):
                store(*it, val)

    def each(fn):
        return lambda chunk: [fn(*it) for it in chunk]

    qf0[...] = q_ref[...].astype(F32)
    kf0[...] = k_ref[...].astype(F32)
    vf0[...] = v_ref[...].astype(F32)

    def split4_store(r, src, dstf, dstb, x):
        rows = slice(r * n1, (r + 1) * n1)
        dstf[rows, :] = x
        dstb[rows, :] = x.astype(BF16)

    grouped([(r, src, dstf, dstb) for r in range(4)
             for src, dstf, dstb in ((qf0, qf1, qb1), (kf0, kf1, kb1), (vf0, vf1, vb1))],
            each(lambda r, src, dstf, dstb: src[pl.ds(r, n1, stride=4), :]), split4_store)

    def split16_store(seg, srcf, dstb, x):
        dstb[seg * n2:(seg + 1) * n2, :] = x

    grouped([(seg, srcf, dstb) for seg in range(16) for srcf, dstb in ((qf1, qb2), (kf1, kb2), (vf1, vb2))],
            each(lambda seg, srcf, dstb: srcf[pl.ds((seg // 4) * n1 + seg % 4, n2, stride=4), :].astype(BF16)),
            split16_store)

    def attend(qb, kb, vb, items, finish):
        def loaders(base, q0, w0, kw):
            return (lambda: (qb[base + q0:base + q0 + ATT_QBLK, :], kb[base + w0:base + w0 + kw, :],
                             bias(w0 - q0, kw)),
                    lambda: jnp.concatenate([vb[base + w0:base + w0 + kw, :], jnp.ones((kw, c), BF16)], axis=1))

        _band_attend([loaders(*it) for it in items], lambda i, olm: finish(*items[i], olm))

    def merged(olm, prev_refs, base, q0):
        return _softmax_merge(olm, tuple(ref[base + q0:base + q0 + ATT_QBLK, :] for ref in prev_refs))

    def finish16(base, q0, w0, kw, olm):
        seg = base // n2
        dst = pl.ds((seg // 4) * n1 + 4 * q0 + seg % 4, ATT_QBLK, stride=4)
        for ref, val in zip((ao, al, am), olm):
            ref[dst, :] = val

    attend(qb2, kb2, vb2, [(seg * n2, *blk) for seg in range(16) for blk in _band_blocks(n2)], finish16)

    def finish4(base, q0, w0, kw, olm):
        dst = pl.ds(4 * q0 + base // n1, ATT_QBLK, stride=4)
        for ref, val in zip((bo, bl, bm), merged(olm, (ao, al, am), base, q0)):
            ref[dst, :] = val

    attend(qb1, kb1, vb1, [(r4 * n1, *blk) for r4 in range(4) for blk in _band_blocks(n1)], finish4)

    def finish1(base, q0, w0, kw, olm):
        o, l, _ = merged(olm, (bo, bl, bm), base, q0)
        o_ref[q0:q0 + ATT_QBLK, :] = (o / l).astype(o_ref.dtype)

    attend(q_ref, k_ref, v_ref, [(0, *blk) for blk in _band_blocks(s)], finish1)


def _dilated_attention(q, k, v):
    b, nh, s, _ = q.shape
    aw = nh * HEAD_DIM
    assert tuple(w // (2 * d) for w, d in DILATED_PATTERNS) == (ATT_RADIUS,) * 3
    assert tuple(d for _, d in DILATED_PATTERNS) == (1, 4, 16)
    assert s % (16 * ATT_QBLK) == 0
    c = HEAD_DIM
    spec = pl.BlockSpec((None, None, s, c), lambda i, h: (i, h, 0, 0))
    out_spec = pl.BlockSpec((None, s, c), lambda i, h: (i, 0, h))
    f32buf = pltpu.VMEM((s, c), F32)
    bf16buf = pltpu.VMEM((s, c), BF16)
    scratch = [f32buf] * 6 + [bf16buf] * 6 + [f32buf] * 6
    est = 12 * _nbytes((s, c), F32) + 10 * _nbytes((s, c), BF16) + 8 * _nbytes((s, c), BF16) + (16 << 20)
    return pl.pallas_call(
        _attn_kernel,
        grid=(b, aw // c),
        in_specs=[spec, spec, spec],
        out_specs=out_spec,
        out_shape=jax.ShapeDtypeStruct((b, s, aw), BF16),
        scratch_shapes=scratch,
        compiler_params=_cparams(("parallel", "arbitrary"), est),
        name="dilated_attention",
    )(q, k, v)


def _filter_kernel(z_ref, w1_ref, b1_ref, w2_ref, b2_ref, fr_ref, w3_ref, t_ref, dl_ref,
                   filt_ref, asum_ref, hi_ref, lo_ref, *, blocks_per_dir):
    def split(a):
        hi = a.astype(BF16)
        return hi, (a - hi.astype(F32)).astype(BF16)

    @pl.when(pl.program_id(0) == 0)
    def _():
        fr = fr_ref[...]
        hdn = jnp.sin(fr * (jnp.dot(z_ref[...], w1_ref[...], precision=HIGHEST,
                                    preferred_element_type=F32) + b1_ref[...]))
        hdn = jnp.sin(fr * (jnp.dot(hdn, w2_ref[...], precision=HIGHEST,
                                    preferred_element_type=F32) + b2_ref[...]))
        hi_ref[...], lo_ref[...] = split(hdn)

    w_hi, w_lo = split(w3_ref[...])
    hf = (jnp.dot(hi_ref[...], w_hi, preferred_element_type=F32)
          + jnp.dot(hi_ref[...], w_lo, preferred_element_type=F32)
          + jnp.dot(lo_ref[...], w_hi, preferred_element_type=F32))
    hf = hf * jnp.exp(-t_ref[...] * dl_ref[...])
    is_bwd = (pl.program_id(0) // blocks_per_dir) % 2 == 1
    row = lax.broadcasted_iota(jnp.int32, hf.shape, 0)
    hf = jnp.where(jnp.logical_and(row == 0, is_bwd), 0.0, hf)
    filt_ref[...] = hf.astype(filt_ref.dtype)
    asum_ref[...] = jnp.sum(jnp.abs(hf), axis=0, keepdims=True)


def _mod4_rows(a):
    return jnp.concatenate([a[c::4] for c in range(4)], axis=0)


def _hyena_filter_taps(seq_len, w1, b1, w2, b2, sin_freq, w3, hy_w, tn=512):
    ffn = w1.shape[1]
    pad = LANES
    t = jnp.linspace(0.0, 1.0, seq_len, dtype=F32)[:, None]
    wpos = 2.0 * math.pi * jnp.arange(seq_len, dtype=F32) / seq_len
    bands = jnp.linspace(1e-4, HY_BANDS - 1, HY_BANDS, dtype=F32)
    ang = wpos[:, None] * bands[None, :]
    z = jnp.concatenate([t, jnp.cos(ang), -jnp.sin(ang)], axis=-1)
    z = _mod4_rows(jnp.pad(z, ((0, 0), (0, pad - z.shape[1]))))
    t = _mod4_rows(t)
    w1p = jnp.pad(w1.astype(F32), ((0, pad - w1.shape[0]), (0, pad - ffn)))
    w2p = jnp.pad(w2.astype(F32), ((0, pad - ffn), (0, pad - ffn)))
    w3p = jnp.pad(w3.astype(F32), ((0, pad - ffn), (0, 0)))
    padv = lambda a: jnp.pad(a.astype(F32), (0, pad - ffn)).reshape(1, pad)
    deltas = jnp.abs(jnp.linspace(HY_MIN_DECAY, HY_MAX_DECAY, hy_w, dtype=F32)).reshape(1, hy_w)
    ncol = w3.shape[1]
    tn = min(tn, hy_w)
    bpd = hy_w // tn
    full = lambda shape: pl.BlockSpec(shape, lambda j: (0, 0))
    return pl.pallas_call(
        functools.partial(_filter_kernel, blocks_per_dir=bpd),
        grid=(ncol // tn,),
        in_specs=[full((seq_len, pad)), full((pad, pad)), full((1, pad)), full((pad, pad)), full((1, pad)),
                  full((1, pad)), pl.BlockSpec((pad, tn), lambda j: (0, j)), full((seq_len, 1)),
                  pl.BlockSpec((1, tn), lambda j: (0, j % bpd))],
        out_specs=[pl.BlockSpec((seq_len, tn), lambda j: (0, j)), pl.BlockSpec((1, tn), lambda j: (0, j))],
        out_shape=[jax.ShapeDtypeStruct((seq_len, ncol), BF16), jax.ShapeDtypeStruct((1, ncol), F32)],
        scratch_shapes=[pltpu.VMEM((seq_len, pad), BF16)] * 2,
        compiler_params=_cparams(("arbitrary",), 32 << 20),
        name="hyena_filter_taps",
    )(z, w1p, padv(b1), w2p, padv(b2), padv(sin_freq), w3p, t, deltas)


def _dft_matrices(s, n_total):
    n = 2 * s
    k = jnp.arange(s, dtype=jnp.int32)

    def tables(cols):
        ang = ((k[:, None] * cols[None, :]) % n).astype(F32) * (2.0 * math.pi / n)
        return jnp.cos(ang), jnp.sin(ang)

    cl, sl = tables(jnp.arange(LANES, dtype=jnp.int32))
    ch, sh = tables(jnp.arange(s // LANES, dtype=jnp.int32) * LANES)
    cosm = (ch[:, :, None] * cl[:, None, :] - sh[:, :, None] * sl[:, None, :]).reshape(s, s)
    sinm = (sh[:, :, None] * cl[:, None, :] + ch[:, :, None] * sl[:, None, :]).reshape(s, s)
    alt = jnp.where(k % 2 == 0, 1.0, -1.0).astype(F32)
    im = (-sinm).at[0].set(alt)
    scale_re = jnp.full((s,), 2.0 / n_total, F32).at[0].set(1.0 / n_total)[:, None]
    r = min(HY_FREQ_CHUNK, s)
    chunked = lambda a: a.reshape(2, s // r, r, -1).transpose(1, 0, 2, 3).reshape(n, -1)
    fwds, invs = [], []
    for c in range(4):
        theta = ((c * k) % n_total).astype(F32) * (2.0 * math.pi / n_total)
        cs, sn = jnp.cos(theta)[:, None], jnp.sin(theta)[:, None]
        fwds.append(chunked(jnp.concatenate([cs * cosm + sn * im, cs * im - sn * cosm], axis=0)))
        invs.append(chunked(jnp.concatenate([(cs * cosm + sn * im) * scale_re, (cs * im - sn * cosm) * scale_re],
                                            axis=0)).T)
    return jnp.stack(fwds).astype(BF16), jnp.stack(invs).astype(BF16)


HY_FREQ_CHUNK = 256


def _freq_chunks(m):
    r = min(HY_FREQ_CHUNK, m)
    return [slice(r0, r0 + r) for r0 in range(0, m, r)]


def _first_row(shape):
    return lax.broadcasted_iota(jnp.int32, shape, 0) == 0


COS_PI_4 = math.sqrt(0.5)


def _quarter_spectrum(z_ref, f_ref, rows):
    q = z_ref.shape[0] // 4
    r = rows.stop - rows.start
    frows = slice(2 * rows.start, 2 * rows.stop)
    sub = []
    for c in range(4):
        f = jnp.dot(f_ref[c, frows, :], z_ref[c * q:(c + 1) * q, :], preferred_element_type=F32)
        sub.append((f[:r], f[r:]))
    (t0r, t0i), (t1r, t1i), (t2r, t2i), (t3r, t3i) = sub
    ur, ui, vr, vi = t0r + t2r, t0i + t2i, t0r - t2r, t0i - t2i
    wr, wi, xr, xi = t1r + t3r, t1i + t3i, t1r - t3r, t1i - t3i
    slots = [(ur + wr, ui + wi), (vr - xi, vi + xr), (vr + xi, vi - xr), (ur - wr, ui - wi)]
    if rows.start == 0:
        s0, s1, s2, s3 = (sub[c][0][0:1, :] for c in range(4))
        n0, n1, n2, n3 = (sub[c][1][0:1, :] for c in range(4))
        k4 = COS_PI_4
        row0 = [(s0 + s1 + s2 + s3, s0 - s1 + s2 - s3),
                (s0 - s2, s3 - s1),
                (n0 + k4 * (n1 - n3), -n2 - k4 * (n1 + n3)),
                (n0 - k4 * (n1 - n3), n2 - k4 * (n1 + n3))]
        first = _first_row(ur.shape)
        slots = [(jnp.where(first, p0[0], p[0]), jnp.where(first, p0[1], p[1])) for p, p0 in zip(slots, row0)]
    return slots


def _spectrum_product(h, z, with_first):
    out = []
    for i, ((hr, hi), (zr, zi)) in enumerate(zip(h, z)):
        rr, ii = hr * zr, hi * zi
        yr, yi = rr - ii, hr * zi + hi * zr
        if i == 0 and with_first:
            first = _first_row(rr.shape)
            yr, yi = jnp.where(first, rr, yr), jnp.where(first, ii, yi)
        out.append((yr, yi))
    return out


def _inverse_butterfly(y, rows, dtype):
    (ar, ai), (br, bi), (cr, ci), (dr, di) = y
    g = [(ar + br + cr + dr, ai + bi + ci + di),
         (ar + bi - ci - dr, ai - br + cr - di),
         (ar - br - cr + dr, ai - bi - ci + di),
         (ar - bi + ci - dr, ai + br - cr - di)]
    if rows.start == 0:
        y0, yn, qr, qi = ar[0:1, :], ai[0:1, :], br[0:1, :], bi[0:1, :]
        a, b, a2, b2 = cr[0:1, :], ci[0:1, :], dr[0:1, :], di[0:1, :]
        k4 = 2.0 * COS_PI_4
        dc = [y0 + yn + 2.0 * qr, y0 - yn - 2.0 * qi, y0 + yn - 2.0 * qr, y0 - yn + 2.0 * qi]
        half = [2.0 * (a + a2), k4 * (a - b) - k4 * (a2 + b2), 2.0 * (b2 - b), k4 * (a2 - b2) - k4 * (a + b)]
        first = _first_row(ar.shape)
        g = [(jnp.where(first, d0, gr), jnp.where(first, h0, gi)) for (gr, gi), d0, h0 in zip(g, dc, half)]
    return [jnp.concatenate([gr.astype(dtype), gi.astype(dtype)], axis=0) for gr, gi in g]


def _filter_spectrum_kernel(tf_ref, tb_ref, nf_ref, nb_ref, f_ref, o_ref):
    inv = 1.0 / (nf_ref[...] + nb_ref[...])
    for rows in _freq_chunks(o_ref.shape[1]):
        fwd = _quarter_spectrum(tf_ref, f_ref, rows)
        bwd = _quarter_spectrum(tb_ref, f_ref, rows)
        for i, ((fr, fi), (gr, gi)) in enumerate(zip(fwd, bwd)):
            im = fi - gi
            if i == 0 and rows.start == 0:
                im = jnp.where(_first_row(im.shape), fi + gi, im)
            o_ref[2 * i, rows, :] = (fr + gr) * inv
            o_ref[2 * i + 1, rows, :] = im * inv


def _filter_spectrum(taps, asum, fwd, hy_w, tc):
    l = taps.shape[0]
    q = fwd.shape[2]
    nb = hy_w // tc
    col = lambda d: (lambda o, c: (0, (2 * o + d) * nb + c))
    const = lambda shape: pl.BlockSpec(shape, lambda o, c: (0,) * len(shape), pipeline_mode=pl.Buffered(1))
    est = (_nbytes((4, 2 * q, q), BF16) + 4 * _nbytes((l, tc), BF16)
           + 2 * _nbytes((8, q, tc), F32) + 48 * _nbytes((HY_FREQ_CHUNK, tc), F32) + (8 << 20))
    return pl.pallas_call(
        _filter_spectrum_kernel,
        grid=(HY_ORDER, nb),
        in_specs=[pl.BlockSpec((l, tc), col(0)), pl.BlockSpec((l, tc), col(1)),
                  pl.BlockSpec((1, tc), col(0)), pl.BlockSpec((1, tc), col(1)),
                  const((4, 2 * q, q))],
        out_specs=pl.BlockSpec((None, 8, q, tc), lambda o, c: (o, 0, 0, c)),
        out_shape=jax.ShapeDtypeStruct((HY_ORDER, 8, q, hy_w), F32),
        compiler_params=_cparams(("parallel", "arbitrary"), est),
        name="hyena_filter_spectrum",
    )(taps, taps, asum, asum, fwd)


HY_SEQS_PER_STEP = 3


def _hyena_conv_kernel(z_ref, g_ref, h_ref, b_ref, f_ref, inv_ref, o_ref, spec_ref, *scratch, natural_out):
    nseq, s, tc = z_ref.shape
    q = s // 4
    for b in range(nseq):
        for rows in _freq_chunks(q):
            spec = _quarter_spectrum(z_ref.at[b], f_ref, rows)
            filt = [(h_ref[2 * i, rows, :], h_ref[2 * i + 1, rows, :]) for i in range(4)]
            prod = _spectrum_product(filt, spec, rows.start == 0)
            for c, packed in enumerate(_inverse_butterfly(prod, rows, spec_ref.dtype)):
                spec_ref[b, c, 2 * rows.start:2 * rows.stop, :] = packed
    bias = b_ref[...]
    for b in range(nseq):
        for rows in _freq_chunks(q):
            r = rows.stop - rows.start
            outs = []
            for c in range(4):
                cls = slice(c * q + rows.start, c * q + rows.stop)
                conv = jnp.dot(inv_ref[c, rows, :], spec_ref[b, c], preferred_element_type=F32)
                outs.append((cls, g_ref[b, cls, :].astype(F32) * (conv + bias * z_ref[b, cls, :].astype(F32))))
            if natural_out:
                (nat,) = scratch
                span = slice(4 * rows.start, 4 * rows.stop)
                for lane in range(tc // LANES):
                    cols = slice(lane * LANES, (lane + 1) * LANES)
                    for c, (_, val) in enumerate(outs):
                        nat[b, lane, pl.ds(4 * rows.start + c, r, stride=4), :] = val[:, cols]
                    o_ref[b, span, cols] = nat[b, lane, span, :].astype(o_ref.dtype)
            else:
                for cls, val in outs:
                    o_ref[b, cls, :] = val.astype(o_ref.dtype)


def _hyena_conv(z, z_col0, gate, gate_col0, filt, order, bias_row, mats, width, natural_out):
    fwd, inv, tc = mats
    b, s, _ = z.shape
    q = s // 4
    nc = width // tc
    nseq = max(n for n in range(1, HY_SEQS_PER_STEP + 1) if b % n == 0)
    zb0, gb0 = z_col0 // tc, gate_col0 // tc
    const = lambda shape: pl.BlockSpec(shape, lambda c, i: (0,) * len(shape), pipeline_mode=pl.Buffered(1))
    scratch = [pltpu.VMEM((nseq, 4, 2 * q, tc), BF16)]
    if natural_out:
        scratch.append(pltpu.VMEM((nseq, tc // LANES, s, LANES), F32))
    est = (2 * _nbytes((4, 2 * q, q), BF16) + 2 * _nbytes((8, q, tc), F32)
           + nseq * (6 * _nbytes((s, tc), BF16) + 4 * _nbytes((2 * q, tc), BF16) + _nbytes((s, tc), F32))
           + 64 * _nbytes((HY_FREQ_CHUNK, tc), F32) + (8 << 20))
    return pl.pallas_call(
        functools.partial(_hyena_conv_kernel, natural_out=natural_out),
        grid=(nc, b // nseq),
        in_specs=[pl.BlockSpec((nseq, s, tc), lambda c, i: (i, 0, zb0 + c)),
                  pl.BlockSpec((nseq, s, tc), lambda c, i: (i, 0, gb0 + c)),
                  pl.BlockSpec((None, 8, q, tc), lambda c, i: (order, 0, 0, c)),
                  pl.BlockSpec((1, tc), lambda c, i: (0, c)),
                  const((4, 2 * q, q)), const((4, q, 2 * q))],
        out_specs=pl.BlockSpec((nseq, s, tc), lambda c, i: (i, 0, c)),
        out_shape=jax.ShapeDtypeStruct((b, s, width), BF16),
        scratch_shapes=scratch,
        compiler_params=_cparams(("parallel", "arbitrary"), est),
        name="hyena_conv",
    )(z, gate, filt, bias_row, fwd, inv)


MATMUL_ROW_CHUNK = 512


def _chunk_rows(n_rows):
    chunk = min(MATMUL_ROW_CHUNK, n_rows)
    return [slice(r0, r0 + chunk) for r0 in range(0, n_rows, chunk)]


def _merge_kernel(a_ref, hy_ref, wa_ref, wh_ref, g0_ref, g1_ref, o_ref):
    for rows in _chunk_rows(a_ref.shape[0]):
        ya = jnp.dot(a_ref[rows, :], wa_ref[...], preferred_element_type=F32)
        yh = jnp.dot(hy_ref[rows, :], wh_ref[...], preferred_element_type=F32)
        o_ref[rows, :] = (g0_ref[rows, :].astype(F32) * ya + g1_ref[rows, :].astype(F32) * yh).astype(o_ref.dtype)


def _weight_spec(k, tn, n):
    if tn == n:
        return pl.BlockSpec((k, n), lambda i, j: (0, 0), pipeline_mode=pl.Buffered(1))
    return pl.BlockSpec((k, tn), lambda i, j: (0, j))


def _gated_merge(attn, hy, w_att, w_hy, gates, tm=512, tn=2048):
    t, k = attn.shape
    n = w_att.shape[1]
    tm, tn = min(tm, t), min(tn, n)
    g1b0 = n // tn
    est = 4 * _nbytes((tm, k), BF16) + 4 * _nbytes((k, tn), BF16) + 6 * _nbytes((tm, tn), F32) + (8 << 20)
    return pl.pallas_call(
        _merge_kernel,
        grid=(t // tm, n // tn),
        in_specs=[pl.BlockSpec((tm, k), lambda i, j: (i, 0)),
                  pl.BlockSpec((tm, k), lambda i, j: (i, 0)),
                  _weight_spec(k, tn, n),
                  _weight_spec(k, tn, n),
                  pl.BlockSpec((tm, tn), lambda i, j: (i, j)),
                  pl.BlockSpec((tm, tn), lambda i, j: (i, g1b0 + j))],
        out_specs=pl.BlockSpec((tm, tn), lambda i, j: (i, j)),
        out_shape=jax.ShapeDtypeStruct((t, n), BF16),
        compiler_params=_cparams(("parallel", "arbitrary"), est),
        name="gated_merge",
    )(attn, hy, w_att, w_hy, gates, gates)


def _mm_residual_kernel(a_ref, w_ref, ra_ref, rb_ref, o_ref, *, na):
    def body(r_ref):
        for rows in _chunk_rows(a_ref.shape[0]):
            o_ref[rows, :] = r_ref[rows, :] + jnp.dot(a_ref[rows, :], w_ref[...], preferred_element_type=F32)

    _pick_group(na, ra_ref, rb_ref, body)


def _matmul_residual(a, w, res_a, res_b, tm=512, tn=2048):
    t, k = a.shape
    n = w.shape[1]
    tm, tn = min(tm, res_a.shape[0]), min(tn, n)
    assert res_a.shape[0] % tm == 0 and res_b.shape[0] % tm == 0
    na = res_a.shape[0] // tm
    est = 2 * _nbytes((tm, k), BF16) + 2 * _nbytes((k, tn), BF16) + 8 * _nbytes((tm, tn), F32) + (8 << 20)
    return pl.pallas_call(
        functools.partial(_mm_residual_kernel, na=na),
        grid=(t // tm, n // tn),
        in_specs=[pl.BlockSpec((tm, k), lambda i, j: (i, 0)),
                  _weight_spec(k, tn, n),
                  *_two_group_specs((tm, tn), na, col=lambda j: j)],
        out_specs=pl.BlockSpec((tm, tn), lambda i, j: (i, j)),
        out_shape=jax.ShapeDtypeStruct((t, n), F32),
        compiler_params=_cparams(("parallel", "arbitrary"), est),
        name="matmul_residual",
    )(a, w, res_a, res_b)


def _rmsnorm_rows(x, g):
    ms = jnp.mean(x * x, axis=-1, keepdims=True)
    return x * lax.rsqrt(ms + EPS) * g


def _mlp_kernel(x_ref, g_ref, wu_ref, wd_ref, o_ref, hm_ref):
    def body(first):
        for rows in _chunk_rows(hm_ref.shape[0]):
            if first:
                x = x_ref[rows, :]
                hm = _rmsnorm_rows(x, g_ref[...]).astype(hm_ref.dtype)
                hm_ref[rows, :] = hm
            else:
                hm = hm_ref[rows, :]
            a = jnp.dot(hm, wu_ref[...], preferred_element_type=F32)
            a = jnp.square(jnp.maximum(a, 0.0)).astype(BF16)
            y = jnp.dot(a, wd_ref[...], preferred_element_type=F32)
            o_ref[rows, :] = (x if first else o_ref[rows, :]) + y

    j = pl.program_id(1)
    pl.when(j == 0)(lambda: body(True))
    pl.when(j > 0)(lambda: body(False))


def _norm_v_kernel(xa_ref, xb_ref, g_ref, w_ref, h_ref, v_ref, *, na):
    def body(x_ref):
        h = _rmsnorm_rows(x_ref[...], g_ref[...]).astype(h_ref.dtype)
        h_ref[...] = h
        y = jnp.dot(h, w_ref[...], preferred_element_type=F32)
        for hh in range(v_ref.shape[0]):
            v_ref[hh] = y[:, hh * HEAD_DIM:(hh + 1) * HEAD_DIM].astype(v_ref.dtype)

    _pick_group(na, xa_ref, xb_ref, body)


def _rmsnorm_and_v(xa, xb, g, w, col0, width, s, tm=512):
    d = xa.shape[1]
    assert xa.shape[0] % tm == 0 and xb.shape[0] % tm == 0 and s % tm == 0
    na, t = xa.shape[0] // tm, xa.shape[0] + xb.shape[0]
    nh, per_seq = width // HEAD_DIM, s // tm
    est = (8 * _nbytes((tm, d), F32) + _nbytes((d, width), BF16) + 4 * _nbytes((tm, d), BF16)
           + 4 * _nbytes((tm, width), F32) + (8 << 20))
    return pl.pallas_call(
        functools.partial(_norm_v_kernel, na=na),
        grid=(t // tm,),
        in_specs=[*_two_group_specs((tm, d), na), pl.BlockSpec((1, d), lambda i: (0, 0)),
                  pl.BlockSpec((d, width), lambda i: (0, col0 // width), pipeline_mode=pl.Buffered(1))],
        out_specs=[pl.BlockSpec((tm, d), lambda i: (i, 0)),
                   pl.BlockSpec((None, nh, tm, HEAD_DIM), lambda i: (i // per_seq, 0, i % per_seq, 0))],
        out_shape=[jax.ShapeDtypeStruct((t, d), BF16), jax.ShapeDtypeStruct((t // s, nh, s, HEAD_DIM), BF16)],
        compiler_params=_cparams(("parallel",), est),
        name="rmsnorm_and_v",
    )(xa, xb, g.reshape(1, d), w)


def _relu2_mlp(x, g, w_up, w_down, tm=1024, tf=1024):
    t, d = x.shape
    ff = w_up.shape[1]
    tm, tf = min(tm, t), min(tf, ff)
    est = (4 * _nbytes((tm, d), F32) + _nbytes((tm, d), BF16) + 4 * _nbytes((d, tf), BF16)
           + 3 * _nbytes((tm, tf), F32) + (8 << 20))
    return pl.pallas_call(
        _mlp_kernel,
        grid=(t // tm, ff // tf),
        in_specs=[pl.BlockSpec((tm, d), lambda i, j: (i, 0)),
                  pl.BlockSpec((1, d), lambda i, j: (0, 0)),
                  pl.BlockSpec((d, tf), lambda i, j: (0, j)),
                  pl.BlockSpec((tf, d), lambda i, j: (j, 0))],
        out_specs=pl.BlockSpec((tm, d), lambda i, j: (i, 0)),
        out_shape=jax.ShapeDtypeStruct((t, d), F32),
        scratch_shapes=[pltpu.VMEM((tm, d), BF16)],
        compiler_params=_cparams(("parallel", "arbitrary"), est),
        name="relu2_mlp",
    )(x, g.reshape(1, d), w_up, w_down)


def _ple_kernel(x_ref, g_ref, wg_ref, p_ref, wp_ref, o_ref):
    for rows in _chunk_rows(x_ref.shape[0]):
        x = x_ref[rows, :]
        hn = _rmsnorm_rows(x, g_ref[...]).astype(BF16)
        gate = jax.nn.sigmoid(jnp.dot(hn, wg_ref[...], preferred_element_type=F32))
        proj = jnp.dot(p_ref[rows, :].astype(BF16), wp_ref[...], preferred_element_type=F32)
        o_ref[rows, :] = x + gate * proj


def _gated_ple(x, row0, p, g, w_gate, w_proj, tm=512):
    t, pd = p.shape
    d = x.shape[1]
    tm = min(tm, t)
    rb0 = row0 // tm
    est = (4 * _nbytes((tm, d), F32) + _nbytes((d, d), BF16) + _nbytes((pd, d), BF16)
           + 2 * _nbytes((tm, pd), F32) + 8 * _nbytes((MATMUL_ROW_CHUNK, d), F32) + (8 << 20))
    return pl.pallas_call(
        _ple_kernel,
        grid=(t // tm, 1),
        in_specs=[pl.BlockSpec((tm, d), lambda i, j: (rb0 + i, 0)),
                  pl.BlockSpec((1, d), lambda i, j: (0, 0)),
                  _weight_spec(d, d, d),
                  pl.BlockSpec((tm, pd), lambda i, j: (i, 0)),
                  _weight_spec(pd, d, d)],
        out_specs=pl.BlockSpec((tm, d), lambda i, j: (i, 0)),
        out_shape=jax.ShapeDtypeStruct((t, d), F32),
        compiler_params=_cparams(("parallel", "arbitrary"), est),
        name="gated_ple",
    )(x, g.reshape(1, d), w_gate, p, w_proj)


def _encoder_layer(x_list, p_list, norm_mix_g, w_in, gate_b, q_norm_g, k_norm_g, hy_conv_w, hy_conv_b,
                   hy_w1, hy_b1, hy_w2, hy_b2, hy_sin_freq, hy_w3, hy_bias,
                   w_att_out, w_hy_out, w_o, norm_mlp_g, w_up, w_down,
                   norm_ple_g, w_ple_gate, w_ple_proj):
    xa3, xb3 = x_list
    s, d = xa3.shape[1:]
    b = xa3.shape[0] + xb3.shape[0]
    att_w = w_att_out.shape[0]
    hy_w = w_hy_out.shape[0]
    xa, xb = xa3.reshape(-1, d), xb3.reshape(-1, d)
    bf = lambda a: a.astype(BF16)

    w_in_b = bf(w_in)
    h2, v = _rmsnorm_and_v(xa, xb, norm_mix_g, w_in_b, 2 * att_w, att_w, s)
    h3 = h2.reshape(b, s, d)
    rope_c, rope_s1, rope_s2 = _rope_tables(s)
    tab_spec = pl.BlockSpec((s, HEAD_DIM), lambda i, j: (0, 0))
    vec_spec = lambda tn, blk0=0: pl.BlockSpec((1, tn), lambda i, j: (0, blk0 + j))
    head_spec = pl.BlockSpec((1, HEAD_DIM), lambda i, j: (0, 0))
    tn = 1024
    qk = lambda g, col0, scale: _inproj(
        h3, w_in_b, col0, att_w, functools.partial(_inproj_qk_kernel, scale=scale),
        (g.reshape(1, HEAD_DIM), rope_c, rope_s1, rope_s2), [head_spec, tab_spec, tab_spec, tab_spec], tn,
        head_major=True)
    q = qk(q_norm_g, 0, HEAD_DIM ** -0.5 * math.log2(math.e))
    k = qk(k_norm_g, att_w, 1.0)
    tnc = min(tn, hy_w)
    u = _inproj(h3, w_in_b, 3 * att_w, 3 * hy_w, _inproj_conv_kernel,
                (hy_conv_w, hy_conv_b.reshape(1, 3 * hy_w)),
                [pl.BlockSpec((3, tnc), lambda i, j: (0, j)), vec_spec(tnc)], tnc,
                scratch=[pltpu.VMEM((tnc // LANES, s + 2 * CONV_PAD_ROWS, LANES), F32)])
    tng = min(tn, d)
    gates = _inproj(h3, w_in_b, 3 * att_w + 3 * hy_w, N_BRANCH * d, _inproj_gate_kernel,
                    (gate_b.reshape(1, N_BRANCH * d),), [vec_spec(tng)], tng)

    attn = _dilated_attention(q, k, v)

    hy_tc = min(256, hy_w)
    mats = (*_dft_matrices(s // 4, 2 * s), hy_tc)
    taps, tap_asum = _hyena_filter_taps(s, hy_w1, hy_b1, hy_w2, hy_b2, hy_sin_freq, hy_w3, hy_w)
    filt = _filter_spectrum(taps, tap_asum, mats[0], hy_w, hy_tc)
    bias = hy_bias.astype(F32)
    z = _hyena_conv(u, 0, u, hy_w, filt, 0, bias[0:1], mats, hy_w, natural_out=False)
    hy = _hyena_conv(z, 0, u, 2 * hy_w, filt, 1, bias[1:2], mats, hy_w, natural_out=True)

    merged = _gated_merge(attn.reshape(b * s, att_w), hy.reshape(b * s, hy_w),
                          bf(w_att_out), bf(w_hy_out), gates.reshape(b * s, N_BRANCH * d))
    x2 = _matmul_residual(merged, bf(w_o), xa, xb)
    x2 = _relu2_mlp(x2, norm_mlp_g, bf(w_up), bf(w_down))

    outs, row0 = [], 0
    w_pg, w_pp = bf(w_ple_gate), bf(w_ple_proj)
    for p in p_list:
        pb = p.shape[0]
        p2 = p.reshape(pb * s, p.shape[-1])
        outs.append(_gated_ple(x2, row0, p2, norm_ple_g, w_pg, w_pp).reshape(pb, s, d))
        row0 += pb * s
    return outs


def kernel(x_prompt, x_sample, p_prompt, p_sample, norm_mix_g, w_in, gate_b, q_norm_g, k_norm_g, hy_conv_w, hy_conv_b, hy_w1, hy_b1, hy_w2, hy_b2, hy_sin_freq, hy_w3, hy_bias, w_att_out, w_hy_out, w_o, norm_mlp_g, w_up, w_down, norm_ple_g, w_ple_gate, w_ple_proj):
    assert x_prompt.shape[1:] == x_sample.shape[1:]
    y_prompt, y_sample = x_prompt, x_sample
    for i in range(w_in.shape[0]):
        y_prompt, y_sample = _encoder_layer(
            (y_prompt, y_sample), (p_prompt[i], p_sample[i]), norm_mix_g[i], w_in[i], gate_b[i], q_norm_g[i], k_norm_g[i],
            hy_conv_w[i], hy_conv_b[i], hy_w1[i], hy_b1[i], hy_w2[i], hy_b2[i], hy_sin_freq[i], hy_w3[i],
            hy_bias[i], w_att_out[i], w_hy_out[i], w_o[i], norm_mlp_g[i], w_up[i], w_down[i],
            norm_ple_g[i], w_ple_gate[i], w_ple_proj[i])
    return (y_prompt, y_sample)
```
